```python
import math
import jax, jax.numpy as jnp
from jax import lax
import numpy as np

D_MODEL = 1024
BATCH = 16
SEQ = 2048
DEPTH = 1

GRID_W = 64
CTX_LEN = 256
EPS = 1e-6
N_MOD = 6

GLA_HEADS = 4
GLA_DK = D_MODEL // 2
GLA_DV = D_MODEL
GLA_HEAD_K = GLA_DK // GLA_HEADS
GLA_HEAD_V = GLA_DV // GLA_HEADS
GLA_GATE_RANK = 16
GLA_GATE_NORM = 16.0
GLA_CHUNK = 64

HY_WIDTH = D_MODEL
HY_SHORT = 3
HY_EMB = 33
HY_ORDER = 64
HY_FAST_DECAY = 0.3
HY_SLOW_DECAY = 1.5
HY_TARGET = 1e-2

N_EXPERTS = 32
TOP_K = 4
D_FF = D_MODEL
SWIGLU_LIMIT = 7.0
SWIGLU_ALPHA = 1.702

IN_SIZES = (GLA_DK, GLA_DK, GLA_DV, GLA_DV, GLA_GATE_RANK, GLA_GATE_RANK, 3 * HY_WIDTH, D_MODEL, D_MODEL)
IN_COLS = GLA_DK * 2 + GLA_DV * 2 + GLA_GATE_RANK * 2 + 3 * HY_WIDTH + 2 * D_MODEL

kernel_name = "hybrid_gla_hyena_moe_dit_block"


def rms_norm(x, w):
    xf = x.astype(jnp.float32)
    y = xf * lax.rsqrt(jnp.mean(xf * xf, axis=-1, keepdims=True) + EPS)
    return (y * w.astype(jnp.float32)).astype(x.dtype)


def ada_params(cvec, w_ada, b_ada):
    m = jax.nn.silu(cvec) @ w_ada + b_ada
    return jnp.split(m[:, None, :], N_MOD, axis=-1)


def split_proj(p):
    idx = np.cumsum(np.array(IN_SIZES))[:-1].tolist()
    return jnp.split(p, idx, axis=-1)


def gla_scan(q, k, v, log_g, s0):
    B, H, L, dk = q.shape
    dv = v.shape[-1]
    n = L // GLA_CHUNK
    C = GLA_CHUNK
    q = q.reshape(B, H, n, C, dk)
    k = k.reshape(B, H, n, C, dk)
    v = v.reshape(B, H, n, C, dv)
    b = jnp.cumsum(log_g.reshape(B, H, n, C, dk), axis=3)
    b_last = b[:, :, :, -1:, :]
    q_dec = q * jnp.exp(b)
    k_inv = k * jnp.exp(-b)
    k_upd = k * jnp.exp(b_last - b)
    mask = jnp.tril(jnp.ones((C, C), dtype=bool))
    att = jnp.where(mask, jnp.einsum('bhnid,bhnjd->bhnij', q_dec, k_inv), 0.0)
    o_intra = jnp.einsum('bhnij,bhnjv->bhniv', att, v)
    upd = jnp.einsum('bhncd,bhncv->bhndv', k_upd, v)
    decay = jnp.exp(b_last[:, :, :, 0, :])

    def step(s, xs):
        q_c, dec_c, upd_c = xs
        o_c = jnp.einsum('bhcd,bhdv->bhcv', q_c, s)
        s = dec_c[..., None] * s + upd_c
        return s, o_c

    xs = (jnp.moveaxis(q_dec, 2, 0), jnp.moveaxis(decay, 2, 0), jnp.moveaxis(upd, 2, 0))
    s_final, o_inter = lax.scan(step, s0, xs)
    o = o_intra + jnp.moveaxis(o_inter, 0, 2)
    return o.reshape(B, H, L, dv), s_final


def gla_bidir(q, k, v, g_f, g_b, s0_f, s0_b):
    o_f, s_f = gla_scan(q, k, v, g_f, s0_f)
    flip = lambda t: jnp.flip(t, axis=2)
    o_b, s_b = gla_scan(flip(q), flip(k), flip(v), flip(g_b), s0_b)
    return o_f + flip(o_b), s_f, s_b


def gla_inputs(q, k, v, r_f, r_b, w_gk_f, b_gk_f, w_gk_b, b_gk_b):
    B, L, _ = q.shape
    heads = lambda t, d: jnp.moveaxis(t.astype(jnp.float32).reshape(B, L, GLA_HEADS, d), 2, 1)
    qh = heads(q, GLA_HEAD_K) * (GLA_HEAD_K ** -0.5)
    kh = heads(k, GLA_HEAD_K)
    vh = heads(v, GLA_HEAD_V)
    g_f = jax.nn.log_sigmoid((r_f @ w_gk_f + b_gk_f).astype(jnp.float32)) / GLA_GATE_NORM
    g_b = jax.nn.log_sigmoid((r_b @ w_gk_b + b_gk_b).astype(jnp.float32)) / GLA_GATE_NORM
    return qh, kh, vh, heads(g_f, GLA_HEAD_K), heads(g_b, GLA_HEAD_K)


def short_conv(u, w, b):
    L = u.shape[1]
    up = jnp.pad(u, ((0, 0), (1, 1), (0, 0)))
    return up[:, :L] * w[0] + up[:, 1:L + 1] * w[1] + up[:, 2:] * w[2] + b


def hyena_kernel(L, w1, b1, w2, b2, w3, b3, w4, freq):
    f32 = jnp.float32
    bands = (HY_EMB - 1) // 2
    pos = jnp.arange(L, dtype=f32)[:, None]
    t = pos / max(L - 1, 1)
    f = jnp.linspace(1e-4, bands - 1, bands, dtype=f32)[None]
    ang = (2.0 * math.pi / L) * pos * f
    z = jnp.concatenate([t, jnp.cos(ang), -jnp.sin(ang)], axis=-1)
    fr = freq.astype(f32)
    h = jnp.sin(fr * (z @ w1.astype(f32) + b1.astype(f32)))
    h = jnp.sin(fr * (h @ w2.astype(f32) + b2.astype(f32)))
    h = jnp.sin(fr * (h @ w3.astype(f32) + b3.astype(f32)))
    h = h @ w4.astype(f32)
    deltas = jnp.abs(jnp.linspace(math.log(HY_TARGET) / HY_SLOW_DECAY,
                                  math.log(HY_TARGET) / HY_FAST_DECAY, HY_WIDTH, dtype=f32))
    window = jnp.exp(-t * deltas[None])
    h_f = h[:, :HY_WIDTH] * window
    h_b = h[:, HY_WIDTH:] * window
    return jnp.concatenate([h_f, jnp.zeros((1, HY_WIDTH), f32), jnp.flip(h_b[1:], axis=0)], axis=0)


def long_conv_bidir(u, k2l, bias):
    L = u.shape[1]
    uf = u.astype(jnp.float32)
    spec = jnp.fft.rfft(uf, n=2 * L, axis=1) * jnp.fft.rfft(k2l, axis=0)[None]
    y = jnp.fft.irfft(spec, n=2 * L, axis=1)[:, :L]
    return y + uf * bias.astype(jnp.float32)


def hyena_branch(u3, grid_rows, lp):
    B, L, C3 = u3.shape
    if grid_rows is not None:
        uc = short_conv(u3.reshape(B * grid_rows, GRID_W, C3), lp["hy_conv_w"], lp["hy_conv_b"]).reshape(B, L, C3)
    else:
        uc = short_conv(u3, lp["hy_conv_w"], lp["hy_conv_b"])
    x0, x1, v = jnp.split(uc, 3, axis=-1)
    k2l = hyena_kernel(L, lp["hy_f_w1"], lp["hy_f_b1"], lp["hy_f_w2"], lp["hy_f_b2"],
                       lp["hy_f_w3"], lp["hy_f_b3"], lp["hy_f_w4"], lp["hy_sin_freq"])
    z = long_conv_bidir(v * x1, k2l, lp["hy_bias"])
    y = x0 * z.astype(x0.dtype)
    return y @ lp["w_hy_out"]


def mixer_pre(h, lp):
    q, k, v, og, r_f, r_b, hy, m_a, m_b = split_proj(h @ lp["w_in"])
    gla_in = gla_inputs(q, k, v, r_f, r_b, lp["w_gk_f"], lp["b_gk_f"], lp["w_gk_b"], lp["b_gk_b"])
    return gla_in, (og, hy, m_a, m_b)


def mixer_post(o_gla, rest, grid_rows, lp):
    og, hy, m_a, m_b = rest
    B, L, _ = og.shape
    o = jnp.moveaxis(o_gla, 1, 2)
    o = rms_norm(o, lp["gla_norm"]).reshape(B, L, GLA_DV).astype(og.dtype)
    branch_a = (o * jax.nn.silu(og)) @ lp["w_gla_out"]
    branch_b = hyena_branch(hy, grid_rows, lp)
    y = jax.nn.sigmoid(m_a) * branch_a + jax.nn.sigmoid(m_b) * branch_b
    return y @ lp["w_out"]


def moe(h, w_router, b_router, w_up, b_up, w_down, b_down):
    B, L, D = h.shape
    t = h.reshape(B * L, D)
    logits = (t @ w_router + b_router).astype(jnp.float32)
    top_val, top_idx = lax.top_k(logits, TOP_K)
    top_w = jax.nn.softmax(top_val, axis=-1)
    gates = jnp.einsum('nk,nke->ne', top_w, jax.nn.one_hot(top_idx, N_EXPERTS, dtype=jnp.float32))
    out = jnp.zeros((B * L, D), jnp.float32)
    for e in range(N_EXPERTS):
        u = t @ w_up[e] + b_up[e]
        glu = jnp.minimum(u[:, 0::2], SWIGLU_LIMIT)
        lin = jnp.clip(u[:, 1::2], -SWIGLU_LIMIT, SWIGLU_LIMIT)
        a = glu * jax.nn.sigmoid(SWIGLU_ALPHA * glu) * (lin + 1.0)
        out = out + gates[:, e:e + 1] * (a @ w_down[e] + b_down[e])
    return out.astype(h.dtype).reshape(B, L, D)


def setup_inputs(seed: int = 0) -> dict:
    key = jax.random.key(seed)
    ks = iter(jax.random.split(key, 40))
    nrm = lambda shape, s: jax.random.normal(next(ks), shape, jnp.float32) * s
    Dp = DEPTH
    E = N_EXPERTS
    return {
        "x": nrm((BATCH, SEQ, D_MODEL), 1.0),
        "c": nrm((BATCH, D_MODEL), 1.0),
        "ctx": nrm((BATCH, CTX_LEN, D_MODEL), 1.0),
        "c_ctx": nrm((D_MODEL,), 1.0),
        "w_ada": nrm((Dp, D_MODEL, N_MOD * D_MODEL), D_MODEL ** -0.5),
        "b_ada": nrm((Dp, N_MOD * D_MODEL), 0.02),
        "norm_mix": 1.0 + nrm((Dp, D_MODEL), 0.02),
        "norm_ffn": 1.0 + nrm((Dp, D_MODEL), 0.02),
        "w_in": nrm((Dp, D_MODEL, IN_COLS), D_MODEL ** -0.5),
        "w_gk_f": nrm((Dp, GLA_GATE_RANK, GLA_DK), GLA_GATE_RANK ** -0.5),
        "b_gk_f": nrm((Dp, GLA_DK), 0.1),
        "w_gk_b": nrm((Dp, GLA_GATE_RANK, GLA_DK), GLA_GATE_RANK ** -0.5),
        "b_gk_b": nrm((Dp, GLA_DK), 0.1),
        "gla_norm": 1.0 + nrm((Dp, GLA_HEAD_V), 0.02),
        "w_gla_out": nrm((Dp, GLA_DV, D_MODEL), GLA_DV ** -0.5),
        "hy_conv_w": nrm((Dp, HY_SHORT, 3 * HY_WIDTH), HY_SHORT ** -0.5),
        "hy_conv_b": nrm((Dp, 3 * HY_WIDTH), 0.02),
        "hy_f_w1": nrm((Dp, HY_EMB, HY_ORDER), HY_EMB ** -0.5),
        "hy_f_b1": nrm((Dp, HY_ORDER), 0.1),
        "hy_f_w2": nrm((Dp, HY_ORDER, HY_ORDER), HY_ORDER ** -0.5),
        "hy_f_b2": nrm((Dp, HY_ORDER), 0.1),
        "hy_f_w3": nrm((Dp, HY_ORDER, HY_ORDER), HY_ORDER ** -0.5),
        "hy_f_b3": nrm((Dp, HY_ORDER), 0.1),
        "hy_f_w4": nrm((Dp, HY_ORDER, 2 * HY_WIDTH), 0.02),
        "hy_sin_freq": 1.0 + nrm((Dp, HY_ORDER), 0.02),
        "hy_bias": nrm((Dp, HY_WIDTH), 0.5),
        "w_hy_out": nrm((Dp, HY_WIDTH, D_MODEL), HY_WIDTH ** -0.5),
        "w_out": nrm((Dp, D_MODEL, D_MODEL), D_MODEL ** -0.5),
        "w_router": nrm((Dp, D_MODEL, E), D_MODEL ** -0.5),
        "b_router": nrm((Dp, E), 0.01),
        "w_up": nrm((Dp, E, D_MODEL, 2 * D_FF), D_MODEL ** -0.5),
        "b_up": nrm((Dp, E, 2 * D_FF), 0.02),
        "w_down": nrm((Dp, E, D_FF, D_MODEL), D_FF ** -0.5),
        "b_down": nrm((Dp, E, D_MODEL), 0.02),
        "norm_final": 1.0 + nrm((D_MODEL,), 0.02),
    }


def reference(x, c, ctx, c_ctx, w_ada, b_ada, norm_mix, norm_ffn, w_in, w_gk_f, b_gk_f, w_gk_b, b_gk_b,
              gla_norm, w_gla_out, hy_conv_w, hy_conv_b, hy_f_w1, hy_f_b1, hy_f_w2, hy_f_b2, hy_f_w3, hy_f_b3,
              hy_f_w4, hy_sin_freq, hy_bias, w_hy_out, w_out, w_router, b_router, w_up, b_up, w_down, b_down,
              norm_final):
    params = dict(w_ada=w_ada, b_ada=b_ada, norm_mix=norm_mix, norm_ffn=norm_ffn, w_in=w_in,
                  w_gk_f=w_gk_f, b_gk_f=b_gk_f, w_gk_b=w_gk_b, b_gk_b=b_gk_b, gla_norm=gla_norm,
                  w_gla_out=w_gla_out, hy_conv_w=hy_conv_w, hy_conv_b=hy_conv_b, hy_f_w1=hy_f_w1,
                  hy_f_b1=hy_f_b1, hy_f_w2=hy_f_w2, hy_f_b2=hy_f_b2, hy_f_w3=hy_f_w3, hy_f_b3=hy_f_b3,
                  hy_f_w4=hy_f_w4, hy_sin_freq=hy_sin_freq, hy_bias=hy_bias, w_hy_out=w_hy_out, w_out=w_out,
                  w_router=w_router, b_router=b_router, w_up=w_up, b_up=b_up, w_down=w_down, b_down=b_down)
    B = x.shape[0]
    rows = x.shape[1] // GRID_W
    xc = ctx
    for i in range(DEPTH):
        lp = {name: arr[i] for name, arr in params.items()}
        last = i == DEPTH - 1
        sh1, sc1, g1, sh2, sc2, g2 = ada_params(c, lp["w_ada"], lp["b_ada"])
        csh1, csc1, cg1, csh2, csc2, cg2 = ada_params(c_ctx[None], lp["w_ada"], lp["b_ada"])

        hx = rms_norm(x, lp["norm_mix"]) * (1.0 + sc1) + sh1
        hc = rms_norm(xc, lp["norm_mix"]) * (1.0 + csc1) + csh1
        gla_x, rest_x = mixer_pre(hx, lp)
        gla_c, rest_c = mixer_pre(hc, lp)
        zero = jnp.zeros((B, GLA_HEADS, GLA_HEAD_K, GLA_HEAD_V), jnp.float32)
        o_c, s_ctx_f, s_ctx_b = gla_bidir(*gla_c, zero, zero)
        o_x, _, _ = gla_bidir(*gla_x, s_ctx_f, s_ctx_b)
        x = x + g1 * mixer_post(o_x, rest_x, rows, lp)

        hx2 = rms_norm(x, lp["norm_ffn"]) * (1.0 + sc2) + sh2
        x = x + g2 * moe(hx2, lp["w_router"], lp["b_router"], lp["w_up"], lp["b_up"], lp["w_down"], lp["b_down"])

        if not last:
            xc = xc + cg1 * mixer_post(o_c, rest_c, None, lp)
            hc2 = rms_norm(xc, lp["norm_ffn"]) * (1.0 + csc2) + csh2
            xc = xc + cg2 * moe(hc2, lp["w_router"], lp["b_router"], lp["w_up"], lp["b_up"], lp["w_down"], lp["b_down"])
    return rms_norm(x, norm_final)
```

```python
import functools
import math

import jax
import jax.numpy as jnp
import numpy as np
from jax import lax
from jax.experimental import pallas as pl
from jax.experimental.pallas import tpu as pltpu

F32 = jnp.float32
BF16 = jnp.bfloat16

D_MODEL = 1024
GRID_W = 64
EPS = 1e-6
N_MOD = 6

GLA_HEADS = 4
GLA_HEAD_K = 128
GLA_HEAD_V = 256
GLA_DK = GLA_HEADS * GLA_HEAD_K
GLA_DV = GLA_HEADS * GLA_HEAD_V
GLA_GATE_RANK = 16
GLA_GATE_NORM = 16.0
GLA_CHUNK = 64

HY_WIDTH = D_MODEL
HY_EMB = 33
HY_FAST_DECAY = 0.3
HY_SLOW_DECAY = 1.5
HY_TARGET = 1e-2

N_EXPERTS = 32
TOP_K = 4
SWIGLU_LIMIT = 7.0
SWIGLU_ALPHA = 1.702

V7X_VMEM_BYTES = 64 * 1024 * 1024
VMEM_LIMIT = V7X_VMEM_BYTES - 8 * 1024 * 1024
LANES = 128

MOD_SH1, MOD_SC1, MOD_G1, MOD_SH2, MOD_SC2, MOD_G2 = range(N_MOD)


def _params(*sem):
    return pltpu.CompilerParams(dimension_semantics=sem, vmem_limit_bytes=VMEM_LIMIT)


def _resident(shape):
    return pl.BlockSpec(shape, lambda *_: (0,) * len(shape), pipeline_mode=pl.Buffered(1))


def _dot(a, b):
    return jnp.dot(a, b, preferred_element_type=F32)


def _dot_nt(a, b):
    return lax.dot_general(a, b, (((1,), (1,)), ((), ())), preferred_element_type=F32)


def _dot_tn(a, b):
    return lax.dot_general(a, b, (((0,), (0,)), ((), ())), preferred_element_type=F32)


def _split(a):
    hi = a.astype(BF16)
    lo = (a - hi.astype(F32)).astype(BF16)
    return hi, lo


def _dot3(a, b):
    ah, al = _split(a)
    bh, bl = _split(b)
    return _dot(ah, bh) + (_dot(ah, bl) + _dot(al, bh))


def _sigmoid(x):
    return 1.0 / (1.0 + jnp.exp(-x))


def _log_sigmoid(x):
    return jnp.minimum(x, 0.0) - jnp.log(1.0 + jnp.exp(-jnp.abs(x)))


def _rms(x, w):
    return x * lax.rsqrt(jnp.mean(x * x, axis=-1, keepdims=True) + EPS) * w


def _ada_kernel(c_ref, w_ref, b_ref, o_ref):
    c = c_ref[...]
    o_ref[...] = _dot3(c * _sigmoid(c), w_ref[...]) + b_ref[...]


def _ada(cc, w_ada, b_ada):
    rows = cc.shape[0]
    return pl.pallas_call(
        _ada_kernel,
        grid=(N_MOD,),
        in_specs=[pl.BlockSpec((rows, D_MODEL), lambda j: (0, 0)),
                  pl.BlockSpec((D_MODEL, D_MODEL), lambda j: (0, j)),
                  pl.BlockSpec((1, D_MODEL), lambda j: (0, j))],
        out_specs=pl.BlockSpec((rows, D_MODEL), lambda j: (0, j)),
        out_shape=jax.ShapeDtypeStruct((rows, N_MOD * D_MODEL), F32),
        compiler_params=_params("parallel"),
        name="ada",
    )(cc, w_ada, b_ada.reshape(1, -1))


INPROJ_COL_CHUNK = 512


def _inproj_kernel(n_out, x_ref, sc_ref, sh_ref, nw_ref, *refs):
    w_refs, o_refs = refs[:n_out], refs[n_out:]
    h = _rms(x_ref[0], nw_ref[...]) * (1.0 + sc_ref[0]) + sh_ref[0]
    hb = h.astype(BF16)
    for w_ref, o_ref in zip(w_refs, o_refs):
        n = w_ref.shape[1]
        for c0 in range(0, n, INPROJ_COL_CHUNK):
            c1 = min(c0 + INPROJ_COL_CHUNK, n)
            o_ref[0, :, c0:c1] = _dot(hb, w_ref[:, c0:c1]).astype(o_ref.dtype)


def _inproj(x, mods, mod_row, norm_w, weights, out_dtypes, tl):
    bsz, seq, _ = x.shape
    n_out = len(weights)
    in_specs = [pl.BlockSpec((1, tl, D_MODEL), lambda b, i: (b, i, 0)),
                pl.BlockSpec((1, 1, D_MODEL), lambda b, i: (mod_row(b), 0, MOD_SC1)),
                pl.BlockSpec((1, 1, D_MODEL), lambda b, i: (mod_row(b), 0, MOD_SH1)),
                _resident((1, D_MODEL))]
    in_specs += [_resident(w.shape) for w in weights]
    out_specs = [pl.BlockSpec((1, tl, w.shape[1]), lambda b, i: (b, i, 0)) for w in weights]
    out_shape = [jax.ShapeDtypeStruct((bsz, seq, w.shape[1]), dt) for w, dt in zip(weights, out_dtypes)]
    return pl.pallas_call(
        functools.partial(_inproj_kernel, n_out),
        grid=(bsz, seq // tl),
        in_specs=in_specs, out_specs=out_specs, out_shape=out_shape,
        compiler_params=_params("parallel", "parallel"),
        name="inproj",
    )(x, mods, mods, norm_w.reshape(1, -1), *weights)


GLA_TILE = 256
GLA_NCH = GLA_TILE // GLA_CHUNK
GLA_SCALE = GLA_HEAD_K ** -0.5


def _gla_kernel(qf_ref, kf_ref, vf_ref, rf_ref, qb_ref, kb_ref, vb_ref, rb_ref,
                kc_ref, vc_ref, rcf_ref, rcb_ref, wgk_ref, bgk_ref,
                of_ref, ob_ref, sf_ref, sb_ref):
    t = pl.program_id(2)
    row = lax.broadcasted_iota(jnp.int32, (GLA_TILE, GLA_TILE), 0)
    col = lax.broadcasted_iota(jnp.int32, (GLA_TILE, GLA_TILE), 1)
    same = (row // GLA_CHUNK) == (col // GLA_CHUNK)
    tri_f = jnp.where(same & (col <= row), 1.0, 0.0).astype(BF16)
    tri_b = jnp.where(same & (col >= row), 1.0, 0.0).astype(BF16)
    crow = lax.broadcasted_iota(jnp.int32, (GLA_CHUNK, GLA_CHUNK), 0)
    ccol = lax.broadcasted_iota(jnp.int32, (GLA_CHUNK, GLA_CHUNK), 1)
    mask_f = ccol <= crow
    mask_b = ccol >= crow

    def cum_decay(r, d, tri):
        z = _dot3(r, wgk_ref[d]) + bgk_ref[d]
        g = _log_sigmoid(z) * (1.0 / GLA_GATE_NORM)
        gh, gl = _split(g)
        return _dot(tri, gh) + _dot(tri, gl)

    def chunks(fwd):
        return range(GLA_NCH) if fwd else range(GLA_NCH - 1, -1, -1)

    def rows(c):
        return slice(c * GLA_CHUNK, (c + 1) * GLA_CHUNK)

    def last_row(b, c, fwd):
        i = c * GLA_CHUNK + (GLA_CHUNK - 1 if fwd else 0)
        return b[i:i + 1, :]

    def state_step(st, k, v, b, b_last):
        k_upd = (k * jnp.exp(b_last - b)).astype(BF16)
        return st * jnp.exp(b_last) + _dot_tn(v, k_upd)

    def scan_state(st, k_ref, v_ref, b, fwd):
        for c in chunks(fwd):
            k = k_ref[0, rows(c), :].astype(F32)
            st = state_step(st, k, v_ref[0, rows(c), :], b[rows(c)], last_row(b, c, fwd))
        return st

    def scan_out(st, q_ref, k_ref, v_ref, o_ref, b, fwd):
        mask = mask_f if fwd else mask_b
        for c in chunks(fwd):
            bc = b[rows(c)]
            q = q_ref[0, rows(c), :].astype(F32)
            k = k_ref[0, rows(c), :].astype(F32)
            v = v_ref[0, rows(c), :]
            q_dec = (q * jnp.exp(bc) * GLA_SCALE).astype(BF16)
            k_inv = (k * jnp.exp(-bc)).astype(BF16)
            att = jnp.where(mask, _dot_nt(q_dec, k_inv), 0.0).astype(BF16)
            o = _dot(att, v) + _dot_nt(q_dec, st.astype(BF16))
            o_ref[0, rows(c), :] = o.astype(o_ref.dtype)
            st = state_step(st, k, v, bc, last_row(b, c, fwd))
        return st

    @pl.when(t == 0)
    def _():
        zero = jnp.zeros((GLA_HEAD_V, GLA_HEAD_K), F32)
        sf_ref[...] = scan_state(zero, kc_ref, vc_ref, cum_decay(rcf_ref[0], 0, tri_f), True)
        sb_ref[...] = scan_state(zero, kc_ref, vc_ref, cum_decay(rcb_ref[0], 1, tri_b), False)

    sf_ref[...] = scan_out(sf_ref[...], qf_ref, kf_ref, vf_ref, of_ref, cum_decay(rf_ref[0], 0, tri_f), True)
    sb_ref[...] = scan_out(sb_ref[...], qb_ref, kb_ref, vb_ref, ob_ref, cum_decay(rb_ref[0], 1, tri_b), False)


def _gla(q, k, v, r_f, r_b, k_c, v_c, rc_f, rc_b, w_gk, b_gk):
    bsz, seq, _ = q.shape
    ctx_len = k_c.shape[1]
    assert ctx_len == GLA_TILE
    nt = seq // GLA_TILE
    fwd = lambda b, h, t: (b, t, h)
    bwd = lambda b, h, t: (b, nt - 1 - t, h)
    fwd0 = lambda b, h, t: (b, t, 0)
    bwd0 = lambda b, h, t: (b, nt - 1 - t, 0)
    ctx = lambda b, h, t: (b, 0, h)
    ctx0 = lambda b, h, t: (b, 0, 0)
    kblk = (1, GLA_TILE, GLA_HEAD_K)
    vblk = (1, GLA_TILE, GLA_HEAD_V)
    rblk = (1, GLA_TILE, GLA_GATE_RANK)
    in_specs = [pl.BlockSpec(kblk, fwd), pl.BlockSpec(kblk, fwd), pl.BlockSpec(vblk, fwd), pl.BlockSpec(rblk, fwd0),
                pl.BlockSpec(kblk, bwd), pl.BlockSpec(kblk, bwd), pl.BlockSpec(vblk, bwd), pl.BlockSpec(rblk, bwd0),
                pl.BlockSpec(kblk, ctx), pl.BlockSpec(vblk, ctx), pl.BlockSpec(rblk, ctx0), pl.BlockSpec(rblk, ctx0),
                pl.BlockSpec((2, GLA_GATE_RANK, GLA_HEAD_K), lambda b, h, t: (0, 0, h)),
                pl.BlockSpec((2, 1, GLA_HEAD_K), lambda b, h, t: (0, 0, h))]
    out_specs = [pl.BlockSpec(vblk, fwd), pl.BlockSpec(vblk, bwd)]
    out_shape = [jax.ShapeDtypeStruct((bsz, seq, GLA_DV), BF16)] * 2
    return pl.pallas_call(
        _gla_kernel,
        grid=(bsz, GLA_HEADS, nt),
        in_specs=in_specs, out_specs=out_specs, out_shape=out_shape,
        scratch_shapes=[pltpu.VMEM((GLA_HEAD_V, GLA_HEAD_K), F32)] * 2,
        compiler_params=_params("parallel", "parallel", "arbitrary"),
        name="gla",
    )(q, k, v, r_f, q, k, v, r_b, k_c, v_c, rc_f, rc_b, w_gk, b_gk)


HY_CW = 256
HY_ROW_CHUNK = 512
MLP_PAD = 128


@functools.lru_cache(maxsize=None)
def _dft_table(seq):
    n = 2 * seq
    ph = (np.arange(seq)[:, None] * np.arange(seq)[None, :]) % n
    ang = 2.0 * np.pi * np.arange(n) / n
    c = np.cos(ang)[ph]
    s = np.sin(ang)[ph]
    s[0, :] = np.where(np.arange(seq) % 2 == 0, 1.0, -1.0)
    return np.concatenate([c, s], axis=0).astype(np.float32)


@functools.lru_cache(maxsize=None)
def _filter_features(seq):
    bands = (HY_EMB - 1) // 2
    pos = np.arange(seq, dtype=np.float64)[:, None]
    t = pos / max(seq - 1, 1)
    f = np.linspace(1e-4, bands - 1, bands)[None]
    ang = (2.0 * math.pi / seq) * pos * f
    z = np.concatenate([t, np.cos(ang), -np.sin(ang)], axis=-1)
    out = np.zeros((seq, MLP_PAD), np.float32)
    out[:, :HY_EMB] = z
    deltas = np.abs(np.linspace(math.log(HY_TARGET) / HY_SLOW_DECAY, math.log(HY_TARGET) / HY_FAST_DECAY, HY_WIDTH))
    return out, deltas.astype(np.float32)[None]


def _filter_kernel(z_ref, w1_ref, b1_ref, w2_ref, b2_ref, w3_ref, b3_ref, fr_ref, w4f_ref, w4b_ref, dl_ref,
                   hs_ref, hd_ref, pn_ref):
    z = z_ref[...]
    fr = fr_ref[...]
    h = jnp.sin(fr * (_dot3(z, w1_ref[...]) + b1_ref[...]))
    h = jnp.sin(fr * (_dot3(h, w2_ref[...]) + b2_ref[...]))
    h = jnp.sin(fr * (_dot3(h, w3_ref[...]) + b3_ref[...]))
    window = jnp.exp(-z[:, 0:1] * dl_ref[...])
    h_f = _dot3(h, w4f_ref[...]) * window
    h_b = _dot3(h, w4b_ref[...]) * window
    pos = lax.broadcasted_iota(jnp.int32, (z.shape[0], 1), 0)
    h_b = jnp.where(pos == 0, 0.0, h_b)
    hs = h_f + h_b
    hs_ref[...] = hs.astype(BF16)
    hd_ref[...] = (h_b - h_f).astype(BF16)
    sign = jnp.where(pos % 2 == 0, 1.0, -1.0)
    pn_ref[...] = jnp.sum(hs * sign, axis=0, keepdims=True)


def _pad2(a, rows, cols):
    return jnp.zeros((rows, cols), F32).at[:a.shape[0], :a.shape[1]].set(a.astype(F32))


def _hyena_filter(seq, w1, b1, w2, b2, w3, b3, w4, freq):
    z_np, deltas_np = _filter_features(seq)
    p = MLP_PAD
    args = [jnp.asarray(z_np), _pad2(w1, p, p), _pad2(b1[None], 1, p), _pad2(w2, p, p), _pad2(b2[None], 1, p),
            _pad2(w3, p, p), _pad2(b3[None], 1, p), _pad2(freq[None], 1, p),
            _pad2(w4[:, :HY_WIDTH], p, HY_WIDTH), _pad2(w4[:, HY_WIDTH:], p, HY_WIDTH), jnp.asarray(deltas_np)]
    full = lambda shape: pl.BlockSpec(shape, lambda j: (0, 0))
    colblk = lambda rows: pl.BlockSpec((rows, HY_CW), lambda j: (0, j))
    in_specs = [full((seq, p)), full((p, p)), full((1, p)), full((p, p)), full((1, p)), full((p, p)), full((1, p)),
                full((1, p)), colblk(p), colblk(p), colblk(1)]
    return pl.pallas_call(
        _filter_kernel,
        grid=(HY_WIDTH // HY_CW,),
        in_specs=in_specs,
        out_specs=[colblk(seq), colblk(seq), colblk(1)],
        out_shape=[jax.ShapeDtypeStruct((seq, HY_WIDTH), BF16), jax.ShapeDtypeStruct((seq, HY_WIDTH), BF16),
                   jax.ShapeDtypeStruct((1, HY_WIDTH), F32)],
        compiler_params=_params("parallel"),
        name="hy_filter",
    )(*args)


def _spectrum_kernel(f_ref, hs_ref, hd_ref, pn_ref, bias_ref, p_ref, q_ref):
    seq = hs_ref.shape[0]
    bias = bias_ref[...]
    for r0 in range(0, seq, HY_ROW_CHUNK):
        r1 = r0 + HY_ROW_CHUNK
        p_ref[r0:r1, :] = _dot(f_ref[r0:r1, :], hs_ref[...]) + bias
        q = _dot(f_ref[seq + r0:seq + r1, :], hd_ref[...])
        if r0 == 0:
            pos = lax.broadcasted_iota(jnp.int32, (HY_ROW_CHUNK, 1), 0)
            q = jnp.where(pos == 0, pn_ref[...] + bias, q)
        q_ref[r0:r1, :] = q


def _hyena_spectrum(f_tab, hs, hd, pn, bias):
    seq = hs.shape[0]
    colblk = lambda rows: pl.BlockSpec((rows, HY_CW), lambda j: (0, j))
    return pl.pallas_call(
        _spectrum_kernel,
        grid=(HY_WIDTH // HY_CW,),
        in_specs=[_resident(f_tab.shape), colblk(seq), colblk(seq), colblk(1), colblk(1)],
        out_specs=[colblk(seq), colblk(seq)],
        out_shape=[jax.ShapeDtypeStruct((seq, HY_WIDTH), F32)] * 2,
        compiler_params=_params("parallel"),
        name="hy_spectrum",
    )(f_tab, hs, hd, pn, bias.reshape(1, -1))


def _short_conv(u, w, b):
    n = u.shape[0]
    pos = lax.broadcasted_iota(jnp.int32, (n, 1), 0) % GRID_W
    up = jnp.where(pos == 0, 0.0, pltpu.roll(u, 1, 0))
    dn = jnp.where(pos == GRID_W - 1, 0.0, pltpu.roll(u, n - 1, 0))
    return up * w[0:1] + u * w[1:2] + dn * w[2:3] + b


def _hy_fwd_kernel(f_ref, x1_ref, v_ref, w1_ref, b1_ref, wv_ref, bv_ref, p_ref, q_ref, s_ref):
    seq = x1_ref.shape[1]
    n = 2 * seq
    x1 = _short_conv(x1_ref[0].astype(F32), w1_ref[...], b1_ref[...])
    v = _short_conv(v_ref[0].astype(F32), wv_ref[...], bv_ref[...])
    u = (v * x1).astype(BF16)
    for r0 in range(0, seq, HY_ROW_CHUNK):
        r1 = r0 + HY_ROW_CHUNK
        a = _dot(f_ref[r0:r1, :], u)
        bm = _dot(f_ref[seq + r0:seq + r1, :], u)
        p = p_ref[r0:r1, :]
        q = q_ref[r0:r1, :]
        sr = (a * p + bm * q) * (2.0 / n)
        si = (bm * p - a * q) * (2.0 / n)
        if r0 == 0:
            pos = lax.broadcasted_iota(jnp.int32, (HY_ROW_CHUNK, 1), 0)
            sr = jnp.where(pos == 0, a * p * (1.0 / n), sr)
            si = jnp.where(pos == 0, bm * q * (1.0 / n), si)
        s_ref[0, r0:r1, :] = sr.astype(BF16)
        s_ref[0, seq + r0:seq + r1, :] = si.astype(BF16)


def _hy_inv_kernel(g_ref, s_ref, x0_ref, w0_ref, b0_ref, o_ref):
    seq = x0_ref.shape[1]
    x0 = _short_conv(x0_ref[0].astype(F32), w0_ref[...], b0_ref[...])
    s = s_ref[0]
    for r0 in range(0, seq, HY_ROW_CHUNK):
        r1 = r0 + HY_ROW_CHUNK
        y = _dot(g_ref[r0:r1, :], s)
        o_ref[0, r0:r1, :] = (x0[r0:r1] * y).astype(o_ref.dtype)


def _hyena_conv(hy, conv_w, conv_b, f_tab, g_tab, p_spec, q_spec):
    bsz, seq, _ = hy.shape
    nc = HY_WIDTH // HY_CW
    conv_b = conv_b.reshape(1, -1)
    sig = lambda part: pl.BlockSpec((1, seq, HY_CW), lambda j, b: (b, 0, part * nc + j))
    cw = lambda part: pl.BlockSpec((3, HY_CW), lambda j, b: (0, part * nc + j))
    cb = lambda part: pl.BlockSpec((1, HY_CW), lambda j, b: (0, part * nc + j))
    spec = pl.BlockSpec((seq, HY_CW), lambda j, b: (0, j))
    s = pl.pallas_call(
        _hy_fwd_kernel,
        grid=(nc, bsz),
        in_specs=[_resident(f_tab.shape), sig(1), sig(2), cw(1), cb(1), cw(2), cb(2), spec, spec],
        out_specs=pl.BlockSpec((1, 2 * seq, HY_CW), lambda j, b: (b, 0, j)),
        out_shape=jax.ShapeDtypeStruct((bsz, 2 * seq, HY_WIDTH), BF16),
        compiler_params=_params("parallel", "parallel"),
        name="hy_fwd",
    )(f_tab, hy, hy, conv_w, conv_b, conv_w, conv_b, p_spec, q_spec)
    return pl.pallas_call(
        _hy_inv_kernel,
        grid=(nc, bsz),
        in_specs=[_resident(g_tab.shape), pl.BlockSpec((1, 2 * seq, HY_CW), lambda j, b: (b, 0, j)),
                  sig(0), cw(0), cb(0)],
        out_specs=pl.BlockSpec((1, seq, HY_CW), lambda j, b: (b, 0, j)),
        out_shape=jax.ShapeDtypeStruct((bsz, seq, HY_WIDTH), BF16),
        compiler_params=_params("parallel", "parallel"),
        name="hy_inv",
    )(g_tab, s, hy, conv_w, conv_b)


POST_TILE = 512


def _post_kernel(of_ref, ob_ref, og_ref, yh_ref, ma_ref, mb_ref, x_ref, g1_ref, sc2_ref, sh2_ref,
                 gn_ref, nf_ref, wa_ref, wb_ref, wo_ref, wr_ref, br_ref,
                 x1_ref, h2_ref, gates_ref):
    o = of_ref[0].astype(F32) + ob_ref[0].astype(F32)
    gn = gn_ref[...]
    heads = [_rms(o[:, h * GLA_HEAD_V:(h + 1) * GLA_HEAD_V], gn) for h in range(GLA_HEADS)]
    og = og_ref[0].astype(F32)
    a_in = jnp.concatenate(heads, axis=-1) * (og * _sigmoid(og))
    branch_a = _dot(a_in.astype(BF16), wa_ref[...])
    branch_b = _dot(yh_ref[0], wb_ref[...])
    y = _sigmoid(ma_ref[0].astype(F32)) * branch_a + _sigmoid(mb_ref[0].astype(F32)) * branch_b
    x1 = x_ref[0] + g1_ref[0] * _dot(y.astype(BF16), wo_ref[...])
    x1_ref[0] = x1
    h2 = _rms(x1, nf_ref[...]) * (1.0 + sc2_ref[0]) + sh2_ref[0]
    h2_ref[0] = h2.astype(BF16)

    logits = _dot3(h2, wr_ref[...]) + br_ref[...]
    lane = lax.broadcasted_iota(jnp.int32, logits.shape, 1)
    work = logits
    tops, hots = [], []
    for _ in range(TOP_K):
        m = jnp.max(work, axis=-1, keepdims=True)
        first = jnp.min(jnp.where(work == m, lane, N_EXPERTS), axis=-1, keepdims=True)
        hot = lane == first
        tops.append(m)
        hots.append(hot)
        work = jnp.where(hot, -jnp.inf, work)
    es = [jnp.exp(m - tops[0]) for m in tops]
    inv = 1.0 / functools.reduce(lambda a, b: a + b, es)
    gates = jnp.zeros_like(logits)
    for hot, e in zip(hots, es):
        gates = gates + jnp.where(hot, e * inv, 0.0)
    gates_ref[0] = gates


def _post(o_f, o_b, og, y_h, m_a, m_b, x, mods, gla_norm, norm_ffn, w_a, w_b, w_o, w_router, b_router):
    bsz, seq, _ = x.shape
    tok = lambda n: pl.BlockSpec((1, POST_TILE, n), lambda b, i: (b, i, 0))
    mod = lambda which: pl.BlockSpec((1, 1, D_MODEL), lambda b, i: (b, 0, which))
    in_specs = [tok(D_MODEL)] * 7 + [mod(MOD_G1), mod(MOD_SC2), mod(MOD_SH2),
                                     _resident((1, GLA_HEAD_V)), _resident((1, D_MODEL)),
                                     _resident(w_a.shape), _resident(w_b.shape), _resident(w_o.shape),
                                     _resident(w_router.shape), _resident((1, N_EXPERTS))]
    return pl.pallas_call(
        _post_kernel,
        grid=(bsz, seq // POST_TILE),
        in_specs=in_specs,
        out_specs=[tok(D_MODEL), tok(D_MODEL), tok(N_EXPERTS)],
        out_shape=[jax.ShapeDtypeStruct((bsz, seq, D_MODEL), F32), jax.ShapeDtypeStruct((bsz, seq, D_MODEL), BF16),
                   jax.ShapeDtypeStruct((bsz, seq, N_EXPERTS), F32)],
        compiler_params=_params("parallel", "parallel"),
        name="post",
    )(o_f, o_b, og, y_h, m_a, m_b, x, mods, mods, mods, gla_norm.reshape(1, -1), norm_ffn.reshape(1, -1),
      w_a, w_b, w_o, w_router, b_router.reshape(1, -1))


MOE_TILE = 512


def _moe_kernel(h_ref, gates_ref, wg_ref, bg_ref, wl_ref, bl_ref, wd_ref, bd_ref, x1_ref, g2_ref, nf_ref,
                o_ref, acc_ref):
    e = pl.program_id(2)

    @pl.when(e == 0)
    def _():
        acc_ref[...] = jnp.zeros_like(acc_ref)

    h = h_ref[0]
    glu = jnp.minimum(_dot(h, wg_ref[0]) + bg_ref[0], SWIGLU_LIMIT)
    lin = jnp.clip(_dot(h, wl_ref[0]) + bl_ref[0], -SWIGLU_LIMIT, SWIGLU_LIMIT)
    a = glu * _sigmoid(SWIGLU_ALPHA * glu) * (lin + 1.0)
    y = _dot(a.astype(BF16), wd_ref[0]) + bd_ref[0]
    gates = gates_ref[0]
    lane = lax.broadcasted_iota(jnp.int32, gates.shape, 1)
    gate = jnp.sum(jnp.where(lane == e, gates, 0.0), axis=-1, keepdims=True)
    acc_ref[...] += gate * y

    @pl.when(e == N_EXPERTS - 1)
    def _():
        x2 = x1_ref[0] + g2_ref[0] * acc_ref[...]
        o_ref[0] = _rms(x2, nf_ref[...])


def _moe(h2, gates, w_glu, b_glu, w_lin, b_lin, w_down, b_down, x1, mods, norm_final):
    bsz, seq, _ = x1.shape
    tok = lambda n: pl.BlockSpec((1, MOE_TILE, n), lambda b, i, e: (b, i, 0))
    wblk = pl.BlockSpec((1, D_MODEL, D_MODEL), lambda b, i, e: (e, 0, 0))
    bblk = pl.BlockSpec((1, 1, D_MODEL), lambda b, i, e: (e, 0, 0))
    return pl.pallas_call(
        _moe_kernel,
        grid=(bsz, seq // MOE_TILE, N_EXPERTS),
        in_specs=[tok(D_MODEL), tok(N_EXPERTS), wblk, bblk, wblk, bblk, wblk, bblk, tok(D_MODEL),
                  pl.BlockSpec((1, 1, D_MODEL), lambda b, i, e: (b, 0, MOD_G2)), _resident((1, D_MODEL))],
        out_specs=tok(D_MODEL),
        out_shape=jax.ShapeDtypeStruct((bsz, seq, D_MODEL), F32),
        scratch_shapes=[pltpu.VMEM((MOE_TILE, D_MODEL), F32)],
        compiler_params=_params("parallel", "parallel", "arbitrary"),
        name="moe",
    )(h2, gates, w_glu, b_glu, w_lin, b_lin, w_down, b_down, x1, mods, norm_final.reshape(1, -1))


def kernel(x, c, ctx, c_ctx, w_ada, b_ada, norm_mix, norm_ffn, w_in, w_gk_f, b_gk_f, w_gk_b, b_gk_b, gla_norm, w_gla_out, hy_conv_w, hy_conv_b, hy_f_w1, hy_f_b1, hy_f_w2, hy_f_b2, hy_f_w3, hy_f_b3, hy_f_w4, hy_sin_freq, hy_bias, w_hy_out, w_out, w_router, b_router, w_up, b_up, w_down, b_down, norm_final):
    depth = w_ada.shape[0]
    assert depth == 1, "single-layer block: the context stream only feeds later layers"
    bsz, seq, _ = x.shape

    n_rows = -(-(bsz + 1) // 8) * 8
    cc = jnp.zeros((n_rows, D_MODEL), F32).at[:bsz].set(c).at[bsz].set(c_ctx)
    mods = _ada(cc, w_ada[0], b_ada[0]).reshape(n_rows, 1, N_MOD * D_MODEL)

    sizes = (GLA_DK, GLA_DK, GLA_DV, GLA_DV, GLA_GATE_RANK, GLA_GATE_RANK, 3 * HY_WIDTH, D_MODEL, D_MODEL)
    offs = np.concatenate([[0], np.cumsum(sizes)])
    w_in_b = w_in[0].astype(BF16)
    wq, wk, wv, wog, wrf, wrb, why, wma, wmb = [w_in_b[:, offs[i]:offs[i + 1]] for i in range(len(sizes))]
    q, k, v, og, r_f, r_b, hy, m_a, m_b = _inproj(
        x, mods, lambda b: b, norm_mix[0], [wq, wk, wv, wog, wrf, wrb, why, wma, wmb],
        [BF16, BF16, BF16, BF16, F32, F32, BF16, BF16, BF16], tl=512)
    k_c, v_c, rc_f, rc_b = _inproj(
        ctx, mods, lambda b: bsz, norm_mix[0], [wk, wv, wrf, wrb], [BF16, BF16, F32, F32], tl=ctx.shape[1])

    w_gk = jnp.stack([w_gk_f[0], w_gk_b[0]])
    b_gk = jnp.stack([b_gk_f[0], b_gk_b[0]])[:, None, :]
    o_f, o_b = _gla(q, k, v, r_f, r_b, k_c, v_c, rc_f, rc_b, w_gk, b_gk)

    f_tab = jnp.asarray(_dft_table(seq)).astype(BF16)
    g_tab = f_tab.T
    hs, hd, pn = _hyena_filter(seq, hy_f_w1[0], hy_f_b1[0], hy_f_w2[0], hy_f_b2[0], hy_f_w3[0], hy_f_b3[0],
                               hy_f_w4[0], hy_sin_freq[0])
    p_spec, q_spec = _hyena_spectrum(f_tab, hs, hd, pn, hy_bias[0])
    y_h = _hyena_conv(hy, hy_conv_w[0], hy_conv_b[0], f_tab, g_tab, p_spec, q_spec)

    x1, h2, gates = _post(o_f, o_b, og, y_h, m_a, m_b, x, mods, gla_norm[0], norm_ffn[0],
                          w_gla_out[0].astype(BF16), w_hy_out[0].astype(BF16), w_out[0].astype(BF16),
                          w_router[0], b_router[0])

    w_up_b = w_up[0].astype(BF16)
    return _moe(h2, gates, w_up_b[:, :, 0::2], b_up[0][:, None, 0::2], w_up_b[:, :, 1::2], b_up[0][:, None, 1::2],
                w_down[0].astype(BF16), b_down[0][:, None, :], x1, mods, norm_final)
```

```python
import functools
import math

import jax
import jax.numpy as jnp
import numpy as np
from jax import lax
from jax.experimental import pallas as pl
from jax.experimental.pallas import tpu as pltpu

F32 = jnp.float32
BF16 = jnp.bfloat16

D_MODEL = 1024
GRID_W = 64
EPS = 1e-6
N_MOD = 6

GLA_HEADS = 4
GLA_HEAD_K = 128
GLA_HEAD_V = 256
GLA_DK = GLA_HEADS * GLA_HEAD_K
GLA_DV = GLA_HEADS * GLA_HEAD_V
GLA_GATE_RANK = 16
GLA_GATE_NORM = 16.0
GLA_CHUNK = 64

HY_WIDTH = D_MODEL
HY_EMB = 33
HY_FAST_DECAY = 0.3
HY_SLOW_DECAY = 1.5
HY_TARGET = 1e-2

N_EXPERTS = 32
TOP_K = 4
SWIGLU_LIMIT = 7.0
SWIGLU_ALPHA = 1.702

V7X_VMEM_BYTES = 64 * 1024 * 1024
VMEM_LIMIT = V7X_VMEM_BYTES - 8 * 1024 * 1024
LANES = 128

MOD_SH1, MOD_SC1, MOD_G1, MOD_SH2, MOD_SC2, MOD_G2 = range(N_MOD)


def _params(*sem):
    return pltpu.CompilerParams(dimension_semantics=sem, vmem_limit_bytes=VMEM_LIMIT)


def _resident(shape):
    return pl.BlockSpec(shape, lambda *_: (0,) * len(shape), pipeline_mode=pl.Buffered(1))


def _dot(a, b):
    return jnp.dot(a, b, preferred_element_type=F32)


def _dot_nt(a, b):
    return lax.dot_general(a, b, (((1,), (1,)), ((), ())), preferred_element_type=F32)


def _dot_tn(a, b):
    return lax.dot_general(a, b, (((0,), (0,)), ((), ())), preferred_element_type=F32)


def _split(a):
    hi = a.astype(BF16)
    lo = (a - hi.astype(F32)).astype(BF16)
    return hi, lo


def _dot3(a, b):
    ah, al = _split(a)
    bh, bl = _split(b)
    return _dot(ah, bh) + (_dot(ah, bl) + _dot(al, bh))


def _sigmoid(x):
    return 1.0 / (1.0 + jnp.exp(-x))


def _log_sigmoid(x):
    return jnp.minimum(x, 0.0) - jnp.log(1.0 + jnp.exp(-jnp.abs(x)))


def _rms(x, w):
    return x * lax.rsqrt(jnp.mean(x * x, axis=-1, keepdims=True) + EPS) * w


def _ada_kernel(c_ref, w_ref, b_ref, o_ref):
    c = c_ref[...]
    o_ref[...] = _dot3(c * _sigmoid(c), w_ref[...]) + b_ref[...]


def _ada(cc, w_ada, b_ada):
    rows = cc.shape[0]
    return pl.pallas_call(
        _ada_kernel,
        grid=(N_MOD,),
        in_specs=[pl.BlockSpec((rows, D_MODEL), lambda j: (0, 0)),
                  pl.BlockSpec((D_MODEL, D_MODEL), lambda j: (0, j)),
                  pl.BlockSpec((1, D_MODEL), lambda j: (0, j))],
        out_specs=pl.BlockSpec((rows, D_MODEL), lambda j: (0, j)),
        out_shape=jax.ShapeDtypeStruct((rows, N_MOD * D_MODEL), F32),
        compiler_params=_params("parallel"),
        name="ada",
    )(cc, w_ada, b_ada.reshape(1, -1))


INPROJ_COL_CHUNK = 512


def _inproj_kernel(n_out, x_ref, sc_ref, sh_ref, nw_ref, *refs):
    w_refs, o_refs = refs[:n_out], refs[n_out:]
    h = _rms(x_ref[0], nw_ref[...]) * (1.0 + sc_ref[0]) + sh_ref[0]
    hb = h.astype(BF16)
    for w_ref, o_ref in zip(w_refs, o_refs):
        n = w_ref.shape[1]
        for c0 in range(0, n, INPROJ_COL_CHUNK):
            c1 = min(c0 + INPROJ_COL_CHUNK, n)
            o_ref[0, :, c0:c1] = _dot(hb, w_ref[:, c0:c1]).astype(o_ref.dtype)


def _inproj(x, mods, mod_row, norm_w, weights, out_dtypes, tl):
    bsz, seq, _ = x.shape
    n_out = len(weights)
    in_specs = [pl.BlockSpec((1, tl, D_MODEL), lambda b, i: (b, i, 0)),
                pl.BlockSpec((1, 1, D_MODEL), lambda b, i: (mod_row(b), 0, MOD_SC1)),
                pl.BlockSpec((1, 1, D_MODEL), lambda b, i: (mod_row(b), 0, MOD_SH1)),
                _resident((1, D_MODEL))]
    in_specs += [_resident(w.shape) for w in weights]
    out_specs = [pl.BlockSpec((1, tl, w.shape[1]), lambda b, i: (b, i, 0)) for w in weights]
    out_shape = [jax.ShapeDtypeStruct((bsz, seq, w.shape[1]), dt) for w, dt in zip(weights, out_dtypes)]
    return pl.pallas_call(
        functools.partial(_inproj_kernel, n_out),
        grid=(bsz, seq // tl),
        in_specs=in_specs, out_specs=out_specs, out_shape=out_shape,
        compiler_params=_params("parallel", "parallel"),
        name="inproj",
    )(x, mods, mods, norm_w.reshape(1, -1), *weights)


GLA_TILE = 256
GLA_NCH = GLA_TILE // GLA_CHUNK
GLA_SCALE = GLA_HEAD_K ** -0.5


def _gla_kernel(qf_ref, kf_ref, vf_ref, rf_ref, qb_ref, kb_ref, vb_ref, rb_ref,
                kc_ref, vc_ref, rcf_ref, rcb_ref, wgk_ref, bgk_ref,
                of_ref, ob_ref, sf_ref, sb_ref):
    t = pl.program_id(2)
    row = lax.broadcasted_iota(jnp.int32, (GLA_TILE, GLA_TILE), 0)
    col = lax.broadcasted_iota(jnp.int32, (GLA_TILE, GLA_TILE), 1)
    same = (row // GLA_CHUNK) == (col // GLA_CHUNK)
    tri_f = jnp.where(same & (col <= row), 1.0, 0.0).astype(BF16)
    tri_b = jnp.where(same & (col >= row), 1.0, 0.0).astype(BF16)
    crow = lax.broadcasted_iota(jnp.int32, (GLA_CHUNK, GLA_CHUNK), 0)
    ccol = lax.broadcasted_iota(jnp.int32, (GLA_CHUNK, GLA_CHUNK), 1)
    mask_f = ccol <= crow
    mask_b = ccol >= crow

    def cum_decay(r, d, tri):
        z = _dot3(r, wgk_ref[d]) + bgk_ref[d]
        g = _log_sigmoid(z) * (1.0 / GLA_GATE_NORM)
        gh, gl = _split(g)
        return _dot(tri, gh) + _dot(tri, gl)

    def chunks(fwd):
        return range(GLA_NCH) if fwd else range(GLA_NCH - 1, -1, -1)

    def rows(c):
        return slice(c * GLA_CHUNK, (c + 1) * GLA_CHUNK)

    def last_row(b, c, fwd):
        i = c * GLA_CHUNK + (GLA_CHUNK - 1 if fwd else 0)
        return b[i:i + 1, :]

    def state_step(st, k, v, b, b_last):
        k_upd = (k * jnp.exp(b_last - b)).astype(BF16)
        return st * jnp.exp(b_last) + _dot_tn(v, k_upd)

    def scan_state(st, k_ref, v_ref, b, fwd):
        for c in chunks(fwd):
            k = k_ref[0, rows(c), :].astype(F32)
            st = state_step(st, k, v_ref[0, rows(c), :], b[rows(c)], last_row(b, c, fwd))
        return st

    def scan_out(st, q_ref, k_ref, v_ref, o_ref, b, fwd):
        mask = mask_f if fwd else mask_b
        for c in chunks(fwd):
            bc = b[rows(c)]
            q = q_ref[0, rows(c), :].astype(F32)
            k = k_ref[0, rows(c), :].astype(F32)
            v = v_ref[0, rows(c), :]
            q_dec = (q * jnp.exp(bc) * GLA_SCALE).astype(BF16)
            k_inv = (k * jnp.exp(-bc)).astype(BF16)
            att = jnp.where(mask, _dot_nt(q_dec, k_inv), 0.0).astype(BF16)
            o = _dot(att, v) + _dot_nt(q_dec, st.astype(BF16))
            o_ref[0, rows(c), :] = o.astype(o_ref.dtype)
            st = state_step(st, k, v, bc, last_row(b, c, fwd))
        return st

    @pl.when(t == 0)
    def _():
        zero = jnp.zeros((GLA_HEAD_V, GLA_HEAD_K), F32)
        sf_ref[...] = scan_state(zero, kc_ref, vc_ref, cum_decay(rcf_ref[0], 0, tri_f), True)
        sb_ref[...] = scan_state(zero, kc_ref, vc_ref, cum_decay(rcb_ref[0], 1, tri_b), False)

    sf_ref[...] = scan_out(sf_ref[...], qf_ref, kf_ref, vf_ref, of_ref, cum_decay(rf_ref[0], 0, tri_f), True)
    sb_ref[...] = scan_out(sb_ref[...], qb_ref, kb_ref, vb_ref, ob_ref, cum_decay(rb_ref[0], 1, tri_b), False)


def _gla(q, k, v, r_f, r_b, k_c, v_c, rc_f, rc_b, w_gk, b_gk):
    bsz, seq, _ = q.shape
    ctx_len = k_c.shape[1]
    assert ctx_len == GLA_TILE
    nt = seq // GLA_TILE
    fwd = lambda b, h, t: (b, t, h)
    bwd = lambda b, h, t: (b, nt - 1 - t, h)
    fwd0 = lambda b, h, t: (b, t, 0)
    bwd0 = lambda b, h, t: (b, nt - 1 - t, 0)
    ctx = lambda b, h, t: (b, 0, h)
    ctx0 = lambda b, h, t: (b, 0, 0)
    kblk = (1, GLA_TILE, GLA_HEAD_K)
    vblk = (1, GLA_TILE, GLA_HEAD_V)
    rblk = (1, GLA_TILE, GLA_GATE_RANK)
    in_specs = [pl.BlockSpec(kblk, fwd), pl.BlockSpec(kblk, fwd), pl.BlockSpec(vblk, fwd), pl.BlockSpec(rblk, fwd0),
                pl.BlockSpec(kblk, bwd), pl.BlockSpec(kblk, bwd), pl.BlockSpec(vblk, bwd), pl.BlockSpec(rblk, bwd0),
                pl.BlockSpec(kblk, ctx), pl.BlockSpec(vblk, ctx), pl.BlockSpec(rblk, ctx0), pl.BlockSpec(rblk, ctx0),
                pl.BlockSpec((2, GLA_GATE_RANK, GLA_HEAD_K), lambda b, h, t: (0, 0, h)),
                pl.BlockSpec((2, 1, GLA_HEAD_K), lambda b, h, t: (0, 0, h))]
    out_specs = [pl.BlockSpec(vblk, fwd), pl.BlockSpec(vblk, bwd)]
    out_shape = [jax.ShapeDtypeStruct((bsz, seq, GLA_DV), BF16)] * 2
    return pl.pallas_call(
        _gla_kernel,
        grid=(bsz, GLA_HEADS, nt),
        in_specs=in_specs, out_specs=out_specs, out_shape=out_shape,
        scratch_shapes=[pltpu.VMEM((GLA_HEAD_V, GLA_HEAD_K), F32)] * 2,
        compiler_params=_params("parallel", "parallel", "arbitrary"),
        name="gla",
    )(q, k, v, r_f, q, k, v, r_b, k_c, v_c, rc_f, rc_b, w_gk, b_gk)


HY_CW = 256
HY_ROW_CHUNK = 512
MLP_PAD = 128


@functools.lru_cache(maxsize=None)
def _dft_table(seq):
    n = 2 * seq
    ph = (np.arange(seq)[:, None] * np.arange(seq)[None, :]) % n
    ang = 2.0 * np.pi * np.arange(n) / n
    c = np.cos(ang)[ph]
    s = np.sin(ang)[ph]
    s[0, :] = np.where(np.arange(seq) % 2 == 0, 1.0, -1.0)
    return np.concatenate([c, s], axis=0).astype(np.float32)


@functools.lru_cache(maxsize=None)
def _filter_features(seq):
    bands = (HY_EMB - 1) // 2
    pos = np.arange(seq, dtype=np.float64)[:, None]
    t = pos / max(seq - 1, 1)
    f = np.linspace(1e-4, bands - 1, bands)[None]
    ang = (2.0 * math.pi / seq) * pos * f
    z = np.concatenate([t, np.cos(ang), -np.sin(ang)], axis=-1)
    out = np.zeros((seq, MLP_PAD), np.float32)
    out[:, :HY_EMB] = z
    deltas = np.abs(np.linspace(math.log(HY_TARGET) / HY_SLOW_DECAY, math.log(HY_TARGET) / HY_FAST_DECAY, HY_WIDTH))
    return out, deltas.astype(np.float32)[None]


def _filter_kernel(z_ref, w1_ref, b1_ref, w2_ref, b2_ref, w3_ref, b3_ref, fr_ref, w4f_ref, w4b_ref, dl_ref,
                   hs_ref, hd_ref, pn_ref):
    z = z_ref[...]
    fr = fr_ref[...]
    h = jnp.sin(fr * (_dot3(z, w1_ref[...]) + b1_ref[...]))
    h = jnp.sin(fr * (_dot3(h, w2_ref[...]) + b2_ref[...]))
    h = jnp.sin(fr * (_dot3(h, w3_ref[...]) + b3_ref[...]))
    window = jnp.exp(-z[:, 0:1] * dl_ref[...])
    h_f = _dot3(h, w4f_ref[...]) * window
    h_b = _dot3(h, w4b_ref[...]) * window
    pos = lax.broadcasted_iota(jnp.int32, (z.shape[0], 1), 0)
    h_b = jnp.where(pos == 0, 0.0, h_b)
    hs = h_f + h_b
    hs_ref[...] = hs.astype(BF16)
    hd_ref[...] = (h_b - h_f).astype(BF16)
    sign = jnp.where(pos % 2 == 0, 1.0, -1.0)
    pn_ref[...] = jnp.sum(hs * sign, axis=0, keepdims=True)


def _pad2(a, rows, cols):
    return jnp.zeros((rows, cols), F32).at[:a.shape[0], :a.shape[1]].set(a.astype(F32))


def _hyena_filter(seq, w1, b1, w2, b2, w3, b3, w4, freq):
    z_np, deltas_np = _filter_features(seq)
    p = MLP_PAD
    args = [jnp.asarray(z_np), _pad2(w1, p, p), _pad2(b1[None], 1, p), _pad2(w2, p, p), _pad2(b2[None], 1, p),
            _pad2(w3, p, p), _pad2(b3[None], 1, p), _pad2(freq[None], 1, p),
            _pad2(w4[:, :HY_WIDTH], p, HY_WIDTH), _pad2(w4[:, HY_WIDTH:], p, HY_WIDTH), jnp.asarray(deltas_np)]
    full = lambda shape: pl.BlockSpec(shape, lambda j: (0, 0))
    colblk = lambda rows: pl.BlockSpec((rows, HY_CW), lambda j: (0, j))
    in_specs = [full((seq, p)), full((p, p)), full((1, p)), full((p, p)), full((1, p)), full((p, p)), full((1, p)),
                full((1, p)), colblk(p), colblk(p), colblk(1)]
    return pl.pallas_call(
        _filter_kernel,
        grid=(HY_WIDTH // HY_CW,),
        in_specs=in_specs,
        out_specs=[colblk(seq), colblk(seq), colblk(1)],
        out_shape=[jax.ShapeDtypeStruct((seq, HY_WIDTH), BF16), jax.ShapeDtypeStruct((seq, HY_WIDTH), BF16),
                   jax.ShapeDtypeStruct((1, HY_WIDTH), F32)],
        compiler_params=_params("parallel"),
        name="hy_filter",
    )(*args)


def _spectrum_kernel(f_ref, hs_ref, hd_ref, pn_ref, bias_ref, p_ref, q_ref):
    seq = hs_ref.shape[0]
    bias = bias_ref[...]
    for r0 in range(0, seq, HY_ROW_CHUNK):
        r1 = r0 + HY_ROW_CHUNK
        p_ref[r0:r1, :] = _dot(f_ref[r0:r1, :], hs_ref[...]) + bias
        q = _dot(f_ref[seq + r0:seq + r1, :], hd_ref[...])
        if r0 == 0:
            pos = lax.broadcasted_iota(jnp.int32, (HY_ROW_CHUNK, 1), 0)
            q = jnp.where(pos == 0, pn_ref[...] + bias, q)
        q_ref[r0:r1, :] = q


def _hyena_spectrum(f_tab, hs, hd, pn, bias):
    seq = hs.shape[0]
    colblk = lambda rows: pl.BlockSpec((rows, HY_CW), lambda j: (0, j))
    return pl.pallas_call(
        _spectrum_kernel,
        grid=(HY_WIDTH // HY_CW,),
        in_specs=[_resident(f_tab.shape), colblk(seq), colblk(seq), colblk(1), colblk(1)],
        out_specs=[colblk(seq), colblk(seq)],
        out_shape=[jax.ShapeDtypeStruct((seq, HY_WIDTH), F32)] * 2,
        compiler_params=_params("parallel"),
        name="hy_spectrum",
    )(f_tab, hs, hd, pn, bias.reshape(1, -1))


def _short_conv(u, w, b):
    n = u.shape[0]
    pos = lax.broadcasted_iota(jnp.int32, (n, 1), 0) % GRID_W
    up = jnp.where(pos == 0, 0.0, pltpu.roll(u, 1, 0))
    dn = jnp.where(pos == GRID_W - 1, 0.0, pltpu.roll(u, n - 1, 0))
    return up * w[0:1] + u * w[1:2] + dn * w[2:3] + b


def _hy_fwd_kernel(f_ref, x1_ref, v_ref, w1_ref, b1_ref, wv_ref, bv_ref, p_ref, q_ref, s_ref):
    seq = x1_ref.shape[1]
    n = 2 * seq
    x1 = _short_conv(x1_ref[0].astype(F32), w1_ref[...], b1_ref[...])
    v = _short_conv(v_ref[0].astype(F32), wv_ref[...], bv_ref[...])
    u = (v * x1).astype(BF16)
    for r0 in range(0, seq, HY_ROW_CHUNK):
        r1 = r0 + HY_ROW_CHUNK
        a = _dot(f_ref[r0:r1, :], u)
        bm = _dot(f_ref[seq + r0:seq + r1, :], u)
        p = p_ref[r0:r1, :]
        q = q_ref[r0:r1, :]
        sr = (a * p + bm * q) * (2.0 / n)
        si = (bm * p - a * q) * (2.0 / n)
        if r0 == 0:
            pos = lax.broadcasted_iota(jnp.int32, (HY_ROW_CHUNK, 1), 0)
            sr = jnp.where(pos == 0, a * p * (1.0 / n), sr)
            si = jnp.where(pos == 0, bm * q * (1.0 / n), si)
        s_ref[0, r0:r1, :] = sr.astype(BF16)
        s_ref[0, seq + r0:seq + r1, :] = si.astype(BF16)


def _hy_inv_kernel(g_ref, s_ref, x0_ref, w0_ref, b0_ref, o_ref):
    seq = x0_ref.shape[1]
    x0 = _short_conv(x0_ref[0].astype(F32), w0_ref[...], b0_ref[...])
    s = s_ref[0]
    for r0 in range(0, seq, HY_ROW_CHUNK):
        r1 = r0 + HY_ROW_CHUNK
        y = _dot(g_ref[r0:r1, :], s)
        o_ref[0, r0:r1, :] = (x0[r0:r1] * y).astype(o_ref.dtype)


def _hyena_conv(hy, conv_w, conv_b, f_tab, g_tab, p_spec, q_spec):
    bsz, seq, _ = hy.shape
    nc = HY_WIDTH // HY_CW
    conv_b = conv_b.reshape(1, -1)
    sig = lambda part: pl.BlockSpec((1, seq, HY_CW), lambda j, b: (b, 0, part * nc + j))
    cw = lambda part: pl.BlockSpec((3, HY_CW), lambda j, b: (0, part * nc + j))
    cb = lambda part: pl.BlockSpec((1, HY_CW), lambda j, b: (0, part * nc + j))
    spec = pl.BlockSpec((seq, HY_CW), lambda j, b: (0, j))
    s = pl.pallas_call(
        _hy_fwd_kernel,
        grid=(nc, bsz),
        in_specs=[_resident(f_tab.shape), sig(1), sig(2), cw(1), cb(1), cw(2), cb(2), spec, spec],
        out_specs=pl.BlockSpec((1, 2 * seq, HY_CW), lambda j, b: (b, 0, j)),
        out_shape=jax.ShapeDtypeStruct((bsz, 2 * seq, HY_WIDTH), BF16),
        compiler_params=_params("parallel", "parallel"),
        name="hy_fwd",
    )(f_tab, hy, hy, conv_w, conv_b, conv_w, conv_b, p_spec, q_spec)
    return pl.pallas_call(
        _hy_inv_kernel,
        grid=(nc, bsz),
        in_specs=[_resident(g_tab.shape), pl.BlockSpec((1, 2 * seq, HY_CW), lambda j, b: (b, 0, j)),
                  sig(0), cw(0), cb(0)],
        out_specs=pl.BlockSpec((1, seq, HY_CW), lambda j, b: (b, 0, j)),
        out_shape=jax.ShapeDtypeStruct((bsz, seq, HY_WIDTH), BF16),
        compiler_params=_params("parallel", "parallel"),
        name="hy_inv",
    )(g_tab, s, hy, conv_w, conv_b)


POST_TILE = 512


def _post_kernel(of_ref, ob_ref, og_ref, yh_ref, ma_ref, mb_ref, x_ref, g1_ref, sc2_ref, sh2_ref,
                 gn_ref, nf_ref, wa_ref, wb_ref, wo_ref, wr_ref, br_ref,
                 x1_ref, h2_ref, gates_ref):
    o = of_ref[0].astype(F32) + ob_ref[0].astype(F32)
    gn = gn_ref[...]
    heads = [_rms(o[:, h * GLA_HEAD_V:(h + 1) * GLA_HEAD_V], gn) for h in range(GLA_HEADS)]
    og = og_ref[0].astype(F32)
    a_in = jnp.concatenate(heads, axis=-1) * (og * _sigmoid(og))
    branch_a = _dot(a_in.astype(BF16), wa_ref[...])
    branch_b = _dot(yh_ref[0], wb_ref[...])
    y = _sigmoid(ma_ref[0].astype(F32)) * branch_a + _sigmoid(mb_ref[0].astype(F32)) * branch_b
    x1 = x_ref[0] + g1_ref[0] * _dot(y.astype(BF16), wo_ref[...])
    x1_ref[0] = x1
    h2 = _rms(x1, nf_ref[...]) * (1.0 + sc2_ref[0]) + sh2_ref[0]
    h2_ref[0] = h2.astype(BF16)

    logits = _dot3(h2, wr_ref[...]) + br_ref[...]
    lane = lax.broadcasted_iota(jnp.int32, logits.shape, 1)
    work = logits
    tops, hots = [], []
    for _ in range(TOP_K):
        m = jnp.max(work, axis=-1, keepdims=True)
        first = jnp.min(jnp.where(work == m, lane, N_EXPERTS), axis=-1, keepdims=True)
        hot = lane == first
        tops.append(m)
        hots.append(hot)
        work = jnp.where(hot, -jnp.inf, work)
    es = [jnp.exp(m - tops[0]) for m in tops]
    inv = 1.0 / functools.reduce(lambda a, b: a + b, es)
    gates = jnp.zeros_like(logits)
    for hot, e in zip(hots, es):
        gates = gates + jnp.where(hot, e * inv, 0.0)
    gates_ref[0] = gates


def _post(o_f, o_b, og, y_h, m_a, m_b, x, mods, gla_norm, norm_ffn, w_a, w_b, w_o, w_router, b_router):
    bsz, seq, _ = x.shape
    tok = lambda n: pl.BlockSpec((1, POST_TILE, n), lambda b, i: (b, i, 0))
    mod = lambda which: pl.BlockSpec((1, 1, D_MODEL), lambda b, i: (b, 0, which))
    in_specs = [tok(D_MODEL)] * 7 + [mod(MOD_G1), mod(MOD_SC2), mod(MOD_SH2),
                                     _resident((1, GLA_HEAD_V)), _resident((1, D_MODEL)),
                                     _resident(w_a.shape), _resident(w_b.shape), _resident(w_o.shape),
                                     _resident(w_router.shape), _resident((1, N_EXPERTS))]
    return pl.pallas_call(
        _post_kernel,
        grid=(bsz, seq // POST_TILE),
        in_specs=in_specs,
        out_specs=[tok(D_MODEL), tok(D_MODEL), tok(N_EXPERTS)],
        out_shape=[jax.ShapeDtypeStruct((bsz, seq, D_MODEL), F32), jax.ShapeDtypeStruct((bsz, seq, D_MODEL), BF16),
                   jax.ShapeDtypeStruct((bsz, seq, N_EXPERTS), F32)],
        compiler_params=_params("parallel", "parallel"),
        name="post",
    )(o_f, o_b, og, y_h, m_a, m_b, x, mods, mods, mods, gla_norm.reshape(1, -1), norm_ffn.reshape(1, -1),
      w_a, w_b, w_o, w_router, b_router.reshape(1, -1))


MOE_TILE = 512
UP_GROUP = 2 * LANES
N_UP_GROUPS = 2 * D_MODEL // UP_GROUP


def _deinterleave_kernel(w_ref, o_ref):
    r = lax.broadcasted_iota(jnp.int32, (UP_GROUP, UP_GROUP), 0)
    c = lax.broadcasted_iota(jnp.int32, (UP_GROUP, UP_GROUP), 1)
    perm = jnp.where(c == (r % 2) * LANES + r // 2, 1.0, 0.0).astype(BF16)
    for g in range(w_ref.shape[2] // UP_GROUP):
        cols = slice(g * UP_GROUP, (g + 1) * UP_GROUP)
        o_ref[0, :, cols] = _dot(w_ref[0, :, cols].astype(BF16), perm).astype(BF16)


def _deinterleave_up(w_up):
    n_exp, d_in, d_up = w_up.shape
    blk = pl.BlockSpec((1, d_in, d_up // 2), lambda e, j: (e, 0, j))
    return pl.pallas_call(
        _deinterleave_kernel,
        grid=(n_exp, 2),
        in_specs=[blk], out_specs=blk,
        out_shape=jax.ShapeDtypeStruct(w_up.shape, BF16),
        compiler_params=_params("parallel", "parallel"),
        name="deinterleave",
    )(w_up)


def _moe_kernel(h_ref, gates_ref, wu_ref, bu_ref, wd_ref, bd_ref, x1_ref, g2_ref, nf_ref, o_ref, acc_ref):
    e = pl.program_id(2)

    @pl.when(e == 0)
    def _():
        acc_ref[...] = jnp.zeros_like(acc_ref)

    h = h_ref[0]
    acts = []
    for g in range(N_UP_GROUPS):
        cols = slice(g * UP_GROUP, (g + 1) * UP_GROUP)
        u = _dot(h, wu_ref[0, :, cols]) + bu_ref[0, :, cols]
        glu = jnp.minimum(u[:, :LANES], SWIGLU_LIMIT)
        lin = jnp.clip(u[:, LANES:], -SWIGLU_LIMIT, SWIGLU_LIMIT)
        acts.append((glu * _sigmoid(SWIGLU_ALPHA * glu) * (lin + 1.0)).astype(BF16))
    y = _dot(jnp.concatenate(acts, axis=-1), wd_ref[0]) + bd_ref[0]
    gates = gates_ref[0]
    lane = lax.broadcasted_iota(jnp.int32, gates.shape, 1)
    gate = jnp.sum(jnp.where(lane == e, gates, 0.0), axis=-1, keepdims=True)
    acc_ref[...] += gate * y

    @pl.when(e == N_EXPERTS - 1)
    def _():
        x2 = x1_ref[0] + g2_ref[0] * acc_ref[...]
        o_ref[0] = _rms(x2, nf_ref[...])


def _moe(h2, gates, w_up_g, b_up_g, w_down, b_down, x1, mods, norm_final):
    bsz, seq, _ = x1.shape
    tok = lambda n: pl.BlockSpec((1, MOE_TILE, n), lambda b, i, e: (b, i, 0))
    per_expert = lambda r, c: pl.BlockSpec((1, r, c), lambda b, i, e: (e, 0, 0))
    return pl.pallas_call(
        _moe_kernel,
        grid=(bsz, seq // MOE_TILE, N_EXPERTS),
        in_specs=[tok(D_MODEL), tok(N_EXPERTS), per_expert(D_MODEL, 2 * D_MODEL), per_expert(1, 2 * D_MODEL),
                  per_expert(D_MODEL, D_MODEL), per_expert(1, D_MODEL), tok(D_MODEL),
                  pl.BlockSpec((1, 1, D_MODEL), lambda b, i, e: (b, 0, MOD_G2)), _resident((1, D_MODEL))],
        out_specs=tok(D_MODEL),
        out_shape=jax.ShapeDtypeStruct((bsz, seq, D_MODEL), F32),
        scratch_shapes=[pltpu.VMEM((MOE_TILE, D_MODEL), F32)],
        compiler_params=_params("parallel", "parallel", "arbitrary"),
        name="moe",
    )(h2, gates, w_up_g, b_up_g, w_down, b_down, x1, mods, norm_final.reshape(1, -1))


def kernel(x, c, ctx, c_ctx, w_ada, b_ada, norm_mix, norm_ffn, w_in, w_gk_f, b_gk_f, w_gk_b, b_gk_b, gla_norm, w_gla_out, hy_conv_w, hy_conv_b, hy_f_w1, hy_f_b1, hy_f_w2, hy_f_b2, hy_f_w3, hy_f_b3, hy_f_w4, hy_sin_freq, hy_bias, w_hy_out, w_out, w_router, b_router, w_up, b_up, w_down, b_down, norm_final):
    depth = w_ada.shape[0]
    assert depth == 1, "single-layer block: the context stream only feeds later layers"
    bsz, seq, _ = x.shape

    n_rows = -(-(bsz + 1) // 8) * 8
    cc = jnp.zeros((n_rows, D_MODEL), F32).at[:bsz].set(c).at[bsz].set(c_ctx)
    mods = _ada(cc, w_ada[0], b_ada[0]).reshape(n_rows, 1, N_MOD * D_MODEL)

    sizes = (GLA_DK, GLA_DK, GLA_DV, GLA_DV, GLA_GATE_RANK, GLA_GATE_RANK, 3 * HY_WIDTH, D_MODEL, D_MODEL)
    offs = np.concatenate([[0], np.cumsum(sizes)])
    w_in_b = w_in[0].astype(BF16)
    wq, wk, wv, wog, wrf, wrb, why, wma, wmb = [w_in_b[:, offs[i]:offs[i + 1]] for i in range(len(sizes))]
    q, k, v, og, r_f, r_b, hy, m_a, m_b = _inproj(
        x, mods, lambda b: b, norm_mix[0], [wq, wk, wv, wog, wrf, wrb, why, wma, wmb],
        [BF16, BF16, BF16, BF16, F32, F32, BF16, BF16, BF16], tl=512)
    k_c, v_c, rc_f, rc_b = _inproj(
        ctx, mods, lambda b: bsz, norm_mix[0], [wk, wv, wrf, wrb], [BF16, BF16, F32, F32], tl=ctx.shape[1])

    w_gk = jnp.stack([w_gk_f[0], w_gk_b[0]])
    b_gk = jnp.stack([b_gk_f[0], b_gk_b[0]])[:, None, :]
    o_f, o_b = _gla(q, k, v, r_f, r_b, k_c, v_c, rc_f, rc_b, w_gk, b_gk)

    f_tab = jnp.asarray(_dft_table(seq)).astype(BF16)
    g_tab = f_tab.T
    hs, hd, pn = _hyena_filter(seq, hy_f_w1[0], hy_f_b1[0], hy_f_w2[0], hy_f_b2[0], hy_f_w3[0], hy_f_b3[0],
                               hy_f_w4[0], hy_sin_freq[0])
    p_spec, q_spec = _hyena_spectrum(f_tab, hs, hd, pn, hy_bias[0])
    y_h = _hyena_conv(hy, hy_conv_w[0], hy_conv_b[0], f_tab, g_tab, p_spec, q_spec)

    x1, h2, gates = _post(o_f, o_b, og, y_h, m_a, m_b, x, mods, gla_norm[0], norm_ffn[0],
                          w_gla_out[0].astype(BF16), w_hy_out[0].astype(BF16), w_out[0].astype(BF16),
                          w_router[0], b_router[0])

    b_up_g = b_up[0].reshape(N_EXPERTS, N_UP_GROUPS, LANES, 2).transpose(0, 1, 3, 2).reshape(N_EXPERTS, 1, -1)
    return _moe(h2, gates, _deinterleave_up(w_up[0]), b_up_g, w_down[0].astype(BF16), b_down[0][:, None, :],
                x1, mods, norm_final)
```

```python
import functools
import math

import jax
import jax.numpy as jnp
import numpy as np
from jax import lax
from jax.experimental import pallas as pl
from jax.experimental.pallas import tpu as pltpu

F32 = jnp.float32
BF16 = jnp.bfloat16

D_MODEL = 1024
GRID_W = 64
EPS = 1e-6
N_MOD = 6

GLA_HEADS = 4
GLA_HEAD_K = 128
GLA_HEAD_V = 256
GLA_DK = GLA_HEADS * GLA_HEAD_K
GLA_DV = GLA_HEADS * GLA_HEAD_V
GLA_GATE_RANK = 16
GLA_GATE_NORM = 16.0
GLA_CHUNK = 64

HY_WIDTH = D_MODEL
HY_EMB = 33
HY_FAST_DECAY = 0.3
HY_SLOW_DECAY = 1.5
HY_TARGET = 1e-2

N_EXPERTS = 32
TOP_K = 4
SWIGLU_LIMIT = 7.0
SWIGLU_ALPHA = 1.702

V7X_VMEM_BYTES = 64 * 1024 * 1024
VMEM_LIMIT = V7X_VMEM_BYTES - 8 * 1024 * 1024
LANES = 128

MOD_SH1, MOD_SC1, MOD_G1, MOD_SH2, MOD_SC2, MOD_G2 = range(N_MOD)


def _params(*sem):
    return pltpu.CompilerParams(dimension_semantics=sem, vmem_limit_bytes=VMEM_LIMIT)


def _resident(shape):
    return pl.BlockSpec(shape, lambda *_: (0,) * len(shape), pipeline_mode=pl.Buffered(1))


def _dot(a, b):
    return jnp.dot(a, b, preferred_element_type=F32)


def _dot_nt(a, b):
    return lax.dot_general(a, b, (((1,), (1,)), ((), ())), preferred_element_type=F32)


def _dot_tn(a, b):
    return lax.dot_general(a, b, (((0,), (0,)), ((), ())), preferred_element_type=F32)


def _split(a):
    hi = a.astype(BF16)
    lo = (a - hi.astype(F32)).astype(BF16)
    return hi, lo


def _dot3(a, b):
    ah, al = _split(a)
    bh, bl = _split(b)
    return _dot(ah, bh) + (_dot(ah, bl) + _dot(al, bh))


def _sigmoid(x):
    return 1.0 / (1.0 + jnp.exp(-x))


def _log_sigmoid(x):
    return jnp.minimum(x, 0.0) - jnp.log(1.0 + jnp.exp(-jnp.abs(x)))


def _rms(x, w):
    return x * lax.rsqrt(jnp.mean(x * x, axis=-1, keepdims=True) + EPS) * w


def _ada_kernel(c_ref, w_ref, b_ref, o_ref):
    c = c_ref[...]
    o_ref[...] = _dot3(c * _sigmoid(c), w_ref[...]) + b_ref[...]


def _ada(cc, w_ada, b_ada):
    rows = cc.shape[0]
    return pl.pallas_call(
        _ada_kernel,
        grid=(N_MOD,),
        in_specs=[pl.BlockSpec((rows, D_MODEL), lambda j: (0, 0)),
                  pl.BlockSpec((D_MODEL, D_MODEL), lambda j: (0, j)),
                  pl.BlockSpec((1, D_MODEL), lambda j: (0, j))],
        out_specs=pl.BlockSpec((rows, D_MODEL), lambda j: (0, j)),
        out_shape=jax.ShapeDtypeStruct((rows, N_MOD * D_MODEL), F32),
        compiler_params=_params("parallel"),
        name="ada",
    )(cc, w_ada, b_ada.reshape(1, -1))


INPROJ_COL_CHUNK = 512


def _inproj_kernel(n_out, x_ref, sc_ref, sh_ref, nw_ref, *refs):
    w_refs, o_refs = refs[:n_out], refs[n_out:]
    h = _rms(x_ref[0], nw_ref[...]) * (1.0 + sc_ref[0]) + sh_ref[0]
    hb = h.astype(BF16)
    for w_ref, o_ref in zip(w_refs, o_refs):
        n = w_ref.shape[1]
        for c0 in range(0, n, INPROJ_COL_CHUNK):
            c1 = min(c0 + INPROJ_COL_CHUNK, n)
            o_ref[0, :, c0:c1] = _dot(hb, w_ref[:, c0:c1]).astype(o_ref.dtype)


def _inproj(x, mods, mod_row, norm_w, weights, out_dtypes, tl):
    bsz, seq, _ = x.shape
    n_out = len(weights)
    in_specs = [pl.BlockSpec((1, tl, D_MODEL), lambda b, i: (b, i, 0)),
                pl.BlockSpec((1, 1, D_MODEL), lambda b, i: (mod_row(b), 0, MOD_SC1)),
                pl.BlockSpec((1, 1, D_MODEL), lambda b, i: (mod_row(b), 0, MOD_SH1)),
                _resident((1, D_MODEL))]
    in_specs += [_resident(w.shape) for w in weights]
    out_specs = [pl.BlockSpec((1, tl, w.shape[1]), lambda b, i: (b, i, 0)) for w in weights]
    out_shape = [jax.ShapeDtypeStruct((bsz, seq, w.shape[1]), dt) for w, dt in zip(weights, out_dtypes)]
    return pl.pallas_call(
        functools.partial(_inproj_kernel, n_out),
        grid=(bsz, seq // tl),
        in_specs=in_specs, out_specs=out_specs, out_shape=out_shape,
        compiler_params=_params("parallel", "parallel"),
        name="inproj",
    )(x, mods, mods, norm_w.reshape(1, -1), *weights)


GLA_TILE = 256
GLA_NCH = GLA_TILE // GLA_CHUNK
GLA_SCALE = GLA_HEAD_K ** -0.5


def _gla_kernel(qf_ref, kf_ref, vf_ref, rf_ref, qb_ref, kb_ref, vb_ref, rb_ref,
                kc_ref, vc_ref, rcf_ref, rcb_ref, wgk_ref, bgk_ref,
                of_ref, ob_ref, sf_ref, sb_ref):
    t = pl.program_id(2)
    row = lax.broadcasted_iota(jnp.int32, (GLA_TILE, GLA_TILE), 0)
    col = lax.broadcasted_iota(jnp.int32, (GLA_TILE, GLA_TILE), 1)
    same = (row // GLA_CHUNK) == (col // GLA_CHUNK)
    tri_f = jnp.where(same & (col <= row), 1.0, 0.0).astype(BF16)
    tri_b = jnp.where(same & (col >= row), 1.0, 0.0).astype(BF16)
    crow = lax.broadcasted_iota(jnp.int32, (GLA_CHUNK, GLA_CHUNK), 0)
    ccol = lax.broadcasted_iota(jnp.int32, (GLA_CHUNK, GLA_CHUNK), 1)
    mask_f = ccol <= crow
    mask_b = ccol >= crow

    def cum_decay(r, d, tri):
        z = _dot3(r, wgk_ref[d]) + bgk_ref[d]
        g = _log_sigmoid(z) * (1.0 / GLA_GATE_NORM)
        gh, gl = _split(g)
        return _dot(tri, gh) + _dot(tri, gl)

    def chunks(fwd):
        return range(GLA_NCH) if fwd else range(GLA_NCH - 1, -1, -1)

    def rows(c):
        return slice(c * GLA_CHUNK, (c + 1) * GLA_CHUNK)

    def last_row(b, c, fwd):
        i = c * GLA_CHUNK + (GLA_CHUNK - 1 if fwd else 0)
        return b[i:i + 1, :]

    def state_step(st, k, v, b, b_last):
        k_upd = (k * jnp.exp(b_last - b)).astype(BF16)
        return st * jnp.exp(b_last) + _dot_tn(v, k_upd)

    def scan_state(st, k_ref, v_ref, b, fwd):
        for c in chunks(fwd):
            k = k_ref[0, rows(c), :].astype(F32)
            st = state_step(st, k, v_ref[0, rows(c), :], b[rows(c)], last_row(b, c, fwd))
        return st

    def scan_out(st, q_ref, k_ref, v_ref, o_ref, b, fwd):
        mask = mask_f if fwd else mask_b
        for c in chunks(fwd):
            bc = b[rows(c)]
            q = q_ref[0, rows(c), :].astype(F32)
            k = k_ref[0, rows(c), :].astype(F32)
            v = v_ref[0, rows(c), :]
            q_dec = (q * jnp.exp(bc) * GLA_SCALE).astype(BF16)
            k_inv = (k * jnp.exp(-bc)).astype(BF16)
            att = jnp.where(mask, _dot_nt(q_dec, k_inv), 0.0).astype(BF16)
            o = _dot(att, v) + _dot_nt(q_dec, st.astype(BF16))
            o_ref[0, rows(c), :] = o.astype(o_ref.dtype)
            st = state_step(st, k, v, bc, last_row(b, c, fwd))
        return st

    @pl.when(t == 0)
    def _():
        zero = jnp.zeros((GLA_HEAD_V, GLA_HEAD_K), F32)
        sf_ref[...] = scan_state(zero, kc_ref, vc_ref, cum_decay(rcf_ref[0], 0, tri_f), True)
        sb_ref[...] = scan_state(zero, kc_ref, vc_ref, cum_decay(rcb_ref[0], 1, tri_b), False)

    sf_ref[...] = scan_out(sf_ref[...], qf_ref, kf_ref, vf_ref, of_ref, cum_decay(rf_ref[0], 0, tri_f), True)
    sb_ref[...] = scan_out(sb_ref[...], qb_ref, kb_ref, vb_ref, ob_ref, cum_decay(rb_ref[0], 1, tri_b), False)


def _gla(q, k, v, r_f, r_b, k_c, v_c, rc_f, rc_b, w_gk, b_gk):
    bsz, seq, _ = q.shape
    ctx_len = k_c.shape[1]
    assert ctx_len == GLA_TILE
    nt = seq // GLA_TILE
    fwd = lambda b, h, t: (b, t, h)
    bwd = lambda b, h, t: (b, nt - 1 - t, h)
    fwd0 = lambda b, h, t: (b, t, 0)
    bwd0 = lambda b, h, t: (b, nt - 1 - t, 0)
    ctx = lambda b, h, t: (b, 0, h)
    ctx0 = lambda b, h, t: (b, 0, 0)
    kblk = (1, GLA_TILE, GLA_HEAD_K)
    vblk = (1, GLA_TILE, GLA_HEAD_V)
    rblk = (1, GLA_TILE, GLA_GATE_RANK)
    in_specs = [pl.BlockSpec(kblk, fwd), pl.BlockSpec(kblk, fwd), pl.BlockSpec(vblk, fwd), pl.BlockSpec(rblk, fwd0),
                pl.BlockSpec(kblk, bwd), pl.BlockSpec(kblk, bwd), pl.BlockSpec(vblk, bwd), pl.BlockSpec(rblk, bwd0),
                pl.BlockSpec(kblk, ctx), pl.BlockSpec(vblk, ctx), pl.BlockSpec(rblk, ctx0), pl.BlockSpec(rblk, ctx0),
                pl.BlockSpec((2, GLA_GATE_RANK, GLA_HEAD_K), lambda b, h, t: (0, 0, h)),
                pl.BlockSpec((2, 1, GLA_HEAD_K), lambda b, h, t: (0, 0, h))]
    out_specs = [pl.BlockSpec(vblk, fwd), pl.BlockSpec(vblk, bwd)]
    out_shape = [jax.ShapeDtypeStruct((bsz, seq, GLA_DV), BF16)] * 2
    return pl.pallas_call(
        _gla_kernel,
        grid=(bsz, GLA_HEADS, nt),
        in_specs=in_specs, out_specs=out_specs, out_shape=out_shape,
        scratch_shapes=[pltpu.VMEM((GLA_HEAD_V, GLA_HEAD_K), F32)] * 2,
        compiler_params=_params("parallel", "parallel", "arbitrary"),
        name="gla",
    )(q, k, v, r_f, q, k, v, r_b, k_c, v_c, rc_f, rc_b, w_gk, b_gk)


HY_CW = 256
HY_ROW_CHUNK = 512
MLP_PAD = 128


@functools.lru_cache(maxsize=None)
def _dft_table(seq):
    n = 2 * seq
    ph = (np.arange(seq)[:, None] * np.arange(seq)[None, :]) % n
    ang = 2.0 * np.pi * np.arange(n) / n
    c = np.cos(ang)[ph]
    s = np.sin(ang)[ph]
    s[0, :] = np.where(np.arange(seq) % 2 == 0, 1.0, -1.0)
    return np.concatenate([c, s], axis=0).astype(np.float32)


@functools.lru_cache(maxsize=None)
def _filter_features(seq):
    bands = (HY_EMB - 1) // 2
    pos = np.arange(seq, dtype=np.float64)[:, None]
    t = pos / max(seq - 1, 1)
    f = np.linspace(1e-4, bands - 1, bands)[None]
    ang = (2.0 * math.pi / seq) * pos * f
    z = np.concatenate([t, np.cos(ang), -np.sin(ang)], axis=-1)
    out = np.zeros((seq, MLP_PAD), np.float32)
    out[:, :HY_EMB] = z
    deltas = np.abs(np.linspace(math.log(HY_TARGET) / HY_SLOW_DECAY, math.log(HY_TARGET) / HY_FAST_DECAY, HY_WIDTH))
    return out, deltas.astype(np.float32)[None]


def _filter_kernel(z_ref, w1_ref, b1_ref, w2_ref, b2_ref, w3_ref, b3_ref, fr_ref, w4f_ref, w4b_ref, dl_ref,
                   hs_ref, hd_ref, pn_ref):
    z = z_ref[...]
    fr = fr_ref[...]
    h = jnp.sin(fr * (_dot3(z, w1_ref[...]) + b1_ref[...]))
    h = jnp.sin(fr * (_dot3(h, w2_ref[...]) + b2_ref[...]))
    h = jnp.sin(fr * (_dot3(h, w3_ref[...]) + b3_ref[...]))
    window = jnp.exp(-z[:, 0:1] * dl_ref[...])
    h_f = _dot3(h, w4f_ref[...]) * window
    h_b = _dot3(h, w4b_ref[...]) * window
    pos = lax.broadcasted_iota(jnp.int32, (z.shape[0], 1), 0)
    h_b = jnp.where(pos == 0, 0.0, h_b)
    hs = h_f + h_b
    hs_ref[...] = hs.astype(BF16)
    hd_ref[...] = (h_b - h_f).astype(BF16)
    sign = jnp.where(pos % 2 == 0, 1.0, -1.0)
    pn_ref[...] = jnp.sum(hs * sign, axis=0, keepdims=True)


def _pad2(a, rows, cols):
    return jnp.zeros((rows, cols), F32).at[:a.shape[0], :a.shape[1]].set(a.astype(F32))


def _hyena_filter(seq, w1, b1, w2, b2, w3, b3, w4, freq):
    z_np, deltas_np = _filter_features(seq)
    p = MLP_PAD
    args = [jnp.asarray(z_np), _pad2(w1, p, p), _pad2(b1[None], 1, p), _pad2(w2, p, p), _pad2(b2[None], 1, p),
            _pad2(w3, p, p), _pad2(b3[None], 1, p), _pad2(freq[None], 1, p),
            _pad2(w4[:, :HY_WIDTH], p, HY_WIDTH), _pad2(w4[:, HY_WIDTH:], p, HY_WIDTH), jnp.asarray(deltas_np)]
    full = lambda shape: pl.BlockSpec(shape, lambda j: (0, 0))
    colblk = lambda rows: pl.BlockSpec((rows, HY_CW), lambda j: (0, j))
    in_specs = [full((seq, p)), full((p, p)), full((1, p)), full((p, p)), full((1, p)), full((p, p)), full((1, p)),
                full((1, p)), colblk(p), colblk(p), colblk(1)]
    return pl.pallas_call(
        _filter_kernel,
        grid=(HY_WIDTH // HY_CW,),
        in_specs=in_specs,
        out_specs=[colblk(seq), colblk(seq), colblk(1)],
        out_shape=[jax.ShapeDtypeStruct((seq, HY_WIDTH), BF16), jax.ShapeDtypeStruct((seq, HY_WIDTH), BF16),
                   jax.ShapeDtypeStruct((1, HY_WIDTH), F32)],
        compiler_params=_params("parallel"),
        name="hy_filter",
    )(*args)


def _spectrum_kernel(f_ref, hs_ref, hd_ref, pn_ref, bias_ref, p_ref, q_ref):
    seq = hs_ref.shape[0]
    bias = bias_ref[...]
    for r0 in range(0, seq, HY_ROW_CHUNK):
        r1 = r0 + HY_ROW_CHUNK
        p_ref[r0:r1, :] = _dot(f_ref[r0:r1, :], hs_ref[...]) + bias
        q = _dot(f_ref[seq + r0:seq + r1, :], hd_ref[...])
        if r0 == 0:
            pos = lax.broadcasted_iota(jnp.int32, (HY_ROW_CHUNK, 1), 0)
            q = jnp.where(pos == 0, pn_ref[...] + bias, q)
        q_ref[r0:r1, :] = q


def _hyena_spectrum(f_tab, hs, hd, pn, bias):
    seq = hs.shape[0]
    colblk = lambda rows: pl.BlockSpec((rows, HY_CW), lambda j: (0, j))
    return pl.pallas_call(
        _spectrum_kernel,
        grid=(HY_WIDTH // HY_CW,),
        in_specs=[_resident(f_tab.shape), colblk(seq), colblk(seq), colblk(1), colblk(1)],
        out_specs=[colblk(seq), colblk(seq)],
        out_shape=[jax.ShapeDtypeStruct((seq, HY_WIDTH), F32)] * 2,
        compiler_params=_params("parallel"),
        name="hy_spectrum",
    )(f_tab, hs, hd, pn, bias.reshape(1, -1))


def _short_conv(u, w, b):
    n = u.shape[0]
    pos = lax.broadcasted_iota(jnp.int32, (n, 1), 0) % GRID_W
    up = jnp.where(pos == 0, 0.0, pltpu.roll(u, 1, 0))
    dn = jnp.where(pos == GRID_W - 1, 0.0, pltpu.roll(u, n - 1, 0))
    return up * w[0:1] + u * w[1:2] + dn * w[2:3] + b


def _hy_fwd_kernel(f_ref, x1_ref, v_ref, w1_ref, b1_ref, wv_ref, bv_ref, p_ref, q_ref, s_ref):
    seq = x1_ref.shape[1]
    n = 2 * seq
    x1 = _short_conv(x1_ref[0].astype(F32), w1_ref[...], b1_ref[...])
    v = _short_conv(v_ref[0].astype(F32), wv_ref[...], bv_ref[...])
    u = (v * x1).astype(BF16)
    for r0 in range(0, seq, HY_ROW_CHUNK):
        r1 = r0 + HY_ROW_CHUNK
        a = _dot(f_ref[r0:r1, :], u)
        bm = _dot(f_ref[seq + r0:seq + r1, :], u)
        p = p_ref[r0:r1, :]
        q = q_ref[r0:r1, :]
        sr = (a * p + bm * q) * (2.0 / n)
        si = (bm * p - a * q) * (2.0 / n)
        if r0 == 0:
            pos = lax.broadcasted_iota(jnp.int32, (HY_ROW_CHUNK, 1), 0)
            sr = jnp.where(pos == 0, a * p * (1.0 / n), sr)
            si = jnp.where(pos == 0, bm * q * (1.0 / n), si)
        s_ref[0, r0:r1, :] = sr.astype(BF16)
        s_ref[0, seq + r0:seq + r1, :] = si.astype(BF16)


def _hy_inv_kernel(g_ref, s_ref, x0_ref, w0_ref, b0_ref, o_ref):
    seq = x0_ref.shape[1]
    x0 = _short_conv(x0_ref[0].astype(F32), w0_ref[...], b0_ref[...])
    s = s_ref[0]
    for r0 in range(0, seq, HY_ROW_CHUNK):
        r1 = r0 + HY_ROW_CHUNK
        y = _dot(g_ref[r0:r1, :], s)
        o_ref[0, r0:r1, :] = (x0[r0:r1] * y).astype(o_ref.dtype)


def _hyena_conv(hy, conv_w, conv_b, f_tab, g_tab, p_spec, q_spec):
    bsz, seq, _ = hy.shape
    nc = HY_WIDTH // HY_CW
    conv_b = conv_b.reshape(1, -1)
    sig = lambda part: pl.BlockSpec((1, seq, HY_CW), lambda j, b: (b, 0, part * nc + j))
    cw = lambda part: pl.BlockSpec((3, HY_CW), lambda j, b: (0, part * nc + j))
    cb = lambda part: pl.BlockSpec((1, HY_CW), lambda j, b: (0, part * nc + j))
    spec = pl.BlockSpec((seq, HY_CW), lambda j, b: (0, j))
    s = pl.pallas_call(
        _hy_fwd_kernel,
        grid=(nc, bsz),
        in_specs=[_resident(f_tab.shape), sig(1), sig(2), cw(1), cb(1), cw(2), cb(2), spec, spec],
        out_specs=pl.BlockSpec((1, 2 * seq, HY_CW), lambda j, b: (b, 0, j)),
        out_shape=jax.ShapeDtypeStruct((bsz, 2 * seq, HY_WIDTH), BF16),
        compiler_params=_params("parallel", "parallel"),
        name="hy_fwd",
    )(f_tab, hy, hy, conv_w, conv_b, conv_w, conv_b, p_spec, q_spec)
    return pl.pallas_call(
        _hy_inv_kernel,
        grid=(nc, bsz),
        in_specs=[_resident(g_tab.shape), pl.BlockSpec((1, 2 * seq, HY_CW), lambda j, b: (b, 0, j)),
                  sig(0), cw(0), cb(0)],
        out_specs=pl.BlockSpec((1, seq, HY_CW), lambda j, b: (b, 0, j)),
        out_shape=jax.ShapeDtypeStruct((bsz, seq, HY_WIDTH), BF16),
        compiler_params=_params("parallel", "parallel"),
        name="hy_inv",
    )(g_tab, s, hy, conv_w, conv_b)


POST_TILE = 512


def _post_kernel(of_ref, ob_ref, og_ref, yh_ref, ma_ref, mb_ref, x_ref, g1_ref, sc2_ref, sh2_ref,
                 gn_ref, nf_ref, wa_ref, wb_ref, wo_ref, wr_ref, br_ref,
                 x1_ref, h2_ref, idx_ref, wts_ref):
    o = of_ref[0].astype(F32) + ob_ref[0].astype(F32)
    gn = gn_ref[...]
    heads = [_rms(o[:, h * GLA_HEAD_V:(h + 1) * GLA_HEAD_V], gn) for h in range(GLA_HEADS)]
    og = og_ref[0].astype(F32)
    a_in = jnp.concatenate(heads, axis=-1) * (og * _sigmoid(og))
    branch_a = _dot(a_in.astype(BF16), wa_ref[...])
    branch_b = _dot(yh_ref[0], wb_ref[...])
    y = _sigmoid(ma_ref[0].astype(F32)) * branch_a + _sigmoid(mb_ref[0].astype(F32)) * branch_b
    x1 = x_ref[0] + g1_ref[0] * _dot(y.astype(BF16), wo_ref[...])
    x1_ref[0] = x1
    h2 = _rms(x1, nf_ref[...]) * (1.0 + sc2_ref[0]) + sh2_ref[0]
    h2_ref[0] = h2

    logits = _dot3(h2, wr_ref[...]) + br_ref[...]
    lane = lax.broadcasted_iota(jnp.int32, logits.shape, 1)
    slot = lax.broadcasted_iota(jnp.int32, (logits.shape[0], TOP_K), 1)
    work = logits
    tops = []
    idx = jnp.zeros((logits.shape[0], TOP_K), jnp.int32)
    for k in range(TOP_K):
        m = jnp.max(work, axis=-1, keepdims=True)
        first = jnp.min(jnp.where(work == m, lane, N_EXPERTS), axis=-1, keepdims=True)
        tops.append(m)
        idx = jnp.where(slot == k, first, idx)
        work = jnp.where(lane == first, -jnp.inf, work)
    es = [jnp.exp(m - tops[0]) for m in tops]
    inv = 1.0 / functools.reduce(lambda a, b: a + b, es)
    wts = jnp.zeros((logits.shape[0], TOP_K), F32)
    for k, e in enumerate(es):
        wts = jnp.where(slot == k, e * inv, wts)
    idx_ref[0] = idx
    wts_ref[0] = wts


def _post(o_f, o_b, og, y_h, m_a, m_b, x, mods, gla_norm, norm_ffn, w_a, w_b, w_o, w_router, b_router):
    bsz, seq, _ = x.shape
    tok = lambda n: pl.BlockSpec((1, POST_TILE, n), lambda b, i: (b, i, 0))
    mod = lambda which: pl.BlockSpec((1, 1, D_MODEL), lambda b, i: (b, 0, which))
    in_specs = [tok(D_MODEL)] * 7 + [mod(MOD_G1), mod(MOD_SC2), mod(MOD_SH2),
                                     _resident((1, GLA_HEAD_V)), _resident((1, D_MODEL)),
                                     _resident(w_a.shape), _resident(w_b.shape), _resident(w_o.shape),
                                     _resident(w_router.shape), _resident((1, N_EXPERTS))]
    return pl.pallas_call(
        _post_kernel,
        grid=(bsz, seq // POST_TILE),
        in_specs=in_specs,
        out_specs=[tok(D_MODEL), tok(D_MODEL), tok(TOP_K), tok(TOP_K)],
        out_shape=[jax.ShapeDtypeStruct((bsz, seq, D_MODEL), F32), jax.ShapeDtypeStruct((bsz, seq, D_MODEL), F32),
                   jax.ShapeDtypeStruct((bsz, seq, TOP_K), jnp.int32), jax.ShapeDtypeStruct((bsz, seq, TOP_K), F32)],
        compiler_params=_params("parallel", "parallel"),
        name="post",
    )(o_f, o_b, og, y_h, m_a, m_b, x, mods, mods, mods, gla_norm.reshape(1, -1), norm_ffn.reshape(1, -1),
      w_a, w_b, w_o, w_router, b_router.reshape(1, -1))


UP_GROUP = 2 * LANES
N_UP_GROUPS = 2 * D_MODEL // UP_GROUP


def _deinterleave_kernel(w_ref, o_ref):
    r = lax.broadcasted_iota(jnp.int32, (UP_GROUP, UP_GROUP), 0)
    c = lax.broadcasted_iota(jnp.int32, (UP_GROUP, UP_GROUP), 1)
    perm = jnp.where(c == (r % 2) * LANES + r // 2, 1.0, 0.0).astype(BF16)
    for g in range(w_ref.shape[2] // UP_GROUP):
        cols = slice(g * UP_GROUP, (g + 1) * UP_GROUP)
        o_ref[0, :, cols] = _dot(w_ref[0, :, cols].astype(BF16), perm).astype(BF16)


def _deinterleave_up(w_up):
    n_exp, d_in, d_up = w_up.shape
    blk = pl.BlockSpec((1, d_in, d_up // 2), lambda e, j: (e, 0, j))
    return pl.pallas_call(
        _deinterleave_kernel,
        grid=(n_exp, 2),
        in_specs=[blk], out_specs=blk,
        out_shape=jax.ShapeDtypeStruct(w_up.shape, BF16),
        compiler_params=_params("parallel", "parallel"),
        name="deinterleave",
    )(w_up)


ROUTE_TILE = 512
EXPERT_TILE = 512
COMBINE_TILE = 256


def _exact_count_dot(a, m):
    a0 = a.astype(BF16)
    r1 = a - a0.astype(F32)
    a1 = r1.astype(BF16)
    a2 = (r1 - a1.astype(F32)).astype(BF16)
    return _dot(a0, m) + (_dot(a1, m) + _dot(a2, m))


def _route_kernel(idx_ref, pos_ref, cnt_ref, counts, base, offs):
    p = pl.program_id(0)
    i = pl.program_id(1)
    n_tok = idx_ref.shape[0]
    idx = idx_ref[...]
    lane = lax.broadcasted_iota(jnp.int32, (n_tok, N_EXPERTS), 1)
    hot = jnp.zeros((n_tok, N_EXPERTS), F32)
    for k in range(TOP_K):
        hot = hot + jnp.where(lane == idx[:, k:k + 1], 1.0, 0.0)
    colsum = jnp.sum(hot, axis=0, keepdims=True)

    @pl.when((p == 0) & (i == 0))
    def _():
        counts[...] = jnp.zeros_like(counts)

    @pl.when(p == 0)
    def _():
        counts[...] += colsum

    @pl.when((p == 1) & (i == 0))
    def _():
        r = lax.broadcasted_iota(jnp.int32, (N_EXPERTS, N_EXPERTS), 0)
        c = lax.broadcasted_iota(jnp.int32, (N_EXPERTS, N_EXPERTS), 1)
        before = jnp.where(r < c, 1.0, 0.0).astype(BF16)
        offs[...] = _exact_count_dot(jnp.broadcast_to(counts[...], (8, N_EXPERTS)), before)[0:1]
        base[...] = jnp.zeros_like(base)

    @pl.when(p == 1)
    def _():
        r = lax.broadcasted_iota(jnp.int32, (n_tok, n_tok), 0)
        c = lax.broadcasted_iota(jnp.int32, (n_tok, n_tok), 1)
        earlier = jnp.where(c < r, 1.0, 0.0).astype(BF16)
        dense = offs[...] + base[...] + _dot(earlier, hot.astype(BF16))
        slot = lax.broadcasted_iota(jnp.int32, (n_tok, TOP_K), 1)
        pos = jnp.zeros((n_tok, TOP_K), F32)
        for k in range(TOP_K):
            mine = jnp.sum(jnp.where(lane == idx[:, k:k + 1], dense, 0.0), axis=-1, keepdims=True)
            pos = jnp.where(slot == k, mine, pos)
        pos_ref[...] = pos.astype(jnp.int32)
        base[...] += colsum

    cnt_ref[...] = counts[...].astype(jnp.int32)


def _route(idx):
    n_tok = idx.shape[0]
    return pl.pallas_call(
        _route_kernel,
        grid=(2, n_tok // ROUTE_TILE),
        in_specs=[pl.BlockSpec((ROUTE_TILE, TOP_K), lambda p, i: (i, 0))],
        out_specs=[pl.BlockSpec((ROUTE_TILE, TOP_K), lambda p, i: (i * p, 0)),
                   pl.BlockSpec((1, N_EXPERTS), lambda p, i: (0, 0))],
        out_shape=[jax.ShapeDtypeStruct((n_tok, TOP_K), jnp.int32), jax.ShapeDtypeStruct((1, N_EXPERTS), jnp.int32)],
        scratch_shapes=[pltpu.VMEM((1, N_EXPERTS), F32)] * 3,
        compiler_params=_params("arbitrary", "arbitrary"),
        name="route",
    )(idx)


def _row_copies(n_tok, copy):
    def issue(t, carry):
        for k in range(TOP_K):
            copy(t, k).start()
        return carry

    def drain(t, carry):
        for k in range(TOP_K):
            copy(t, k).wait()
        return carry

    lax.fori_loop(0, n_tok, issue, 0, unroll=4)
    lax.fori_loop(0, n_tok, drain, 0, unroll=4)


def _dispatch_kernel(pos_ref, h_ref, xs_ref, sem):
    def copy(t, k):
        return pltpu.make_async_copy(h_ref.at[pl.ds(t, 1), :], xs_ref.at[pl.ds(pos_ref[t * TOP_K + k], 1), :], sem)

    _row_copies(h_ref.shape[0], copy)


def _dispatch(pos_flat, h2):
    n_tok = h2.shape[0]
    return pl.pallas_call(
        _dispatch_kernel,
        grid=(n_tok // ROUTE_TILE,),
        in_specs=[pl.BlockSpec((ROUTE_TILE * TOP_K,), lambda i: (i,), memory_space=pltpu.SMEM),
                  pl.BlockSpec((ROUTE_TILE, D_MODEL), lambda i: (i, 0))],
        out_specs=pl.BlockSpec(memory_space=pl.ANY),
        out_shape=jax.ShapeDtypeStruct((n_tok * TOP_K, D_MODEL), F32),
        scratch_shapes=[pltpu.SemaphoreType.DMA],
        compiler_params=_params("arbitrary"),
        name="dispatch",
    )(pos_flat, h2)


def _experts_kernel(tile_ref, exp_ref, lo_ref, hi_ref, x_ref, wu_ref, bu_ref, wd_ref, bd_ref, o_ref):
    w = pl.program_id(0)
    lo = lo_ref[w]
    hi = hi_ref[w]
    start = tile_ref[w] * EXPERT_TILE

    @pl.when(hi > lo)
    def _():
        h = x_ref[...].astype(BF16)
        acts = []
        for g in range(N_UP_GROUPS):
            cols = slice(g * UP_GROUP, (g + 1) * UP_GROUP)
            u = _dot(h, wu_ref[0, :, cols]) + bu_ref[0, :, cols]
            glu = jnp.minimum(u[:, :LANES], SWIGLU_LIMIT)
            lin = jnp.clip(u[:, LANES:], -SWIGLU_LIMIT, SWIGLU_LIMIT)
            acts.append((glu * _sigmoid(SWIGLU_ALPHA * glu) * (lin + 1.0)).astype(BF16))
        y = _dot(jnp.concatenate(acts, axis=-1), wd_ref[0]) + bd_ref[0]
        row = start + lax.broadcasted_iota(jnp.int32, (EXPERT_TILE, 1), 0)
        mine = (row >= lo) & (row < hi)

        @pl.when(lo == start)
        def _():
            o_ref[...] = jnp.where(mine, y, 0.0)

        @pl.when(lo != start)
        def _():
            o_ref[...] = jnp.where(mine, y, o_ref[...])


def _work_items(counts, n_rows):
    n_tiles = n_rows // EXPERT_TILE
    ends = jnp.cumsum(counts)
    tile_ends = jnp.arange(1, n_tiles + 1, dtype=jnp.int32) * EXPERT_TILE
    n_items = n_tiles + N_EXPERTS
    count = lambda cond: jnp.sum(cond.astype(jnp.int32), axis=1)
    slot_t = jnp.arange(n_tiles, dtype=jnp.int32) + count(ends[None, :] < tile_ends[:, None])
    slot_e = jnp.arange(N_EXPERTS, dtype=jnp.int32) + count(tile_ends[None, :] <= ends[:, None])
    slots = jnp.concatenate([slot_t, slot_e])
    vals = jnp.concatenate([tile_ends, ends])
    item = jnp.arange(n_items, dtype=jnp.int32)
    hi = jnp.sum(jnp.where(slots[None, :] == item[:, None], vals[None, :], 0), axis=1)
    lo = jnp.concatenate([jnp.zeros((1,), jnp.int32), hi[:-1]])
    tile_id = jnp.minimum(lo // EXPERT_TILE, n_tiles - 1)
    exp_id = jnp.minimum(count(ends[None, :] <= lo[:, None]), N_EXPERTS - 1)
    return tile_id, exp_id, lo, hi


def _experts(items, xs, w_up_g, b_up_g, w_down, b_down):
    n_items = items[0].shape[0]
    rows = lambda w, t, e, lo, hi: (t[w], 0)
    per_expert = lambda r, c: pl.BlockSpec((1, r, c), lambda w, t, e, lo, hi: (e[w], 0, 0))
    return pl.pallas_call(
        _experts_kernel,
        grid_spec=pltpu.PrefetchScalarGridSpec(
            num_scalar_prefetch=4,
            grid=(n_items,),
            in_specs=[pl.BlockSpec((EXPERT_TILE, D_MODEL), rows),
                      per_expert(D_MODEL, 2 * D_MODEL), per_expert(1, 2 * D_MODEL),
                      per_expert(D_MODEL, D_MODEL), per_expert(1, D_MODEL)],
            out_specs=pl.BlockSpec((EXPERT_TILE, D_MODEL), rows)),
        out_shape=jax.ShapeDtypeStruct(xs.shape, F32),
        compiler_params=_params("arbitrary"),
        name="experts",
    )(*items, xs, w_up_g, b_up_g, w_down, b_down)


def _combine_kernel(pos_ref, ys_ref, wts_ref, x1_ref, g2_ref, nf_ref, o_ref, buf, sem):
    def copy(t, k):
        return pltpu.make_async_copy(ys_ref.at[pl.ds(pos_ref[t * TOP_K + k], 1), :], buf.at[k, pl.ds(t, 1), :], sem)

    _row_copies(x1_ref.shape[0], copy)
    wts = wts_ref[...]
    acc = wts[:, 0:1] * buf[0]
    for k in range(1, TOP_K):
        acc = acc + wts[:, k:k + 1] * buf[k]
    o_ref[...] = _rms(x1_ref[...] + g2_ref[0] * acc, nf_ref[...])


def _combine(pos_flat, ys, wts, x1, mods, norm_final, seq):
    n_tok = x1.shape[0]
    tiles_per_sample = seq // COMBINE_TILE
    tok = lambda n: pl.BlockSpec((COMBINE_TILE, n), lambda i: (i, 0))
    return pl.pallas_call(
        _combine_kernel,
        grid=(n_tok // COMBINE_TILE,),
        in_specs=[pl.BlockSpec((COMBINE_TILE * TOP_K,), lambda i: (i,), memory_space=pltpu.SMEM),
                  pl.BlockSpec(memory_space=pl.ANY), tok(TOP_K), tok(D_MODEL),
                  pl.BlockSpec((1, 1, D_MODEL), lambda i: (i // tiles_per_sample, 0, MOD_G2)),
                  pl.BlockSpec((1, D_MODEL), lambda i: (0, 0))],
        out_specs=tok(D_MODEL),
        out_shape=jax.ShapeDtypeStruct((n_tok, D_MODEL), F32),
        scratch_shapes=[pltpu.VMEM((TOP_K, COMBINE_TILE, D_MODEL), F32), pltpu.SemaphoreType.DMA],
        compiler_params=_params("arbitrary"),
        name="combine",
    )(pos_flat, ys, wts, x1, mods, norm_final.reshape(1, -1))


def _moe(h2, idx, wts, w_up_g, b_up_g, w_down, b_down, x1, mods, norm_final):
    bsz, seq, _ = x1.shape
    n_tok = bsz * seq
    pos, counts = _route(idx.reshape(n_tok, TOP_K))
    pos_flat = pos.reshape(-1)
    xs = _dispatch(pos_flat, h2.reshape(n_tok, D_MODEL))
    ys = _experts(_work_items(counts.reshape(-1), n_tok * TOP_K), xs, w_up_g, b_up_g, w_down, b_down)
    out = _combine(pos_flat, ys, wts.reshape(n_tok, TOP_K), x1.reshape(n_tok, D_MODEL), mods, norm_final, seq)
    return out.reshape(bsz, seq, D_MODEL)


def kernel(x, c, ctx, c_ctx, w_ada, b_ada, norm_mix, norm_ffn, w_in, w_gk_f, b_gk_f, w_gk_b, b_gk_b, gla_norm, w_gla_out, hy_conv_w, hy_conv_b, hy_f_w1, hy_f_b1, hy_f_w2, hy_f_b2, hy_f_w3, hy_f_b3, hy_f_w4, hy_sin_freq, hy_bias, w_hy_out, w_out, w_router, b_router, w_up, b_up, w_down, b_down, norm_final):
    depth = w_ada.shape[0]
    assert depth == 1, "single-layer block: the context stream only feeds later layers"
    bsz, seq, _ = x.shape

    n_rows = -(-(bsz + 1) // 8) * 8
    cc = jnp.zeros((n_rows, D_MODEL), F32).at[:bsz].set(c).at[bsz].set(c_ctx)
    mods = _ada(cc, w_ada[0], b_ada[0]).reshape(n_rows, 1, N_MOD * D_MODEL)

    sizes = (GLA_DK, GLA_DK, GLA_DV, GLA_DV, GLA_GATE_RANK, GLA_GATE_RANK, 3 * HY_WIDTH, D_MODEL, D_MODEL)
    offs = np.concatenate([[0], np.cumsum(sizes)])
    w_in_b = w_in[0].astype(BF16)
    wq, wk, wv, wog, wrf, wrb, why, wma, wmb = [w_in_b[:, offs[i]:offs[i + 1]] for i in range(len(sizes))]
    q, k, v, og, r_f, r_b, hy, m_a, m_b = _inproj(
        x, mods, lambda b: b, norm_mix[0], [wq, wk, wv, wog, wrf, wrb, why, wma, wmb],
        [BF16, BF16, BF16, BF16, F32, F32, BF16, BF16, BF16], tl=512)
    k_c, v_c, rc_f, rc_b = _inproj(
        ctx, mods, lambda b: bsz, norm_mix[0], [wk, wv, wrf, wrb], [BF16, BF16, F32, F32], tl=ctx.shape[1])

    w_gk = jnp.stack([w_gk_f[0], w_gk_b[0]])
    b_gk = jnp.stack([b_gk_f[0], b_gk_b[0]])[:, None, :]
    o_f, o_b = _gla(q, k, v, r_f, r_b, k_c, v_c, rc_f, rc_b, w_gk, b_gk)

    f_tab = jnp.asarray(_dft_table(seq)).astype(BF16)
    g_tab = f_tab.T
    hs, hd, pn = _hyena_filter(seq, hy_f_w1[0], hy_f_b1[0], hy_f_w2[0], hy_f_b2[0], hy_f_w3[0], hy_f_b3[0],
                               hy_f_w4[0], hy_sin_freq[0])
    p_spec, q_spec = _hyena_spectrum(f_tab, hs, hd, pn, hy_bias[0])
    y_h = _hyena_conv(hy, hy_conv_w[0], hy_conv_b[0], f_tab, g_tab, p_spec, q_spec)

    x1, h2, idx, wts = _post(o_f, o_b, og, y_h, m_a, m_b, x, mods, gla_norm[0], norm_ffn[0],
                             w_gla_out[0].astype(BF16), w_hy_out[0].astype(BF16), w_out[0].astype(BF16),
                             w_router[0], b_router[0])

    b_up_g = b_up[0].reshape(N_EXPERTS, N_UP_GROUPS, LANES, 2).transpose(0, 1, 3, 2).reshape(N_EXPERTS, 1, -1)
    return _moe(h2, idx, wts, _deinterleave_up(w_up[0]), b_up_g, w_down[0].astype(BF16), b_down[0][:, None, :],
                x1, mods, norm_final)
```

```python
import functools
import math

import jax
import jax.numpy as jnp
import numpy as np
from jax import lax
from jax.experimental import pallas as pl
from jax.experimental.pallas import tpu as pltpu

F32 = jnp.float32
BF16 = jnp.bfloat16

D_MODEL = 1024
GRID_W = 64
EPS = 1e-6
N_MOD = 6

GLA_HEADS = 4
GLA_HEAD_K = 128
GLA_HEAD_V = 256
GLA_DK = GLA_HEADS * GLA_HEAD_K
GLA_DV = GLA_HEADS * GLA_HEAD_V
GLA_GATE_RANK = 16
GLA_GATE_NORM = 16.0
GLA_CHUNK = 64

HY_WIDTH = D_MODEL
HY_EMB = 33
HY_FAST_DECAY = 0.3
HY_SLOW_DECAY = 1.5
HY_TARGET = 1e-2

N_EXPERTS = 32
TOP_K = 4
SWIGLU_LIMIT = 7.0
SWIGLU_ALPHA = 1.702

V7X_VMEM_BYTES = 64 * 1024 * 1024
VMEM_LIMIT = V7X_VMEM_BYTES - 8 * 1024 * 1024
LANES = 128

MOD_SH1, MOD_SC1, MOD_G1, MOD_SH2, MOD_SC2, MOD_G2 = range(N_MOD)


def _params(*sem):
    return pltpu.CompilerParams(dimension_semantics=sem, vmem_limit_bytes=VMEM_LIMIT)


def _resident(shape):
    return pl.BlockSpec(shape, lambda *_: (0,) * len(shape), pipeline_mode=pl.Buffered(1))


def _dot(a, b):
    return jnp.dot(a, b, preferred_element_type=F32)


def _dot_nt(a, b):
    return lax.dot_general(a, b, (((1,), (1,)), ((), ())), preferred_element_type=F32)


def _dot_tn(a, b):
    return lax.dot_general(a, b, (((0,), (0,)), ((), ())), preferred_element_type=F32)


def _split(a):
    hi = a.astype(BF16)
    lo = (a - hi.astype(F32)).astype(BF16)
    return hi, lo


def _dot3(a, b):
    ah, al = _split(a)
    bh, bl = _split(b)
    return _dot(ah, bh) + (_dot(ah, bl) + _dot(al, bh))


def _sigmoid(x):
    return 1.0 / (1.0 + jnp.exp(-x))


def _log_sigmoid(x):
    return jnp.minimum(x, 0.0) - jnp.log(1.0 + jnp.exp(-jnp.abs(x)))


def _rms(x, w):
    return x * lax.rsqrt(jnp.mean(x * x, axis=-1, keepdims=True) + EPS) * w


def _ada_kernel(c_ref, w_ref, b_ref, o_ref):
    c = c_ref[...]
    o_ref[...] = _dot3(c * _sigmoid(c), w_ref[...]) + b_ref[...]


def _ada(cc, w_ada, b_ada):
    rows = cc.shape[0]
    return pl.pallas_call(
        _ada_kernel,
        grid=(N_MOD,),
        in_specs=[pl.BlockSpec((rows, D_MODEL), lambda j: (0, 0)),
                  pl.BlockSpec((D_MODEL, D_MODEL), lambda j: (0, j)),
                  pl.BlockSpec((1, D_MODEL), lambda j: (0, j))],
        out_specs=pl.BlockSpec((rows, D_MODEL), lambda j: (0, j)),
        out_shape=jax.ShapeDtypeStruct((rows, N_MOD * D_MODEL), F32),
        compiler_params=_params("parallel"),
        name="ada",
    )(cc, w_ada, b_ada.reshape(1, -1))


INPROJ_COL_CHUNK = 512


def _inproj_kernel(n_out, x_ref, sc_ref, sh_ref, nw_ref, *refs):
    w_refs, o_refs = refs[:n_out], refs[n_out:]
    h = _rms(x_ref[0], nw_ref[...]) * (1.0 + sc_ref[0]) + sh_ref[0]
    hb = h.astype(BF16)
    for w_ref, o_ref in zip(w_refs, o_refs):
        n = w_ref.shape[1]
        for c0 in range(0, n, INPROJ_COL_CHUNK):
            c1 = min(c0 + INPROJ_COL_CHUNK, n)
            o_ref[0, :, c0:c1] = _dot(hb, w_ref[:, c0:c1]).astype(o_ref.dtype)


def _inproj(x, mods, mod_row, norm_w, weights, out_dtypes, tl):
    bsz, seq, _ = x.shape
    n_out = len(weights)
    in_specs = [pl.BlockSpec((1, tl, D_MODEL), lambda b, i: (b, i, 0)),
                pl.BlockSpec((1, 1, D_MODEL), lambda b, i: (mod_row(b), 0, MOD_SC1)),
                pl.BlockSpec((1, 1, D_MODEL), lambda b, i: (mod_row(b), 0, MOD_SH1)),
                _resident((1, D_MODEL))]
    in_specs += [_resident(w.shape) for w in weights]
    out_specs = [pl.BlockSpec((1, tl, w.shape[1]), lambda b, i: (b, i, 0)) for w in weights]
    out_shape = [jax.ShapeDtypeStruct((bsz, seq, w.shape[1]), dt) for w, dt in zip(weights, out_dtypes)]
    return pl.pallas_call(
        functools.partial(_inproj_kernel, n_out),
        grid=(bsz, seq // tl),
        in_specs=in_specs, out_specs=out_specs, out_shape=out_shape,
        compiler_params=_params("parallel", "parallel"),
        name="inproj",
    )(x, mods, mods, norm_w.reshape(1, -1), *weights)


GLA_TILE = 256
GLA_NCH = GLA_TILE // GLA_CHUNK
GLA_SCALE = GLA_HEAD_K ** -0.5
GLA_HPS = 4
GLA_KW = GLA_HPS * GLA_HEAD_K
GLA_VW = GLA_HPS * GLA_HEAD_V


@functools.lru_cache(maxsize=None)
def _chunk_triangles():
    i = np.arange(GLA_TILE)
    same = (i[:, None] // GLA_CHUNK) == (i[None, :] // GLA_CHUNK)
    return np.stack([same & (i[None, :] <= i[:, None]), same & (i[None, :] >= i[:, None])]).astype(np.float32)


def _gla_kernel(tri_ref, qf_ref, kf_ref, vf_ref, rf_ref, qb_ref, kb_ref, vb_ref, rb_ref,
                kc_ref, vc_ref, rcf_ref, rcb_ref, wgk_ref, bgk_ref,
                of_ref, ob_ref, sf_ref, sb_ref):
    t = pl.program_id(2)
    crow = lax.broadcasted_iota(jnp.int32, (GLA_CHUNK, GLA_CHUNK), 0)
    ccol = lax.broadcasted_iota(jnp.int32, (GLA_CHUNK, GLA_CHUNK), 1)
    heads = range(GLA_HPS)

    def rows(c):
        return slice(c * GLA_CHUNK, (c + 1) * GLA_CHUNK)

    def hk(h):
        return slice(h * GLA_HEAD_K, (h + 1) * GLA_HEAD_K)

    def hv(h):
        return slice(h * GLA_HEAD_V, (h + 1) * GLA_HEAD_V)

    def order(fwd):
        return range(GLA_NCH) if fwd else range(GLA_NCH - 1, -1, -1)

    def cum_decay(r, d):
        z = _dot3(r, wgk_ref[d]) + bgk_ref[d]
        g = _log_sigmoid(z) * (1.0 / GLA_GATE_NORM)
        gh, gl = _split(g)
        return _dot(tri_ref[d], gh) + _dot(tri_ref[d], gl)

    def updates(k_ref, v_ref, b, fwd):
        out = {}
        for c in range(GLA_NCH):
            i = c * GLA_CHUNK + (GLA_CHUNK - 1 if fwd else 0)
            total = b[i:i + 1, :]
            k_upd = (k_ref[0, rows(c), :].astype(F32) * jnp.exp(total - b[rows(c)])).astype(BF16)
            dec = jnp.exp(total)
            for h in heads:
                out[c, h] = (dec[:, hk(h)], _dot_tn(v_ref[0, rows(c), hv(h)], k_upd[:, hk(h)]))
        return out

    def context_state(s_ref, r_ref, d, fwd):
        upd = updates(kc_ref, vc_ref, cum_decay(r_ref[0], d), fwd)
        for h in heads:
            st = jnp.zeros((GLA_HEAD_V, GLA_HEAD_K), F32)
            for c in order(fwd):
                dec, inc = upd[c, h]
                st = st * dec + inc
            s_ref[h] = st

    def scan(s_ref, q_ref, k_ref, v_ref, r_ref, o_ref, d, fwd):
        mask = (ccol <= crow) if fwd else (ccol >= crow)
        b = cum_decay(r_ref[0], d)
        q_dec = (q_ref[0].astype(F32) * (jnp.exp(b) * GLA_SCALE)).astype(BF16)
        k_inv = (k_ref[0].astype(F32) * jnp.exp(-b)).astype(BF16)
        upd = updates(k_ref, v_ref, b, fwd)
        intra = {}
        for c in range(GLA_NCH):
            for h in heads:
                att = jnp.where(mask, _dot_nt(q_dec[rows(c), hk(h)], k_inv[rows(c), hk(h)]), 0.0).astype(BF16)
                intra[c, h] = _dot(att, v_ref[0, rows(c), hv(h)])
        enter = {}
        for h in heads:
            st = s_ref[h]
            for c in order(fwd):
                enter[c, h] = st.astype(BF16)
                dec, inc = upd[c, h]
                st = st * dec + inc
            s_ref[h] = st
        for c in range(GLA_NCH):
            for h in heads:
                o = intra[c, h] + _dot_nt(q_dec[rows(c), hk(h)], enter[c, h])
                o_ref[0, rows(c), hv(h)] = o.astype(o_ref.dtype)

    @pl.when(t == 0)
    def _():
        context_state(sf_ref, rcf_ref, 0, True)
        context_state(sb_ref, rcb_ref, 1, False)

    scan(sf_ref, qf_ref, kf_ref, vf_ref, rf_ref, of_ref, 0, True)
    scan(sb_ref, qb_ref, kb_ref, vb_ref, rb_ref, ob_ref, 1, False)


def _gla(q, k, v, r_f, r_b, k_c, v_c, rc_f, rc_b, w_gk, b_gk):
    bsz, seq, _ = q.shape
    assert k_c.shape[1] == GLA_TILE
    nt = seq // GLA_TILE
    tri = jnp.asarray(_chunk_triangles()).astype(BF16)
    fwd = lambda b, h, t: (b, t, h)
    bwd = lambda b, h, t: (b, nt - 1 - t, h)
    fwd0 = lambda b, h, t: (b, t, 0)
    bwd0 = lambda b, h, t: (b, nt - 1 - t, 0)
    ctx = lambda b, h, t: (b, 0, h)
    ctx0 = lambda b, h, t: (b, 0, 0)
    kblk = (1, GLA_TILE, GLA_KW)
    vblk = (1, GLA_TILE, GLA_VW)
    rblk = (1, GLA_TILE, GLA_GATE_RANK)
    in_specs = [_resident(tri.shape),
                pl.BlockSpec(kblk, fwd), pl.BlockSpec(kblk, fwd), pl.BlockSpec(vblk, fwd), pl.BlockSpec(rblk, fwd0),
                pl.BlockSpec(kblk, bwd), pl.BlockSpec(kblk, bwd), pl.BlockSpec(vblk, bwd), pl.BlockSpec(rblk, bwd0),
                pl.BlockSpec(kblk, ctx), pl.BlockSpec(vblk, ctx), pl.BlockSpec(rblk, ctx0), pl.BlockSpec(rblk, ctx0),
                pl.BlockSpec((2, GLA_GATE_RANK, GLA_KW), lambda b, h, t: (0, 0, h)),
                pl.BlockSpec((2, 1, GLA_KW), lambda b, h, t: (0, 0, h))]
    out_specs = [pl.BlockSpec(vblk, fwd), pl.BlockSpec(vblk, bwd)]
    out_shape = [jax.ShapeDtypeStruct((bsz, seq, GLA_DV), BF16)] * 2
    return pl.pallas_call(
        _gla_kernel,
        grid=(bsz, GLA_HEADS // GLA_HPS, nt),
        in_specs=in_specs, out_specs=out_specs, out_shape=out_shape,
        scratch_shapes=[pltpu.VMEM((GLA_HPS, GLA_HEAD_V, GLA_HEAD_K), F32)] * 2,
        compiler_params=_params("parallel", "parallel", "arbitrary"),
        name="gla",
    )(tri, q, k, v, r_f, q, k, v, r_b, k_c, v_c, rc_f, rc_b, w_gk, b_gk)


HY_CW = 256
HY_ROW_CHUNK = 512
MLP_PAD = 128


@functools.lru_cache(maxsize=None)
def _dft_table(seq):
    n = 2 * seq
    ph = (np.arange(seq)[:, None] * np.arange(seq)[None, :]) % n
    ang = 2.0 * np.pi * np.arange(n) / n
    c = np.cos(ang)[ph]
    s = np.sin(ang)[ph]
    s[0, :] = np.where(np.arange(seq) % 2 == 0, 1.0, -1.0)
    return np.concatenate([c, s], axis=0).astype(np.float32)


@functools.lru_cache(maxsize=None)
def _filter_features(seq):
    bands = (HY_EMB - 1) // 2
    pos = np.arange(seq, dtype=np.float64)[:, None]
    t = pos / max(seq - 1, 1)
    f = np.linspace(1e-4, bands - 1, bands)[None]
    ang = (2.0 * math.pi / seq) * pos * f
    z = np.concatenate([t, np.cos(ang), -np.sin(ang)], axis=-1)
    out = np.zeros((seq, MLP_PAD), np.float32)
    out[:, :HY_EMB] = z
    deltas = np.abs(np.linspace(math.log(HY_TARGET) / HY_SLOW_DECAY, math.log(HY_TARGET) / HY_FAST_DECAY, HY_WIDTH))
    return out, deltas.astype(np.float32)[None]


def _filter_kernel(z_ref, w1_ref, b1_ref, w2_ref, b2_ref, w3_ref, b3_ref, fr_ref, w4f_ref, w4b_ref, dl_ref,
                   hs_ref, hd_ref, pn_ref):
    z = z_ref[...]
    fr = fr_ref[...]
    h = jnp.sin(fr * (_dot3(z, w1_ref[...]) + b1_ref[...]))
    h = jnp.sin(fr * (_dot3(h, w2_ref[...]) + b2_ref[...]))
    h = jnp.sin(fr * (_dot3(h, w3_ref[...]) + b3_ref[...]))
    window = jnp.exp(-z[:, 0:1] * dl_ref[...])
    h_f = _dot3(h, w4f_ref[...]) * window
    h_b = _dot3(h, w4b_ref[...]) * window
    pos = lax.broadcasted_iota(jnp.int32, (z.shape[0], 1), 0)
    h_b = jnp.where(pos == 0, 0.0, h_b)
    hs = h_f + h_b
    hs_ref[...] = hs.astype(BF16)
    hd_ref[...] = (h_b - h_f).astype(BF16)
    sign = jnp.where(pos % 2 == 0, 1.0, -1.0)
    pn_ref[...] = jnp.sum(hs * sign, axis=0, keepdims=True)


def _pad2(a, rows, cols):
    return jnp.zeros((rows, cols), F32).at[:a.shape[0], :a.shape[1]].set(a.astype(F32))


def _hyena_filter(seq, w1, b1, w2, b2, w3, b3, w4, freq):
    z_np, deltas_np = _filter_features(seq)
    p = MLP_PAD
    args = [jnp.asarray(z_np), _pad2(w1, p, p), _pad2(b1[None], 1, p), _pad2(w2, p, p), _pad2(b2[None], 1, p),
            _pad2(w3, p, p), _pad2(b3[None], 1, p), _pad2(freq[None], 1, p),
            _pad2(w4[:, :HY_WIDTH], p, HY_WIDTH), _pad2(w4[:, HY_WIDTH:], p, HY_WIDTH), jnp.asarray(deltas_np)]
    full = lambda shape: pl.BlockSpec(shape, lambda j: (0, 0))
    colblk = lambda rows: pl.BlockSpec((rows, HY_CW), lambda j: (0, j))
    in_specs = [full((seq, p)), full((p, p)), full((1, p)), full((p, p)), full((1, p)), full((p, p)), full((1, p)),
                full((1, p)), colblk(p), colblk(p), colblk(1)]
    return pl.pallas_call(
        _filter_kernel,
        grid=(HY_WIDTH // HY_CW,),
        in_specs=in_specs,
        out_specs=[colblk(seq), colblk(seq), colblk(1)],
        out_shape=[jax.ShapeDtypeStruct((seq, HY_WIDTH), BF16), jax.ShapeDtypeStruct((seq, HY_WIDTH), BF16),
                   jax.ShapeDtypeStruct((1, HY_WIDTH), F32)],
        compiler_params=_params("parallel"),
        name="hy_filter",
    )(*args)


def _spectrum_kernel(f_ref, hs_ref, hd_ref, pn_ref, bias_ref, p_ref, q_ref):
    seq = hs_ref.shape[0]
    bias = bias_ref[...]
    for r0 in range(0, seq, HY_ROW_CHUNK):
        r1 = r0 + HY_ROW_CHUNK
        p_ref[r0:r1, :] = _dot(f_ref[r0:r1, :], hs_ref[...]) + bias
        q = _dot(f_ref[seq + r0:seq + r1, :], hd_ref[...])
        if r0 == 0:
            pos = lax.broadcasted_iota(jnp.int32, (HY_ROW_CHUNK, 1), 0)
            q = jnp.where(pos == 0, pn_ref[...] + bias, q)
        q_ref[r0:r1, :] = q


def _hyena_spectrum(f_tab, hs, hd, pn, bias):
    seq = hs.shape[0]
    colblk = lambda rows: pl.BlockSpec((rows, HY_CW), lambda j: (0, j))
    return pl.pallas_call(
        _spectrum_kernel,
        grid=(HY_WIDTH // HY_CW,),
        in_specs=[_resident(f_tab.shape), colblk(seq), colblk(seq), colblk(1), colblk(1)],
        out_specs=[colblk(seq), colblk(seq)],
        out_shape=[jax.ShapeDtypeStruct((seq, HY_WIDTH), F32)] * 2,
        compiler_params=_params("parallel"),
        name="hy_spectrum",
    )(f_tab, hs, hd, pn, bias.reshape(1, -1))


def _short_conv(u, w, b):
    n = u.shape[0]
    pos = lax.broadcasted_iota(jnp.int32, (n, 1), 0) % GRID_W
    up = jnp.where(pos == 0, 0.0, pltpu.roll(u, 1, 0))
    dn = jnp.where(pos == GRID_W - 1, 0.0, pltpu.roll(u, n - 1, 0))
    return up * w[0:1] + u * w[1:2] + dn * w[2:3] + b


def _hy_fwd_kernel(f_ref, x1_ref, v_ref, w1_ref, b1_ref, wv_ref, bv_ref, p_ref, q_ref, s_ref):
    seq = x1_ref.shape[1]
    n = 2 * seq
    x1 = _short_conv(x1_ref[0].astype(F32), w1_ref[...], b1_ref[...])
    v = _short_conv(v_ref[0].astype(F32), wv_ref[...], bv_ref[...])
    u = (v * x1).astype(BF16)
    for r0 in range(0, seq, HY_ROW_CHUNK):
        r1 = r0 + HY_ROW_CHUNK
        a = _dot(f_ref[r0:r1, :], u)
        bm = _dot(f_ref[seq + r0:seq + r1, :], u)
        p = p_ref[r0:r1, :]
        q = q_ref[r0:r1, :]
        sr = (a * p + bm * q) * (2.0 / n)
        si = (bm * p - a * q) * (2.0 / n)
        if r0 == 0:
            pos = lax.broadcasted_iota(jnp.int32, (HY_ROW_CHUNK, 1), 0)
            sr = jnp.where(pos == 0, a * p * (1.0 / n), sr)
            si = jnp.where(pos == 0, bm * q * (1.0 / n), si)
        s_ref[0, r0:r1, :] = sr.astype(BF16)
        s_ref[0, seq + r0:seq + r1, :] = si.astype(BF16)


def _hy_inv_kernel(g_ref, s_ref, x0_ref, w0_ref, b0_ref, o_ref):
    seq = x0_ref.shape[1]
    x0 = _short_conv(x0_ref[0].astype(F32), w0_ref[...], b0_ref[...])
    s = s_ref[0]
    for r0 in range(0, seq, HY_ROW_CHUNK):
        r1 = r0 + HY_ROW_CHUNK
        y = _dot(g_ref[r0:r1, :], s)
        o_ref[0, r0:r1, :] = (x0[r0:r1] * y).astype(o_ref.dtype)


def _hyena_conv(hy, conv_w, conv_b, f_tab, g_tab, p_spec, q_spec):
    bsz, seq, _ = hy.shape
    nc = HY_WIDTH // HY_CW
    conv_b = conv_b.reshape(1, -1)
    sig = lambda part: pl.BlockSpec((1, seq, HY_CW), lambda j, b: (b, 0, part * nc + j))
    cw = lambda part: pl.BlockSpec((3, HY_CW), lambda j, b: (0, part * nc + j))
    cb = lambda part: pl.BlockSpec((1, HY_CW), lambda j, b: (0, part * nc + j))
    spec = pl.BlockSpec((seq, HY_CW), lambda j, b: (0, j))
    s = pl.pallas_call(
        _hy_fwd_kernel,
        grid=(nc, bsz),
        in_specs=[_resident(f_tab.shape), sig(1), sig(2), cw(1), cb(1), cw(2), cb(2), spec, spec],
        out_specs=pl.BlockSpec((1, 2 * seq, HY_CW), lambda j, b: (b, 0, j)),
        out_shape=jax.ShapeDtypeStruct((bsz, 2 * seq, HY_WIDTH), BF16),
        compiler_params=_params("parallel", "parallel"),
        name="hy_fwd",
    )(f_tab, hy, hy, conv_w, conv_b, conv_w, conv_b, p_spec, q_spec)
    return pl.pallas_call(
        _hy_inv_kernel,
        grid=(nc, bsz),
        in_specs=[_resident(g_tab.shape), pl.BlockSpec((1, 2 * seq, HY_CW), lambda j, b: (b, 0, j)),
                  sig(0), cw(0), cb(0)],
        out_specs=pl.BlockSpec((1, seq, HY_CW), lambda j, b: (b, 0, j)),
        out_shape=jax.ShapeDtypeStruct((bsz, seq, HY_WIDTH), BF16),
        compiler_params=_params("parallel", "parallel"),
        name="hy_inv",
    )(g_tab, s, hy, conv_w, conv_b)


POST_TILE = 512


def _post_kernel(of_ref, ob_ref, og_ref, yh_ref, ma_ref, mb_ref, x_ref, g1_ref, sc2_ref, sh2_ref,
                 gn_ref, nf_ref, wa_ref, wb_ref, wo_ref, wr_ref, br_ref,
                 x1_ref, h2_ref, idx_ref, wts_ref):
    o = of_ref[0].astype(F32) + ob_ref[0].astype(F32)
    gn = gn_ref[...]
    heads = [_rms(o[:, h * GLA_HEAD_V:(h + 1) * GLA_HEAD_V], gn) for h in range(GLA_HEADS)]
    og = og_ref[0].astype(F32)
    a_in = jnp.concatenate(heads, axis=-1) * (og * _sigmoid(og))
    branch_a = _dot(a_in.astype(BF16), wa_ref[...])
    branch_b = _dot(yh_ref[0], wb_ref[...])
    y = _sigmoid(ma_ref[0].astype(F32)) * branch_a + _sigmoid(mb_ref[0].astype(F32)) * branch_b
    x1 = x_ref[0] + g1_ref[0] * _dot(y.astype(BF16), wo_ref[...])
    x1_ref[0] = x1
    h2 = _rms(x1, nf_ref[...]) * (1.0 + sc2_ref[0]) + sh2_ref[0]
    h2_ref[0] = h2

    logits = _dot3(h2, wr_ref[...]) + br_ref[...]
    lane = lax.broadcasted_iota(jnp.int32, logits.shape, 1)
    slot = lax.broadcasted_iota(jnp.int32, (logits.shape[0], TOP_K), 1)
    work = logits
    tops = []
    idx = jnp.zeros((logits.shape[0], TOP_K), jnp.int32)
    for k in range(TOP_K):
        m = jnp.max(work, axis=-1, keepdims=True)
        first = jnp.min(jnp.where(work == m, lane, N_EXPERTS), axis=-1, keepdims=True)
        tops.append(m)
        idx = jnp.where(slot == k, first, idx)
        work = jnp.where(lane == first, -jnp.inf, work)
    es = [jnp.exp(m - tops[0]) for m in tops]
    inv = 1.0 / functools.reduce(lambda a, b: a + b, es)
    wts = jnp.zeros((logits.shape[0], TOP_K), F32)
    for k, e in enumerate(es):
        wts = jnp.where(slot == k, e * inv, wts)
    idx_ref[0] = idx
    wts_ref[0] = wts


def _post(o_f, o_b, og, y_h, m_a, m_b, x, mods, gla_norm, norm_ffn, w_a, w_b, w_o, w_router, b_router):
    bsz, seq, _ = x.shape
    tok = lambda n: pl.BlockSpec((1, POST_TILE, n), lambda b, i: (b, i, 0))
    mod = lambda which: pl.BlockSpec((1, 1, D_MODEL), lambda b, i: (b, 0, which))
    in_specs = [tok(D_MODEL)] * 7 + [mod(MOD_G1), mod(MOD_SC2), mod(MOD_SH2),
                                     _resident((1, GLA_HEAD_V)), _resident((1, D_MODEL)),
                                     _resident(w_a.shape), _resident(w_b.shape), _resident(w_o.shape),
                                     _resident(w_router.shape), _resident((1, N_EXPERTS))]
    return pl.pallas_call(
        _post_kernel,
        grid=(bsz, seq // POST_TILE),
        in_specs=in_specs,
        out_specs=[tok(D_MODEL), tok(D_MODEL), tok(TOP_K), tok(TOP_K)],
        out_shape=[jax.ShapeDtypeStruct((bsz, seq, D_MODEL), F32), jax.ShapeDtypeStruct((bsz, seq, D_MODEL), F32),
                   jax.ShapeDtypeStruct((bsz, seq, TOP_K), jnp.int32), jax.ShapeDtypeStruct((bsz, seq, TOP_K), F32)],
        compiler_params=_params("parallel", "parallel"),
        name="post",
    )(o_f, o_b, og, y_h, m_a, m_b, x, mods, mods, mods, gla_norm.reshape(1, -1), norm_ffn.reshape(1, -1),
      w_a, w_b, w_o, w_router, b_router.reshape(1, -1))


UP_GROUP = 2 * LANES
N_UP_GROUPS = 2 * D_MODEL // UP_GROUP


def _deinterleave_kernel(w_ref, o_ref):
    r = lax.broadcasted_iota(jnp.int32, (UP_GROUP, UP_GROUP), 0)
    c = lax.broadcasted_iota(jnp.int32, (UP_GROUP, UP_GROUP), 1)
    perm = jnp.where(c == (r % 2) * LANES + r // 2, 1.0, 0.0).astype(BF16)
    for g in range(w_ref.shape[2] // UP_GROUP):
        cols = slice(g * UP_GROUP, (g + 1) * UP_GROUP)
        o_ref[0, :, cols] = _dot(w_ref[0, :, cols].astype(BF16), perm).astype(BF16)


def _deinterleave_up(w_up):
    n_exp, d_in, d_up = w_up.shape
    blk = pl.BlockSpec((1, d_in, d_up // 2), lambda e, j: (e, 0, j))
    return pl.pallas_call(
        _deinterleave_kernel,
        grid=(n_exp, 2),
        in_specs=[blk], out_specs=blk,
        out_shape=jax.ShapeDtypeStruct(w_up.shape, BF16),
        compiler_params=_params("parallel", "parallel"),
        name="deinterleave",
    )(w_up)


ROUTE_TILE = 512
EXPERT_TILE = 512
COMBINE_TILE = 256


def _exact_count_dot(a, m):
    a0 = a.astype(BF16)
    r1 = a - a0.astype(F32)
    a1 = r1.astype(BF16)
    a2 = (r1 - a1.astype(F32)).astype(BF16)
    return _dot(a0, m) + (_dot(a1, m) + _dot(a2, m))


def _route_kernel(idx_ref, pos_ref, cnt_ref, counts, base, offs):
    p = pl.program_id(0)
    i = pl.program_id(1)
    n_tok = idx_ref.shape[0]
    idx = idx_ref[...]
    lane = lax.broadcasted_iota(jnp.int32, (n_tok, N_EXPERTS), 1)
    hot = jnp.zeros((n_tok, N_EXPERTS), F32)
    for k in range(TOP_K):
        hot = hot + jnp.where(lane == idx[:, k:k + 1], 1.0, 0.0)
    colsum = jnp.sum(hot, axis=0, keepdims=True)

    @pl.when((p == 0) & (i == 0))
    def _():
        counts[...] = jnp.zeros_like(counts)

    @pl.when(p == 0)
    def _():
        counts[...] += colsum

    @pl.when((p == 1) & (i == 0))
    def _():
        r = lax.broadcasted_iota(jnp.int32, (N_EXPERTS, N_EXPERTS), 0)
        c = lax.broadcasted_iota(jnp.int32, (N_EXPERTS, N_EXPERTS), 1)
        before = jnp.where(r < c, 1.0, 0.0).astype(BF16)
        offs[...] = _exact_count_dot(jnp.broadcast_to(counts[...], (8, N_EXPERTS)), before)[0:1]
        base[...] = jnp.zeros_like(base)

    @pl.when(p == 1)
    def _():
        r = lax.broadcasted_iota(jnp.int32, (n_tok, n_tok), 0)
        c = lax.broadcasted_iota(jnp.int32, (n_tok, n_tok), 1)
        earlier = jnp.where(c < r, 1.0, 0.0).astype(BF16)
        dense = offs[...] + base[...] + _dot(earlier, hot.astype(BF16))
        slot = lax.broadcasted_iota(jnp.int32, (n_tok, TOP_K), 1)
        pos = jnp.zeros((n_tok, TOP_K), F32)
        for k in range(TOP_K):
            mine = jnp.sum(jnp.where(lane == idx[:, k:k + 1], dense, 0.0), axis=-1, keepdims=True)
            pos = jnp.where(slot == k, mine, pos)
        pos_ref[...] = pos.astype(jnp.int32)
        base[...] += colsum

    cnt_ref[...] = counts[...].astype(jnp.int32)


def _route(idx):
    n_tok = idx.shape[0]
    return pl.pallas_call(
        _route_kernel,
        grid=(2, n_tok // ROUTE_TILE),
        in_specs=[pl.BlockSpec((ROUTE_TILE, TOP_K), lambda p, i: (i, 0))],
        out_specs=[pl.BlockSpec((ROUTE_TILE, TOP_K), lambda p, i: (i * p, 0)),
                   pl.BlockSpec((1, N_EXPERTS), lambda p, i: (0, 0))],
        out_shape=[jax.ShapeDtypeStruct((n_tok, TOP_K), jnp.int32), jax.ShapeDtypeStruct((1, N_EXPERTS), jnp.int32)],
        scratch_shapes=[pltpu.VMEM((1, N_EXPERTS), F32)] * 3,
        compiler_params=_params("arbitrary", "arbitrary"),
        name="route",
    )(idx)


def _row_copies(n_tok, copy):
    def issue(t, carry):
        for k in range(TOP_K):
            copy(t, k).start(priority=k % 2)
        return carry

    def drain(t, carry):
        for k in range(TOP_K):
            copy(t, k).wait()
        return carry

    lax.fori_loop(0, n_tok, issue, 0, unroll=4)
    lax.fori_loop(0, n_tok, drain, 0, unroll=4)


def _dispatch_kernel(pos_ref, h_ref, xs_ref, sem):
    def copy(t, k):
        return pltpu.make_async_copy(h_ref.at[pl.ds(t, 1), :], xs_ref.at[pl.ds(pos_ref[t * TOP_K + k], 1), :], sem)

    _row_copies(h_ref.shape[0], copy)


def _dispatch(pos_flat, h2):
    n_tok = h2.shape[0]
    return pl.pallas_call(
        _dispatch_kernel,
        grid=(n_tok // ROUTE_TILE,),
        in_specs=[pl.BlockSpec((ROUTE_TILE * TOP_K,), lambda i: (i,), memory_space=pltpu.SMEM),
                  pl.BlockSpec((ROUTE_TILE, D_MODEL), lambda i: (i, 0))],
        out_specs=pl.BlockSpec(memory_space=pl.ANY),
        out_shape=jax.ShapeDtypeStruct((n_tok * TOP_K, D_MODEL), F32),
        scratch_shapes=[pltpu.SemaphoreType.DMA],
        compiler_params=_params("arbitrary"),
        name="dispatch",
    )(pos_flat, h2)


def _experts_kernel(tile_ref, exp_ref, lo_ref, hi_ref, x_ref, wu_ref, bu_ref, wd_ref, bd_ref, o_ref):
    w = pl.program_id(0)
    lo = lo_ref[w]
    hi = hi_ref[w]
    start = tile_ref[w] * EXPERT_TILE

    @pl.when(hi > lo)
    def _():
        h = x_ref[...].astype(BF16)
        acts = []
        for g in range(N_UP_GROUPS):
            cols = slice(g * UP_GROUP, (g + 1) * UP_GROUP)
            u = _dot(h, wu_ref[0, :, cols]) + bu_ref[0, :, cols]
            glu = jnp.minimum(u[:, :LANES], SWIGLU_LIMIT)
            lin = jnp.clip(u[:, LANES:], -SWIGLU_LIMIT, SWIGLU_LIMIT)
            acts.append((glu * _sigmoid(SWIGLU_ALPHA * glu) * (lin + 1.0)).astype(BF16))
        y = _dot(jnp.concatenate(acts, axis=-1), wd_ref[0]) + bd_ref[0]
        row = start + lax.broadcasted_iota(jnp.int32, (EXPERT_TILE, 1), 0)
        mine = (row >= lo) & (row < hi)

        @pl.when(lo == start)
        def _():
            o_ref[...] = jnp.where(mine, y, 0.0)

        @pl.when(lo != start)
        def _():
            o_ref[...] = jnp.where(mine, y, o_ref[...])


def _work_items(counts, n_rows):
    n_tiles = n_rows // EXPERT_TILE
    ends = jnp.cumsum(counts)
    tile_ends = jnp.arange(1, n_tiles + 1, dtype=jnp.int32) * EXPERT_TILE
    n_items = n_tiles + N_EXPERTS
    count = lambda cond: jnp.sum(cond.astype(jnp.int32), axis=1)
    slot_t = jnp.arange(n_tiles, dtype=jnp.int32) + count(ends[None, :] < tile_ends[:, None])
    slot_e = jnp.arange(N_EXPERTS, dtype=jnp.int32) + count(tile_ends[None, :] <= ends[:, None])
    slots = jnp.concatenate([slot_t, slot_e])
    vals = jnp.concatenate([tile_ends, ends])
    item = jnp.arange(n_items, dtype=jnp.int32)
    hi = jnp.sum(jnp.where(slots[None, :] == item[:, None], vals[None, :], 0), axis=1)
    lo = jnp.concatenate([jnp.zeros((1,), jnp.int32), hi[:-1]])
    tile_id = jnp.minimum(lo // EXPERT_TILE, n_tiles - 1)
    exp_id = jnp.minimum(count(ends[None, :] <= lo[:, None]), N_EXPERTS - 1)
    return tile_id, exp_id, lo, hi


def _experts(items, xs, w_up_g, b_up_g, w_down, b_down):
    n_items = items[0].shape[0]
    rows = lambda w, t, e, lo, hi: (t[w], 0)
    per_expert = lambda r, c: pl.BlockSpec((1, r, c), lambda w, t, e, lo, hi: (e[w], 0, 0))
    return pl.pallas_call(
        _experts_kernel,
        grid_spec=pltpu.PrefetchScalarGridSpec(
            num_scalar_prefetch=4,
            grid=(n_items,),
            in_specs=[pl.BlockSpec((EXPERT_TILE, D_MODEL), rows),
                      per_expert(D_MODEL, 2 * D_MODEL), per_expert(1, 2 * D_MODEL),
                      per_expert(D_MODEL, D_MODEL), per_expert(1, D_MODEL)],
            out_specs=pl.BlockSpec((EXPERT_TILE, D_MODEL), rows)),
        out_shape=jax.ShapeDtypeStruct(xs.shape, F32),
        compiler_params=_params("arbitrary"),
        name="experts",
    )(*items, xs, w_up_g, b_up_g, w_down, b_down)


def _combine_kernel(pos_ref, ys_ref, wts_ref, x1_ref, g2_ref, nf_ref, o_ref, buf, sem):
    def copy(t, k):
        return pltpu.make_async_copy(ys_ref.at[pl.ds(pos_ref[t * TOP_K + k], 1), :], buf.at[k, pl.ds(t, 1), :], sem)

    _row_copies(x1_ref.shape[0], copy)
    wts = wts_ref[...]
    acc = wts[:, 0:1] * buf[0]
    for k in range(1, TOP_K):
        acc = acc + wts[:, k:k + 1] * buf[k]
    o_ref[...] = _rms(x1_ref[...] + g2_ref[0] * acc, nf_ref[...])


def _combine(pos_flat, ys, wts, x1, mods, norm_final, seq):
    n_tok = x1.shape[0]
    tiles_per_sample = seq // COMBINE_TILE
    tok = lambda n: pl.BlockSpec((COMBINE_TILE, n), lambda i: (i, 0))
    return pl.pallas_call(
        _combine_kernel,
        grid=(n_tok // COMBINE_TILE,),
        in_specs=[pl.BlockSpec((COMBINE_TILE * TOP_K,), lambda i: (i,), memory_space=pltpu.SMEM),
                  pl.BlockSpec(memory_space=pl.ANY), tok(TOP_K), tok(D_MODEL),
                  pl.BlockSpec((1, 1, D_MODEL), lambda i: (i // tiles_per_sample, 0, MOD_G2)),
                  pl.BlockSpec((1, D_MODEL), lambda i: (0, 0))],
        out_specs=tok(D_MODEL),
        out_shape=jax.ShapeDtypeStruct((n_tok, D_MODEL), F32),
        scratch_shapes=[pltpu.VMEM((TOP_K, COMBINE_TILE, D_MODEL), F32), pltpu.SemaphoreType.DMA],
        compiler_params=_params("arbitrary"),
        name="combine",
    )(pos_flat, ys, wts, x1, mods, norm_final.reshape(1, -1))


def _moe(h2, idx, wts, w_up_g, b_up_g, w_down, b_down, x1, mods, norm_final):
    bsz, seq, _ = x1.shape
    n_tok = bsz * seq
    pos, counts = _route(idx.reshape(n_tok, TOP_K))
    pos_flat = pos.reshape(-1)
    xs = _dispatch(pos_flat, h2.reshape(n_tok, D_MODEL))
    ys = _experts(_work_items(counts.reshape(-1), n_tok * TOP_K), xs, w_up_g, b_up_g, w_down, b_down)
    out = _combine(pos_flat, ys, wts.reshape(n_tok, TOP_K), x1.reshape(n_tok, D_MODEL), mods, norm_final, seq)
    return out.reshape(bsz, seq, D_MODEL)


def kernel(x, c, ctx, c_ctx, w_ada, b_ada, norm_mix, norm_ffn, w_in, w_gk_f, b_gk_f, w_gk_b, b_gk_b, gla_norm, w_gla_out, hy_conv_w, hy_conv_b, hy_f_w1, hy_f_b1, hy_f_w2, hy_f_b2, hy_f_w3, hy_f_b3, hy_f_w4, hy_sin_freq, hy_bias, w_hy_out, w_out, w_router, b_router, w_up, b_up, w_down, b_down, norm_final):
    depth = w_ada.shape[0]
    assert depth == 1, "single-layer block: the context stream only feeds later layers"
    bsz, seq, _ = x.shape

    n_rows = -(-(bsz + 1) // 8) * 8
    cc = jnp.zeros((n_rows, D_MODEL), F32).at[:bsz].set(c).at[bsz].set(c_ctx)
    mods = _ada(cc, w_ada[0], b_ada[0]).reshape(n_rows, 1, N_MOD * D_MODEL)

    sizes = (GLA_DK, GLA_DK, GLA_DV, GLA_DV, GLA_GATE_RANK, GLA_GATE_RANK, 3 * HY_WIDTH, D_MODEL, D_MODEL)
    offs = np.concatenate([[0], np.cumsum(sizes)])
    w_in_b = w_in[0].astype(BF16)
    wq, wk, wv, wog, wrf, wrb, why, wma, wmb = [w_in_b[:, offs[i]:offs[i + 1]] for i in range(len(sizes))]
    q, k, v, og, r_f, r_b, hy, m_a, m_b = _inproj(
        x, mods, lambda b: b, norm_mix[0], [wq, wk, wv, wog, wrf, wrb, why, wma, wmb],
        [BF16, BF16, BF16, BF16, F32, F32, BF16, BF16, BF16], tl=512)
    k_c, v_c, rc_f, rc_b = _inproj(
        ctx, mods, lambda b: bsz, norm_mix[0], [wk, wv, wrf, wrb], [BF16, BF16, F32, F32], tl=ctx.shape[1])

    w_gk = jnp.stack([w_gk_f[0], w_gk_b[0]])
    b_gk = jnp.stack([b_gk_f[0], b_gk_b[0]])[:, None, :]
    o_f, o_b = _gla(q, k, v, r_f, r_b, k_c, v_c, rc_f, rc_b, w_gk, b_gk)

    f_tab = jnp.asarray(_dft_table(seq)).astype(BF16)
    g_tab = f_tab.T
    hs, hd, pn = _hyena_filter(seq, hy_f_w1[0], hy_f_b1[0], hy_f_w2[0], hy_f_b2[0], hy_f_w3[0], hy_f_b3[0],
                               hy_f_w4[0], hy_sin_freq[0])
    p_spec, q_spec = _hyena_spectrum(f_tab, hs, hd, pn, hy_bias[0])
    y_h = _hyena_conv(hy, hy_conv_w[0], hy_conv_b[0], f_tab, g_tab, p_spec, q_spec)

    x1, h2, idx, wts = _post(o_f, o_b, og, y_h, m_a, m_b, x, mods, gla_norm[0], norm_ffn[0],
                             w_gla_out[0].astype(BF16), w_hy_out[0].astype(BF16), w_out[0].astype(BF16),
                             w_router[0], b_router[0])

    b_up_g = b_up[0].reshape(N_EXPERTS, N_UP_GROUPS, LANES, 2).transpose(0, 1, 3, 2).reshape(N_EXPERTS, 1, -1)
    return _moe(h2, idx, wts, _deinterleave_up(w_up[0]), b_up_g, w_down[0].astype(BF16), b_down[0][:, None, :],
                x1, mods, norm_final)
```

```python
import functools
import math

import jax
import jax.numpy as jnp
import numpy as np
from jax import lax
from jax.experimental import pallas as pl
from jax.experimental.pallas import tpu as pltpu

F32 = jnp.float32
BF16 = jnp.bfloat16

D_MODEL = 1024
GRID_W = 64
EPS = 1e-6
N_MOD = 6

GLA_HEADS = 4
GLA_HEAD_K = 128
GLA_HEAD_V = 256
GLA_DK = GLA_HEADS * GLA_HEAD_K
GLA_DV = GLA_HEADS * GLA_HEAD_V
GLA_GATE_RANK = 16
GLA_GATE_NORM = 16.0
GLA_CHUNK = 64

HY_WIDTH = D_MODEL
HY_EMB = 33
HY_FAST_DECAY = 0.3
HY_SLOW_DECAY = 1.5
HY_TARGET = 1e-2

N_EXPERTS = 32
TOP_K = 4
SWIGLU_LIMIT = 7.0
SWIGLU_ALPHA = 1.702

V7X_VMEM_BYTES = 64 * 1024 * 1024
VMEM_LIMIT = V7X_VMEM_BYTES - 8 * 1024 * 1024
LANES = 128

MOD_SH1, MOD_SC1, MOD_G1, MOD_SH2, MOD_SC2, MOD_G2 = range(N_MOD)


def _params(*sem):
    return pltpu.CompilerParams(dimension_semantics=sem, vmem_limit_bytes=VMEM_LIMIT)


def _resident(shape):
    return pl.BlockSpec(shape, lambda *_: (0,) * len(shape), pipeline_mode=pl.Buffered(1))


def _dot(a, b):
    return jnp.dot(a, b, preferred_element_type=F32)


def _dot_nt(a, b):
    return lax.dot_general(a, b, (((1,), (1,)), ((), ())), preferred_element_type=F32)


def _dot_tn(a, b):
    return lax.dot_general(a, b, (((0,), (0,)), ((), ())), preferred_element_type=F32)


def _split(a):
    hi = a.astype(BF16)
    lo = (a - hi.astype(F32)).astype(BF16)
    return hi, lo


def _dot3(a, b):
    ah, al = _split(a)
    bh, bl = _split(b)
    return _dot(ah, bh) + (_dot(ah, bl) + _dot(al, bh))


def _sigmoid(x):
    return 1.0 / (1.0 + jnp.exp(-x))


def _log_sigmoid(x):
    return jnp.minimum(x, 0.0) - jnp.log(1.0 + jnp.exp(-jnp.abs(x)))


def _rms(x, w):
    return x * lax.rsqrt(jnp.mean(x * x, axis=-1, keepdims=True) + EPS) * w


def _ada_kernel(c_ref, w_ref, b_ref, o_ref):
    c = c_ref[...]
    o_ref[...] = _dot3(c * _sigmoid(c), w_ref[...]) + b_ref[...]


def _ada(cc, w_ada, b_ada):
    rows = cc.shape[0]
    return pl.pallas_call(
        _ada_kernel,
        grid=(N_MOD,),
        in_specs=[pl.BlockSpec((rows, D_MODEL), lambda j: (0, 0)),
                  pl.BlockSpec((D_MODEL, D_MODEL), lambda j: (0, j)),
                  pl.BlockSpec((1, D_MODEL), lambda j: (0, j))],
        out_specs=pl.BlockSpec((rows, D_MODEL), lambda j: (0, j)),
        out_shape=jax.ShapeDtypeStruct((rows, N_MOD * D_MODEL), F32),
        compiler_params=_params("parallel"),
        name="ada",
    )(cc, w_ada, b_ada.reshape(1, -1))


INPROJ_COL_CHUNK = 512


def _inproj_kernel(n_out, x_ref, sc_ref, sh_ref, nw_ref, *refs):
    w_refs, o_refs = refs[:n_out], refs[n_out:]
    h = _rms(x_ref[0], nw_ref[...]) * (1.0 + sc_ref[0]) + sh_ref[0]
    hb = h.astype(BF16)
    for w_ref, o_ref in zip(w_refs, o_refs):
        n = w_ref.shape[1]
        for c0 in range(0, n, INPROJ_COL_CHUNK):
            c1 = min(c0 + INPROJ_COL_CHUNK, n)
            o_ref[0, :, c0:c1] = _dot(hb, w_ref[:, c0:c1]).astype(o_ref.dtype)


def _inproj(x, mods, mod_row, norm_w, weights, out_dtypes, tl):
    bsz, seq, _ = x.shape
    n_out = len(weights)
    in_specs = [pl.BlockSpec((1, tl, D_MODEL), lambda b, i: (b, i, 0)),
                pl.BlockSpec((1, 1, D_MODEL), lambda b, i: (mod_row(b), 0, MOD_SC1)),
                pl.BlockSpec((1, 1, D_MODEL), lambda b, i: (mod_row(b), 0, MOD_SH1)),
                _resident((1, D_MODEL))]
    in_specs += [_resident(w.shape) for w in weights]
    out_specs = [pl.BlockSpec((1, tl, w.shape[1]), lambda b, i: (b, i, 0)) for w in weights]
    out_shape = [jax.ShapeDtypeStruct((bsz, seq, w.shape[1]), dt) for w, dt in zip(weights, out_dtypes)]
    return pl.pallas_call(
        functools.partial(_inproj_kernel, n_out),
        grid=(bsz, seq // tl),
        in_specs=in_specs, out_specs=out_specs, out_shape=out_shape,
        compiler_params=_params("parallel", "parallel"),
        name="inproj",
    )(x, mods, mods, norm_w.reshape(1, -1), *weights)


GLA_TILE = 256
GLA_NCH = GLA_TILE // GLA_CHUNK
GLA_SCALE = GLA_HEAD_K ** -0.5
GLA_HPS = 4
GLA_KW = GLA_HPS * GLA_HEAD_K
GLA_VW = GLA_HPS * GLA_HEAD_V


@functools.lru_cache(maxsize=None)
def _chunk_triangles():
    i = np.arange(GLA_TILE)
    same = (i[:, None] // GLA_CHUNK) == (i[None, :] // GLA_CHUNK)
    return np.stack([same & (i[None, :] <= i[:, None]), same & (i[None, :] >= i[:, None])]).astype(np.float32)


def _gla_kernel(tri_ref, qf_ref, kf_ref, vf_ref, rf_ref, qb_ref, kb_ref, vb_ref, rb_ref,
                kc_ref, vc_ref, rcf_ref, rcb_ref, wgk_ref, bgk_ref,
                of_ref, ob_ref, sf_ref, sb_ref):
    t = pl.program_id(2)
    crow = lax.broadcasted_iota(jnp.int32, (GLA_CHUNK, GLA_CHUNK), 0)
    ccol = lax.broadcasted_iota(jnp.int32, (GLA_CHUNK, GLA_CHUNK), 1)
    heads = range(GLA_HPS)

    def rows(c):
        return slice(c * GLA_CHUNK, (c + 1) * GLA_CHUNK)

    def hk(h):
        return slice(h * GLA_HEAD_K, (h + 1) * GLA_HEAD_K)

    def hv(h):
        return slice(h * GLA_HEAD_V, (h + 1) * GLA_HEAD_V)

    def order(fwd):
        return range(GLA_NCH) if fwd else range(GLA_NCH - 1, -1, -1)

    def cum_decay(r, d):
        z = _dot3(r, wgk_ref[d]) + bgk_ref[d]
        g = _log_sigmoid(z) * (1.0 / GLA_GATE_NORM)
        gh, gl = _split(g)
        return _dot(tri_ref[d], gh) + _dot(tri_ref[d], gl)

    def updates(k_ref, v_ref, b, fwd):
        out = {}
        for c in range(GLA_NCH):
            i = c * GLA_CHUNK + (GLA_CHUNK - 1 if fwd else 0)
            total = b[i:i + 1, :]
            k_upd = (k_ref[0, rows(c), :].astype(F32) * jnp.exp(total - b[rows(c)])).astype(BF16)
            dec = jnp.exp(total)
            for h in heads:
                out[c, h] = (dec[:, hk(h)], _dot_tn(v_ref[0, rows(c), hv(h)], k_upd[:, hk(h)]))
        return out

    def context_state(s_ref, r_ref, d, fwd):
        upd = updates(kc_ref, vc_ref, cum_decay(r_ref[0], d), fwd)
        for h in heads:
            st = jnp.zeros((GLA_HEAD_V, GLA_HEAD_K), F32)
            for c in order(fwd):
                dec, inc = upd[c, h]
                st = st * dec + inc
            s_ref[h] = st

    def scan(s_ref, q_ref, k_ref, v_ref, r_ref, o_ref, d, fwd):
        mask = (ccol <= crow) if fwd else (ccol >= crow)
        b = cum_decay(r_ref[0], d)
        q_dec = (q_ref[0].astype(F32) * (jnp.exp(b) * GLA_SCALE)).astype(BF16)
        k_inv = (k_ref[0].astype(F32) * jnp.exp(-b)).astype(BF16)
        upd = updates(k_ref, v_ref, b, fwd)
        intra = {}
        for c in range(GLA_NCH):
            for h in heads:
                att = jnp.where(mask, _dot_nt(q_dec[rows(c), hk(h)], k_inv[rows(c), hk(h)]), 0.0).astype(BF16)
                intra[c, h] = _dot(att, v_ref[0, rows(c), hv(h)])
        enter = {}
        for h in heads:
            st = s_ref[h]
            for c in order(fwd):
                enter[c, h] = st.astype(BF16)
                dec, inc = upd[c, h]
                st = st * dec + inc
            s_ref[h] = st
        for c in range(GLA_NCH):
            for h in heads:
                o = intra[c, h] + _dot_nt(q_dec[rows(c), hk(h)], enter[c, h])
                o_ref[0, rows(c), hv(h)] = o.astype(o_ref.dtype)

    @pl.when(t == 0)
    def _():
        context_state(sf_ref, rcf_ref, 0, True)
        context_state(sb_ref, rcb_ref, 1, False)

    scan(sf_ref, qf_ref, kf_ref, vf_ref, rf_ref, of_ref, 0, True)
    scan(sb_ref, qb_ref, kb_ref, vb_ref, rb_ref, ob_ref, 1, False)


def _gla(q, k, v, r_f, r_b, k_c, v_c, rc_f, rc_b, w_gk, b_gk):
    bsz, seq, _ = q.shape
    assert k_c.shape[1] == GLA_TILE
    nt = seq // GLA_TILE
    tri = jnp.asarray(_chunk_triangles()).astype(BF16)
    fwd = lambda b, h, t: (b, t, h)
    bwd = lambda b, h, t: (b, nt - 1 - t, h)
    fwd0 = lambda b, h, t: (b, t, 0)
    bwd0 = lambda b, h, t: (b, nt - 1 - t, 0)
    ctx = lambda b, h, t: (b, 0, h)
    ctx0 = lambda b, h, t: (b, 0, 0)
    kblk = (1, GLA_TILE, GLA_KW)
    vblk = (1, GLA_TILE, GLA_VW)
    rblk = (1, GLA_TILE, GLA_GATE_RANK)
    in_specs = [_resident(tri.shape),
                pl.BlockSpec(kblk, fwd), pl.BlockSpec(kblk, fwd), pl.BlockSpec(vblk, fwd), pl.BlockSpec(rblk, fwd0),
                pl.BlockSpec(kblk, bwd), pl.BlockSpec(kblk, bwd), pl.BlockSpec(vblk, bwd), pl.BlockSpec(rblk, bwd0),
                pl.BlockSpec(kblk, ctx), pl.BlockSpec(vblk, ctx), pl.BlockSpec(rblk, ctx0), pl.BlockSpec(rblk, ctx0),
                pl.BlockSpec((2, GLA_GATE_RANK, GLA_KW), lambda b, h, t: (0, 0, h)),
                pl.BlockSpec((2, 1, GLA_KW), lambda b, h, t: (0, 0, h))]
    out_specs = [pl.BlockSpec(vblk, fwd), pl.BlockSpec(vblk, bwd)]
    out_shape = [jax.ShapeDtypeStruct((bsz, seq, GLA_DV), BF16)] * 2
    return pl.pallas_call(
        _gla_kernel,
        grid=(bsz, GLA_HEADS // GLA_HPS, nt),
        in_specs=in_specs, out_specs=out_specs, out_shape=out_shape,
        scratch_shapes=[pltpu.VMEM((GLA_HPS, GLA_HEAD_V, GLA_HEAD_K), F32)] * 2,
        compiler_params=_params("parallel", "parallel", "arbitrary"),
        name="gla",
    )(tri, q, k, v, r_f, q, k, v, r_b, k_c, v_c, rc_f, rc_b, w_gk, b_gk)


HY_CW = 256
HY_ROW_CHUNK = 512
MLP_PAD = 128


@functools.lru_cache(maxsize=None)
def _dft_table(seq):
    n = 2 * seq
    ph = (np.arange(seq)[:, None] * np.arange(seq)[None, :]) % n
    ang = 2.0 * np.pi * np.arange(n) / n
    c = np.cos(ang)[ph]
    s = np.sin(ang)[ph]
    s[0, :] = np.where(np.arange(seq) % 2 == 0, 1.0, -1.0)
    return np.concatenate([c, s], axis=0).astype(np.float32)


@functools.lru_cache(maxsize=None)
def _filter_features(seq):
    bands = (HY_EMB - 1) // 2
    pos = np.arange(seq, dtype=np.float64)[:, None]
    t = pos / max(seq - 1, 1)
    f = np.linspace(1e-4, bands - 1, bands)[None]
    ang = (2.0 * math.pi / seq) * pos * f
    z = np.concatenate([t, np.cos(ang), -np.sin(ang)], axis=-1)
    out = np.zeros((seq, MLP_PAD), np.float32)
    out[:, :HY_EMB] = z
    deltas = np.abs(np.linspace(math.log(HY_TARGET) / HY_SLOW_DECAY, math.log(HY_TARGET) / HY_FAST_DECAY, HY_WIDTH))
    return out, deltas.astype(np.float32)[None]


def _filter_kernel(z_ref, w1_ref, b1_ref, w2_ref, b2_ref, w3_ref, b3_ref, fr_ref, w4f_ref, w4b_ref, dl_ref,
                   hs_ref, hd_ref, pn_ref):
    z = z_ref[...]
    fr = fr_ref[...]
    h = jnp.sin(fr * (_dot3(z, w1_ref[...]) + b1_ref[...]))
    h = jnp.sin(fr * (_dot3(h, w2_ref[...]) + b2_ref[...]))
    h = jnp.sin(fr * (_dot3(h, w3_ref[...]) + b3_ref[...]))
    window = jnp.exp(-z[:, 0:1] * dl_ref[...])
    h_f = _dot3(h, w4f_ref[...]) * window
    h_b = _dot3(h, w4b_ref[...]) * window
    pos = lax.broadcasted_iota(jnp.int32, (z.shape[0], 1), 0)
    h_b = jnp.where(pos == 0, 0.0, h_b)
    hs = h_f + h_b
    hs_ref[...] = hs.astype(BF16)
    hd_ref[...] = (h_b - h_f).astype(BF16)
    sign = jnp.where(pos % 2 == 0, 1.0, -1.0)
    pn_ref[...] = jnp.sum(hs * sign, axis=0, keepdims=True)


def _pad2(a, rows, cols):
    return jnp.zeros((rows, cols), F32).at[:a.shape[0], :a.shape[1]].set(a.astype(F32))


def _hyena_filter(seq, w1, b1, w2, b2, w3, b3, w4, freq):
    z_np, deltas_np = _filter_features(seq)
    p = MLP_PAD
    args = [jnp.asarray(z_np), _pad2(w1, p, p), _pad2(b1[None], 1, p), _pad2(w2, p, p), _pad2(b2[None], 1, p),
            _pad2(w3, p, p), _pad2(b3[None], 1, p), _pad2(freq[None], 1, p),
            _pad2(w4[:, :HY_WIDTH], p, HY_WIDTH), _pad2(w4[:, HY_WIDTH:], p, HY_WIDTH), jnp.asarray(deltas_np)]
    full = lambda shape: pl.BlockSpec(shape, lambda j: (0, 0))
    colblk = lambda rows: pl.BlockSpec((rows, HY_CW), lambda j: (0, j))
    in_specs = [full((seq, p)), full((p, p)), full((1, p)), full((p, p)), full((1, p)), full((p, p)), full((1, p)),
                full((1, p)), colblk(p), colblk(p), colblk(1)]
    return pl.pallas_call(
        _filter_kernel,
        grid=(HY_WIDTH // HY_CW,),
        in_specs=in_specs,
        out_specs=[colblk(seq), colblk(seq), colblk(1)],
        out_shape=[jax.ShapeDtypeStruct((seq, HY_WIDTH), BF16), jax.ShapeDtypeStruct((seq, HY_WIDTH), BF16),
                   jax.ShapeDtypeStruct((1, HY_WIDTH), F32)],
        compiler_params=_params("parallel"),
        name="hy_filter",
    )(*args)


def _spectrum_kernel(f_ref, hs_ref, hd_ref, pn_ref, bias_ref, p_ref, q_ref):
    seq = hs_ref.shape[0]
    bias = bias_ref[...]
    for r0 in range(0, seq, HY_ROW_CHUNK):
        r1 = r0 + HY_ROW_CHUNK
        p_ref[r0:r1, :] = _dot(f_ref[r0:r1, :], hs_ref[...]) + bias
        q = _dot(f_ref[seq + r0:seq + r1, :], hd_ref[...])
        if r0 == 0:
            pos = lax.broadcasted_iota(jnp.int32, (HY_ROW_CHUNK, 1), 0)
            q = jnp.where(pos == 0, pn_ref[...] + bias, q)
        q_ref[r0:r1, :] = q


def _hyena_spectrum(f_tab, hs, hd, pn, bias):
    seq = hs.shape[0]
    colblk = lambda rows: pl.BlockSpec((rows, HY_CW), lambda j: (0, j))
    return pl.pallas_call(
        _spectrum_kernel,
        grid=(HY_WIDTH // HY_CW,),
        in_specs=[_resident(f_tab.shape), colblk(seq), colblk(seq), colblk(1), colblk(1)],
        out_specs=[colblk(seq), colblk(seq)],
        out_shape=[jax.ShapeDtypeStruct((seq, HY_WIDTH), F32)] * 2,
        compiler_params=_params("parallel"),
        name="hy_spectrum",
    )(f_tab, hs, hd, pn, bias.reshape(1, -1))


def _short_conv(u, w, b):
    n = u.shape[0]
    pos = lax.broadcasted_iota(jnp.int32, (n, 1), 0) % GRID_W
    up = jnp.where(pos == 0, 0.0, pltpu.roll(u, 1, 0))
    dn = jnp.where(pos == GRID_W - 1, 0.0, pltpu.roll(u, n - 1, 0))
    return up * w[0:1] + u * w[1:2] + dn * w[2:3] + b


def _hy_fwd_kernel(f_ref, x1_ref, v_ref, w1_ref, b1_ref, wv_ref, bv_ref, p_ref, q_ref, s_ref):
    seq = x1_ref.shape[1]
    n = 2 * seq
    x1 = _short_conv(x1_ref[0].astype(F32), w1_ref[...], b1_ref[...])
    v = _short_conv(v_ref[0].astype(F32), wv_ref[...], bv_ref[...])
    u = (v * x1).astype(BF16)
    for r0 in range(0, seq, HY_ROW_CHUNK):
        r1 = r0 + HY_ROW_CHUNK
        a = _dot(f_ref[r0:r1, :], u)
        bm = _dot(f_ref[seq + r0:seq + r1, :], u)
        p = p_ref[r0:r1, :]
        q = q_ref[r0:r1, :]
        sr = (a * p + bm * q) * (2.0 / n)
        si = (bm * p - a * q) * (2.0 / n)
        if r0 == 0:
            pos = lax.broadcasted_iota(jnp.int32, (HY_ROW_CHUNK, 1), 0)
            sr = jnp.where(pos == 0, a * p * (1.0 / n), sr)
            si = jnp.where(pos == 0, bm * q * (1.0 / n), si)
        s_ref[0, r0:r1, :] = sr.astype(BF16)
        s_ref[0, seq + r0:seq + r1, :] = si.astype(BF16)


def _hy_inv_kernel(g_ref, s_ref, x0_ref, w0_ref, b0_ref, o_ref):
    seq = x0_ref.shape[1]
    x0 = _short_conv(x0_ref[0].astype(F32), w0_ref[...], b0_ref[...])
    s = s_ref[0]
    for r0 in range(0, seq, HY_ROW_CHUNK):
        r1 = r0 + HY_ROW_CHUNK
        y = _dot(g_ref[r0:r1, :], s)
        o_ref[0, r0:r1, :] = (x0[r0:r1] * y).astype(o_ref.dtype)


def _hyena_conv(hy, conv_w, conv_b, f_tab, g_tab, p_spec, q_spec):
    bsz, seq, _ = hy.shape
    nc = HY_WIDTH // HY_CW
    conv_b = conv_b.reshape(1, -1)
    sig = lambda part: pl.BlockSpec((1, seq, HY_CW), lambda j, b: (b, 0, part * nc + j))
    cw = lambda part: pl.BlockSpec((3, HY_CW), lambda j, b: (0, part * nc + j))
    cb = lambda part: pl.BlockSpec((1, HY_CW), lambda j, b: (0, part * nc + j))
    spec = pl.BlockSpec((seq, HY_CW), lambda j, b: (0, j))
    s = pl.pallas_call(
        _hy_fwd_kernel,
        grid=(nc, bsz),
        in_specs=[_resident(f_tab.shape), sig(1), sig(2), cw(1), cb(1), cw(2), cb(2), spec, spec],
        out_specs=pl.BlockSpec((1, 2 * seq, HY_CW), lambda j, b: (b, 0, j)),
        out_shape=jax.ShapeDtypeStruct((bsz, 2 * seq, HY_WIDTH), BF16),
        compiler_params=_params("parallel", "parallel"),
        name="hy_fwd",
    )(f_tab, hy, hy, conv_w, conv_b, conv_w, conv_b, p_spec, q_spec)
    return pl.pallas_call(
        _hy_inv_kernel,
        grid=(nc, bsz),
        in_specs=[_resident(g_tab.shape), pl.BlockSpec((1, 2 * seq, HY_CW), lambda j, b: (b, 0, j)),
                  sig(0), cw(0), cb(0)],
        out_specs=pl.BlockSpec((1, seq, HY_CW), lambda j, b: (b, 0, j)),
        out_shape=jax.ShapeDtypeStruct((bsz, seq, HY_WIDTH), BF16),
        compiler_params=_params("parallel", "parallel"),
        name="hy_inv",
    )(g_tab, s, hy, conv_w, conv_b)


ROW_PARTS = D_MODEL // LANES


def _token_tile(t):
    return pl.ds(pl.multiple_of(t * ROW_PARTS, ROW_PARTS), ROW_PARTS)


def _load_token_rows(ref, n_tok):
    return jnp.concatenate([ref[pl.ds(j, n_tok, stride=ROW_PARTS), :] for j in range(ROW_PARTS)], axis=-1)


def _store_token_rows(ref, val):
    for j in range(ROW_PARTS):
        ref[pl.ds(j, val.shape[0], stride=ROW_PARTS), :] = val[:, j * LANES:(j + 1) * LANES]


POST_TILE = 512


def _post_kernel(of_ref, ob_ref, og_ref, yh_ref, ma_ref, mb_ref, x_ref, g1_ref, sc2_ref, sh2_ref,
                 gn_ref, nf_ref, wa_ref, wb_ref, wo_ref, wr_ref, br_ref,
                 x1_ref, h2_ref, idx_ref, wts_ref):
    o = of_ref[0].astype(F32) + ob_ref[0].astype(F32)
    gn = gn_ref[...]
    heads = [_rms(o[:, h * GLA_HEAD_V:(h + 1) * GLA_HEAD_V], gn) for h in range(GLA_HEADS)]
    og = og_ref[0].astype(F32)
    a_in = jnp.concatenate(heads, axis=-1) * (og * _sigmoid(og))
    branch_a = _dot(a_in.astype(BF16), wa_ref[...])
    branch_b = _dot(yh_ref[0], wb_ref[...])
    y = _sigmoid(ma_ref[0].astype(F32)) * branch_a + _sigmoid(mb_ref[0].astype(F32)) * branch_b
    x1 = x_ref[0] + g1_ref[0] * _dot(y.astype(BF16), wo_ref[...])
    x1_ref[0] = x1
    h2 = _rms(x1, nf_ref[...]) * (1.0 + sc2_ref[0]) + sh2_ref[0]
    _store_token_rows(h2_ref.at[0], h2)

    logits = _dot3(h2, wr_ref[...]) + br_ref[...]
    lane = lax.broadcasted_iota(jnp.int32, logits.shape, 1)
    slot = lax.broadcasted_iota(jnp.int32, (logits.shape[0], TOP_K), 1)
    work = logits
    tops = []
    idx = jnp.zeros((logits.shape[0], TOP_K), jnp.int32)
    for k in range(TOP_K):
        m = jnp.max(work, axis=-1, keepdims=True)
        first = jnp.min(jnp.where(work == m, lane, N_EXPERTS), axis=-1, keepdims=True)
        tops.append(m)
        idx = jnp.where(slot == k, first, idx)
        work = jnp.where(lane == first, -jnp.inf, work)
    es = [jnp.exp(m - tops[0]) for m in tops]
    inv = 1.0 / functools.reduce(lambda a, b: a + b, es)
    wts = jnp.zeros((logits.shape[0], TOP_K), F32)
    for k, e in enumerate(es):
        wts = jnp.where(slot == k, e * inv, wts)
    idx_ref[0] = idx
    wts_ref[0] = wts


def _post(o_f, o_b, og, y_h, m_a, m_b, x, mods, gla_norm, norm_ffn, w_a, w_b, w_o, w_router, b_router):
    bsz, seq, _ = x.shape
    tok = lambda n: pl.BlockSpec((1, POST_TILE, n), lambda b, i: (b, i, 0))
    mod = lambda which: pl.BlockSpec((1, 1, D_MODEL), lambda b, i: (b, 0, which))
    in_specs = [tok(D_MODEL)] * 7 + [mod(MOD_G1), mod(MOD_SC2), mod(MOD_SH2),
                                     _resident((1, GLA_HEAD_V)), _resident((1, D_MODEL)),
                                     _resident(w_a.shape), _resident(w_b.shape), _resident(w_o.shape),
                                     _resident(w_router.shape), _resident((1, N_EXPERTS))]
    return pl.pallas_call(
        _post_kernel,
        grid=(bsz, seq // POST_TILE),
        in_specs=in_specs,
        out_specs=[tok(D_MODEL), pl.BlockSpec((1, POST_TILE * ROW_PARTS, LANES), lambda b, i: (b, i, 0)),
                   tok(TOP_K), tok(TOP_K)],
        out_shape=[jax.ShapeDtypeStruct((bsz, seq, D_MODEL), F32),
                   jax.ShapeDtypeStruct((bsz, seq * ROW_PARTS, LANES), F32),
                   jax.ShapeDtypeStruct((bsz, seq, TOP_K), jnp.int32), jax.ShapeDtypeStruct((bsz, seq, TOP_K), F32)],
        compiler_params=_params("parallel", "parallel"),
        name="post",
    )(o_f, o_b, og, y_h, m_a, m_b, x, mods, mods, mods, gla_norm.reshape(1, -1), norm_ffn.reshape(1, -1),
      w_a, w_b, w_o, w_router, b_router.reshape(1, -1))


UP_GROUP = 2 * LANES
N_UP_GROUPS = 2 * D_MODEL // UP_GROUP


def _deinterleave_kernel(w_ref, o_ref):
    r = lax.broadcasted_iota(jnp.int32, (UP_GROUP, UP_GROUP), 0)
    c = lax.broadcasted_iota(jnp.int32, (UP_GROUP, UP_GROUP), 1)
    perm = jnp.where(c == (r % 2) * LANES + r // 2, 1.0, 0.0).astype(BF16)
    for g in range(w_ref.shape[2] // UP_GROUP):
        cols = slice(g * UP_GROUP, (g + 1) * UP_GROUP)
        o_ref[0, :, cols] = _dot(w_ref[0, :, cols].astype(BF16), perm).astype(BF16)


def _deinterleave_up(w_up):
    n_exp, d_in, d_up = w_up.shape
    blk = pl.BlockSpec((1, d_in, d_up // 2), lambda e, j: (e, 0, j))
    return pl.pallas_call(
        _deinterleave_kernel,
        grid=(n_exp, 2),
        in_specs=[blk], out_specs=blk,
        out_shape=jax.ShapeDtypeStruct(w_up.shape, BF16),
        compiler_params=_params("parallel", "parallel"),
        name="deinterleave",
    )(w_up)


ROUTE_TILE = 512
EXPERT_TILE = 512
COMBINE_TILE = 256


def _exact_count_dot(a, m):
    a0 = a.astype(BF16)
    r1 = a - a0.astype(F32)
    a1 = r1.astype(BF16)
    a2 = (r1 - a1.astype(F32)).astype(BF16)
    return _dot(a0, m) + (_dot(a1, m) + _dot(a2, m))


def _route_kernel(idx_ref, pos_ref, cnt_ref, counts, base, offs):
    p = pl.program_id(0)
    i = pl.program_id(1)
    n_tok = idx_ref.shape[0]
    idx = idx_ref[...]
    lane = lax.broadcasted_iota(jnp.int32, (n_tok, N_EXPERTS), 1)
    hot = jnp.zeros((n_tok, N_EXPERTS), F32)
    for k in range(TOP_K):
        hot = hot + jnp.where(lane == idx[:, k:k + 1], 1.0, 0.0)
    colsum = jnp.sum(hot, axis=0, keepdims=True)

    @pl.when((p == 0) & (i == 0))
    def _():
        counts[...] = jnp.zeros_like(counts)

    @pl.when(p == 0)
    def _():
        counts[...] += colsum

    @pl.when((p == 1) & (i == 0))
    def _():
        r = lax.broadcasted_iota(jnp.int32, (N_EXPERTS, N_EXPERTS), 0)
        c = lax.broadcasted_iota(jnp.int32, (N_EXPERTS, N_EXPERTS), 1)
        before = jnp.where(r < c, 1.0, 0.0).astype(BF16)
        offs[...] = _exact_count_dot(jnp.broadcast_to(counts[...], (8, N_EXPERTS)), before)[0:1]
        base[...] = jnp.zeros_like(base)

    @pl.when(p == 1)
    def _():
        r = lax.broadcasted_iota(jnp.int32, (n_tok, n_tok), 0)
        c = lax.broadcasted_iota(jnp.int32, (n_tok, n_tok), 1)
        earlier = jnp.where(c < r, 1.0, 0.0).astype(BF16)
        dense = offs[...] + base[...] + _dot(earlier, hot.astype(BF16))
        slot = lax.broadcasted_iota(jnp.int32, (n_tok, TOP_K), 1)
        pos = jnp.zeros((n_tok, TOP_K), F32)
        for k in range(TOP_K):
            mine = jnp.sum(jnp.where(lane == idx[:, k:k + 1], dense, 0.0), axis=-1, keepdims=True)
            pos = jnp.where(slot == k, mine, pos)
        pos_ref[...] = pos.astype(jnp.int32)
        base[...] += colsum

    cnt_ref[...] = counts[...].astype(jnp.int32)


def _route(idx):
    n_tok = idx.shape[0]
    return pl.pallas_call(
        _route_kernel,
        grid=(2, n_tok // ROUTE_TILE),
        in_specs=[pl.BlockSpec((ROUTE_TILE, TOP_K), lambda p, i: (i, 0))],
        out_specs=[pl.BlockSpec((ROUTE_TILE, TOP_K), lambda p, i: (i * p, 0)),
                   pl.BlockSpec((1, N_EXPERTS), lambda p, i: (0, 0))],
        out_shape=[jax.ShapeDtypeStruct((n_tok, TOP_K), jnp.int32), jax.ShapeDtypeStruct((1, N_EXPERTS), jnp.int32)],
        scratch_shapes=[pltpu.VMEM((1, N_EXPERTS), F32)] * 3,
        compiler_params=_params("arbitrary", "arbitrary"),
        name="route",
    )(idx)


def _row_copies(n_tok, copy):
    def issue(t, carry):
        for k in range(TOP_K):
            copy(t, k).start(priority=k % 2)
        return carry

    def drain(t, carry):
        for k in range(TOP_K):
            copy(t, k).wait()
        return carry

    lax.fori_loop(0, n_tok, issue, 0, unroll=4)
    lax.fori_loop(0, n_tok, drain, 0, unroll=4)


def _dispatch_kernel(pos_ref, h_ref, xs_ref, sem):
    def copy(t, k):
        return pltpu.make_async_copy(h_ref.at[_token_tile(t), :], xs_ref.at[_token_tile(pos_ref[t * TOP_K + k]), :], sem)

    _row_copies(h_ref.shape[0] // ROW_PARTS, copy)


def _dispatch(pos_flat, h2):
    n_tok = h2.shape[0] // ROW_PARTS
    return pl.pallas_call(
        _dispatch_kernel,
        grid=(n_tok // ROUTE_TILE,),
        in_specs=[pl.BlockSpec((ROUTE_TILE * TOP_K,), lambda i: (i,), memory_space=pltpu.SMEM),
                  pl.BlockSpec((ROUTE_TILE * ROW_PARTS, LANES), lambda i: (i, 0))],
        out_specs=pl.BlockSpec(memory_space=pl.ANY),
        out_shape=jax.ShapeDtypeStruct((n_tok * TOP_K * ROW_PARTS, LANES), F32),
        scratch_shapes=[pltpu.SemaphoreType.DMA],
        compiler_params=_params("arbitrary"),
        name="dispatch",
    )(pos_flat, h2)


def _experts_kernel(tile_ref, exp_ref, lo_ref, hi_ref, x_ref, wu_ref, bu_ref, wd_ref, bd_ref, o_ref):
    w = pl.program_id(0)
    lo = lo_ref[w]
    hi = hi_ref[w]
    start = tile_ref[w] * EXPERT_TILE

    @pl.when(hi > lo)
    def _():
        h = _load_token_rows(x_ref, EXPERT_TILE).astype(BF16)
        acts = []
        for g in range(N_UP_GROUPS):
            cols = slice(g * UP_GROUP, (g + 1) * UP_GROUP)
            u = _dot(h, wu_ref[0, :, cols]) + bu_ref[0, :, cols]
            glu = jnp.minimum(u[:, :LANES], SWIGLU_LIMIT)
            lin = jnp.clip(u[:, LANES:], -SWIGLU_LIMIT, SWIGLU_LIMIT)
            acts.append((glu * _sigmoid(SWIGLU_ALPHA * glu) * (lin + 1.0)).astype(BF16))
        y = _dot(jnp.concatenate(acts, axis=-1), wd_ref[0]) + bd_ref[0]
        row = start + lax.broadcasted_iota(jnp.int32, (EXPERT_TILE, 1), 0)
        mine = (row >= lo) & (row < hi)

        @pl.when(lo == start)
        def _():
            _store_token_rows(o_ref, jnp.where(mine, y, 0.0))

        @pl.when(lo != start)
        def _():
            _store_token_rows(o_ref, jnp.where(mine, y, _load_token_rows(o_ref, EXPERT_TILE)))


def _work_items(counts, n_rows):
    n_tiles = n_rows // EXPERT_TILE
    ends = jnp.cumsum(counts)
    tile_ends = jnp.arange(1, n_tiles + 1, dtype=jnp.int32) * EXPERT_TILE
    n_items = n_tiles + N_EXPERTS
    count = lambda cond: jnp.sum(cond.astype(jnp.int32), axis=1)
    slot_t = jnp.arange(n_tiles, dtype=jnp.int32) + count(ends[None, :] < tile_ends[:, None])
    slot_e = jnp.arange(N_EXPERTS, dtype=jnp.int32) + count(tile_ends[None, :] <= ends[:, None])
    slots = jnp.concatenate([slot_t, slot_e])
    vals = jnp.concatenate([tile_ends, ends])
    item = jnp.arange(n_items, dtype=jnp.int32)
    hi = jnp.sum(jnp.where(slots[None, :] == item[:, None], vals[None, :], 0), axis=1)
    lo = jnp.concatenate([jnp.zeros((1,), jnp.int32), hi[:-1]])
    tile_id = jnp.minimum(lo // EXPERT_TILE, n_tiles - 1)
    exp_id = jnp.minimum(count(ends[None, :] <= lo[:, None]), N_EXPERTS - 1)
    return tile_id, exp_id, lo, hi


def _experts(items, xs, w_up_g, b_up_g, w_down, b_down):
    n_items = items[0].shape[0]
    rows = lambda w, t, e, lo, hi: (t[w], 0)
    per_expert = lambda r, c: pl.BlockSpec((1, r, c), lambda w, t, e, lo, hi: (e[w], 0, 0))
    return pl.pallas_call(
        _experts_kernel,
        grid_spec=pltpu.PrefetchScalarGridSpec(
            num_scalar_prefetch=4,
            grid=(n_items,),
            in_specs=[pl.BlockSpec((EXPERT_TILE * ROW_PARTS, LANES), rows),
                      per_expert(D_MODEL, 2 * D_MODEL), per_expert(1, 2 * D_MODEL),
                      per_expert(D_MODEL, D_MODEL), per_expert(1, D_MODEL)],
            out_specs=pl.BlockSpec((EXPERT_TILE * ROW_PARTS, LANES), rows)),
        out_shape=jax.ShapeDtypeStruct(xs.shape, F32),
        compiler_params=_params("arbitrary"),
        name="experts",
    )(*items, xs, w_up_g, b_up_g, w_down, b_down)


def _combine_kernel(pos_ref, ys_ref, wts_ref, x1_ref, g2_ref, nf_ref, o_ref, buf, sem):
    def copy(t, k):
        return pltpu.make_async_copy(ys_ref.at[_token_tile(pos_ref[t * TOP_K + k]), :], buf.at[k, _token_tile(t), :], sem)

    n_tok = x1_ref.shape[0]
    _row_copies(n_tok, copy)
    wts = wts_ref[...]
    acc = wts[:, 0:1] * _load_token_rows(buf.at[0], n_tok)
    for k in range(1, TOP_K):
        acc = acc + wts[:, k:k + 1] * _load_token_rows(buf.at[k], n_tok)
    o_ref[...] = _rms(x1_ref[...] + g2_ref[0] * acc, nf_ref[...])


def _combine(pos_flat, ys, wts, x1, mods, norm_final, seq):
    n_tok = x1.shape[0]
    tiles_per_sample = seq // COMBINE_TILE
    tok = lambda n: pl.BlockSpec((COMBINE_TILE, n), lambda i: (i, 0))
    return pl.pallas_call(
        _combine_kernel,
        grid=(n_tok // COMBINE_TILE,),
        in_specs=[pl.BlockSpec((COMBINE_TILE * TOP_K,), lambda i: (i,), memory_space=pltpu.SMEM),
                  pl.BlockSpec(memory_space=pl.ANY), tok(TOP_K), tok(D_MODEL),
                  pl.BlockSpec((1, 1, D_MODEL), lambda i: (i // tiles_per_sample, 0, MOD_G2)),
                  pl.BlockSpec((1, D_MODEL), lambda i: (0, 0))],
        out_specs=tok(D_MODEL),
        out_shape=jax.ShapeDtypeStruct((n_tok, D_MODEL), F32),
        scratch_shapes=[pltpu.VMEM((TOP_K, COMBINE_TILE * ROW_PARTS, LANES), F32), pltpu.SemaphoreType.DMA],
        compiler_params=_params("arbitrary"),
        name="combine",
    )(pos_flat, ys, wts, x1, mods, norm_final.reshape(1, -1))


def _moe(h2, idx, wts, w_up_g, b_up_g, w_down, b_down, x1, mods, norm_final):
    bsz, seq, _ = x1.shape
    n_tok = bsz * seq
    pos, counts = _route(idx.reshape(n_tok, TOP_K))
    pos_flat = pos.reshape(-1)
    xs = _dispatch(pos_flat, h2.reshape(n_tok * ROW_PARTS, LANES))
    ys = _experts(_work_items(counts.reshape(-1), n_tok * TOP_K), xs, w_up_g, b_up_g, w_down, b_down)
    out = _combine(pos_flat, ys, wts.reshape(n_tok, TOP_K), x1.reshape(n_tok, D_MODEL), mods, norm_final, seq)
    return out.reshape(bsz, seq, D_MODEL)


def kernel(x, c, ctx, c_ctx, w_ada, b_ada, norm_mix, norm_ffn, w_in, w_gk_f, b_gk_f, w_gk_b, b_gk_b, gla_norm, w_gla_out, hy_conv_w, hy_conv_b, hy_f_w1, hy_f_b1, hy_f_w2, hy_f_b2, hy_f_w3, hy_f_b3, hy_f_w4, hy_sin_freq, hy_bias, w_hy_out, w_out, w_router, b_router, w_up, b_up, w_down, b_down, norm_final):
    depth = w_ada.shape[0]
    assert depth == 1, "single-layer block: the context stream only feeds later layers"
    bsz, seq, _ = x.shape

    n_rows = -(-(bsz + 1) // 8) * 8
    cc = jnp.zeros((n_rows, D_MODEL), F32).at[:bsz].set(c).at[bsz].set(c_ctx)
    mods = _ada(cc, w_ada[0], b_ada[0]).reshape(n_rows, 1, N_MOD * D_MODEL)

    sizes = (GLA_DK, GLA_DK, GLA_DV, GLA_DV, GLA_GATE_RANK, GLA_GATE_RANK, 3 * HY_WIDTH, D_MODEL, D_MODEL)
    offs = np.concatenate([[0], np.cumsum(sizes)])
    w_in_b = w_in[0].astype(BF16)
    wq, wk, wv, wog, wrf, wrb, why, wma, wmb = [w_in_b[:, offs[i]:offs[i + 1]] for i in range(len(sizes))]
    q, k, v, og, r_f, r_b, hy, m_a, m_b = _inproj(
        x, mods, lambda b: b, norm_mix[0], [wq, wk, wv, wog, wrf, wrb, why, wma, wmb],
        [BF16, BF16, BF16, BF16, F32, F32, BF16, BF16, BF16], tl=512)
    k_c, v_c, rc_f, rc_b = _inproj(
        ctx, mods, lambda b: bsz, norm_mix[0], [wk, wv, wrf, wrb], [BF16, BF16, F32, F32], tl=ctx.shape[1])

    w_gk = jnp.stack([w_gk_f[0], w_gk_b[0]])
    b_gk = jnp.stack([b_gk_f[0], b_gk_b[0]])[:, None, :]
    o_f, o_b = _gla(q, k, v, r_f, r_b, k_c, v_c, rc_f, rc_b, w_gk, b_gk)

    f_tab = jnp.asarray(_dft_table(seq)).astype(BF16)
    g_tab = f_tab.T
    hs, hd, pn = _hyena_filter(seq, hy_f_w1[0], hy_f_b1[0], hy_f_w2[0], hy_f_b2[0], hy_f_w3[0], hy_f_b3[0],
                               hy_f_w4[0], hy_sin_freq[0])
    p_spec, q_spec = _hyena_spectrum(f_tab, hs, hd, pn, hy_bias[0])
    y_h = _hyena_conv(hy, hy_conv_w[0], hy_conv_b[0], f_tab, g_tab, p_spec, q_spec)

    x1, h2, idx, wts = _post(o_f, o_b, og, y_h, m_a, m_b, x, mods, gla_norm[0], norm_ffn[0],
                             w_gla_out[0].astype(BF16), w_hy_out[0].astype(BF16), w_out[0].astype(BF16),
                             w_router[0], b_router[0])

    b_up_g = b_up[0].reshape(N_EXPERTS, N_UP_GROUPS, LANES, 2).transpose(0, 1, 3, 2).reshape(N_EXPERTS, 1, -1)
    return _moe(h2, idx, wts, _deinterleave_up(w_up[0]), b_up_g, w_down[0].astype(BF16), b_down[0][:, None, :],
                x1, mods, norm_final)
```

```python
import functools
import math

import jax
import jax.numpy as jnp
import numpy as np
from jax import lax
from jax.experimental import pallas as pl
from jax.experimental.pallas import tpu as pltpu

F32 = jnp.float32
BF16 = jnp.bfloat16

D_MODEL = 1024
GRID_W = 64
EPS = 1e-6
N_MOD = 6

GLA_HEADS = 4
GLA_HEAD_K = 128
GLA_HEAD_V = 256
GLA_DK = GLA_HEADS * GLA_HEAD_K
GLA_DV = GLA_HEADS * GLA_HEAD_V
GLA_GATE_RANK = 16
GLA_GATE_NORM = 16.0
GLA_CHUNK = 64

HY_WIDTH = D_MODEL
HY_EMB = 33
HY_FAST_DECAY = 0.3
HY_SLOW_DECAY = 1.5
HY_TARGET = 1e-2

N_EXPERTS = 32
TOP_K = 4
SWIGLU_LIMIT = 7.0
SWIGLU_ALPHA = 1.702

V7X_VMEM_BYTES = 64 * 1024 * 1024
VMEM_LIMIT = V7X_VMEM_BYTES - 8 * 1024 * 1024
LANES = 128

MOD_SH1, MOD_SC1, MOD_G1, MOD_SH2, MOD_SC2, MOD_G2 = range(N_MOD)


def _params(*sem):
    return pltpu.CompilerParams(dimension_semantics=sem, vmem_limit_bytes=VMEM_LIMIT)


def _resident(shape):
    return pl.BlockSpec(shape, lambda *_: (0,) * len(shape), pipeline_mode=pl.Buffered(1))


def _dot(a, b):
    return jnp.dot(a, b, preferred_element_type=F32)


def _dot_nt(a, b):
    return lax.dot_general(a, b, (((1,), (1,)), ((), ())), preferred_element_type=F32)


def _dot_tn(a, b):
    return lax.dot_general(a, b, (((0,), (0,)), ((), ())), preferred_element_type=F32)


def _split(a):
    hi = a.astype(BF16)
    lo = (a - hi.astype(F32)).astype(BF16)
    return hi, lo


def _dot3(a, b):
    ah, al = _split(a)
    bh, bl = _split(b)
    return _dot(ah, bh) + (_dot(ah, bl) + _dot(al, bh))


def _sigmoid(x):
    return 1.0 / (1.0 + jnp.exp(-x))


def _log_sigmoid(x):
    return jnp.minimum(x, 0.0) - jnp.log(1.0 + jnp.exp(-jnp.abs(x)))


def _rms(x, w):
    return x * lax.rsqrt(jnp.mean(x * x, axis=-1, keepdims=True) + EPS) * w


def _ada_kernel(c_ref, w_ref, b_ref, o_ref):
    c = c_ref[...]
    o_ref[...] = _dot3(c * _sigmoid(c), w_ref[...]) + b_ref[...]


def _ada(cc, w_ada, b_ada):
    rows = cc.shape[0]
    return pl.pallas_call(
        _ada_kernel,
        grid=(N_MOD,),
        in_specs=[pl.BlockSpec((rows, D_MODEL), lambda j: (0, 0)),
                  pl.BlockSpec((D_MODEL, D_MODEL), lambda j: (0, j)),
                  pl.BlockSpec((1, D_MODEL), lambda j: (0, j))],
        out_specs=pl.BlockSpec((rows, D_MODEL), lambda j: (0, j)),
        out_shape=jax.ShapeDtypeStruct((rows, N_MOD * D_MODEL), F32),
        compiler_params=_params("parallel"),
        name="ada",
    )(cc, w_ada, b_ada.reshape(1, -1))


INPROJ_COL_CHUNK = 512


def _inproj_kernel(n_out, x_ref, sc_ref, sh_ref, nw_ref, *refs):
    w_refs, o_refs = refs[:n_out], refs[n_out:]
    h = _rms(x_ref[0], nw_ref[...]) * (1.0 + sc_ref[0]) + sh_ref[0]
    hb = h.astype(BF16)
    for w_ref, o_ref in zip(w_refs, o_refs):
        n = w_ref.shape[1]
        for c0 in range(0, n, INPROJ_COL_CHUNK):
            c1 = min(c0 + INPROJ_COL_CHUNK, n)
            o_ref[0, :, c0:c1] = _dot(hb, w_ref[:, c0:c1]).astype(o_ref.dtype)


def _inproj(x, mods, mod_row, norm_w, weights, out_dtypes, tl):
    bsz, seq, _ = x.shape
    n_out = len(weights)
    in_specs = [pl.BlockSpec((1, tl, D_MODEL), lambda b, i: (b, i, 0)),
                pl.BlockSpec((1, 1, D_MODEL), lambda b, i: (mod_row(b), 0, MOD_SC1)),
                pl.BlockSpec((1, 1, D_MODEL), lambda b, i: (mod_row(b), 0, MOD_SH1)),
                _resident((1, D_MODEL))]
    in_specs += [_resident(w.shape) for w in weights]
    out_specs = [pl.BlockSpec((1, tl, w.shape[1]), lambda b, i: (b, i, 0)) for w in weights]
    out_shape = [jax.ShapeDtypeStruct((bsz, seq, w.shape[1]), dt) for w, dt in zip(weights, out_dtypes)]
    return pl.pallas_call(
        functools.partial(_inproj_kernel, n_out),
        grid=(bsz, seq // tl),
        in_specs=in_specs, out_specs=out_specs, out_shape=out_shape,
        compiler_params=_params("parallel", "parallel"),
        name="inproj",
    )(x, mods, mods, norm_w.reshape(1, -1), *weights)


GLA_TILE = 256
GLA_NCH = GLA_TILE // GLA_CHUNK
GLA_SCALE = GLA_HEAD_K ** -0.5
GLA_HPS = 4
GLA_KW = GLA_HPS * GLA_HEAD_K
GLA_VW = GLA_HPS * GLA_HEAD_V


@functools.lru_cache(maxsize=None)
def _chunk_triangles():
    i = np.arange(GLA_TILE)
    same = (i[:, None] // GLA_CHUNK) == (i[None, :] // GLA_CHUNK)
    return np.stack([same & (i[None, :] <= i[:, None]), same & (i[None, :] >= i[:, None])]).astype(np.float32)


def _gla_kernel(tri_ref, qf_ref, kf_ref, vf_ref, rf_ref, qb_ref, kb_ref, vb_ref, rb_ref,
                kc_ref, vc_ref, rcf_ref, rcb_ref, wgk_ref, bgk_ref,
                of_ref, ob_ref, sf_ref, sb_ref):
    t = pl.program_id(2)
    crow = lax.broadcasted_iota(jnp.int32, (GLA_CHUNK, GLA_CHUNK), 0)
    ccol = lax.broadcasted_iota(jnp.int32, (GLA_CHUNK, GLA_CHUNK), 1)
    heads = range(GLA_HPS)

    def rows(c):
        return slice(c * GLA_CHUNK, (c + 1) * GLA_CHUNK)

    def hk(h):
        return slice(h * GLA_HEAD_K, (h + 1) * GLA_HEAD_K)

    def hv(h):
        return slice(h * GLA_HEAD_V, (h + 1) * GLA_HEAD_V)

    def order(fwd):
        return range(GLA_NCH) if fwd else range(GLA_NCH - 1, -1, -1)

    def cum_decay(r, d):
        z = _dot3(r, wgk_ref[d]) + bgk_ref[d]
        g = _log_sigmoid(z) * (1.0 / GLA_GATE_NORM)
        gh, gl = _split(g)
        return _dot(tri_ref[d], gh) + _dot(tri_ref[d], gl)

    def updates(k_ref, v_ref, b, fwd):
        out = {}
        for c in range(GLA_NCH):
            i = c * GLA_CHUNK + (GLA_CHUNK - 1 if fwd else 0)
            total = b[i:i + 1, :]
            k_upd = (k_ref[0, rows(c), :].astype(F32) * jnp.exp(total - b[rows(c)])).astype(BF16)
            dec = jnp.exp(total)
            for h in heads:
                out[c, h] = (dec[:, hk(h)], _dot_tn(v_ref[0, rows(c), hv(h)], k_upd[:, hk(h)]))
        return out

    def context_state(s_ref, r_ref, d, fwd):
        upd = updates(kc_ref, vc_ref, cum_decay(r_ref[0], d), fwd)
        for h in heads:
            st = jnp.zeros((GLA_HEAD_V, GLA_HEAD_K), F32)
            for c in order(fwd):
                dec, inc = upd[c, h]
                st = st * dec + inc
            s_ref[h] = st

    def scan(s_ref, q_ref, k_ref, v_ref, r_ref, o_ref, d, fwd):
        mask = (ccol <= crow) if fwd else (ccol >= crow)
        b = cum_decay(r_ref[0], d)
        q_dec = (q_ref[0].astype(F32) * (jnp.exp(b) * GLA_SCALE)).astype(BF16)
        k_inv = (k_ref[0].astype(F32) * jnp.exp(-b)).astype(BF16)
        upd = updates(k_ref, v_ref, b, fwd)
        intra = {}
        for c in range(GLA_NCH):
            for h in heads:
                att = jnp.where(mask, _dot_nt(q_dec[rows(c), hk(h)], k_inv[rows(c), hk(h)]), 0.0).astype(BF16)
                intra[c, h] = _dot(att, v_ref[0, rows(c), hv(h)])
        enter = {}
        for h in heads:
            st = s_ref[h]
            for c in order(fwd):
                enter[c, h] = st.astype(BF16)
                dec, inc = upd[c, h]
                st = st * dec + inc
            s_ref[h] = st
        for c in range(GLA_NCH):
            for h in heads:
                o = intra[c, h] + _dot_nt(q_dec[rows(c), hk(h)], enter[c, h])
                o_ref[0, rows(c), hv(h)] = o.astype(o_ref.dtype)

    @pl.when(t == 0)
    def _():
        context_state(sf_ref, rcf_ref, 0, True)
        context_state(sb_ref, rcb_ref, 1, False)

    scan(sf_ref, qf_ref, kf_ref, vf_ref, rf_ref, of_ref, 0, True)
    scan(sb_ref, qb_ref, kb_ref, vb_ref, rb_ref, ob_ref, 1, False)


def _gla(q, k, v, r_f, r_b, k_c, v_c, rc_f, rc_b, w_gk, b_gk):
    bsz, seq, _ = q.shape
    assert k_c.shape[1] == GLA_TILE
    nt = seq // GLA_TILE
    tri = jnp.asarray(_chunk_triangles()).astype(BF16)
    fwd = lambda b, h, t: (b, t, h)
    bwd = lambda b, h, t: (b, nt - 1 - t, h)
    fwd0 = lambda b, h, t: (b, t, 0)
    bwd0 = lambda b, h, t: (b, nt - 1 - t, 0)
    ctx = lambda b, h, t: (b, 0, h)
    ctx0 = lambda b, h, t: (b, 0, 0)
    kblk = (1, GLA_TILE, GLA_KW)
    vblk = (1, GLA_TILE, GLA_VW)
    rblk = (1, GLA_TILE, GLA_GATE_RANK)
    in_specs = [_resident(tri.shape),
                pl.BlockSpec(kblk, fwd), pl.BlockSpec(kblk, fwd), pl.BlockSpec(vblk, fwd), pl.BlockSpec(rblk, fwd0),
                pl.BlockSpec(kblk, bwd), pl.BlockSpec(kblk, bwd), pl.BlockSpec(vblk, bwd), pl.BlockSpec(rblk, bwd0),
                pl.BlockSpec(kblk, ctx), pl.BlockSpec(vblk, ctx), pl.BlockSpec(rblk, ctx0), pl.BlockSpec(rblk, ctx0),
                pl.BlockSpec((2, GLA_GATE_RANK, GLA_KW), lambda b, h, t: (0, 0, h)),
                pl.BlockSpec((2, 1, GLA_KW), lambda b, h, t: (0, 0, h))]
    out_specs = [pl.BlockSpec(vblk, fwd), pl.BlockSpec(vblk, bwd)]
    out_shape = [jax.ShapeDtypeStruct((bsz, seq, GLA_DV), BF16)] * 2
    return pl.pallas_call(
        _gla_kernel,
        grid=(bsz, GLA_HEADS // GLA_HPS, nt),
        in_specs=in_specs, out_specs=out_specs, out_shape=out_shape,
        scratch_shapes=[pltpu.VMEM((GLA_HPS, GLA_HEAD_V, GLA_HEAD_K), F32)] * 2,
        compiler_params=_params("parallel", "parallel", "arbitrary"),
        name="gla",
    )(tri, q, k, v, r_f, q, k, v, r_b, k_c, v_c, rc_f, rc_b, w_gk, b_gk)


HY_CW = 256
HY_ROW_CHUNK = 512
MLP_PAD = 128


@functools.lru_cache(maxsize=None)
def _dft_table(seq):
    n = 2 * seq
    ph = (np.arange(seq)[:, None] * np.arange(seq)[None, :]) % n
    ang = 2.0 * np.pi * np.arange(n) / n
    c = np.cos(ang)[ph]
    s = np.sin(ang)[ph]
    s[0, :] = np.where(np.arange(seq) % 2 == 0, 1.0, -1.0)
    return np.concatenate([c, s], axis=0).astype(np.float32)


@functools.lru_cache(maxsize=None)
def _filter_features(seq):
    bands = (HY_EMB - 1) // 2
    pos = np.arange(seq, dtype=np.float64)[:, None]
    t = pos / max(seq - 1, 1)
    f = np.linspace(1e-4, bands - 1, bands)[None]
    ang = (2.0 * math.pi / seq) * pos * f
    z = np.concatenate([t, np.cos(ang), -np.sin(ang)], axis=-1)
    out = np.zeros((seq, MLP_PAD), np.float32)
    out[:, :HY_EMB] = z
    deltas = np.abs(np.linspace(math.log(HY_TARGET) / HY_SLOW_DECAY, math.log(HY_TARGET) / HY_FAST_DECAY, HY_WIDTH))
    return out, deltas.astype(np.float32)[None]


def _filter_kernel(z_ref, w1_ref, b1_ref, w2_ref, b2_ref, w3_ref, b3_ref, fr_ref, w4f_ref, w4b_ref, dl_ref,
                   hs_ref, hd_ref, pn_ref):
    z = z_ref[...]
    fr = fr_ref[...]
    h = jnp.sin(fr * (_dot3(z, w1_ref[...]) + b1_ref[...]))
    h = jnp.sin(fr * (_dot3(h, w2_ref[...]) + b2_ref[...]))
    h = jnp.sin(fr * (_dot3(h, w3_ref[...]) + b3_ref[...]))
    window = jnp.exp(-z[:, 0:1] * dl_ref[...])
    h_f = _dot3(h, w4f_ref[...]) * window
    h_b = _dot3(h, w4b_ref[...]) * window
    pos = lax.broadcasted_iota(jnp.int32, (z.shape[0], 1), 0)
    h_b = jnp.where(pos == 0, 0.0, h_b)
    hs = h_f + h_b
    hs_ref[...] = hs.astype(BF16)
    hd_ref[...] = (h_b - h_f).astype(BF16)
    sign = jnp.where(pos % 2 == 0, 1.0, -1.0)
    pn_ref[...] = jnp.sum(hs * sign, axis=0, keepdims=True)


def _pad2(a, rows, cols):
    return jnp.zeros((rows, cols), F32).at[:a.shape[0], :a.shape[1]].set(a.astype(F32))


def _hyena_filter(seq, w1, b1, w2, b2, w3, b3, w4, freq):
    z_np, deltas_np = _filter_features(seq)
    p = MLP_PAD
    args = [jnp.asarray(z_np), _pad2(w1, p, p), _pad2(b1[None], 1, p), _pad2(w2, p, p), _pad2(b2[None], 1, p),
            _pad2(w3, p, p), _pad2(b3[None], 1, p), _pad2(freq[None], 1, p),
            _pad2(w4[:, :HY_WIDTH], p, HY_WIDTH), _pad2(w4[:, HY_WIDTH:], p, HY_WIDTH), jnp.asarray(deltas_np)]
    full = lambda shape: pl.BlockSpec(shape, lambda j: (0, 0))
    colblk = lambda rows: pl.BlockSpec((rows, HY_CW), lambda j: (0, j))
    in_specs = [full((seq, p)), full((p, p)), full((1, p)), full((p, p)), full((1, p)), full((p, p)), full((1, p)),
                full((1, p)), colblk(p), colblk(p), colblk(1)]
    return pl.pallas_call(
        _filter_kernel,
        grid=(HY_WIDTH // HY_CW,),
        in_specs=in_specs,
        out_specs=[colblk(seq), colblk(seq), colblk(1)],
        out_shape=[jax.ShapeDtypeStruct((seq, HY_WIDTH), BF16), jax.ShapeDtypeStruct((seq, HY_WIDTH), BF16),
                   jax.ShapeDtypeStruct((1, HY_WIDTH), F32)],
        compiler_params=_params("parallel"),
        name="hy_filter",
    )(*args)


def _spectrum_kernel(f_ref, hs_ref, hd_ref, pn_ref, bias_ref, p_ref, q_ref):
    seq = hs_ref.shape[0]
    bias = bias_ref[...]
    for r0 in range(0, seq, HY_ROW_CHUNK):
        r1 = r0 + HY_ROW_CHUNK
        p_ref[r0:r1, :] = _dot(f_ref[r0:r1, :], hs_ref[...]) + bias
        q = _dot(f_ref[seq + r0:seq + r1, :], hd_ref[...])
        if r0 == 0:
            pos = lax.broadcasted_iota(jnp.int32, (HY_ROW_CHUNK, 1), 0)
            q = jnp.where(pos == 0, pn_ref[...] + bias, q)
        q_ref[r0:r1, :] = q


def _hyena_spectrum(f_tab, hs, hd, pn, bias):
    seq = hs.shape[0]
    colblk = lambda rows: pl.BlockSpec((rows, HY_CW), lambda j: (0, j))
    return pl.pallas_call(
        _spectrum_kernel,
        grid=(HY_WIDTH // HY_CW,),
        in_specs=[_resident(f_tab.shape), colblk(seq), colblk(seq), colblk(1), colblk(1)],
        out_specs=[colblk(seq), colblk(seq)],
        out_shape=[jax.ShapeDtypeStruct((seq, HY_WIDTH), F32)] * 2,
        compiler_params=_params("parallel"),
        name="hy_spectrum",
    )(f_tab, hs, hd, pn, bias.reshape(1, -1))


def _short_conv(u, w, b):
    n = u.shape[0]
    pos = lax.broadcasted_iota(jnp.int32, (n, 1), 0) % GRID_W
    up = jnp.where(pos == 0, 0.0, pltpu.roll(u, 1, 0))
    dn = jnp.where(pos == GRID_W - 1, 0.0, pltpu.roll(u, n - 1, 0))
    return up * w[0:1] + u * w[1:2] + dn * w[2:3] + b


def _hy_fwd_kernel(f_ref, x1_ref, v_ref, w1_ref, b1_ref, wv_ref, bv_ref, p_ref, q_ref, s_ref):
    seq = x1_ref.shape[1]
    n = 2 * seq
    x1 = _short_conv(x1_ref[0].astype(F32), w1_ref[...], b1_ref[...])
    v = _short_conv(v_ref[0].astype(F32), wv_ref[...], bv_ref[...])
    u = (v * x1).astype(BF16)
    for r0 in range(0, seq, HY_ROW_CHUNK):
        r1 = r0 + HY_ROW_CHUNK
        a = _dot(f_ref[r0:r1, :], u)
        bm = _dot(f_ref[seq + r0:seq + r1, :], u)
        p = p_ref[r0:r1, :]
        q = q_ref[r0:r1, :]
        sr = (a * p + bm * q) * (2.0 / n)
        si = (bm * p - a * q) * (2.0 / n)
        if r0 == 0:
            pos = lax.broadcasted_iota(jnp.int32, (HY_ROW_CHUNK, 1), 0)
            sr = jnp.where(pos == 0, a * p * (1.0 / n), sr)
            si = jnp.where(pos == 0, bm * q * (1.0 / n), si)
        s_ref[0, r0:r1, :] = sr.astype(BF16)
        s_ref[0, seq + r0:seq + r1, :] = si.astype(BF16)


def _hy_inv_kernel(g_ref, s_ref, x0_ref, w0_ref, b0_ref, o_ref):
    seq = x0_ref.shape[1]
    x0 = _short_conv(x0_ref[0].astype(F32), w0_ref[...], b0_ref[...])
    s = s_ref[0]
    for r0 in range(0, seq, HY_ROW_CHUNK):
        r1 = r0 + HY_ROW_CHUNK
        y = _dot(g_ref[r0:r1, :], s)
        o_ref[0, r0:r1, :] = (x0[r0:r1] * y).astype(o_ref.dtype)


def _hyena_conv(hy, conv_w, conv_b, f_tab, g_tab, p_spec, q_spec):
    bsz, seq, _ = hy.shape
    nc = HY_WIDTH // HY_CW
    conv_b = conv_b.reshape(1, -1)
    sig = lambda part: pl.BlockSpec((1, seq, HY_CW), lambda j, b: (b, 0, part * nc + j))
    cw = lambda part: pl.BlockSpec((3, HY_CW), lambda j, b: (0, part * nc + j))
    cb = lambda part: pl.BlockSpec((1, HY_CW), lambda j, b: (0, part * nc + j))
    spec = pl.BlockSpec((seq, HY_CW), lambda j, b: (0, j))
    s = pl.pallas_call(
        _hy_fwd_kernel,
        grid=(nc, bsz),
        in_specs=[_resident(f_tab.shape), sig(1), sig(2), cw(1), cb(1), cw(2), cb(2), spec, spec],
        out_specs=pl.BlockSpec((1, 2 * seq, HY_CW), lambda j, b: (b, 0, j)),
        out_shape=jax.ShapeDtypeStruct((bsz, 2 * seq, HY_WIDTH), BF16),
        compiler_params=_params("parallel", "parallel"),
        name="hy_fwd",
    )(f_tab, hy, hy, conv_w, conv_b, conv_w, conv_b, p_spec, q_spec)
    return pl.pallas_call(
        _hy_inv_kernel,
        grid=(nc, bsz),
        in_specs=[_resident(g_tab.shape), pl.BlockSpec((1, 2 * seq, HY_CW), lambda j, b: (b, 0, j)),
                  sig(0), cw(0), cb(0)],
        out_specs=pl.BlockSpec((1, seq, HY_CW), lambda j, b: (b, 0, j)),
        out_shape=jax.ShapeDtypeStruct((bsz, seq, HY_WIDTH), BF16),
        compiler_params=_params("parallel", "parallel"),
        name="hy_inv",
    )(g_tab, s, hy, conv_w, conv_b)


ROW_PARTS = D_MODEL // LANES


def _token_tile(t):
    return pl.ds(pl.multiple_of(t * ROW_PARTS, ROW_PARTS), ROW_PARTS)


def _load_token_rows(ref, n_tok):
    return jnp.concatenate([ref[pl.ds(j, n_tok, stride=ROW_PARTS), :] for j in range(ROW_PARTS)], axis=-1)


def _store_token_rows(ref, val):
    for j in range(ROW_PARTS):
        ref[pl.ds(j, val.shape[0], stride=ROW_PARTS), :] = val[:, j * LANES:(j + 1) * LANES]


POST_TILE = 512


def _post_kernel(of_ref, ob_ref, og_ref, yh_ref, ma_ref, mb_ref, x_ref, g1_ref, sc2_ref, sh2_ref,
                 gn_ref, nf_ref, wa_ref, wb_ref, wo_ref, wr_ref, br_ref,
                 x1_ref, h2_ref, idx_ref, wts_ref):
    o = of_ref[0].astype(F32) + ob_ref[0].astype(F32)
    gn = gn_ref[...]
    heads = [_rms(o[:, h * GLA_HEAD_V:(h + 1) * GLA_HEAD_V], gn) for h in range(GLA_HEADS)]
    og = og_ref[0].astype(F32)
    a_in = jnp.concatenate(heads, axis=-1) * (og * _sigmoid(og))
    branch_a = _dot(a_in.astype(BF16), wa_ref[...])
    branch_b = _dot(yh_ref[0], wb_ref[...])
    y = _sigmoid(ma_ref[0].astype(F32)) * branch_a + _sigmoid(mb_ref[0].astype(F32)) * branch_b
    x1 = x_ref[0] + g1_ref[0] * _dot(y.astype(BF16), wo_ref[...])
    x1_ref[0] = x1
    h2 = _rms(x1, nf_ref[...]) * (1.0 + sc2_ref[0]) + sh2_ref[0]
    _store_token_rows(h2_ref.at[0], h2)

    logits = _dot3(h2, wr_ref[...]) + br_ref[...]
    lane = lax.broadcasted_iota(jnp.int32, logits.shape, 1)
    slot = lax.broadcasted_iota(jnp.int32, (logits.shape[0], TOP_K), 1)
    work = logits
    tops = []
    idx = jnp.zeros((logits.shape[0], TOP_K), jnp.int32)
    for k in range(TOP_K):
        m = jnp.max(work, axis=-1, keepdims=True)
        first = jnp.min(jnp.where(work == m, lane, N_EXPERTS), axis=-1, keepdims=True)
        tops.append(m)
        idx = jnp.where(slot == k, first, idx)
        work = jnp.where(lane == first, -jnp.inf, work)
    es = [jnp.exp(m - tops[0]) for m in tops]
    inv = 1.0 / functools.reduce(lambda a, b: a + b, es)
    wts = jnp.zeros((logits.shape[0], TOP_K), F32)
    for k, e in enumerate(es):
        wts = jnp.where(slot == k, e * inv, wts)
    idx_ref[0] = idx
    wts_ref[0] = wts


def _post(o_f, o_b, og, y_h, m_a, m_b, x, mods, gla_norm, norm_ffn, w_a, w_b, w_o, w_router, b_router):
    bsz, seq, _ = x.shape
    tok = lambda n: pl.BlockSpec((1, POST_TILE, n), lambda b, i: (b, i, 0))
    mod = lambda which: pl.BlockSpec((1, 1, D_MODEL), lambda b, i: (b, 0, which))
    in_specs = [tok(D_MODEL)] * 7 + [mod(MOD_G1), mod(MOD_SC2), mod(MOD_SH2),
                                     _resident((1, GLA_HEAD_V)), _resident((1, D_MODEL)),
                                     _resident(w_a.shape), _resident(w_b.shape), _resident(w_o.shape),
                                     _resident(w_router.shape), _resident((1, N_EXPERTS))]
    return pl.pallas_call(
        _post_kernel,
        grid=(bsz, seq // POST_TILE),
        in_specs=in_specs,
        out_specs=[tok(D_MODEL), pl.BlockSpec((1, POST_TILE * ROW_PARTS, LANES), lambda b, i: (b, i, 0)),
                   tok(TOP_K), tok(TOP_K)],
        out_shape=[jax.ShapeDtypeStruct((bsz, seq, D_MODEL), F32),
                   jax.ShapeDtypeStruct((bsz, seq * ROW_PARTS, LANES), F32),
                   jax.ShapeDtypeStruct((bsz, seq, TOP_K), jnp.int32), jax.ShapeDtypeStruct((bsz, seq, TOP_K), F32)],
        compiler_params=_params("parallel", "parallel"),
        name="post",
    )(o_f, o_b, og, y_h, m_a, m_b, x, mods, mods, mods, gla_norm.reshape(1, -1), norm_ffn.reshape(1, -1),
      w_a, w_b, w_o, w_router, b_router.reshape(1, -1))


UP_GROUP = 2 * LANES
N_UP_GROUPS = 2 * D_MODEL // UP_GROUP


def _deinterleave_kernel(w_ref, o_ref):
    r = lax.broadcasted_iota(jnp.int32, (UP_GROUP, UP_GROUP), 0)
    c = lax.broadcasted_iota(jnp.int32, (UP_GROUP, UP_GROUP), 1)
    perm = jnp.where(c == (r % 2) * LANES + r // 2, 1.0, 0.0).astype(BF16)
    for g in range(w_ref.shape[2] // UP_GROUP):
        cols = slice(g * UP_GROUP, (g + 1) * UP_GROUP)
        o_ref[0, :, cols] = _dot(w_ref[0, :, cols].astype(BF16), perm).astype(BF16)


def _deinterleave_up(w_up):
    n_exp, d_in, d_up = w_up.shape
    blk = pl.BlockSpec((1, d_in, d_up // 2), lambda e, j: (e, 0, j))
    return pl.pallas_call(
        _deinterleave_kernel,
        grid=(n_exp, 2),
        in_specs=[blk], out_specs=blk,
        out_shape=jax.ShapeDtypeStruct(w_up.shape, BF16),
        compiler_params=_params("parallel", "parallel"),
        name="deinterleave",
    )(w_up)


ROUTE_TILE = 512
EXPERT_TILE = 512
COMBINE_TILE = 256


def _exact_count_dot(a, m):
    a0 = a.astype(BF16)
    r1 = a - a0.astype(F32)
    a1 = r1.astype(BF16)
    a2 = (r1 - a1.astype(F32)).astype(BF16)
    return _dot(a0, m) + (_dot(a1, m) + _dot(a2, m))


def _route_kernel(idx_ref, pos_ref, cnt_ref, counts, base, offs):
    p = pl.program_id(0)
    i = pl.program_id(1)
    n_tok = idx_ref.shape[0]
    idx = idx_ref[...]
    lane = lax.broadcasted_iota(jnp.int32, (n_tok, N_EXPERTS), 1)
    hot = jnp.zeros((n_tok, N_EXPERTS), F32)
    for k in range(TOP_K):
        hot = hot + jnp.where(lane == idx[:, k:k + 1], 1.0, 0.0)
    colsum = jnp.sum(hot, axis=0, keepdims=True)

    @pl.when((p == 0) & (i == 0))
    def _():
        counts[...] = jnp.zeros_like(counts)

    @pl.when(p == 0)
    def _():
        counts[...] += colsum

    @pl.when((p == 1) & (i == 0))
    def _():
        r = lax.broadcasted_iota(jnp.int32, (N_EXPERTS, N_EXPERTS), 0)
        c = lax.broadcasted_iota(jnp.int32, (N_EXPERTS, N_EXPERTS), 1)
        before = jnp.where(r < c, 1.0, 0.0).astype(BF16)
        offs[...] = _exact_count_dot(jnp.broadcast_to(counts[...], (8, N_EXPERTS)), before)[0:1]
        base[...] = jnp.zeros_like(base)

    @pl.when(p == 1)
    def _():
        r = lax.broadcasted_iota(jnp.int32, (n_tok, n_tok), 0)
        c = lax.broadcasted_iota(jnp.int32, (n_tok, n_tok), 1)
        earlier = jnp.where(c < r, 1.0, 0.0).astype(BF16)
        dense = offs[...] + base[...] + _dot(earlier, hot.astype(BF16))
        slot = lax.broadcasted_iota(jnp.int32, (n_tok, TOP_K), 1)
        pos = jnp.zeros((n_tok, TOP_K), F32)
        for k in range(TOP_K):
            mine = jnp.sum(jnp.where(lane == idx[:, k:k + 1], dense, 0.0), axis=-1, keepdims=True)
            pos = jnp.where(slot == k, mine, pos)
        pos_ref[...] = pos.astype(jnp.int32)
        base[...] += colsum

    cnt_ref[...] = counts[...].astype(jnp.int32)


def _route(idx):
    n_tok = idx.shape[0]
    return pl.pallas_call(
        _route_kernel,
        grid=(2, n_tok // ROUTE_TILE),
        in_specs=[pl.BlockSpec((ROUTE_TILE, TOP_K), lambda p, i: (i, 0))],
        out_specs=[pl.BlockSpec((ROUTE_TILE, TOP_K), lambda p, i: (i * p, 0)),
                   pl.BlockSpec((1, N_EXPERTS), lambda p, i: (0, 0))],
        out_shape=[jax.ShapeDtypeStruct((n_tok, TOP_K), jnp.int32), jax.ShapeDtypeStruct((1, N_EXPERTS), jnp.int32)],
        scratch_shapes=[pltpu.VMEM((1, N_EXPERTS), F32)] * 3,
        compiler_params=_params("arbitrary", "arbitrary"),
        name="route",
    )(idx)


def _start_rows(n_tok, copy):
    def issue(t, carry):
        for k in range(TOP_K):
            copy(t, k).start(priority=k % 2)
        return carry

    lax.fori_loop(0, n_tok, issue, 0, unroll=4)


def _wait_rows(n_tok, copy):
    def drain(t, carry):
        for k in range(TOP_K):
            copy(t, k).wait()
        return carry

    lax.fori_loop(0, n_tok, drain, 0, unroll=4)


def _rows_one_step_ahead(n_tok, copies):
    i = pl.program_id(0)

    @pl.when(i == 0)
    def _():
        _start_rows(n_tok, copies(i, False))

    @pl.when(i + 1 < pl.num_programs(0))
    def _():
        _start_rows(n_tok, copies(i + 1, True))

    _wait_rows(n_tok, copies(i, False))


def _dispatch_kernel(pos_ref, nxt_ref, h_ref, xs_ref, sem):
    def copies(step, use_next):
        p_ref = nxt_ref if use_next else pos_ref

        def copy(t, k):
            src = h_ref.at[_token_tile(step * ROUTE_TILE + t), :]
            return pltpu.make_async_copy(src, xs_ref.at[_token_tile(p_ref[t * TOP_K + k]), :], sem.at[step % 2])
        return copy

    _rows_one_step_ahead(ROUTE_TILE, copies)


def _index_blocks(tile, n_steps):
    blk = (tile * TOP_K,)
    return [pl.BlockSpec(blk, lambda i: (i,), memory_space=pltpu.SMEM),
            pl.BlockSpec(blk, lambda i: (jnp.minimum(i + 1, n_steps - 1),), memory_space=pltpu.SMEM)]


def _dispatch(pos_flat, h2):
    n_tok = h2.shape[0] // ROW_PARTS
    n_steps = n_tok // ROUTE_TILE
    return pl.pallas_call(
        _dispatch_kernel,
        grid=(n_steps,),
        in_specs=_index_blocks(ROUTE_TILE, n_steps) + [pl.BlockSpec(memory_space=pl.ANY)],
        out_specs=pl.BlockSpec(memory_space=pl.ANY),
        out_shape=jax.ShapeDtypeStruct((n_tok * TOP_K * ROW_PARTS, LANES), F32),
        scratch_shapes=[pltpu.SemaphoreType.DMA((2,))],
        compiler_params=_params("arbitrary"),
        name="dispatch",
    )(pos_flat, pos_flat, h2)


def _experts_kernel(tile_ref, exp_ref, lo_ref, hi_ref, x_ref, wu_ref, bu_ref, wd_ref, bd_ref, o_ref):
    w = pl.program_id(0)
    lo = lo_ref[w]
    hi = hi_ref[w]
    start = tile_ref[w] * EXPERT_TILE

    @pl.when(hi > lo)
    def _():
        h = _load_token_rows(x_ref, EXPERT_TILE).astype(BF16)
        acts = []
        for g in range(N_UP_GROUPS):
            cols = slice(g * UP_GROUP, (g + 1) * UP_GROUP)
            u = _dot(h, wu_ref[0, :, cols]) + bu_ref[0, :, cols]
            glu = jnp.minimum(u[:, :LANES], SWIGLU_LIMIT)
            lin = jnp.clip(u[:, LANES:], -SWIGLU_LIMIT, SWIGLU_LIMIT)
            acts.append((glu * _sigmoid(SWIGLU_ALPHA * glu) * (lin + 1.0)).astype(BF16))
        y = _dot(jnp.concatenate(acts, axis=-1), wd_ref[0]) + bd_ref[0]
        row = start + lax.broadcasted_iota(jnp.int32, (EXPERT_TILE, 1), 0)
        mine = (row >= lo) & (row < hi)

        @pl.when(lo == start)
        def _():
            _store_token_rows(o_ref, jnp.where(mine, y, 0.0))

        @pl.when(lo != start)
        def _():
            _store_token_rows(o_ref, jnp.where(mine, y, _load_token_rows(o_ref, EXPERT_TILE)))


def _work_items(counts, n_rows):
    n_tiles = n_rows // EXPERT_TILE
    ends = jnp.cumsum(counts)
    tile_ends = jnp.arange(1, n_tiles + 1, dtype=jnp.int32) * EXPERT_TILE
    n_items = n_tiles + N_EXPERTS
    count = lambda cond: jnp.sum(cond.astype(jnp.int32), axis=1)
    slot_t = jnp.arange(n_tiles, dtype=jnp.int32) + count(ends[None, :] < tile_ends[:, None])
    slot_e = jnp.arange(N_EXPERTS, dtype=jnp.int32) + count(tile_ends[None, :] <= ends[:, None])
    slots = jnp.concatenate([slot_t, slot_e])
    vals = jnp.concatenate([tile_ends, ends])
    item = jnp.arange(n_items, dtype=jnp.int32)
    hi = jnp.sum(jnp.where(slots[None, :] == item[:, None], vals[None, :], 0), axis=1)
    lo = jnp.concatenate([jnp.zeros((1,), jnp.int32), hi[:-1]])
    tile_id = jnp.minimum(lo // EXPERT_TILE, n_tiles - 1)
    exp_id = jnp.minimum(count(ends[None, :] <= lo[:, None]), N_EXPERTS - 1)
    return tile_id, exp_id, lo, hi


def _experts(items, xs, w_up_g, b_up_g, w_down, b_down):
    n_items = items[0].shape[0]
    rows = lambda w, t, e, lo, hi: (t[w], 0)
    per_expert = lambda r, c: pl.BlockSpec((1, r, c), lambda w, t, e, lo, hi: (e[w], 0, 0))
    return pl.pallas_call(
        _experts_kernel,
        grid_spec=pltpu.PrefetchScalarGridSpec(
            num_scalar_prefetch=4,
            grid=(n_items,),
            in_specs=[pl.BlockSpec((EXPERT_TILE * ROW_PARTS, LANES), rows),
                      per_expert(D_MODEL, 2 * D_MODEL), per_expert(1, 2 * D_MODEL),
                      per_expert(D_MODEL, D_MODEL), per_expert(1, D_MODEL)],
            out_specs=pl.BlockSpec((EXPERT_TILE * ROW_PARTS, LANES), rows)),
        out_shape=jax.ShapeDtypeStruct(xs.shape, F32),
        compiler_params=_params("arbitrary"),
        name="experts",
    )(*items, xs, w_up_g, b_up_g, w_down, b_down)


def _combine_kernel(pos_ref, nxt_ref, ys_ref, wts_ref, x1_ref, g2_ref, nf_ref, o_ref, buf, sem):
    def copies(step, use_next):
        p_ref = nxt_ref if use_next else pos_ref

        def copy(t, k):
            src = ys_ref.at[_token_tile(p_ref[t * TOP_K + k]), :]
            return pltpu.make_async_copy(src, buf.at[step % 2, k, _token_tile(t), :], sem.at[step % 2])
        return copy

    n_tok = x1_ref.shape[0]
    _rows_one_step_ahead(n_tok, copies)
    mine = buf.at[pl.program_id(0) % 2]
    wts = wts_ref[...]
    acc = wts[:, 0:1] * _load_token_rows(mine.at[0], n_tok)
    for k in range(1, TOP_K):
        acc = acc + wts[:, k:k + 1] * _load_token_rows(mine.at[k], n_tok)
    o_ref[...] = _rms(x1_ref[...] + g2_ref[0] * acc, nf_ref[...])


def _combine(pos_flat, ys, wts, x1, mods, norm_final, seq):
    n_tok = x1.shape[0]
    tiles_per_sample = seq // COMBINE_TILE
    tok = lambda n: pl.BlockSpec((COMBINE_TILE, n), lambda i: (i, 0))
    n_steps = n_tok // COMBINE_TILE
    return pl.pallas_call(
        _combine_kernel,
        grid=(n_steps,),
        in_specs=_index_blocks(COMBINE_TILE, n_steps) + [
            pl.BlockSpec(memory_space=pl.ANY), tok(TOP_K), tok(D_MODEL),
            pl.BlockSpec((1, 1, D_MODEL), lambda i: (i // tiles_per_sample, 0, MOD_G2)),
            pl.BlockSpec((1, D_MODEL), lambda i: (0, 0))],
        out_specs=tok(D_MODEL),
        out_shape=jax.ShapeDtypeStruct((n_tok, D_MODEL), F32),
        scratch_shapes=[pltpu.VMEM((2, TOP_K, COMBINE_TILE * ROW_PARTS, LANES), F32),
                        pltpu.SemaphoreType.DMA((2,))],
        compiler_params=_params("arbitrary"),
        name="combine",
    )(pos_flat, pos_flat, ys, wts, x1, mods, norm_final.reshape(1, -1))


def _moe(h2, idx, wts, w_up_g, b_up_g, w_down, b_down, x1, mods, norm_final):
    bsz, seq, _ = x1.shape
    n_tok = bsz * seq
    pos, counts = _route(idx.reshape(n_tok, TOP_K))
    pos_flat = pos.reshape(-1)
    xs = _dispatch(pos_flat, h2.reshape(n_tok * ROW_PARTS, LANES))
    ys = _experts(_work_items(counts.reshape(-1), n_tok * TOP_K), xs, w_up_g, b_up_g, w_down, b_down)
    out = _combine(pos_flat, ys, wts.reshape(n_tok, TOP_K), x1.reshape(n_tok, D_MODEL), mods, norm_final, seq)
    return out.reshape(bsz, seq, D_MODEL)


def kernel(x, c, ctx, c_ctx, w_ada, b_ada, norm_mix, norm_ffn, w_in, w_gk_f, b_gk_f, w_gk_b, b_gk_b, gla_norm, w_gla_out, hy_conv_w, hy_conv_b, hy_f_w1, hy_f_b1, hy_f_w2, hy_f_b2, hy_f_w3, hy_f_b3, hy_f_w4, hy_sin_freq, hy_bias, w_hy_out, w_out, w_router, b_router, w_up, b_up, w_down, b_down, norm_final):
    depth = w_ada.shape[0]
    assert depth == 1, "single-layer block: the context stream only feeds later layers"
    bsz, seq, _ = x.shape

    n_rows = -(-(bsz + 1) // 8) * 8
    cc = jnp.zeros((n_rows, D_MODEL), F32).at[:bsz].set(c).at[bsz].set(c_ctx)
    mods = _ada(cc, w_ada[0], b_ada[0]).reshape(n_rows, 1, N_MOD * D_MODEL)

    sizes = (GLA_DK, GLA_DK, GLA_DV, GLA_DV, GLA_GATE_RANK, GLA_GATE_RANK, 3 * HY_WIDTH, D_MODEL, D_MODEL)
    offs = np.concatenate([[0], np.cumsum(sizes)])
    w_in_b = w_in[0].astype(BF16)
    wq, wk, wv, wog, wrf, wrb, why, wma, wmb = [w_in_b[:, offs[i]:offs[i + 1]] for i in range(len(sizes))]
    q, k, v, og, r_f, r_b, hy, m_a, m_b = _inproj(
        x, mods, lambda b: b, norm_mix[0], [wq, wk, wv, wog, wrf, wrb, why, wma, wmb],
        [BF16, BF16, BF16, BF16, F32, F32, BF16, BF16, BF16], tl=512)
    k_c, v_c, rc_f, rc_b = _inproj(
        ctx, mods, lambda b: bsz, norm_mix[0], [wk, wv, wrf, wrb], [BF16, BF16, F32, F32], tl=ctx.shape[1])

    w_gk = jnp.stack([w_gk_f[0], w_gk_b[0]])
    b_gk = jnp.stack([b_gk_f[0], b_gk_b[0]])[:, None, :]
    o_f, o_b = _gla(q, k, v, r_f, r_b, k_c, v_c, rc_f, rc_b, w_gk, b_gk)

    f_tab = jnp.asarray(_dft_table(seq)).astype(BF16)
    g_tab = f_tab.T
    hs, hd, pn = _hyena_filter(seq, hy_f_w1[0], hy_f_b1[0], hy_f_w2[0], hy_f_b2[0], hy_f_w3[0], hy_f_b3[0],
                               hy_f_w4[0], hy_sin_freq[0])
    p_spec, q_spec = _hyena_spectrum(f_tab, hs, hd, pn, hy_bias[0])
    y_h = _hyena_conv(hy, hy_conv_w[0], hy_conv_b[0], f_tab, g_tab, p_spec, q_spec)

    x1, h2, idx, wts = _post(o_f, o_b, og, y_h, m_a, m_b, x, mods, gla_norm[0], norm_ffn[0],
                             w_gla_out[0].astype(BF16), w_hy_out[0].astype(BF16), w_out[0].astype(BF16),
                             w_router[0], b_router[0])

    b_up_g = b_up[0].reshape(N_EXPERTS, N_UP_GROUPS, LANES, 2).transpose(0, 1, 3, 2).reshape(N_EXPERTS, 1, -1)
    return _moe(h2, idx, wts, _deinterleave_up(w_up[0]), b_up_g, w_down[0].astype(BF16), b_down[0][:, None, :],
                x1, mods, norm_final)
```

```python
import functools
import math

import jax
import jax.numpy as jnp
import numpy as np
from jax import lax
from jax.experimental import pallas as pl
from jax.experimental.pallas import tpu as pltpu

F32 = jnp.float32
BF16 = jnp.bfloat16

D_MODEL = 1024
GRID_W = 64
EPS = 1e-6
N_MOD = 6

GLA_HEADS = 4
GLA_HEAD_K = 128
GLA_HEAD_V = 256
GLA_DK = GLA_HEADS * GLA_HEAD_K
GLA_DV = GLA_HEADS * GLA_HEAD_V
GLA_GATE_RANK = 16
GLA_GATE_NORM = 16.0
GLA_CHUNK = 64

HY_WIDTH = D_MODEL
HY_EMB = 33
HY_FAST_DECAY = 0.3
HY_SLOW_DECAY = 1.5
HY_TARGET = 1e-2

N_EXPERTS = 32
TOP_K = 4
SWIGLU_LIMIT = 7.0
SWIGLU_ALPHA = 1.702

V7X_VMEM_BYTES = 64 * 1024 * 1024
VMEM_LIMIT = V7X_VMEM_BYTES - 8 * 1024 * 1024
LANES = 128

MOD_SH1, MOD_SC1, MOD_G1, MOD_SH2, MOD_SC2, MOD_G2 = range(N_MOD)


def _params(*sem):
    return pltpu.CompilerParams(dimension_semantics=sem, vmem_limit_bytes=VMEM_LIMIT)


def _resident(shape):
    return pl.BlockSpec(shape, lambda *_: (0,) * len(shape), pipeline_mode=pl.Buffered(1))


def _dot(a, b):
    return jnp.dot(a, b, preferred_element_type=F32)


def _dot_nt(a, b):
    return lax.dot_general(a, b, (((1,), (1,)), ((), ())), preferred_element_type=F32)


def _dot_tn(a, b):
    return lax.dot_general(a, b, (((0,), (0,)), ((), ())), preferred_element_type=F32)


def _split(a):
    hi = a.astype(BF16)
    lo = (a - hi.astype(F32)).astype(BF16)
    return hi, lo


def _dot3(a, b):
    ah, al = _split(a)
    bh, bl = _split(b)
    return _dot(ah, bh) + (_dot(ah, bl) + _dot(al, bh))


def _sigmoid(x):
    return 1.0 / (1.0 + jnp.exp(-x))


def _log_sigmoid(x):
    return jnp.minimum(x, 0.0) - jnp.log(1.0 + jnp.exp(-jnp.abs(x)))


def _rms(x, w):
    return x * lax.rsqrt(jnp.mean(x * x, axis=-1, keepdims=True) + EPS) * w


def _ada_kernel(c_ref, w_ref, b_ref, o_ref):
    c = c_ref[...]
    o_ref[...] = _dot3(c * _sigmoid(c), w_ref[...]) + b_ref[...]


def _ada(cc, w_ada, b_ada):
    rows = cc.shape[0]
    return pl.pallas_call(
        _ada_kernel,
        grid=(N_MOD,),
        in_specs=[pl.BlockSpec((rows, D_MODEL), lambda j: (0, 0)),
                  pl.BlockSpec((D_MODEL, D_MODEL), lambda j: (0, j)),
                  pl.BlockSpec((1, D_MODEL), lambda j: (0, j))],
        out_specs=pl.BlockSpec((rows, D_MODEL), lambda j: (0, j)),
        out_shape=jax.ShapeDtypeStruct((rows, N_MOD * D_MODEL), F32),
        compiler_params=_params("parallel"),
        name="ada",
    )(cc, w_ada, b_ada.reshape(1, -1))


INPROJ_COL_CHUNK = 512


def _inproj_kernel(n_out, x_ref, sc_ref, sh_ref, nw_ref, *refs):
    w_refs, o_refs = refs[:n_out], refs[n_out:]
    h = _rms(x_ref[0], nw_ref[...]) * (1.0 + sc_ref[0]) + sh_ref[0]
    hb = h.astype(BF16)
    for w_ref, o_ref in zip(w_refs, o_refs):
        n = w_ref.shape[1]
        for c0 in range(0, n, INPROJ_COL_CHUNK):
            c1 = min(c0 + INPROJ_COL_CHUNK, n)
            o_ref[0, :, c0:c1] = _dot(hb, w_ref[:, c0:c1]).astype(o_ref.dtype)


def _inproj(x, mods, mod_row, norm_w, weights, out_dtypes, tl):
    bsz, seq, _ = x.shape
    n_out = len(weights)
    in_specs = [pl.BlockSpec((1, tl, D_MODEL), lambda b, i: (b, i, 0)),
                pl.BlockSpec((1, 1, D_MODEL), lambda b, i: (mod_row(b), 0, MOD_SC1)),
                pl.BlockSpec((1, 1, D_MODEL), lambda b, i: (mod_row(b), 0, MOD_SH1)),
                _resident((1, D_MODEL))]
    in_specs += [_resident(w.shape) for w in weights]
    out_specs = [pl.BlockSpec((1, tl, w.shape[1]), lambda b, i: (b, i, 0)) for w in weights]
    out_shape = [jax.ShapeDtypeStruct((bsz, seq, w.shape[1]), dt) for w, dt in zip(weights, out_dtypes)]
    return pl.pallas_call(
        functools.partial(_inproj_kernel, n_out),
        grid=(bsz, seq // tl),
        in_specs=in_specs, out_specs=out_specs, out_shape=out_shape,
        compiler_params=_params("parallel", "parallel"),
        name="inproj",
    )(x, mods, mods, norm_w.reshape(1, -1), *weights)


GLA_TILE = 256
GLA_NCH = GLA_TILE // GLA_CHUNK
GLA_SCALE = GLA_HEAD_K ** -0.5
GLA_HPS = 4
GLA_KW = GLA_HPS * GLA_HEAD_K
GLA_VW = GLA_HPS * GLA_HEAD_V


@functools.lru_cache(maxsize=None)
def _chunk_triangles():
    i = np.arange(GLA_TILE)
    same = (i[:, None] // GLA_CHUNK) == (i[None, :] // GLA_CHUNK)
    return np.stack([same & (i[None, :] <= i[:, None]), same & (i[None, :] >= i[:, None])]).astype(np.float32)


def _gla_kernel(tri_ref, qf_ref, kf_ref, vf_ref, rf_ref, qb_ref, kb_ref, vb_ref, rb_ref,
                kc_ref, vc_ref, rcf_ref, rcb_ref, wgk_ref, bgk_ref,
                of_ref, ob_ref, sf_ref, sb_ref):
    t = pl.program_id(2)
    crow = lax.broadcasted_iota(jnp.int32, (GLA_CHUNK, GLA_CHUNK), 0)
    ccol = lax.broadcasted_iota(jnp.int32, (GLA_CHUNK, GLA_CHUNK), 1)
    heads = range(GLA_HPS)

    def rows(c):
        return slice(c * GLA_CHUNK, (c + 1) * GLA_CHUNK)

    def hk(h):
        return slice(h * GLA_HEAD_K, (h + 1) * GLA_HEAD_K)

    def hv(h):
        return slice(h * GLA_HEAD_V, (h + 1) * GLA_HEAD_V)

    def order(fwd):
        return range(GLA_NCH) if fwd else range(GLA_NCH - 1, -1, -1)

    def cum_decay(r, d):
        z = _dot3(r, wgk_ref[d]) + bgk_ref[d]
        g = _log_sigmoid(z) * (1.0 / GLA_GATE_NORM)
        gh, gl = _split(g)
        return _dot(tri_ref[d], gh) + _dot(tri_ref[d], gl)

    def updates(k_ref, v_ref, b, fwd):
        out = {}
        for c in range(GLA_NCH):
            i = c * GLA_CHUNK + (GLA_CHUNK - 1 if fwd else 0)
            total = b[i:i + 1, :]
            k_upd = (k_ref[0, rows(c), :].astype(F32) * jnp.exp(total - b[rows(c)])).astype(BF16)
            dec = jnp.exp(total)
            for h in heads:
                out[c, h] = (dec[:, hk(h)], _dot_tn(v_ref[0, rows(c), hv(h)], k_upd[:, hk(h)]))
        return out

    def context_state(s_ref, r_ref, d, fwd):
        upd = updates(kc_ref, vc_ref, cum_decay(r_ref[0], d), fwd)
        for h in heads:
            st = jnp.zeros((GLA_HEAD_V, GLA_HEAD_K), F32)
            for c in order(fwd):
                dec, inc = upd[c, h]
                st = st * dec + inc
            s_ref[h] = st

    def scan(s_ref, q_ref, k_ref, v_ref, r_ref, o_ref, d, fwd):
        mask = (ccol <= crow) if fwd else (ccol >= crow)
        b = cum_decay(r_ref[0], d)
        q_dec = (q_ref[0].astype(F32) * (jnp.exp(b) * GLA_SCALE)).astype(BF16)
        k_inv = (k_ref[0].astype(F32) * jnp.exp(-b)).astype(BF16)
        upd = updates(k_ref, v_ref, b, fwd)
        intra = {}
        for c in range(GLA_NCH):
            for h in heads:
                att = jnp.where(mask, _dot_nt(q_dec[rows(c), hk(h)], k_inv[rows(c), hk(h)]), 0.0).astype(BF16)
                intra[c, h] = _dot(att, v_ref[0, rows(c), hv(h)])
        enter = {}
        for h in heads:
            st = s_ref[h]
            for c in order(fwd):
                enter[c, h] = st.astype(BF16)
                dec, inc = upd[c, h]
                st = st * dec + inc
            s_ref[h] = st
        for c in range(GLA_NCH):
            for h in heads:
                o = intra[c, h] + _dot_nt(q_dec[rows(c), hk(h)], enter[c, h])
                o_ref[0, rows(c), hv(h)] = o.astype(o_ref.dtype)

    @pl.when(t == 0)
    def _():
        context_state(sf_ref, rcf_ref, 0, True)
        context_state(sb_ref, rcb_ref, 1, False)

    scan(sf_ref, qf_ref, kf_ref, vf_ref, rf_ref, of_ref, 0, True)
    scan(sb_ref, qb_ref, kb_ref, vb_ref, rb_ref, ob_ref, 1, False)


def _gla(q, k, v, r_f, r_b, k_c, v_c, rc_f, rc_b, w_gk, b_gk):
    bsz, seq, _ = q.shape
    assert k_c.shape[1] == GLA_TILE
    nt = seq // GLA_TILE
    tri = jnp.asarray(_chunk_triangles()).astype(BF16)
    fwd = lambda b, h, t: (b, t, h)
    bwd = lambda b, h, t: (b, nt - 1 - t, h)
    fwd0 = lambda b, h, t: (b, t, 0)
    bwd0 = lambda b, h, t: (b, nt - 1 - t, 0)
    ctx = lambda b, h, t: (b, 0, h)
    ctx0 = lambda b, h, t: (b, 0, 0)
    kblk = (1, GLA_TILE, GLA_KW)
    vblk = (1, GLA_TILE, GLA_VW)
    rblk = (1, GLA_TILE, GLA_GATE_RANK)
    in_specs = [_resident(tri.shape),
                pl.BlockSpec(kblk, fwd), pl.BlockSpec(kblk, fwd), pl.BlockSpec(vblk, fwd), pl.BlockSpec(rblk, fwd0),
                pl.BlockSpec(kblk, bwd), pl.BlockSpec(kblk, bwd), pl.BlockSpec(vblk, bwd), pl.BlockSpec(rblk, bwd0),
                pl.BlockSpec(kblk, ctx), pl.BlockSpec(vblk, ctx), pl.BlockSpec(rblk, ctx0), pl.BlockSpec(rblk, ctx0),
                pl.BlockSpec((2, GLA_GATE_RANK, GLA_KW), lambda b, h, t: (0, 0, h)),
                pl.BlockSpec((2, 1, GLA_KW), lambda b, h, t: (0, 0, h))]
    out_specs = [pl.BlockSpec(vblk, fwd), pl.BlockSpec(vblk, bwd)]
    out_shape = [jax.ShapeDtypeStruct((bsz, seq, GLA_DV), BF16)] * 2
    return pl.pallas_call(
        _gla_kernel,
        grid=(bsz, GLA_HEADS // GLA_HPS, nt),
        in_specs=in_specs, out_specs=out_specs, out_shape=out_shape,
        scratch_shapes=[pltpu.VMEM((GLA_HPS, GLA_HEAD_V, GLA_HEAD_K), F32)] * 2,
        compiler_params=_params("parallel", "parallel", "arbitrary"),
        name="gla",
    )(tri, q, k, v, r_f, q, k, v, r_b, k_c, v_c, rc_f, rc_b, w_gk, b_gk)


HY_CW = 256
HY_ROW_CHUNK = 512
MLP_PAD = 128


@functools.lru_cache(maxsize=None)
def _dft_tables(seq):
    n, half = 2 * seq, seq // 2
    m = np.arange(half)[:, None]
    s = np.arange(half)[None, :]
    ang = 2.0 * np.pi * np.arange(n) / n
    cos = lambda k: np.cos(ang)[(k * s) % n]
    sin = lambda k: np.sin(ang)[(k * s) % n]
    sym = np.concatenate([cos(2 * m), sin(2 * m + 1)], axis=0)
    anti = np.concatenate([sin(2 * m), cos(2 * m + 1)], axis=0)
    return sym.astype(np.float32), anti.astype(np.float32)


@functools.lru_cache(maxsize=None)
def _filter_features(seq):
    bands = (HY_EMB - 1) // 2
    pos = np.arange(seq, dtype=np.float64)[:, None]
    t = pos / max(seq - 1, 1)
    f = np.linspace(1e-4, bands - 1, bands)[None]
    ang = (2.0 * math.pi / seq) * pos * f
    z = np.concatenate([t, np.cos(ang), -np.sin(ang)], axis=-1)
    out = np.zeros((seq, MLP_PAD), np.float32)
    out[:, :HY_EMB] = z
    deltas = np.abs(np.linspace(math.log(HY_TARGET) / HY_SLOW_DECAY, math.log(HY_TARGET) / HY_FAST_DECAY, HY_WIDTH))
    return out, deltas.astype(np.float32)[None]


def _filter_kernel(z_ref, w1_ref, b1_ref, w2_ref, b2_ref, w3_ref, b3_ref, fr_ref, w4f_ref, w4b_ref, dl_ref,
                   hs_ref, hd_ref, pn_ref):
    z = z_ref[...]
    fr = fr_ref[...]
    h = jnp.sin(fr * (_dot3(z, w1_ref[...]) + b1_ref[...]))
    h = jnp.sin(fr * (_dot3(h, w2_ref[...]) + b2_ref[...]))
    h = jnp.sin(fr * (_dot3(h, w3_ref[...]) + b3_ref[...]))
    window = jnp.exp(-z[:, 0:1] * dl_ref[...])
    h_f = _dot3(h, w4f_ref[...]) * window
    h_b = _dot3(h, w4b_ref[...]) * window
    pos = lax.broadcasted_iota(jnp.int32, (z.shape[0], 1), 0)
    h_b = jnp.where(pos == 0, 0.0, h_b)
    hs = h_f + h_b
    hs_ref[...] = hs.astype(BF16)
    hd_ref[...] = (h_b - h_f).astype(BF16)
    sign = jnp.where(pos % 2 == 0, 1.0, -1.0)
    pn_ref[...] = jnp.sum(hs * sign, axis=0, keepdims=True)


def _pad2(a, rows, cols):
    return jnp.zeros((rows, cols), F32).at[:a.shape[0], :a.shape[1]].set(a.astype(F32))


def _hyena_filter(seq, w1, b1, w2, b2, w3, b3, w4, freq):
    z_np, deltas_np = _filter_features(seq)
    p = MLP_PAD
    args = [jnp.asarray(z_np), _pad2(w1, p, p), _pad2(b1[None], 1, p), _pad2(w2, p, p), _pad2(b2[None], 1, p),
            _pad2(w3, p, p), _pad2(b3[None], 1, p), _pad2(freq[None], 1, p),
            _pad2(w4[:, :HY_WIDTH], p, HY_WIDTH), _pad2(w4[:, HY_WIDTH:], p, HY_WIDTH), jnp.asarray(deltas_np)]
    full = lambda shape: pl.BlockSpec(shape, lambda j: (0, 0))
    colblk = lambda rows: pl.BlockSpec((rows, HY_CW), lambda j: (0, j))
    in_specs = [full((seq, p)), full((p, p)), full((1, p)), full((p, p)), full((1, p)), full((p, p)), full((1, p)),
                full((1, p)), colblk(p), colblk(p), colblk(1)]
    return pl.pallas_call(
        _filter_kernel,
        grid=(HY_WIDTH // HY_CW,),
        in_specs=in_specs,
        out_specs=[colblk(seq), colblk(seq), colblk(1)],
        out_shape=[jax.ShapeDtypeStruct((seq, HY_WIDTH), BF16), jax.ShapeDtypeStruct((seq, HY_WIDTH), BF16),
                   jax.ShapeDtypeStruct((1, HY_WIDTH), F32)],
        compiler_params=_params("parallel"),
        name="hy_filter",
    )(*args)


REV_BLOCK = 256


def _negate_index(x):
    n = x.shape[0]
    r = lax.broadcasted_iota(jnp.int32, (REV_BLOCK, REV_BLOCK), 0)
    c = lax.broadcasted_iota(jnp.int32, (REV_BLOCK, REV_BLOCK), 1)
    flip = jnp.where(r + c == REV_BLOCK - 1, 1.0, 0.0).astype(BF16)
    if x.dtype == BF16:
        pieces = [x]
    else:
        x0 = x.astype(BF16)
        r1 = x - x0.astype(F32)
        x1 = r1.astype(BF16)
        pieces = [x0, x1, (r1 - x1.astype(F32)).astype(BF16)]
    blocks = []
    for i in range(n // REV_BLOCK - 1, -1, -1):
        rows = slice(i * REV_BLOCK, (i + 1) * REV_BLOCK)
        acc = _dot(flip, pieces[0][rows])
        for p in pieces[1:]:
            acc = acc + _dot(flip, p[rows])
        blocks.append(acc)
    return pltpu.roll(jnp.concatenate(blocks, axis=0), 1, 0)


def _fold(x):
    half = x.shape[0] // 2
    lo = x[:half].astype(F32)
    hi = _negate_index(x[half:])
    pos = lax.broadcasted_iota(jnp.int32, (half, 1), 0)
    mid = hi[0:1]
    hi = jnp.where(pos == 0, 0.0, hi)
    return (lo + hi).astype(BF16), (lo - hi).astype(BF16), mid


def _alternating(n):
    pos = lax.broadcasted_iota(jnp.int32, (n, 1), 0)
    return jnp.where(pos % 2 == 0, 1.0, -1.0)


def _spectrum_kernel(ts_ref, ta_ref, hs_ref, hd_ref, pn_ref, bias_ref, p_ref, q_ref, kn_ref):
    half = hs_ref.shape[0] // 2
    bias = bias_ref[...]
    sgn = _alternating(half)
    s_sym, s_anti, s_mid = _fold(hs_ref[...])
    d_sym, d_anti, d_mid = _fold(hd_ref[...])
    p_ref[:half, :] = _dot(ts_ref[:half, :], s_sym) + sgn * s_mid + bias
    p_ref[half:, :] = _dot(ta_ref[half:, :], s_anti) + bias
    q_ref[:half, :] = _dot(ta_ref[:half, :], d_anti)
    q_ref[half:, :] = _dot(ts_ref[half:, :], d_sym) + sgn * d_mid
    kn_ref[...] = pn_ref[...] + bias


def _hyena_spectrum(t_sym, t_anti, hs, hd, pn, bias):
    seq = hs.shape[0]
    colblk = lambda rows: pl.BlockSpec((rows, HY_CW), lambda j: (0, j))
    return pl.pallas_call(
        _spectrum_kernel,
        grid=(HY_WIDTH // HY_CW,),
        in_specs=[_resident(t_sym.shape), _resident(t_anti.shape), colblk(seq), colblk(seq), colblk(1), colblk(1)],
        out_specs=[colblk(seq), colblk(seq), colblk(1)],
        out_shape=[jax.ShapeDtypeStruct((seq, HY_WIDTH), F32)] * 2 + [jax.ShapeDtypeStruct((1, HY_WIDTH), F32)],
        compiler_params=_params("parallel"),
        name="hy_spectrum",
    )(t_sym, t_anti, hs, hd, pn, bias.reshape(1, -1))


def _short_conv(u, w, b):
    n = u.shape[0]
    pos = lax.broadcasted_iota(jnp.int32, (n, 1), 0) % GRID_W
    up = jnp.where(pos == 0, 0.0, pltpu.roll(u, 1, 0))
    dn = jnp.where(pos == GRID_W - 1, 0.0, pltpu.roll(u, n - 1, 0))
    return up * w[0:1] + u * w[1:2] + dn * w[2:3] + b


def _hy_fwd_kernel(ts_ref, ta_ref, x1_ref, v_ref, w1_ref, b1_ref, wv_ref, bv_ref, p_ref, q_ref, kn_ref, s_ref):
    seq = x1_ref.shape[1]
    n, half = 2 * seq, seq // 2
    x1 = _short_conv(x1_ref[0].astype(F32), w1_ref[...], b1_ref[...])
    v = _short_conv(v_ref[0].astype(F32), wv_ref[...], bv_ref[...])
    u_sym, u_anti, u_mid = _fold(v * x1)
    sgn = _alternating(half)
    nyq = (jnp.sum(sgn * u_sym.astype(F32), axis=0, keepdims=True) + u_mid) * kn_ref[...] * (1.0 / n)
    for r0 in range(0, half, HY_ROW_CHUNK):
        r1 = r0 + HY_ROW_CHUNK
        mid = sgn[r0:r1] * u_mid
        re_e = _dot(ts_ref[r0:r1, :], u_sym) + mid
        im_o = _dot(ts_ref[half + r0:half + r1, :], u_sym) + mid
        im_e = _dot(ta_ref[r0:r1, :], u_anti)
        re_o = _dot(ta_ref[half + r0:half + r1, :], u_anti)
        pe, po = p_ref[r0:r1, :], p_ref[half + r0:half + r1, :]
        qe, qo = q_ref[r0:r1, :], q_ref[half + r0:half + r1, :]
        s_re_e = (re_e * pe + im_e * qe) * (2.0 / n)
        s_im_e = (im_e * pe - re_e * qe) * (2.0 / n)
        if r0 == 0:
            pos = lax.broadcasted_iota(jnp.int32, (HY_ROW_CHUNK, 1), 0)
            s_re_e = jnp.where(pos == 0, re_e * pe * (1.0 / n), s_re_e)
            s_im_e = jnp.where(pos == 0, nyq, s_im_e)
        s_ref[0, r0:r1, :] = s_re_e.astype(BF16)
        s_ref[0, half + r0:half + r1, :] = ((im_o * po - re_o * qo) * (2.0 / n)).astype(BF16)
        s_ref[0, seq + r0:seq + r1, :] = s_im_e.astype(BF16)
        s_ref[0, seq + half + r0:seq + half + r1, :] = ((re_o * po + im_o * qo) * (2.0 / n)).astype(BF16)


def _hy_inv_kernel(gs_ref, ga_ref, s_ref, x0_ref, w0_ref, b0_ref, o_ref):
    seq = x0_ref.shape[1]
    half = seq // 2
    x0 = _short_conv(x0_ref[0].astype(F32), w0_ref[...], b0_ref[...])
    sgn = _alternating(half)
    nyq = sgn * s_ref[0, seq:seq + 1, :].astype(F32)
    mirrored = []
    for r0 in range(0, half, HY_ROW_CHUNK):
        r1 = r0 + HY_ROW_CHUNK
        sym = _dot(gs_ref[r0:r1, :], s_ref[0, :seq, :]) + nyq[r0:r1]
        anti = _dot(ga_ref[r0:r1, :], s_ref[0, seq:, :])
        o_ref[0, r0:r1, :] = (x0[r0:r1] * (sym + anti)).astype(o_ref.dtype)
        mirrored.append(sym - anti)
    centre = (jnp.sum(sgn * (s_ref[0, :half, :].astype(F32) + s_ref[0, half:seq, :].astype(F32)), axis=0, keepdims=True)
              + nyq[0:1])
    pos = lax.broadcasted_iota(jnp.int32, (half, 1), 0)
    upper = _negate_index(jnp.where(pos == 0, centre, jnp.concatenate(mirrored, axis=0)))
    o_ref[0, half:, :] = (x0[half:] * upper).astype(o_ref.dtype)


def _hyena_conv(hy, conv_w, conv_b, tabs, p_spec, q_spec, k_nyq):
    bsz, seq, _ = hy.shape
    nc = HY_WIDTH // HY_CW
    t_sym, t_anti, g_sym, g_anti = tabs
    conv_b = conv_b.reshape(1, -1)
    sig = lambda part: pl.BlockSpec((1, seq, HY_CW), lambda j, b: (b, 0, part * nc + j))
    cw = lambda part: pl.BlockSpec((3, HY_CW), lambda j, b: (0, part * nc + j))
    cb = lambda part: pl.BlockSpec((1, HY_CW), lambda j, b: (0, part * nc + j))
    spec = lambda rows: pl.BlockSpec((rows, HY_CW), lambda j, b: (0, j))
    s = pl.pallas_call(
        _hy_fwd_kernel,
        grid=(nc, bsz),
        in_specs=[_resident(t_sym.shape), _resident(t_anti.shape), sig(1), sig(2), cw(1), cb(1), cw(2), cb(2),
                  spec(seq), spec(seq), spec(1)],
        out_specs=pl.BlockSpec((1, 2 * seq, HY_CW), lambda j, b: (b, 0, j)),
        out_shape=jax.ShapeDtypeStruct((bsz, 2 * seq, HY_WIDTH), BF16),
        compiler_params=_params("parallel", "parallel"),
        name="hy_fwd",
    )(t_sym, t_anti, hy, hy, conv_w, conv_b, conv_w, conv_b, p_spec, q_spec, k_nyq)
    return pl.pallas_call(
        _hy_inv_kernel,
        grid=(nc, bsz),
        in_specs=[_resident(g_sym.shape), _resident(g_anti.shape),
                  pl.BlockSpec((1, 2 * seq, HY_CW), lambda j, b: (b, 0, j)), sig(0), cw(0), cb(0)],
        out_specs=pl.BlockSpec((1, seq, HY_CW), lambda j, b: (b, 0, j)),
        out_shape=jax.ShapeDtypeStruct((bsz, seq, HY_WIDTH), BF16),
        compiler_params=_params("parallel", "parallel"),
        name="hy_inv",
    )(g_sym, g_anti, s, hy, conv_w, conv_b)


ROW_PARTS = D_MODEL // LANES


def _token_tile(t):
    return pl.ds(pl.multiple_of(t * ROW_PARTS, ROW_PARTS), ROW_PARTS)


def _load_token_rows(ref, n_tok):
    return jnp.concatenate([ref[pl.ds(j, n_tok, stride=ROW_PARTS), :] for j in range(ROW_PARTS)], axis=-1)


def _store_token_rows(ref, val):
    for j in range(ROW_PARTS):
        ref[pl.ds(j, val.shape[0], stride=ROW_PARTS), :] = val[:, j * LANES:(j + 1) * LANES]


POST_TILE = 512


def _post_kernel(of_ref, ob_ref, og_ref, yh_ref, ma_ref, mb_ref, x_ref, g1_ref, sc2_ref, sh2_ref,
                 gn_ref, nf_ref, wa_ref, wb_ref, wo_ref, wr_ref, br_ref,
                 x1_ref, h2_ref, idx_ref, wts_ref):
    o = of_ref[0].astype(F32) + ob_ref[0].astype(F32)
    gn = gn_ref[...]
    heads = [_rms(o[:, h * GLA_HEAD_V:(h + 1) * GLA_HEAD_V], gn) for h in range(GLA_HEADS)]
    og = og_ref[0].astype(F32)
    a_in = jnp.concatenate(heads, axis=-1) * (og * _sigmoid(og))
    branch_a = _dot(a_in.astype(BF16), wa_ref[...])
    branch_b = _dot(yh_ref[0], wb_ref[...])
    y = _sigmoid(ma_ref[0].astype(F32)) * branch_a + _sigmoid(mb_ref[0].astype(F32)) * branch_b
    x1 = x_ref[0] + g1_ref[0] * _dot(y.astype(BF16), wo_ref[...])
    x1_ref[0] = x1
    h2 = _rms(x1, nf_ref[...]) * (1.0 + sc2_ref[0]) + sh2_ref[0]
    _store_token_rows(h2_ref.at[0], h2)

    logits = _dot3(h2, wr_ref[...]) + br_ref[...]
    lane = lax.broadcasted_iota(jnp.int32, logits.shape, 1)
    slot = lax.broadcasted_iota(jnp.int32, (logits.shape[0], TOP_K), 1)
    work = logits
    tops = []
    idx = jnp.zeros((logits.shape[0], TOP_K), jnp.int32)
    for k in range(TOP_K):
        m = jnp.max(work, axis=-1, keepdims=True)
        first = jnp.min(jnp.where(work == m, lane, N_EXPERTS), axis=-1, keepdims=True)
        tops.append(m)
        idx = jnp.where(slot == k, first, idx)
        work = jnp.where(lane == first, -jnp.inf, work)
    es = [jnp.exp(m - tops[0]) for m in tops]
    inv = 1.0 / functools.reduce(lambda a, b: a + b, es)
    wts = jnp.zeros((logits.shape[0], TOP_K), F32)
    for k, e in enumerate(es):
        wts = jnp.where(slot == k, e * inv, wts)
    idx_ref[0] = idx
    wts_ref[0] = wts


def _post(o_f, o_b, og, y_h, m_a, m_b, x, mods, gla_norm, norm_ffn, w_a, w_b, w_o, w_router, b_router):
    bsz, seq, _ = x.shape
    tok = lambda n: pl.BlockSpec((1, POST_TILE, n), lambda b, i: (b, i, 0))
    mod = lambda which: pl.BlockSpec((1, 1, D_MODEL), lambda b, i: (b, 0, which))
    in_specs = [tok(D_MODEL)] * 7 + [mod(MOD_G1), mod(MOD_SC2), mod(MOD_SH2),
                                     _resident((1, GLA_HEAD_V)), _resident((1, D_MODEL)),
                                     _resident(w_a.shape), _resident(w_b.shape), _resident(w_o.shape),
                                     _resident(w_router.shape), _resident((1, N_EXPERTS))]
    return pl.pallas_call(
        _post_kernel,
        grid=(bsz, seq // POST_TILE),
        in_specs=in_specs,
        out_specs=[tok(D_MODEL), pl.BlockSpec((1, POST_TILE * ROW_PARTS, LANES), lambda b, i: (b, i, 0)),
                   tok(TOP_K), tok(TOP_K)],
        out_shape=[jax.ShapeDtypeStruct((bsz, seq, D_MODEL), F32),
                   jax.ShapeDtypeStruct((bsz, seq * ROW_PARTS, LANES), F32),
                   jax.ShapeDtypeStruct((bsz, seq, TOP_K), jnp.int32), jax.ShapeDtypeStruct((bsz, seq, TOP_K), F32)],
        compiler_params=_params("parallel", "parallel"),
        name="post",
    )(o_f, o_b, og, y_h, m_a, m_b, x, mods, mods, mods, gla_norm.reshape(1, -1), norm_ffn.reshape(1, -1),
      w_a, w_b, w_o, w_router, b_router.reshape(1, -1))


UP_GROUP = 2 * LANES
N_UP_GROUPS = 2 * D_MODEL // UP_GROUP


def _deinterleave_kernel(w_ref, o_ref):
    r = lax.broadcasted_iota(jnp.int32, (UP_GROUP, UP_GROUP), 0)
    c = lax.broadcasted_iota(jnp.int32, (UP_GROUP, UP_GROUP), 1)
    perm = jnp.where(c == (r % 2) * LANES + r // 2, 1.0, 0.0).astype(BF16)
    for g in range(w_ref.shape[2] // UP_GROUP):
        cols = slice(g * UP_GROUP, (g + 1) * UP_GROUP)
        o_ref[0, :, cols] = _dot(w_ref[0, :, cols].astype(BF16), perm).astype(BF16)


def _deinterleave_up(w_up):
    n_exp, d_in, d_up = w_up.shape
    blk = pl.BlockSpec((1, d_in, d_up // 2), lambda e, j: (e, 0, j))
    return pl.pallas_call(
        _deinterleave_kernel,
        grid=(n_exp, 2),
        in_specs=[blk], out_specs=blk,
        out_shape=jax.ShapeDtypeStruct(w_up.shape, BF16),
        compiler_params=_params("parallel", "parallel"),
        name="deinterleave",
    )(w_up)


ROUTE_TILE = 512
EXPERT_TILE = 512
COMBINE_TILE = 256


def _exact_count_dot(a, m):
    a0 = a.astype(BF16)
    r1 = a - a0.astype(F32)
    a1 = r1.astype(BF16)
    a2 = (r1 - a1.astype(F32)).astype(BF16)
    return _dot(a0, m) + (_dot(a1, m) + _dot(a2, m))


def _route_kernel(idx_ref, pos_ref, cnt_ref, counts, base, offs):
    p = pl.program_id(0)
    i = pl.program_id(1)
    n_tok = idx_ref.shape[0]
    idx = idx_ref[...]
    lane = lax.broadcasted_iota(jnp.int32, (n_tok, N_EXPERTS), 1)
    hot = jnp.zeros((n_tok, N_EXPERTS), F32)
    for k in range(TOP_K):
        hot = hot + jnp.where(lane == idx[:, k:k + 1], 1.0, 0.0)
    colsum = jnp.sum(hot, axis=0, keepdims=True)

    @pl.when((p == 0) & (i == 0))
    def _():
        counts[...] = jnp.zeros_like(counts)

    @pl.when(p == 0)
    def _():
        counts[...] += colsum

    @pl.when((p == 1) & (i == 0))
    def _():
        r = lax.broadcasted_iota(jnp.int32, (N_EXPERTS, N_EXPERTS), 0)
        c = lax.broadcasted_iota(jnp.int32, (N_EXPERTS, N_EXPERTS), 1)
        before = jnp.where(r < c, 1.0, 0.0).astype(BF16)
        offs[...] = _exact_count_dot(jnp.broadcast_to(counts[...], (8, N_EXPERTS)), before)[0:1]
        base[...] = jnp.zeros_like(base)

    @pl.when(p == 1)
    def _():
        r = lax.broadcasted_iota(jnp.int32, (n_tok, n_tok), 0)
        c = lax.broadcasted_iota(jnp.int32, (n_tok, n_tok), 1)
        earlier = jnp.where(c < r, 1.0, 0.0).astype(BF16)
        dense = offs[...] + base[...] + _dot(earlier, hot.astype(BF16))
        slot = lax.broadcasted_iota(jnp.int32, (n_tok, TOP_K), 1)
        pos = jnp.zeros((n_tok, TOP_K), F32)
        for k in range(TOP_K):
            mine = jnp.sum(jnp.where(lane == idx[:, k:k + 1], dense, 0.0), axis=-1, keepdims=True)
            pos = jnp.where(slot == k, mine, pos)
        pos_ref[...] = pos.astype(jnp.int32)
        base[...] += colsum

    cnt_ref[...] = counts[...].astype(jnp.int32)


def _route(idx):
    n_tok = idx.shape[0]
    return pl.pallas_call(
        _route_kernel,
        grid=(2, n_tok // ROUTE_TILE),
        in_specs=[pl.BlockSpec((ROUTE_TILE, TOP_K), lambda p, i: (i, 0))],
        out_specs=[pl.BlockSpec((ROUTE_TILE, TOP_K), lambda p, i: (i * p, 0)),
                   pl.BlockSpec((1, N_EXPERTS), lambda p, i: (0, 0))],
        out_shape=[jax.ShapeDtypeStruct((n_tok, TOP_K), jnp.int32), jax.ShapeDtypeStruct((1, N_EXPERTS), jnp.int32)],
        scratch_shapes=[pltpu.VMEM((1, N_EXPERTS), F32)] * 3,
        compiler_params=_params("arbitrary", "arbitrary"),
        name="route",
    )(idx)


def _start_rows(n_tok, copy):
    def issue(t, carry):
        for k in range(TOP_K):
            copy(t, k).start(priority=k % 2)
        return carry

    lax.fori_loop(0, n_tok, issue, 0, unroll=4)


def _wait_rows(n_tok, copy):
    def drain(t, carry):
        for k in range(TOP_K):
            copy(t, k).wait()
        return carry

    lax.fori_loop(0, n_tok, drain, 0, unroll=4)


def _rows_one_step_ahead(n_tok, copies):
    i = pl.program_id(0)

    @pl.when(i == 0)
    def _():
        _start_rows(n_tok, copies(i, False))

    @pl.when(i + 1 < pl.num_programs(0))
    def _():
        _start_rows(n_tok, copies(i + 1, True))

    _wait_rows(n_tok, copies(i, False))


def _dispatch_kernel(pos_ref, h_ref, xs_ref, sem):
    def copy(t, k):
        return pltpu.make_async_copy(h_ref.at[_token_tile(t), :], xs_ref.at[_token_tile(pos_ref[t * TOP_K + k]), :], sem)

    _start_rows(ROUTE_TILE, copy)
    _wait_rows(ROUTE_TILE, copy)


def _index_blocks(tile, n_steps):
    blk = (tile * TOP_K,)
    return [pl.BlockSpec(blk, lambda i: (i,), memory_space=pltpu.SMEM),
            pl.BlockSpec(blk, lambda i: (jnp.minimum(i + 1, n_steps - 1),), memory_space=pltpu.SMEM)]


def _dispatch(pos_flat, h2):
    n_tok = h2.shape[0] // ROW_PARTS
    n_steps = n_tok // ROUTE_TILE
    return pl.pallas_call(
        _dispatch_kernel,
        grid=(n_steps,),
        in_specs=[_index_blocks(ROUTE_TILE, n_steps)[0],
                  pl.BlockSpec((ROUTE_TILE * ROW_PARTS, LANES), lambda i: (i, 0))],
        out_specs=pl.BlockSpec(memory_space=pl.ANY),
        out_shape=jax.ShapeDtypeStruct((n_tok * TOP_K * ROW_PARTS, LANES), F32),
        scratch_shapes=[pltpu.SemaphoreType.DMA],
        compiler_params=_params("arbitrary"),
        name="dispatch",
    )(pos_flat, h2)


def _experts_kernel(tile_ref, exp_ref, lo_ref, hi_ref, x_ref, wu_ref, bu_ref, wd_ref, bd_ref, o_ref):
    w = pl.program_id(0)
    lo = lo_ref[w]
    hi = hi_ref[w]
    start = tile_ref[w] * EXPERT_TILE

    @pl.when(hi > lo)
    def _():
        h = _load_token_rows(x_ref, EXPERT_TILE).astype(BF16)
        acts = []
        for g in range(N_UP_GROUPS):
            cols = slice(g * UP_GROUP, (g + 1) * UP_GROUP)
            u = _dot(h, wu_ref[0, :, cols]) + bu_ref[0, :, cols]
            glu = jnp.minimum(u[:, :LANES], SWIGLU_LIMIT)
            lin = jnp.clip(u[:, LANES:], -SWIGLU_LIMIT, SWIGLU_LIMIT)
            acts.append((glu * _sigmoid(SWIGLU_ALPHA * glu) * (lin + 1.0)).astype(BF16))
        y = _dot(jnp.concatenate(acts, axis=-1), wd_ref[0]) + bd_ref[0]
        row = start + lax.broadcasted_iota(jnp.int32, (EXPERT_TILE, 1), 0)
        mine = (row >= lo) & (row < hi)

        @pl.when(lo == start)
        def _():
            _store_token_rows(o_ref, jnp.where(mine, y, 0.0))

        @pl.when(lo != start)
        def _():
            _store_token_rows(o_ref, jnp.where(mine, y, _load_token_rows(o_ref, EXPERT_TILE)))


def _work_items(counts, n_rows):
    n_tiles = n_rows // EXPERT_TILE
    ends = jnp.cumsum(counts)
    tile_ends = jnp.arange(1, n_tiles + 1, dtype=jnp.int32) * EXPERT_TILE
    n_items = n_tiles + N_EXPERTS
    count = lambda cond: jnp.sum(cond.astype(jnp.int32), axis=1)
    slot_t = jnp.arange(n_tiles, dtype=jnp.int32) + count(ends[None, :] < tile_ends[:, None])
    slot_e = jnp.arange(N_EXPERTS, dtype=jnp.int32) + count(tile_ends[None, :] <= ends[:, None])
    slots = jnp.concatenate([slot_t, slot_e])
    vals = jnp.concatenate([tile_ends, ends])
    item = jnp.arange(n_items, dtype=jnp.int32)
    hi = jnp.sum(jnp.where(slots[None, :] == item[:, None], vals[None, :], 0), axis=1)
    lo = jnp.concatenate([jnp.zeros((1,), jnp.int32), hi[:-1]])
    tile_id = jnp.minimum(lo // EXPERT_TILE, n_tiles - 1)
    exp_id = jnp.minimum(count(ends[None, :] <= lo[:, None]), N_EXPERTS - 1)
    return tile_id, exp_id, lo, hi


def _experts(items, xs, w_up_g, b_up_g, w_down, b_down):
    n_items = items[0].shape[0]
    rows = lambda w, t, e, lo, hi: (t[w], 0)
    per_expert = lambda r, c: pl.BlockSpec((1, r, c), lambda w, t, e, lo, hi: (e[w], 0, 0))
    return pl.pallas_call(
        _experts_kernel,
        grid_spec=pltpu.PrefetchScalarGridSpec(
            num_scalar_prefetch=4,
            grid=(n_items,),
            in_specs=[pl.BlockSpec((EXPERT_TILE * ROW_PARTS, LANES), rows),
                      per_expert(D_MODEL, 2 * D_MODEL), per_expert(1, 2 * D_MODEL),
                      per_expert(D_MODEL, D_MODEL), per_expert(1, D_MODEL)],
            out_specs=pl.BlockSpec((EXPERT_TILE * ROW_PARTS, LANES), rows)),
        out_shape=jax.ShapeDtypeStruct(xs.shape, F32),
        compiler_params=_params("arbitrary"),
        name="experts",
    )(*items, xs, w_up_g, b_up_g, w_down, b_down)


def _combine_kernel(pos_ref, nxt_ref, ys_ref, wts_ref, x1_ref, g2_ref, nf_ref, o_ref, buf, sem):
    def copies(step, use_next):
        p_ref = nxt_ref if use_next else pos_ref

        def copy(t, k):
            src = ys_ref.at[_token_tile(p_ref[t * TOP_K + k]), :]
            return pltpu.make_async_copy(src, buf.at[step % 2, k, _token_tile(t), :], sem.at[step % 2])
        return copy

    n_tok = x1_ref.shape[0]
    _rows_one_step_ahead(n_tok, copies)
    mine = buf.at[pl.program_id(0) % 2]
    wts = wts_ref[...]
    acc = wts[:, 0:1] * _load_token_rows(mine.at[0], n_tok)
    for k in range(1, TOP_K):
        acc = acc + wts[:, k:k + 1] * _load_token_rows(mine.at[k], n_tok)
    o_ref[...] = _rms(x1_ref[...] + g2_ref[0] * acc, nf_ref[...])


def _combine(pos_flat, ys, wts, x1, mods, norm_final, seq):
    n_tok = x1.shape[0]
    tiles_per_sample = seq // COMBINE_TILE
    tok = lambda n: pl.BlockSpec((COMBINE_TILE, n), lambda i: (i, 0))
    n_steps = n_tok // COMBINE_TILE
    return pl.pallas_call(
        _combine_kernel,
        grid=(n_steps,),
        in_specs=_index_blocks(COMBINE_TILE, n_steps) + [
            pl.BlockSpec(memory_space=pl.ANY), tok(TOP_K), tok(D_MODEL),
            pl.BlockSpec((1, 1, D_MODEL), lambda i: (i // tiles_per_sample, 0, MOD_G2)),
            pl.BlockSpec((1, D_MODEL), lambda i: (0, 0))],
        out_specs=tok(D_MODEL),
        out_shape=jax.ShapeDtypeStruct((n_tok, D_MODEL), F32),
        scratch_shapes=[pltpu.VMEM((2, TOP_K, COMBINE_TILE * ROW_PARTS, LANES), F32),
                        pltpu.SemaphoreType.DMA((2,))],
        compiler_params=_params("arbitrary"),
        name="combine",
    )(pos_flat, pos_flat, ys, wts, x1, mods, norm_final.reshape(1, -1))


def _moe(h2, idx, wts, w_up_g, b_up_g, w_down, b_down, x1, mods, norm_final):
    bsz, seq, _ = x1.shape
    n_tok = bsz * seq
    pos, counts = _route(idx.reshape(n_tok, TOP_K))
    pos_flat = pos.reshape(-1)
    xs = _dispatch(pos_flat, h2.reshape(n_tok * ROW_PARTS, LANES))
    ys = _experts(_work_items(counts.reshape(-1), n_tok * TOP_K), xs, w_up_g, b_up_g, w_down, b_down)
    out = _combine(pos_flat, ys, wts.reshape(n_tok, TOP_K), x1.reshape(n_tok, D_MODEL), mods, norm_final, seq)
    return out.reshape(bsz, seq, D_MODEL)


def kernel(x, c, ctx, c_ctx, w_ada, b_ada, norm_mix, norm_ffn, w_in, w_gk_f, b_gk_f, w_gk_b, b_gk_b, gla_norm, w_gla_out, hy_conv_w, hy_conv_b, hy_f_w1, hy_f_b1, hy_f_w2, hy_f_b2, hy_f_w3, hy_f_b3, hy_f_w4, hy_sin_freq, hy_bias, w_hy_out, w_out, w_router, b_router, w_up, b_up, w_down, b_down, norm_final):
    depth = w_ada.shape[0]
    assert depth == 1, "single-layer block: the context stream only feeds later layers"
    bsz, seq, _ = x.shape

    n_rows = -(-(bsz + 1) // 8) * 8
    cc = jnp.zeros((n_rows, D_MODEL), F32).at[:bsz].set(c).at[bsz].set(c_ctx)
    mods = _ada(cc, w_ada[0], b_ada[0]).reshape(n_rows, 1, N_MOD * D_MODEL)

    sizes = (GLA_DK, GLA_DK, GLA_DV, GLA_DV, GLA_GATE_RANK, GLA_GATE_RANK, 3 * HY_WIDTH, D_MODEL, D_MODEL)
    offs = np.concatenate([[0], np.cumsum(sizes)])
    w_in_b = w_in[0].astype(BF16)
    wq, wk, wv, wog, wrf, wrb, why, wma, wmb = [w_in_b[:, offs[i]:offs[i + 1]] for i in range(len(sizes))]
    q, k, v, og, r_f, r_b, hy, m_a, m_b = _inproj(
        x, mods, lambda b: b, norm_mix[0], [wq, wk, wv, wog, wrf, wrb, why, wma, wmb],
        [BF16, BF16, BF16, BF16, F32, F32, BF16, BF16, BF16], tl=512)
    k_c, v_c, rc_f, rc_b = _inproj(
        ctx, mods, lambda b: bsz, norm_mix[0], [wk, wv, wrf, wrb], [BF16, BF16, F32, F32], tl=ctx.shape[1])

    w_gk = jnp.stack([w_gk_f[0], w_gk_b[0]])
    b_gk = jnp.stack([b_gk_f[0], b_gk_b[0]])[:, None, :]
    o_f, o_b = _gla(q, k, v, r_f, r_b, k_c, v_c, rc_f, rc_b, w_gk, b_gk)

    t_sym, t_anti = [jnp.asarray(t).astype(BF16) for t in _dft_tables(seq)]
    hs, hd, pn = _hyena_filter(seq, hy_f_w1[0], hy_f_b1[0], hy_f_w2[0], hy_f_b2[0], hy_f_w3[0], hy_f_b3[0],
                               hy_f_w4[0], hy_sin_freq[0])
    p_spec, q_spec, k_nyq = _hyena_spectrum(t_sym, t_anti, hs, hd, pn, hy_bias[0])
    y_h = _hyena_conv(hy, hy_conv_w[0], hy_conv_b[0], (t_sym, t_anti, t_sym.T, t_anti.T), p_spec, q_spec, k_nyq)

    x1, h2, idx, wts = _post(o_f, o_b, og, y_h, m_a, m_b, x, mods, gla_norm[0], norm_ffn[0],
                             w_gla_out[0].astype(BF16), w_hy_out[0].astype(BF16), w_out[0].astype(BF16),
                             w_router[0], b_router[0])

    b_up_g = b_up[0].reshape(N_EXPERTS, N_UP_GROUPS, LANES, 2).transpose(0, 1, 3, 2).reshape(N_EXPERTS, 1, -1)
    return _moe(h2, idx, wts, _deinterleave_up(w_up[0]), b_up_g, w_down[0].astype(BF16), b_down[0][:, None, :],
                x1, mods, norm_final)
```

```python
import functools
import math

import jax
import jax.numpy as jnp
import numpy as np
from jax import lax
from jax.experimental import pallas as pl
from jax.experimental.pallas import tpu as pltpu

F32 = jnp.float32
BF16 = jnp.bfloat16

D_MODEL = 1024
GRID_W = 64
EPS = 1e-6
N_MOD = 6

GLA_HEADS = 4
GLA_HEAD_K = 128
GLA_HEAD_V = 256
GLA_DK = GLA_HEADS * GLA_HEAD_K
GLA_DV = GLA_HEADS * GLA_HEAD_V
GLA_GATE_RANK = 16
GLA_GATE_NORM = 16.0
GLA_CHUNK = 64

HY_WIDTH = D_MODEL
HY_EMB = 33
HY_FAST_DECAY = 0.3
HY_SLOW_DECAY = 1.5
HY_TARGET = 1e-2

N_EXPERTS = 32
TOP_K = 4
SWIGLU_LIMIT = 7.0
SWIGLU_ALPHA = 1.702

V7X_VMEM_BYTES = 64 * 1024 * 1024
VMEM_LIMIT = V7X_VMEM_BYTES - 8 * 1024 * 1024
LANES = 128

MOD_SH1, MOD_SC1, MOD_G1, MOD_SH2, MOD_SC2, MOD_G2 = range(N_MOD)


def _params(*sem):
    return pltpu.CompilerParams(dimension_semantics=sem, vmem_limit_bytes=VMEM_LIMIT)


def _resident(shape):
    return pl.BlockSpec(shape, lambda *_: (0,) * len(shape), pipeline_mode=pl.Buffered(1))


def _dot(a, b):
    return jnp.dot(a, b, preferred_element_type=F32)


def _dot_nt(a, b):
    return lax.dot_general(a, b, (((1,), (1,)), ((), ())), preferred_element_type=F32)


def _dot_tn(a, b):
    return lax.dot_general(a, b, (((0,), (0,)), ((), ())), preferred_element_type=F32)


def _split(a):
    hi = a.astype(BF16)
    lo = (a - hi.astype(F32)).astype(BF16)
    return hi, lo


def _dot3(a, b):
    ah, al = _split(a)
    bh, bl = _split(b)
    return _dot(ah, bh) + (_dot(ah, bl) + _dot(al, bh))


def _sigmoid(x):
    return 1.0 / (1.0 + jnp.exp(-x))


def _log_sigmoid(x):
    return jnp.minimum(x, 0.0) - jnp.log(1.0 + jnp.exp(-jnp.abs(x)))


def _rms(x, w):
    return x * lax.rsqrt(jnp.mean(x * x, axis=-1, keepdims=True) + EPS) * w


def _ada_kernel(c_ref, w_ref, b_ref, o_ref):
    c = c_ref[...]
    o_ref[...] = _dot3(c * _sigmoid(c), w_ref[...]) + b_ref[...]


def _ada(cc, w_ada, b_ada):
    rows = cc.shape[0]
    return pl.pallas_call(
        _ada_kernel,
        grid=(N_MOD,),
        in_specs=[pl.BlockSpec((rows, D_MODEL), lambda j: (0, 0)),
                  pl.BlockSpec((D_MODEL, D_MODEL), lambda j: (0, j)),
                  pl.BlockSpec((1, D_MODEL), lambda j: (0, j))],
        out_specs=pl.BlockSpec((rows, D_MODEL), lambda j: (0, j)),
        out_shape=jax.ShapeDtypeStruct((rows, N_MOD * D_MODEL), F32),
        compiler_params=_params("parallel"),
        name="ada",
    )(cc, w_ada, b_ada.reshape(1, -1))


INPROJ_COL_CHUNK = 512


def _inproj_kernel(n_out, conv_out, x_ref, sc_ref, sh_ref, nw_ref, *refs):
    n_in = n_out + (0 if conv_out is None else 2)
    w_refs, o_refs = refs[:n_out], refs[n_in:]
    h = _rms(x_ref[0], nw_ref[...]) * (1.0 + sc_ref[0]) + sh_ref[0]
    hb = h.astype(BF16)
    for i, (w_ref, o_ref) in enumerate(zip(w_refs, o_refs)):
        n = w_ref.shape[1]
        for c0 in range(0, n, INPROJ_COL_CHUNK):
            c1 = min(c0 + INPROJ_COL_CHUNK, n)
            y = _dot(hb, w_ref[:, c0:c1])
            if i == conv_out:
                cw_ref, cb_ref = refs[n_out:n_in]
                y = _short_conv(y, cw_ref[:, c0:c1], cb_ref[:, c0:c1])
            o_ref[0, :, c0:c1] = y.astype(o_ref.dtype)


def _inproj(x, mods, mod_row, norm_w, weights, out_dtypes, tl, conv=None):
    bsz, seq, _ = x.shape
    n_out = len(weights)
    assert tl % GRID_W == 0
    in_specs = [pl.BlockSpec((1, tl, D_MODEL), lambda b, i: (b, i, 0)),
                pl.BlockSpec((1, 1, D_MODEL), lambda b, i: (mod_row(b), 0, MOD_SC1)),
                pl.BlockSpec((1, 1, D_MODEL), lambda b, i: (mod_row(b), 0, MOD_SH1)),
                _resident((1, D_MODEL))]
    in_specs += [_resident(w.shape) for w in weights]
    extra = []
    if conv is not None:
        extra = [conv[1], conv[2].reshape(1, -1)]
        in_specs += [_resident(a.shape) for a in extra]
    out_specs = [pl.BlockSpec((1, tl, w.shape[1]), lambda b, i: (b, i, 0)) for w in weights]
    out_shape = [jax.ShapeDtypeStruct((bsz, seq, w.shape[1]), dt) for w, dt in zip(weights, out_dtypes)]
    return pl.pallas_call(
        functools.partial(_inproj_kernel, n_out, None if conv is None else conv[0]),
        grid=(bsz, seq // tl),
        in_specs=in_specs, out_specs=out_specs, out_shape=out_shape,
        compiler_params=_params("parallel", "parallel"),
        name="inproj",
    )(x, mods, mods, norm_w.reshape(1, -1), *weights, *extra)


GLA_TILE = 256
GLA_NCH = GLA_TILE // GLA_CHUNK
GLA_SCALE = GLA_HEAD_K ** -0.5
GLA_HPS = 4
GLA_KW = GLA_HPS * GLA_HEAD_K
GLA_VW = GLA_HPS * GLA_HEAD_V


@functools.lru_cache(maxsize=None)
def _chunk_triangles():
    i = np.arange(GLA_TILE)
    same = (i[:, None] // GLA_CHUNK) == (i[None, :] // GLA_CHUNK)
    return np.stack([same & (i[None, :] <= i[:, None]), same & (i[None, :] >= i[:, None])]).astype(np.float32)


def _gla_kernel(tri_ref, qf_ref, kf_ref, vf_ref, rf_ref, qb_ref, kb_ref, vb_ref, rb_ref,
                kc_ref, vc_ref, rcf_ref, rcb_ref, wgk_ref, bgk_ref,
                of_ref, ob_ref, sf_ref, sb_ref):
    t = pl.program_id(2)
    crow = lax.broadcasted_iota(jnp.int32, (GLA_CHUNK, GLA_CHUNK), 0)
    ccol = lax.broadcasted_iota(jnp.int32, (GLA_CHUNK, GLA_CHUNK), 1)
    heads = range(GLA_HPS)

    def rows(c):
        return slice(c * GLA_CHUNK, (c + 1) * GLA_CHUNK)

    def hk(h):
        return slice(h * GLA_HEAD_K, (h + 1) * GLA_HEAD_K)

    def hv(h):
        return slice(h * GLA_HEAD_V, (h + 1) * GLA_HEAD_V)

    def order(fwd):
        return range(GLA_NCH) if fwd else range(GLA_NCH - 1, -1, -1)

    def cum_decay(r, d):
        z = _dot3(r, wgk_ref[d]) + bgk_ref[d]
        g = _log_sigmoid(z) * (1.0 / GLA_GATE_NORM)
        gh, gl = _split(g)
        return _dot(tri_ref[d], gh) + _dot(tri_ref[d], gl)

    def updates(k_ref, v_ref, b, fwd):
        out = {}
        for c in range(GLA_NCH):
            i = c * GLA_CHUNK + (GLA_CHUNK - 1 if fwd else 0)
            total = b[i:i + 1, :]
            k_upd = (k_ref[0, rows(c), :].astype(F32) * jnp.exp(total - b[rows(c)])).astype(BF16)
            dec = jnp.exp(total)
            for h in heads:
                out[c, h] = (dec[:, hk(h)], _dot_tn(v_ref[0, rows(c), hv(h)], k_upd[:, hk(h)]))
        return out

    def context_state(s_ref, r_ref, d, fwd):
        upd = updates(kc_ref, vc_ref, cum_decay(r_ref[0], d), fwd)
        for h in heads:
            st = jnp.zeros((GLA_HEAD_V, GLA_HEAD_K), F32)
            for c in order(fwd):
                dec, inc = upd[c, h]
                st = st * dec + inc
            s_ref[h] = st

    def scan(s_ref, q_ref, k_ref, v_ref, r_ref, o_ref, d, fwd):
        mask = (ccol <= crow) if fwd else (ccol >= crow)
        b = cum_decay(r_ref[0], d)
        q_dec = (q_ref[0].astype(F32) * (jnp.exp(b) * GLA_SCALE)).astype(BF16)
        k_inv = (k_ref[0].astype(F32) * jnp.exp(-b)).astype(BF16)
        upd = updates(k_ref, v_ref, b, fwd)
        intra = {}
        for c in range(GLA_NCH):
            for h in heads:
                att = jnp.where(mask, _dot_nt(q_dec[rows(c), hk(h)], k_inv[rows(c), hk(h)]), 0.0).astype(BF16)
                intra[c, h] = _dot(att, v_ref[0, rows(c), hv(h)])
        enter = {}
        for h in heads:
            st = s_ref[h]
            for c in order(fwd):
                enter[c, h] = st.astype(BF16)
                dec, inc = upd[c, h]
                st = st * dec + inc
            s_ref[h] = st
        for c in range(GLA_NCH):
            for h in heads:
                o = intra[c, h] + _dot_nt(q_dec[rows(c), hk(h)], enter[c, h])
                o_ref[0, rows(c), hv(h)] = o.astype(o_ref.dtype)

    @pl.when(t == 0)
    def _():
        context_state(sf_ref, rcf_ref, 0, True)
        context_state(sb_ref, rcb_ref, 1, False)

    scan(sf_ref, qf_ref, kf_ref, vf_ref, rf_ref, of_ref, 0, True)
    scan(sb_ref, qb_ref, kb_ref, vb_ref, rb_ref, ob_ref, 1, False)


def _gla(q, k, v, r_f, r_b, k_c, v_c, rc_f, rc_b, w_gk, b_gk):
    bsz, seq, _ = q.shape
    assert k_c.shape[1] == GLA_TILE
    nt = seq // GLA_TILE
    tri = jnp.asarray(_chunk_triangles()).astype(BF16)
    fwd = lambda b, h, t: (b, t, h)
    bwd = lambda b, h, t: (b, nt - 1 - t, h)
    fwd0 = lambda b, h, t: (b, t, 0)
    bwd0 = lambda b, h, t: (b, nt - 1 - t, 0)
    ctx = lambda b, h, t: (b, 0, h)
    ctx0 = lambda b, h, t: (b, 0, 0)
    kblk = (1, GLA_TILE, GLA_KW)
    vblk = (1, GLA_TILE, GLA_VW)
    rblk = (1, GLA_TILE, GLA_GATE_RANK)
    in_specs = [_resident(tri.shape),
                pl.BlockSpec(kblk, fwd), pl.BlockSpec(kblk, fwd), pl.BlockSpec(vblk, fwd), pl.BlockSpec(rblk, fwd0),
                pl.BlockSpec(kblk, bwd), pl.BlockSpec(kblk, bwd), pl.BlockSpec(vblk, bwd), pl.BlockSpec(rblk, bwd0),
                pl.BlockSpec(kblk, ctx), pl.BlockSpec(vblk, ctx), pl.BlockSpec(rblk, ctx0), pl.BlockSpec(rblk, ctx0),
                pl.BlockSpec((2, GLA_GATE_RANK, GLA_KW), lambda b, h, t: (0, 0, h)),
                pl.BlockSpec((2, 1, GLA_KW), lambda b, h, t: (0, 0, h))]
    out_specs = [pl.BlockSpec(vblk, fwd), pl.BlockSpec(vblk, bwd)]
    out_shape = [jax.ShapeDtypeStruct((bsz, seq, GLA_DV), BF16)] * 2
    return pl.pallas_call(
        _gla_kernel,
        grid=(bsz, GLA_HEADS // GLA_HPS, nt),
        in_specs=in_specs, out_specs=out_specs, out_shape=out_shape,
        scratch_shapes=[pltpu.VMEM((GLA_HPS, GLA_HEAD_V, GLA_HEAD_K), F32)] * 2,
        compiler_params=_params("parallel", "parallel", "arbitrary"),
        name="gla",
    )(tri, q, k, v, r_f, q, k, v, r_b, k_c, v_c, rc_f, rc_b, w_gk, b_gk)


HY_CW = 256
HY_ROW_CHUNK = 512
MLP_PAD = 128


@functools.lru_cache(maxsize=None)
def _dft_tables(seq):
    n, half = 2 * seq, seq // 2
    m = np.arange(half)[:, None]
    s = np.arange(half)[None, :]
    ang = 2.0 * np.pi * np.arange(n) / n
    cos = lambda k: np.cos(ang)[(k * s) % n]
    sin = lambda k: np.sin(ang)[(k * s) % n]
    sym = np.concatenate([cos(2 * m), sin(2 * m + 1)], axis=0)
    anti = np.concatenate([sin(2 * m), cos(2 * m + 1)], axis=0)
    return sym.astype(np.float32), anti.astype(np.float32)


@functools.lru_cache(maxsize=None)
def _filter_features(seq):
    bands = (HY_EMB - 1) // 2
    pos = np.arange(seq, dtype=np.float64)[:, None]
    t = pos / max(seq - 1, 1)
    f = np.linspace(1e-4, bands - 1, bands)[None]
    ang = (2.0 * math.pi / seq) * pos * f
    z = np.concatenate([t, np.cos(ang), -np.sin(ang)], axis=-1)
    out = np.zeros((seq, MLP_PAD), np.float32)
    out[:, :HY_EMB] = z
    deltas = np.abs(np.linspace(math.log(HY_TARGET) / HY_SLOW_DECAY, math.log(HY_TARGET) / HY_FAST_DECAY, HY_WIDTH))
    return out, deltas.astype(np.float32)[None]


def _filter_kernel(z_ref, w1_ref, b1_ref, w2_ref, b2_ref, w3_ref, b3_ref, fr_ref, w4f_ref, w4b_ref, dl_ref,
                   hs_ref, hd_ref, pn_ref):
    z = z_ref[...]
    fr = fr_ref[...]
    h = jnp.sin(fr * (_dot3(z, w1_ref[...]) + b1_ref[...]))
    h = jnp.sin(fr * (_dot3(h, w2_ref[...]) + b2_ref[...]))
    h = jnp.sin(fr * (_dot3(h, w3_ref[...]) + b3_ref[...]))
    window = jnp.exp(-z[:, 0:1] * dl_ref[...])
    h_f = _dot3(h, w4f_ref[...]) * window
    h_b = _dot3(h, w4b_ref[...]) * window
    pos = lax.broadcasted_iota(jnp.int32, (z.shape[0], 1), 0)
    h_b = jnp.where(pos == 0, 0.0, h_b)
    hs = h_f + h_b
    hs_ref[...] = hs.astype(BF16)
    hd_ref[...] = (h_b - h_f).astype(BF16)
    sign = jnp.where(pos % 2 == 0, 1.0, -1.0)
    pn_ref[...] = jnp.sum(hs * sign, axis=0, keepdims=True)


def _pad2(a, rows, cols):
    return jnp.zeros((rows, cols), F32).at[:a.shape[0], :a.shape[1]].set(a.astype(F32))


def _hyena_filter(seq, w1, b1, w2, b2, w3, b3, w4, freq):
    z_np, deltas_np = _filter_features(seq)
    p = MLP_PAD
    args = [jnp.asarray(z_np), _pad2(w1, p, p), _pad2(b1[None], 1, p), _pad2(w2, p, p), _pad2(b2[None], 1, p),
            _pad2(w3, p, p), _pad2(b3[None], 1, p), _pad2(freq[None], 1, p),
            _pad2(w4[:, :HY_WIDTH], p, HY_WIDTH), _pad2(w4[:, HY_WIDTH:], p, HY_WIDTH), jnp.asarray(deltas_np)]
    full = lambda shape: pl.BlockSpec(shape, lambda j: (0, 0))
    colblk = lambda rows: pl.BlockSpec((rows, HY_CW), lambda j: (0, j))
    in_specs = [full((seq, p)), full((p, p)), full((1, p)), full((p, p)), full((1, p)), full((p, p)), full((1, p)),
                full((1, p)), colblk(p), colblk(p), colblk(1)]
    return pl.pallas_call(
        _filter_kernel,
        grid=(HY_WIDTH // HY_CW,),
        in_specs=in_specs,
        out_specs=[colblk(seq), colblk(seq), colblk(1)],
        out_shape=[jax.ShapeDtypeStruct((seq, HY_WIDTH), BF16), jax.ShapeDtypeStruct((seq, HY_WIDTH), BF16),
                   jax.ShapeDtypeStruct((1, HY_WIDTH), F32)],
        compiler_params=_params("parallel"),
        name="hy_filter",
    )(*args)


REV_BLOCK = 256


def _negate_index(x):
    n = x.shape[0]
    r = lax.broadcasted_iota(jnp.int32, (REV_BLOCK, REV_BLOCK), 0)
    c = lax.broadcasted_iota(jnp.int32, (REV_BLOCK, REV_BLOCK), 1)
    flip = jnp.where(r + c == REV_BLOCK - 1, 1.0, 0.0).astype(BF16)
    if x.dtype == BF16:
        pieces = [x]
    else:
        x0 = x.astype(BF16)
        r1 = x - x0.astype(F32)
        x1 = r1.astype(BF16)
        pieces = [x0, x1, (r1 - x1.astype(F32)).astype(BF16)]
    blocks = []
    for i in range(n // REV_BLOCK - 1, -1, -1):
        rows = slice(i * REV_BLOCK, (i + 1) * REV_BLOCK)
        acc = _dot(flip, pieces[0][rows])
        for p in pieces[1:]:
            acc = acc + _dot(flip, p[rows])
        blocks.append(acc)
    return pltpu.roll(jnp.concatenate(blocks, axis=0), 1, 0)


def _fold(x):
    half = x.shape[0] // 2
    lo = x[:half].astype(F32)
    hi = _negate_index(x[half:])
    pos = lax.broadcasted_iota(jnp.int32, (half, 1), 0)
    mid = hi[0:1]
    hi = jnp.where(pos == 0, 0.0, hi)
    return (lo + hi).astype(BF16), (lo - hi).astype(BF16), mid


def _alternating(n):
    pos = lax.broadcasted_iota(jnp.int32, (n, 1), 0)
    return jnp.where(pos % 2 == 0, 1.0, -1.0)


def _spectrum_kernel(ts_ref, ta_ref, hs_ref, hd_ref, pn_ref, bias_ref, p_ref, q_ref, kn_ref):
    half = hs_ref.shape[0] // 2
    bias = bias_ref[...]
    sgn = _alternating(half)
    s_sym, s_anti, s_mid = _fold(hs_ref[...])
    d_sym, d_anti, d_mid = _fold(hd_ref[...])
    p_ref[:half, :] = _dot(ts_ref[:half, :], s_sym) + sgn * s_mid + bias
    p_ref[half:, :] = _dot(ta_ref[half:, :], s_anti) + bias
    q_ref[:half, :] = _dot(ta_ref[:half, :], d_anti)
    q_ref[half:, :] = _dot(ts_ref[half:, :], d_sym) + sgn * d_mid
    kn_ref[...] = pn_ref[...] + bias


def _hyena_spectrum(t_sym, t_anti, hs, hd, pn, bias):
    seq = hs.shape[0]
    colblk = lambda rows: pl.BlockSpec((rows, HY_CW), lambda j: (0, j))
    return pl.pallas_call(
        _spectrum_kernel,
        grid=(HY_WIDTH // HY_CW,),
        in_specs=[_resident(t_sym.shape), _resident(t_anti.shape), colblk(seq), colblk(seq), colblk(1), colblk(1)],
        out_specs=[colblk(seq), colblk(seq), colblk(1)],
        out_shape=[jax.ShapeDtypeStruct((seq, HY_WIDTH), F32)] * 2 + [jax.ShapeDtypeStruct((1, HY_WIDTH), F32)],
        compiler_params=_params("parallel"),
        name="hy_spectrum",
    )(t_sym, t_anti, hs, hd, pn, bias.reshape(1, -1))


def _short_conv(u, w, b):
    n = u.shape[0]
    pos = lax.broadcasted_iota(jnp.int32, (n, 1), 0) % GRID_W
    up = jnp.where(pos == 0, 0.0, pltpu.roll(u, 1, 0))
    dn = jnp.where(pos == GRID_W - 1, 0.0, pltpu.roll(u, n - 1, 0))
    return up * w[0:1] + u * w[1:2] + dn * w[2:3] + b


def _hy_fwd_kernel(ts_ref, ta_ref, x1_ref, v_ref, p_ref, q_ref, kn_ref, s_ref):
    seq = x1_ref.shape[1]
    n, half = 2 * seq, seq // 2
    u_sym, u_anti, u_mid = _fold((v_ref[0].astype(F32) * x1_ref[0].astype(F32)).astype(BF16))
    sgn = _alternating(half)
    nyq = (jnp.sum(sgn * u_sym.astype(F32), axis=0, keepdims=True) + u_mid) * kn_ref[...] * (1.0 / n)
    for r0 in range(0, half, HY_ROW_CHUNK):
        r1 = r0 + HY_ROW_CHUNK
        mid = sgn[r0:r1] * u_mid
        re_e = _dot(ts_ref[r0:r1, :], u_sym) + mid
        im_o = _dot(ts_ref[half + r0:half + r1, :], u_sym) + mid
        im_e = _dot(ta_ref[r0:r1, :], u_anti)
        re_o = _dot(ta_ref[half + r0:half + r1, :], u_anti)
        pe, po = p_ref[r0:r1, :], p_ref[half + r0:half + r1, :]
        qe, qo = q_ref[r0:r1, :], q_ref[half + r0:half + r1, :]
        s_re_e = (re_e * pe + im_e * qe) * (2.0 / n)
        s_im_e = (im_e * pe - re_e * qe) * (2.0 / n)
        if r0 == 0:
            pos = lax.broadcasted_iota(jnp.int32, (HY_ROW_CHUNK, 1), 0)
            s_re_e = jnp.where(pos == 0, re_e * pe * (1.0 / n), s_re_e)
            s_im_e = jnp.where(pos == 0, nyq, s_im_e)
        s_ref[0, r0:r1, :] = s_re_e.astype(BF16)
        s_ref[0, half + r0:half + r1, :] = ((im_o * po - re_o * qo) * (2.0 / n)).astype(BF16)
        s_ref[0, seq + r0:seq + r1, :] = s_im_e.astype(BF16)
        s_ref[0, seq + half + r0:seq + half + r1, :] = ((re_o * po + im_o * qo) * (2.0 / n)).astype(BF16)


def _hy_inv_kernel(gs_ref, ga_ref, s_ref, x0_ref, o_ref):
    seq = x0_ref.shape[1]
    half = seq // 2
    x0 = x0_ref[0].astype(F32)
    sgn = _alternating(half)
    nyq = sgn * s_ref[0, seq:seq + 1, :].astype(F32)
    mirrored = []
    for r0 in range(0, half, HY_ROW_CHUNK):
        r1 = r0 + HY_ROW_CHUNK
        sym = _dot(gs_ref[r0:r1, :], s_ref[0, :seq, :]) + nyq[r0:r1]
        anti = _dot(ga_ref[r0:r1, :], s_ref[0, seq:, :])
        o_ref[0, r0:r1, :] = (x0[r0:r1] * (sym + anti)).astype(o_ref.dtype)
        mirrored.append(sym - anti)
    centre = (jnp.sum(sgn * (s_ref[0, :half, :].astype(F32) + s_ref[0, half:seq, :].astype(F32)), axis=0, keepdims=True)
              + nyq[0:1])
    pos = lax.broadcasted_iota(jnp.int32, (half, 1), 0)
    upper = _negate_index(jnp.where(pos == 0, centre, jnp.concatenate(mirrored, axis=0)).astype(BF16))
    o_ref[0, half:, :] = (x0[half:] * upper).astype(o_ref.dtype)


def _hyena_conv(hy, tabs, p_spec, q_spec, k_nyq):
    bsz, seq, _ = hy.shape
    nc = HY_WIDTH // HY_CW
    t_sym, t_anti, g_sym, g_anti = tabs
    sig = lambda part: pl.BlockSpec((1, seq, HY_CW), lambda j, b: (b, 0, part * nc + j))
    spec = lambda rows: pl.BlockSpec((rows, HY_CW), lambda j, b: (0, j))
    s = pl.pallas_call(
        _hy_fwd_kernel,
        grid=(nc, bsz),
        in_specs=[_resident(t_sym.shape), _resident(t_anti.shape), sig(1), sig(2),
                  spec(seq), spec(seq), spec(1)],
        out_specs=pl.BlockSpec((1, 2 * seq, HY_CW), lambda j, b: (b, 0, j)),
        out_shape=jax.ShapeDtypeStruct((bsz, 2 * seq, HY_WIDTH), BF16),
        compiler_params=_params("parallel", "parallel"),
        name="hy_fwd",
    )(t_sym, t_anti, hy, hy, p_spec, q_spec, k_nyq)
    return pl.pallas_call(
        _hy_inv_kernel,
        grid=(nc, bsz),
        in_specs=[_resident(g_sym.shape), _resident(g_anti.shape),
                  pl.BlockSpec((1, 2 * seq, HY_CW), lambda j, b: (b, 0, j)), sig(0)],
        out_specs=pl.BlockSpec((1, seq, HY_CW), lambda j, b: (b, 0, j)),
        out_shape=jax.ShapeDtypeStruct((bsz, seq, HY_WIDTH), BF16),
        compiler_params=_params("parallel", "parallel"),
        name="hy_inv",
    )(g_sym, g_anti, s, hy)


ROW_PARTS = D_MODEL // LANES


def _token_tile(t):
    return pl.ds(pl.multiple_of(t * ROW_PARTS, ROW_PARTS), ROW_PARTS)


def _load_token_rows(ref, n_tok):
    return jnp.concatenate([ref[pl.ds(j, n_tok, stride=ROW_PARTS), :] for j in range(ROW_PARTS)], axis=-1)


def _store_token_rows(ref, val):
    for j in range(ROW_PARTS):
        ref[pl.ds(j, val.shape[0], stride=ROW_PARTS), :] = val[:, j * LANES:(j + 1) * LANES]


POST_TILE = 512


def _post_kernel(of_ref, ob_ref, og_ref, yh_ref, ma_ref, mb_ref, x_ref, g1_ref, sc2_ref, sh2_ref,
                 gn_ref, nf_ref, wa_ref, wb_ref, wo_ref, wr_ref, br_ref,
                 x1_ref, h2_ref, idx_ref, wts_ref):
    o = of_ref[0].astype(F32) + ob_ref[0].astype(F32)
    gn = gn_ref[...]
    heads = [_rms(o[:, h * GLA_HEAD_V:(h + 1) * GLA_HEAD_V], gn) for h in range(GLA_HEADS)]
    og = og_ref[0].astype(F32)
    a_in = jnp.concatenate(heads, axis=-1) * (og * _sigmoid(og))
    branch_a = _dot(a_in.astype(BF16), wa_ref[...])
    branch_b = _dot(yh_ref[0], wb_ref[...])
    y = _sigmoid(ma_ref[0].astype(F32)) * branch_a + _sigmoid(mb_ref[0].astype(F32)) * branch_b
    x1 = x_ref[0] + g1_ref[0] * _dot(y.astype(BF16), wo_ref[...])
    x1_ref[0] = x1
    h2 = _rms(x1, nf_ref[...]) * (1.0 + sc2_ref[0]) + sh2_ref[0]
    _store_token_rows(h2_ref.at[0], h2)

    logits = _dot3(h2, wr_ref[...]) + br_ref[...]
    lane = lax.broadcasted_iota(jnp.int32, logits.shape, 1)
    slot = lax.broadcasted_iota(jnp.int32, (logits.shape[0], TOP_K), 1)
    work = logits
    tops = []
    idx = jnp.zeros((logits.shape[0], TOP_K), jnp.int32)
    for k in range(TOP_K):
        m = jnp.max(work, axis=-1, keepdims=True)
        first = jnp.min(jnp.where(work == m, lane, N_EXPERTS), axis=-1, keepdims=True)
        tops.append(m)
        idx = jnp.where(slot == k, first, idx)
        work = jnp.where(lane == first, -jnp.inf, work)
    es = [jnp.exp(m - tops[0]) for m in tops]
    inv = 1.0 / functools.reduce(lambda a, b: a + b, es)
    wts = jnp.zeros((logits.shape[0], TOP_K), F32)
    for k, e in enumerate(es):
        wts = jnp.where(slot == k, e * inv, wts)
    idx_ref[0] = idx
    wts_ref[0] = wts


def _post(o_f, o_b, og, y_h, m_a, m_b, x, mods, gla_norm, norm_ffn, w_a, w_b, w_o, w_router, b_router):
    bsz, seq, _ = x.shape
    tok = lambda n: pl.BlockSpec((1, POST_TILE, n), lambda b, i: (b, i, 0))
    mod = lambda which: pl.BlockSpec((1, 1, D_MODEL), lambda b, i: (b, 0, which))
    in_specs = [tok(D_MODEL)] * 7 + [mod(MOD_G1), mod(MOD_SC2), mod(MOD_SH2),
                                     _resident((1, GLA_HEAD_V)), _resident((1, D_MODEL)),
                                     _resident(w_a.shape), _resident(w_b.shape), _resident(w_o.shape),
                                     _resident(w_router.shape), _resident((1, N_EXPERTS))]
    return pl.pallas_call(
        _post_kernel,
        grid=(bsz, seq // POST_TILE),
        in_specs=in_specs,
        out_specs=[tok(D_MODEL), pl.BlockSpec((1, POST_TILE * ROW_PARTS, LANES), lambda b, i: (b, i, 0)),
                   tok(TOP_K), tok(TOP_K)],
        out_shape=[jax.ShapeDtypeStruct((bsz, seq, D_MODEL), F32),
                   jax.ShapeDtypeStruct((bsz, seq * ROW_PARTS, LANES), F32),
                   jax.ShapeDtypeStruct((bsz, seq, TOP_K), jnp.int32), jax.ShapeDtypeStruct((bsz, seq, TOP_K), F32)],
        compiler_params=_params("parallel", "parallel"),
        name="post",
    )(o_f, o_b, og, y_h, m_a, m_b, x, mods, mods, mods, gla_norm.reshape(1, -1), norm_ffn.reshape(1, -1),
      w_a, w_b, w_o, w_router, b_router.reshape(1, -1))


UP_GROUP = 2 * LANES
N_UP_GROUPS = 2 * D_MODEL // UP_GROUP


def _deinterleave_kernel(w_ref, o_ref):
    r = lax.broadcasted_iota(jnp.int32, (UP_GROUP, UP_GROUP), 0)
    c = lax.broadcasted_iota(jnp.int32, (UP_GROUP, UP_GROUP), 1)
    perm = jnp.where(c == (r % 2) * LANES + r // 2, 1.0, 0.0).astype(BF16)
    for g in range(w_ref.shape[2] // UP_GROUP):
        cols = slice(g * UP_GROUP, (g + 1) * UP_GROUP)
        o_ref[0, :, cols] = _dot(w_ref[0, :, cols].astype(BF16), perm).astype(BF16)


def _deinterleave_up(w_up):
    n_exp, d_in, d_up = w_up.shape
    blk = pl.BlockSpec((1, d_in, d_up // 2), lambda e, j: (e, 0, j))
    return pl.pallas_call(
        _deinterleave_kernel,
        grid=(n_exp, 2),
        in_specs=[blk], out_specs=blk,
        out_shape=jax.ShapeDtypeStruct(w_up.shape, BF16),
        compiler_params=_params("parallel", "parallel"),
        name="deinterleave",
    )(w_up)


ROUTE_TILE = 512
EXPERT_TILE = 512
COMBINE_TILE = 256


def _exact_count_dot(a, m):
    a0 = a.astype(BF16)
    r1 = a - a0.astype(F32)
    a1 = r1.astype(BF16)
    a2 = (r1 - a1.astype(F32)).astype(BF16)
    return _dot(a0, m) + (_dot(a1, m) + _dot(a2, m))


def _route_kernel(idx_ref, pos_ref, cnt_ref, counts, base, offs):
    p = pl.program_id(0)
    i = pl.program_id(1)
    n_tok = idx_ref.shape[0]
    idx = idx_ref[...]
    lane = lax.broadcasted_iota(jnp.int32, (n_tok, N_EXPERTS), 1)
    hot = jnp.zeros((n_tok, N_EXPERTS), F32)
    for k in range(TOP_K):
        hot = hot + jnp.where(lane == idx[:, k:k + 1], 1.0, 0.0)
    colsum = jnp.sum(hot, axis=0, keepdims=True)

    @pl.when((p == 0) & (i == 0))
    def _():
        counts[...] = jnp.zeros_like(counts)

    @pl.when(p == 0)
    def _():
        counts[...] += colsum

    @pl.when((p == 1) & (i == 0))
    def _():
        r = lax.broadcasted_iota(jnp.int32, (N_EXPERTS, N_EXPERTS), 0)
        c = lax.broadcasted_iota(jnp.int32, (N_EXPERTS, N_EXPERTS), 1)
        before = jnp.where(r < c, 1.0, 0.0).astype(BF16)
        offs[...] = _exact_count_dot(jnp.broadcast_to(counts[...], (8, N_EXPERTS)), before)[0:1]
        base[...] = jnp.zeros_like(base)

    @pl.when(p == 1)
    def _():
        r = lax.broadcasted_iota(jnp.int32, (n_tok, n_tok), 0)
        c = lax.broadcasted_iota(jnp.int32, (n_tok, n_tok), 1)
        earlier = jnp.where(c < r, 1.0, 0.0).astype(BF16)
        dense = offs[...] + base[...] + _dot(earlier, hot.astype(BF16))
        slot = lax.broadcasted_iota(jnp.int32, (n_tok, TOP_K), 1)
        pos = jnp.zeros((n_tok, TOP_K), F32)
        for k in range(TOP_K):
            mine = jnp.sum(jnp.where(lane == idx[:, k:k + 1], dense, 0.0), axis=-1, keepdims=True)
            pos = jnp.where(slot == k, mine, pos)
        pos_ref[...] = pos.astype(jnp.int32)
        base[...] += colsum

    cnt_ref[...] = counts[...].astype(jnp.int32)


def _route(idx):
    n_tok = idx.shape[0]
    return pl.pallas_call(
        _route_kernel,
        grid=(2, n_tok // ROUTE_TILE),
        in_specs=[pl.BlockSpec((ROUTE_TILE, TOP_K), lambda p, i: (i, 0))],
        out_specs=[pl.BlockSpec((ROUTE_TILE, TOP_K), lambda p, i: (i * p, 0)),
                   pl.BlockSpec((1, N_EXPERTS), lambda p, i: (0, 0))],
        out_shape=[jax.ShapeDtypeStruct((n_tok, TOP_K), jnp.int32), jax.ShapeDtypeStruct((1, N_EXPERTS), jnp.int32)],
        scratch_shapes=[pltpu.VMEM((1, N_EXPERTS), F32)] * 3,
        compiler_params=_params("arbitrary", "arbitrary"),
        name="route",
    )(idx)


def _start_rows(n_tok, copy):
    def issue(t, carry):
        for k in range(TOP_K):
            copy(t, k).start(priority=k % 2)
        return carry

    lax.fori_loop(0, n_tok, issue, 0, unroll=4)


def _wait_rows(n_tok, copy):
    def drain(t, carry):
        for k in range(TOP_K):
            copy(t, k).wait()
        return carry

    lax.fori_loop(0, n_tok, drain, 0, unroll=4)


def _rows_one_step_ahead(n_tok, copies):
    i = pl.program_id(0)

    @pl.when(i == 0)
    def _():
        _start_rows(n_tok, copies(i, False))

    @pl.when(i + 1 < pl.num_programs(0))
    def _():
        _start_rows(n_tok, copies(i + 1, True))

    _wait_rows(n_tok, copies(i, False))


def _dispatch_kernel(pos_ref, h_ref, xs_ref, sem):
    def copy(t, k):
        return pltpu.make_async_copy(h_ref.at[_token_tile(t), :], xs_ref.at[_token_tile(pos_ref[t * TOP_K + k]), :], sem)

    _start_rows(ROUTE_TILE, copy)
    _wait_rows(ROUTE_TILE, copy)


def _index_blocks(tile, n_steps):
    blk = (tile * TOP_K,)
    return [pl.BlockSpec(blk, lambda i: (i,), memory_space=pltpu.SMEM),
            pl.BlockSpec(blk, lambda i: (jnp.minimum(i + 1, n_steps - 1),), memory_space=pltpu.SMEM)]


def _dispatch(pos_flat, h2):
    n_tok = h2.shape[0] // ROW_PARTS
    n_steps = n_tok // ROUTE_TILE
    return pl.pallas_call(
        _dispatch_kernel,
        grid=(n_steps,),
        in_specs=[_index_blocks(ROUTE_TILE, n_steps)[0],
                  pl.BlockSpec((ROUTE_TILE * ROW_PARTS, LANES), lambda i: (i, 0))],
        out_specs=pl.BlockSpec(memory_space=pl.ANY),
        out_shape=jax.ShapeDtypeStruct((n_tok * TOP_K * ROW_PARTS, LANES), F32),
        scratch_shapes=[pltpu.SemaphoreType.DMA],
        compiler_params=_params("arbitrary"),
        name="dispatch",
    )(pos_flat, h2)


def _experts_kernel(tile_ref, exp_ref, lo_ref, hi_ref, x_ref, wu_ref, bu_ref, wd_ref, bd_ref, o_ref):
    w = pl.program_id(0)
    lo = lo_ref[w]
    hi = hi_ref[w]
    start = tile_ref[w] * EXPERT_TILE

    @pl.when(hi > lo)
    def _():
        h = _load_token_rows(x_ref, EXPERT_TILE).astype(BF16)
        acts = []
        for g in range(N_UP_GROUPS):
            cols = slice(g * UP_GROUP, (g + 1) * UP_GROUP)
            u = _dot(h, wu_ref[0, :, cols]) + bu_ref[0, :, cols]
            glu = jnp.minimum(u[:, :LANES], SWIGLU_LIMIT)
            lin = jnp.clip(u[:, LANES:], -SWIGLU_LIMIT, SWIGLU_LIMIT)
            acts.append((glu * _sigmoid(SWIGLU_ALPHA * glu) * (lin + 1.0)).astype(BF16))
        y = _dot(jnp.concatenate(acts, axis=-1), wd_ref[0]) + bd_ref[0]
        row = start + lax.broadcasted_iota(jnp.int32, (EXPERT_TILE, 1), 0)
        mine = (row >= lo) & (row < hi)

        @pl.when(lo == start)
        def _():
            _store_token_rows(o_ref, jnp.where(mine, y, 0.0))

        @pl.when(lo != start)
        def _():
            _store_token_rows(o_ref, jnp.where(mine, y, _load_token_rows(o_ref, EXPERT_TILE)))


def _work_items(counts, n_rows):
    n_tiles = n_rows // EXPERT_TILE
    ends = jnp.cumsum(counts)
    tile_ends = jnp.arange(1, n_tiles + 1, dtype=jnp.int32) * EXPERT_TILE
    n_items = n_tiles + N_EXPERTS
    count = lambda cond: jnp.sum(cond.astype(jnp.int32), axis=1)
    slot_t = jnp.arange(n_tiles, dtype=jnp.int32) + count(ends[None, :] < tile_ends[:, None])
    slot_e = jnp.arange(N_EXPERTS, dtype=jnp.int32) + count(tile_ends[None, :] <= ends[:, None])
    slots = jnp.concatenate([slot_t, slot_e])
    vals = jnp.concatenate([tile_ends, ends])
    item = jnp.arange(n_items, dtype=jnp.int32)
    hi = jnp.sum(jnp.where(slots[None, :] == item[:, None], vals[None, :], 0), axis=1)
    lo = jnp.concatenate([jnp.zeros((1,), jnp.int32), hi[:-1]])
    tile_id = jnp.minimum(lo // EXPERT_TILE, n_tiles - 1)
    exp_id = jnp.minimum(count(ends[None, :] <= lo[:, None]), N_EXPERTS - 1)
    return tile_id, exp_id, lo, hi


def _experts(items, xs, w_up_g, b_up_g, w_down, b_down):
    n_items = items[0].shape[0]
    rows = lambda w, t, e, lo, hi: (t[w], 0)
    per_expert = lambda r, c: pl.BlockSpec((1, r, c), lambda w, t, e, lo, hi: (e[w], 0, 0))
    return pl.pallas_call(
        _experts_kernel,
        grid_spec=pltpu.PrefetchScalarGridSpec(
            num_scalar_prefetch=4,
            grid=(n_items,),
            in_specs=[pl.BlockSpec((EXPERT_TILE * ROW_PARTS, LANES), rows),
                      per_expert(D_MODEL, 2 * D_MODEL), per_expert(1, 2 * D_MODEL),
                      per_expert(D_MODEL, D_MODEL), per_expert(1, D_MODEL)],
            out_specs=pl.BlockSpec((EXPERT_TILE * ROW_PARTS, LANES), rows)),
        out_shape=jax.ShapeDtypeStruct(xs.shape, F32),
        compiler_params=_params("arbitrary"),
        name="experts",
    )(*items, xs, w_up_g, b_up_g, w_down, b_down)


def _combine_kernel(pos_ref, nxt_ref, ys_ref, wts_ref, x1_ref, g2_ref, nf_ref, o_ref, buf, sem):
    def copies(step, use_next):
        p_ref = nxt_ref if use_next else pos_ref

        def copy(t, k):
            src = ys_ref.at[_token_tile(p_ref[t * TOP_K + k]), :]
            return pltpu.make_async_copy(src, buf.at[step % 2, k, _token_tile(t), :], sem.at[step % 2])
        return copy

    n_tok = x1_ref.shape[0]
    _rows_one_step_ahead(n_tok, copies)
    mine = buf.at[pl.program_id(0) % 2]
    wts = wts_ref[...]
    acc = wts[:, 0:1] * _load_token_rows(mine.at[0], n_tok)
    for k in range(1, TOP_K):
        acc = acc + wts[:, k:k + 1] * _load_token_rows(mine.at[k], n_tok)
    o_ref[...] = _rms(x1_ref[...] + g2_ref[0] * acc, nf_ref[...])


def _combine(pos_flat, ys, wts, x1, mods, norm_final, seq):
    n_tok = x1.shape[0]
    tiles_per_sample = seq // COMBINE_TILE
    tok = lambda n: pl.BlockSpec((COMBINE_TILE, n), lambda i: (i, 0))
    n_steps = n_tok // COMBINE_TILE
    return pl.pallas_call(
        _combine_kernel,
        grid=(n_steps,),
        in_specs=_index_blocks(COMBINE_TILE, n_steps) + [
            pl.BlockSpec(memory_space=pl.ANY), tok(TOP_K), tok(D_MODEL),
            pl.BlockSpec((1, 1, D_MODEL), lambda i: (i // tiles_per_sample, 0, MOD_G2)),
            pl.BlockSpec((1, D_MODEL), lambda i: (0, 0))],
        out_specs=tok(D_MODEL),
        out_shape=jax.ShapeDtypeStruct((n_tok, D_MODEL), F32),
        scratch_shapes=[pltpu.VMEM((2, TOP_K, COMBINE_TILE * ROW_PARTS, LANES), F32),
                        pltpu.SemaphoreType.DMA((2,))],
        compiler_params=_params("arbitrary"),
        name="combine",
    )(pos_flat, pos_flat, ys, wts, x1, mods, norm_final.reshape(1, -1))


def _moe(h2, idx, wts, w_up_g, b_up_g, w_down, b_down, x1, mods, norm_final):
    bsz, seq, _ = x1.shape
    n_tok = bsz * seq
    pos, counts = _route(idx.reshape(n_tok, TOP_K))
    pos_flat = pos.reshape(-1)
    xs = _dispatch(pos_flat, h2.reshape(n_tok * ROW_PARTS, LANES))
    ys = _experts(_work_items(counts.reshape(-1), n_tok * TOP_K), xs, w_up_g, b_up_g, w_down, b_down)
    out = _combine(pos_flat, ys, wts.reshape(n_tok, TOP_K), x1.reshape(n_tok, D_MODEL), mods, norm_final, seq)
    return out.reshape(bsz, seq, D_MODEL)


def kernel(x, c, ctx, c_ctx, w_ada, b_ada, norm_mix, norm_ffn, w_in, w_gk_f, b_gk_f, w_gk_b, b_gk_b, gla_norm, w_gla_out, hy_conv_w, hy_conv_b, hy_f_w1, hy_f_b1, hy_f_w2, hy_f_b2, hy_f_w3, hy_f_b3, hy_f_w4, hy_sin_freq, hy_bias, w_hy_out, w_out, w_router, b_router, w_up, b_up, w_down, b_down, norm_final):
    depth = w_ada.shape[0]
    assert depth == 1, "single-layer block: the context stream only feeds later layers"
    bsz, seq, _ = x.shape

    n_rows = -(-(bsz + 1) // 8) * 8
    cc = jnp.zeros((n_rows, D_MODEL), F32).at[:bsz].set(c).at[bsz].set(c_ctx)
    mods = _ada(cc, w_ada[0], b_ada[0]).reshape(n_rows, 1, N_MOD * D_MODEL)

    sizes = (GLA_DK, GLA_DK, GLA_DV, GLA_DV, GLA_GATE_RANK, GLA_GATE_RANK, 3 * HY_WIDTH, D_MODEL, D_MODEL)
    offs = np.concatenate([[0], np.cumsum(sizes)])
    w_in_b = w_in[0].astype(BF16)
    wq, wk, wv, wog, wrf, wrb, why, wma, wmb = [w_in_b[:, offs[i]:offs[i + 1]] for i in range(len(sizes))]
    q, k, v, og, r_f, r_b, hy, m_a, m_b = _inproj(
        x, mods, lambda b: b, norm_mix[0], [wq, wk, wv, wog, wrf, wrb, why, wma, wmb],
        [BF16, BF16, BF16, BF16, F32, F32, BF16, BF16, BF16], tl=512, conv=(6, hy_conv_w[0], hy_conv_b[0]))
    k_c, v_c, rc_f, rc_b = _inproj(
        ctx, mods, lambda b: bsz, norm_mix[0], [wk, wv, wrf, wrb], [BF16, BF16, F32, F32], tl=ctx.shape[1])

    w_gk = jnp.stack([w_gk_f[0], w_gk_b[0]])
    b_gk = jnp.stack([b_gk_f[0], b_gk_b[0]])[:, None, :]
    o_f, o_b = _gla(q, k, v, r_f, r_b, k_c, v_c, rc_f, rc_b, w_gk, b_gk)

    t_sym, t_anti = [jnp.asarray(t).astype(BF16) for t in _dft_tables(seq)]
    hs, hd, pn = _hyena_filter(seq, hy_f_w1[0], hy_f_b1[0], hy_f_w2[0], hy_f_b2[0], hy_f_w3[0], hy_f_b3[0],
                               hy_f_w4[0], hy_sin_freq[0])
    p_spec, q_spec, k_nyq = _hyena_spectrum(t_sym, t_anti, hs, hd, pn, hy_bias[0])
    y_h = _hyena_conv(hy, (t_sym, t_anti, t_sym.T, t_anti.T), p_spec, q_spec, k_nyq)

    x1, h2, idx, wts = _post(o_f, o_b, og, y_h, m_a, m_b, x, mods, gla_norm[0], norm_ffn[0],
                             w_gla_out[0].astype(BF16), w_hy_out[0].astype(BF16), w_out[0].astype(BF16),
                             w_router[0], b_router[0])

    b_up_g = b_up[0].reshape(N_EXPERTS, N_UP_GROUPS, LANES, 2).transpose(0, 1, 3, 2).reshape(N_EXPERTS, 1, -1)
    return _moe(h2, idx, wts, _deinterleave_up(w_up[0]), b_up_g, w_down[0].astype(BF16), b_down[0][:, None, :],
                x1, mods, norm_final)
```

```python
import functools
import math

import jax
import jax.numpy as jnp
import numpy as np
from jax import lax
from jax.experimental import pallas as pl
from jax.experimental.pallas import tpu as pltpu

F32 = jnp.float32
BF16 = jnp.bfloat16

D_MODEL = 1024
GRID_W = 64
EPS = 1e-6
N_MOD = 6

GLA_HEADS = 4
GLA_HEAD_K = 128
GLA_HEAD_V = 256
GLA_DK = GLA_HEADS * GLA_HEAD_K
GLA_DV = GLA_HEADS * GLA_HEAD_V
GLA_GATE_RANK = 16
GLA_GATE_NORM = 16.0
GLA_CHUNK = 64

HY_WIDTH = D_MODEL
HY_EMB = 33
HY_FAST_DECAY = 0.3
HY_SLOW_DECAY = 1.5
HY_TARGET = 1e-2

N_EXPERTS = 32
TOP_K = 4
SWIGLU_LIMIT = 7.0
SWIGLU_ALPHA = 1.702

V7X_VMEM_BYTES = 64 * 1024 * 1024
VMEM_LIMIT = V7X_VMEM_BYTES - 8 * 1024 * 1024
LANES = 128

MOD_SH1, MOD_SC1, MOD_G1, MOD_SH2, MOD_SC2, MOD_G2 = range(N_MOD)


def _params(*sem):
    return pltpu.CompilerParams(dimension_semantics=sem, vmem_limit_bytes=VMEM_LIMIT)


def _resident(shape):
    return pl.BlockSpec(shape, lambda *_: (0,) * len(shape), pipeline_mode=pl.Buffered(1))


def _dot(a, b):
    return jnp.dot(a, b, preferred_element_type=F32)


def _dot_nt(a, b):
    return lax.dot_general(a, b, (((1,), (1,)), ((), ())), preferred_element_type=F32)


def _dot_tn(a, b):
    return lax.dot_general(a, b, (((0,), (0,)), ((), ())), preferred_element_type=F32)


def _split(a):
    hi = a.astype(BF16)
    lo = (a - hi.astype(F32)).astype(BF16)
    return hi, lo


def _dot3(a, b):
    ah, al = _split(a)
    bh, bl = _split(b)
    return _dot(ah, bh) + (_dot(ah, bl) + _dot(al, bh))


def _sigmoid(x):
    return 1.0 / (1.0 + jnp.exp(-x))


def _log_sigmoid(x):
    return jnp.minimum(x, 0.0) - jnp.log(1.0 + jnp.exp(-jnp.abs(x)))


def _rms(x, w):
    return x * lax.rsqrt(jnp.mean(x * x, axis=-1, keepdims=True) + EPS) * w


def _ada_kernel(c_ref, w_ref, b_ref, o_ref):
    c = c_ref[...]
    o_ref[...] = _dot3(c * _sigmoid(c), w_ref[...]) + b_ref[...]


def _ada(cc, w_ada, b_ada):
    rows = cc.shape[0]
    return pl.pallas_call(
        _ada_kernel,
        grid=(N_MOD,),
        in_specs=[pl.BlockSpec((rows, D_MODEL), lambda j: (0, 0)),
                  pl.BlockSpec((D_MODEL, D_MODEL), lambda j: (0, j)),
                  pl.BlockSpec((1, D_MODEL), lambda j: (0, j))],
        out_specs=pl.BlockSpec((rows, D_MODEL), lambda j: (0, j)),
        out_shape=jax.ShapeDtypeStruct((rows, N_MOD * D_MODEL), F32),
        compiler_params=_params("parallel"),
        name="ada",
    )(cc, w_ada, b_ada.reshape(1, -1))


INPROJ_COL_CHUNK = 512


def _inproj_kernel(n_out, conv_out, x_ref, sc_ref, sh_ref, nw_ref, *refs):
    n_in = n_out + (0 if conv_out is None else 2)
    w_refs, o_refs = refs[:n_out], refs[n_in:]
    h = _rms(x_ref[0], nw_ref[...]) * (1.0 + sc_ref[0]) + sh_ref[0]
    hb = h.astype(BF16)
    for i, (w_ref, o_ref) in enumerate(zip(w_refs, o_refs)):
        n = w_ref.shape[1]
        for c0 in range(0, n, INPROJ_COL_CHUNK):
            c1 = min(c0 + INPROJ_COL_CHUNK, n)
            y = _dot(hb, w_ref[:, c0:c1])
            if i == conv_out:
                cw_ref, cb_ref = refs[n_out:n_in]
                y = _short_conv(y, cw_ref[:, c0:c1], cb_ref[:, c0:c1])
            o_ref[0, :, c0:c1] = y.astype(o_ref.dtype)


def _inproj(x, mods, mod_row, norm_w, weights, out_dtypes, tl, conv=None):
    bsz, seq, _ = x.shape
    n_out = len(weights)
    assert tl % GRID_W == 0
    in_specs = [pl.BlockSpec((1, tl, D_MODEL), lambda b, i: (b, i, 0)),
                pl.BlockSpec((1, 1, D_MODEL), lambda b, i: (mod_row(b), 0, MOD_SC1)),
                pl.BlockSpec((1, 1, D_MODEL), lambda b, i: (mod_row(b), 0, MOD_SH1)),
                _resident((1, D_MODEL))]
    in_specs += [_resident(w.shape) for w in weights]
    extra = []
    if conv is not None:
        extra = [conv[1], conv[2].reshape(1, -1)]
        in_specs += [_resident(a.shape) for a in extra]
    out_specs = [pl.BlockSpec((1, tl, w.shape[1]), lambda b, i: (b, i, 0)) for w in weights]
    out_shape = [jax.ShapeDtypeStruct((bsz, seq, w.shape[1]), dt) for w, dt in zip(weights, out_dtypes)]
    return pl.pallas_call(
        functools.partial(_inproj_kernel, n_out, None if conv is None else conv[0]),
        grid=(bsz, seq // tl),
        in_specs=in_specs, out_specs=out_specs, out_shape=out_shape,
        compiler_params=_params("parallel", "parallel"),
        name="inproj",
    )(x, mods, mods, norm_w.reshape(1, -1), *weights, *extra)


GLA_TILE = 256
GLA_NCH = GLA_TILE // GLA_CHUNK
GLA_SCALE = GLA_HEAD_K ** -0.5
GLA_HPS = 4
GLA_KW = GLA_HPS * GLA_HEAD_K
GLA_VW = GLA_HPS * GLA_HEAD_V


@functools.lru_cache(maxsize=None)
def _chunk_triangles():
    i = np.arange(GLA_TILE)
    same = (i[:, None] // GLA_CHUNK) == (i[None, :] // GLA_CHUNK)
    return np.stack([same & (i[None, :] <= i[:, None]), same & (i[None, :] >= i[:, None])]).astype(np.float32)


def _gla_kernel(tri_ref, qf_ref, kf_ref, vf_ref, rf_ref, qb_ref, kb_ref, vb_ref, rb_ref,
                kc_ref, vc_ref, rcf_ref, rcb_ref, wgk_ref, bgk_ref,
                of_ref, ob_ref, sf_ref, sb_ref):
    t = pl.program_id(2)
    crow = lax.broadcasted_iota(jnp.int32, (GLA_CHUNK, GLA_CHUNK), 0)
    ccol = lax.broadcasted_iota(jnp.int32, (GLA_CHUNK, GLA_CHUNK), 1)
    heads = range(GLA_HPS)

    def rows(c):
        return slice(c * GLA_CHUNK, (c + 1) * GLA_CHUNK)

    def hk(h):
        return slice(h * GLA_HEAD_K, (h + 1) * GLA_HEAD_K)

    def hv(h):
        return slice(h * GLA_HEAD_V, (h + 1) * GLA_HEAD_V)

    def order(fwd):
        return range(GLA_NCH) if fwd else range(GLA_NCH - 1, -1, -1)

    def cum_decay(r, d):
        z = _dot3(r, wgk_ref[d]) + bgk_ref[d]
        g = _log_sigmoid(z) * (1.0 / GLA_GATE_NORM)
        gh, gl = _split(g)
        return _dot(tri_ref[d], gh) + _dot(tri_ref[d], gl)

    def updates(k_ref, v_ref, b, fwd):
        out = {}
        for c in range(GLA_NCH):
            i = c * GLA_CHUNK + (GLA_CHUNK - 1 if fwd else 0)
            total = b[i:i + 1, :]
            k_upd = (k_ref[0, rows(c), :].astype(F32) * jnp.exp(total - b[rows(c)])).astype(BF16)
            dec = jnp.exp(total)
            for h in heads:
                out[c, h] = (dec[:, hk(h)], _dot_tn(v_ref[0, rows(c), hv(h)], k_upd[:, hk(h)]))
        return out

    def context_state(s_ref, r_ref, d, fwd):
        upd = updates(kc_ref, vc_ref, cum_decay(r_ref[0], d), fwd)
        for h in heads:
            st = jnp.zeros((GLA_HEAD_V, GLA_HEAD_K), F32)
            for c in order(fwd):
                dec, inc = upd[c, h]
                st = st * dec + inc
            s_ref[h] = st

    def scan(s_ref, q_ref, k_ref, v_ref, r_ref, o_ref, d, fwd):
        mask = (ccol <= crow) if fwd else (ccol >= crow)
        b = cum_decay(r_ref[0], d)
        q_dec = (q_ref[0].astype(F32) * (jnp.exp(b) * GLA_SCALE)).astype(BF16)
        k_inv = (k_ref[0].astype(F32) * jnp.exp(-b)).astype(BF16)
        upd = updates(k_ref, v_ref, b, fwd)
        intra = {}
        for c in range(GLA_NCH):
            for h in heads:
                att = jnp.where(mask, _dot_nt(q_dec[rows(c), hk(h)], k_inv[rows(c), hk(h)]), 0.0).astype(BF16)
                intra[c, h] = _dot(att, v_ref[0, rows(c), hv(h)])
        enter = {}
        for h in heads:
            st = s_ref[h]
            for c in order(fwd):
                enter[c, h] = st.astype(BF16)
                dec, inc = upd[c, h]
                st = st * dec + inc
            s_ref[h] = st
        for c in range(GLA_NCH):
            for h in heads:
                o = intra[c, h] + _dot_nt(q_dec[rows(c), hk(h)], enter[c, h])
                o_ref[0, rows(c), hv(h)] = o.astype(o_ref.dtype)

    @pl.when(t == 0)
    def _():
        context_state(sf_ref, rcf_ref, 0, True)
        context_state(sb_ref, rcb_ref, 1, False)

    scan(sf_ref, qf_ref, kf_ref, vf_ref, rf_ref, of_ref, 0, True)
    scan(sb_ref, qb_ref, kb_ref, vb_ref, rb_ref, ob_ref, 1, False)


def _gla(q, k, v, r_f, r_b, k_c, v_c, rc_f, rc_b, w_gk, b_gk):
    bsz, seq, _ = q.shape
    assert k_c.shape[1] == GLA_TILE
    nt = seq // GLA_TILE
    tri = jnp.asarray(_chunk_triangles()).astype(BF16)
    fwd = lambda b, h, t: (b, t, h)
    bwd = lambda b, h, t: (b, nt - 1 - t, h)
    fwd0 = lambda b, h, t: (b, t, 0)
    bwd0 = lambda b, h, t: (b, nt - 1 - t, 0)
    ctx = lambda b, h, t: (b, 0, h)
    ctx0 = lambda b, h, t: (b, 0, 0)
    kblk = (1, GLA_TILE, GLA_KW)
    vblk = (1, GLA_TILE, GLA_VW)
    rblk = (1, GLA_TILE, GLA_GATE_RANK)
    in_specs = [_resident(tri.shape),
                pl.BlockSpec(kblk, fwd), pl.BlockSpec(kblk, fwd), pl.BlockSpec(vblk, fwd), pl.BlockSpec(rblk, fwd0),
                pl.BlockSpec(kblk, bwd), pl.BlockSpec(kblk, bwd), pl.BlockSpec(vblk, bwd), pl.BlockSpec(rblk, bwd0),
                pl.BlockSpec(kblk, ctx), pl.BlockSpec(vblk, ctx), pl.BlockSpec(rblk, ctx0), pl.BlockSpec(rblk, ctx0),
                pl.BlockSpec((2, GLA_GATE_RANK, GLA_KW), lambda b, h, t: (0, 0, h)),
                pl.BlockSpec((2, 1, GLA_KW), lambda b, h, t: (0, 0, h))]
    out_specs = [pl.BlockSpec(vblk, fwd), pl.BlockSpec(vblk, bwd)]
    out_shape = [jax.ShapeDtypeStruct((bsz, seq, GLA_DV), BF16)] * 2
    return pl.pallas_call(
        _gla_kernel,
        grid=(bsz, GLA_HEADS // GLA_HPS, nt),
        in_specs=in_specs, out_specs=out_specs, out_shape=out_shape,
        scratch_shapes=[pltpu.VMEM((GLA_HPS, GLA_HEAD_V, GLA_HEAD_K), F32)] * 2,
        compiler_params=_params("parallel", "parallel", "arbitrary"),
        name="gla",
    )(tri, q, k, v, r_f, q, k, v, r_b, k_c, v_c, rc_f, rc_b, w_gk, b_gk)


HY_CW = 256
HY_ROW_CHUNK = 512
MLP_PAD = 128


@functools.lru_cache(maxsize=None)
def _dft_tables(seq):
    n, half = 2 * seq, seq // 2
    m = np.arange(half)[:, None]
    s = np.arange(half)[None, :]
    ang = 2.0 * np.pi * np.arange(n) / n
    cos = lambda k: np.cos(ang)[(k * s) % n]
    sin = lambda k: np.sin(ang)[(k * s) % n]
    sym = np.concatenate([cos(2 * m), sin(2 * m + 1)], axis=0)
    anti = np.concatenate([sin(2 * m), cos(2 * m + 1)], axis=0)
    return sym.astype(np.float32), anti.astype(np.float32)


@functools.lru_cache(maxsize=None)
def _filter_features(seq):
    bands = (HY_EMB - 1) // 2
    pos = np.arange(seq, dtype=np.float64)[:, None]
    t = pos / max(seq - 1, 1)
    f = np.linspace(1e-4, bands - 1, bands)[None]
    ang = (2.0 * math.pi / seq) * pos * f
    z = np.concatenate([t, np.cos(ang), -np.sin(ang)], axis=-1)
    out = np.zeros((seq, MLP_PAD), np.float32)
    out[:, :HY_EMB] = z
    deltas = np.abs(np.linspace(math.log(HY_TARGET) / HY_SLOW_DECAY, math.log(HY_TARGET) / HY_FAST_DECAY, HY_WIDTH))
    return out, deltas.astype(np.float32)[None]


def _filter_kernel(z_ref, w1_ref, b1_ref, w2_ref, b2_ref, w3_ref, b3_ref, fr_ref, w4f_ref, w4b_ref, dl_ref,
                   hs_ref, hd_ref, pn_ref):
    z = z_ref[...]
    fr = fr_ref[...]
    h = jnp.sin(fr * (_dot3(z, w1_ref[...]) + b1_ref[...]))
    h = jnp.sin(fr * (_dot3(h, w2_ref[...]) + b2_ref[...]))
    h = jnp.sin(fr * (_dot3(h, w3_ref[...]) + b3_ref[...]))
    window = jnp.exp(-z[:, 0:1] * dl_ref[...])
    h_f = _dot3(h, w4f_ref[...]) * window
    h_b = _dot3(h, w4b_ref[...]) * window
    pos = lax.broadcasted_iota(jnp.int32, (z.shape[0], 1), 0)
    h_b = jnp.where(pos == 0, 0.0, h_b)
    hs = h_f + h_b
    hs_ref[...] = hs.astype(BF16)
    hd_ref[...] = (h_b - h_f).astype(BF16)
    sign = jnp.where(pos % 2 == 0, 1.0, -1.0)
    pn_ref[...] = jnp.sum(hs * sign, axis=0, keepdims=True)


def _pad2(a, rows, cols):
    return jnp.zeros((rows, cols), F32).at[:a.shape[0], :a.shape[1]].set(a.astype(F32))


def _hyena_filter(seq, w1, b1, w2, b2, w3, b3, w4, freq):
    z_np, deltas_np = _filter_features(seq)
    p = MLP_PAD
    args = [jnp.asarray(z_np), _pad2(w1, p, p), _pad2(b1[None], 1, p), _pad2(w2, p, p), _pad2(b2[None], 1, p),
            _pad2(w3, p, p), _pad2(b3[None], 1, p), _pad2(freq[None], 1, p),
            _pad2(w4[:, :HY_WIDTH], p, HY_WIDTH), _pad2(w4[:, HY_WIDTH:], p, HY_WIDTH), jnp.asarray(deltas_np)]
    full = lambda shape: pl.BlockSpec(shape, lambda j: (0, 0))
    colblk = lambda rows: pl.BlockSpec((rows, HY_CW), lambda j: (0, j))
    in_specs = [full((seq, p)), full((p, p)), full((1, p)), full((p, p)), full((1, p)), full((p, p)), full((1, p)),
                full((1, p)), colblk(p), colblk(p), colblk(1)]
    return pl.pallas_call(
        _filter_kernel,
        grid=(HY_WIDTH // HY_CW,),
        in_specs=in_specs,
        out_specs=[colblk(seq), colblk(seq), colblk(1)],
        out_shape=[jax.ShapeDtypeStruct((seq, HY_WIDTH), BF16), jax.ShapeDtypeStruct((seq, HY_WIDTH), BF16),
                   jax.ShapeDtypeStruct((1, HY_WIDTH), F32)],
        compiler_params=_params("parallel"),
        name="hy_filter",
    )(*args)


REV_BLOCK = 256


def _negate_index(x):
    n = x.shape[0]
    r = lax.broadcasted_iota(jnp.int32, (REV_BLOCK, REV_BLOCK), 0)
    c = lax.broadcasted_iota(jnp.int32, (REV_BLOCK, REV_BLOCK), 1)
    flip = jnp.where(r + c == REV_BLOCK - 1, 1.0, 0.0).astype(BF16)
    if x.dtype == BF16:
        pieces = [x]
    else:
        x0 = x.astype(BF16)
        r1 = x - x0.astype(F32)
        x1 = r1.astype(BF16)
        pieces = [x0, x1, (r1 - x1.astype(F32)).astype(BF16)]
    blocks = []
    for i in range(n // REV_BLOCK - 1, -1, -1):
        rows = slice(i * REV_BLOCK, (i + 1) * REV_BLOCK)
        acc = _dot(flip, pieces[0][rows])
        for p in pieces[1:]:
            acc = acc + _dot(flip, p[rows])
        blocks.append(acc)
    return pltpu.roll(jnp.concatenate(blocks, axis=0), 1, 0)


def _fold(x):
    half = x.shape[0] // 2
    lo = x[:half].astype(F32)
    hi = _negate_index(x[half:])
    pos = lax.broadcasted_iota(jnp.int32, (half, 1), 0)
    mid = hi[0:1]
    hi = jnp.where(pos == 0, 0.0, hi)
    return (lo + hi).astype(BF16), (lo - hi).astype(BF16), mid


def _alternating(n):
    pos = lax.broadcasted_iota(jnp.int32, (n, 1), 0)
    return jnp.where(pos % 2 == 0, 1.0, -1.0)


def _spectrum_kernel(ts_ref, ta_ref, hs_ref, hd_ref, pn_ref, bias_ref, p_ref, q_ref, kn_ref):
    half = hs_ref.shape[0] // 2
    bias = bias_ref[...]
    sgn = _alternating(half)
    s_sym, s_anti, s_mid = _fold(hs_ref[...])
    d_sym, d_anti, d_mid = _fold(hd_ref[...])
    p_ref[:half, :] = _dot(ts_ref[:half, :], s_sym) + sgn * s_mid + bias
    p_ref[half:, :] = _dot(ta_ref[half:, :], s_anti) + bias
    q_ref[:half, :] = _dot(ta_ref[:half, :], d_anti)
    q_ref[half:, :] = _dot(ts_ref[half:, :], d_sym) + sgn * d_mid
    kn_ref[...] = pn_ref[...] + bias


def _hyena_spectrum(t_sym, t_anti, hs, hd, pn, bias):
    seq = hs.shape[0]
    colblk = lambda rows: pl.BlockSpec((rows, HY_CW), lambda j: (0, j))
    return pl.pallas_call(
        _spectrum_kernel,
        grid=(HY_WIDTH // HY_CW,),
        in_specs=[_resident(t_sym.shape), _resident(t_anti.shape), colblk(seq), colblk(seq), colblk(1), colblk(1)],
        out_specs=[colblk(seq), colblk(seq), colblk(1)],
        out_shape=[jax.ShapeDtypeStruct((seq, HY_WIDTH), F32)] * 2 + [jax.ShapeDtypeStruct((1, HY_WIDTH), F32)],
        compiler_params=_params("parallel"),
        name="hy_spectrum",
    )(t_sym, t_anti, hs, hd, pn, bias.reshape(1, -1))


def _short_conv(u, w, b):
    n = u.shape[0]
    pos = lax.broadcasted_iota(jnp.int32, (n, 1), 0) % GRID_W
    up = jnp.where(pos == 0, 0.0, pltpu.roll(u, 1, 0))
    dn = jnp.where(pos == GRID_W - 1, 0.0, pltpu.roll(u, n - 1, 0))
    return up * w[0:1] + u * w[1:2] + dn * w[2:3] + b


def _hy_fwd_kernel(ts_ref, ta_ref, x1_ref, v_ref, p_ref, q_ref, kn_ref, s_ref):
    seq = x1_ref.shape[1]
    n, half = 2 * seq, seq // 2
    u_sym, u_anti, u_mid = _fold((v_ref[0].astype(F32) * x1_ref[0].astype(F32)).astype(BF16))
    sgn = _alternating(half)
    nyq = (jnp.sum(sgn * u_sym.astype(F32), axis=0, keepdims=True) + u_mid) * kn_ref[...] * (1.0 / n)
    for r0 in range(0, half, HY_ROW_CHUNK):
        r1 = r0 + HY_ROW_CHUNK
        mid = sgn[r0:r1] * u_mid
        re_e = _dot(ts_ref[r0:r1, :], u_sym) + mid
        im_o = _dot(ts_ref[half + r0:half + r1, :], u_sym) + mid
        im_e = _dot(ta_ref[r0:r1, :], u_anti)
        re_o = _dot(ta_ref[half + r0:half + r1, :], u_anti)
        pe, po = p_ref[r0:r1, :], p_ref[half + r0:half + r1, :]
        qe, qo = q_ref[r0:r1, :], q_ref[half + r0:half + r1, :]
        s_re_e = (re_e * pe + im_e * qe) * (2.0 / n)
        s_im_e = (im_e * pe - re_e * qe) * (2.0 / n)
        if r0 == 0:
            pos = lax.broadcasted_iota(jnp.int32, (HY_ROW_CHUNK, 1), 0)
            s_re_e = jnp.where(pos == 0, re_e * pe * (1.0 / n), s_re_e)
            s_im_e = jnp.where(pos == 0, nyq, s_im_e)
        s_ref[0, r0:r1, :] = s_re_e.astype(BF16)
        s_ref[0, half + r0:half + r1, :] = ((im_o * po - re_o * qo) * (2.0 / n)).astype(BF16)
        s_ref[0, seq + r0:seq + r1, :] = s_im_e.astype(BF16)
        s_ref[0, seq + half + r0:seq + half + r1, :] = ((re_o * po + im_o * qo) * (2.0 / n)).astype(BF16)


def _hy_inv_kernel(gs_ref, ga_ref, s_ref, x0_ref, o_ref):
    seq = x0_ref.shape[1]
    half = seq // 2
    x0 = x0_ref[0].astype(F32)
    sgn = _alternating(half)
    nyq = sgn * s_ref[0, seq:seq + 1, :].astype(F32)
    mirrored = []
    for r0 in range(0, half, HY_ROW_CHUNK):
        r1 = r0 + HY_ROW_CHUNK
        sym = _dot(gs_ref[r0:r1, :], s_ref[0, :seq, :]) + nyq[r0:r1]
        anti = _dot(ga_ref[r0:r1, :], s_ref[0, seq:, :])
        o_ref[0, r0:r1, :] = (x0[r0:r1] * (sym + anti)).astype(o_ref.dtype)
        mirrored.append(sym - anti)
    centre = (jnp.sum(sgn * (s_ref[0, :half, :].astype(F32) + s_ref[0, half:seq, :].astype(F32)), axis=0, keepdims=True)
              + nyq[0:1])
    pos = lax.broadcasted_iota(jnp.int32, (half, 1), 0)
    upper = _negate_index(jnp.where(pos == 0, centre, jnp.concatenate(mirrored, axis=0)).astype(BF16))
    o_ref[0, half:, :] = (x0[half:] * upper).astype(o_ref.dtype)


def _hyena_conv(hy, tabs, p_spec, q_spec, k_nyq):
    bsz, seq, _ = hy.shape
    nc = HY_WIDTH // HY_CW
    t_sym, t_anti, g_sym, g_anti = tabs
    sig = lambda part: pl.BlockSpec((1, seq, HY_CW), lambda j, b: (b, 0, part * nc + j))
    spec = lambda rows: pl.BlockSpec((rows, HY_CW), lambda j, b: (0, j))
    s = pl.pallas_call(
        _hy_fwd_kernel,
        grid=(nc, bsz),
        in_specs=[_resident(t_sym.shape), _resident(t_anti.shape), sig(1), sig(2),
                  spec(seq), spec(seq), spec(1)],
        out_specs=pl.BlockSpec((1, 2 * seq, HY_CW), lambda j, b: (b, 0, j)),
        out_shape=jax.ShapeDtypeStruct((bsz, 2 * seq, HY_WIDTH), BF16),
        compiler_params=_params("parallel", "parallel"),
        name="hy_fwd",
    )(t_sym, t_anti, hy, hy, p_spec, q_spec, k_nyq)
    return pl.pallas_call(
        _hy_inv_kernel,
        grid=(nc, bsz),
        in_specs=[_resident(g_sym.shape), _resident(g_anti.shape),
                  pl.BlockSpec((1, 2 * seq, HY_CW), lambda j, b: (b, 0, j)), sig(0)],
        out_specs=pl.BlockSpec((1, seq, HY_CW), lambda j, b: (b, 0, j)),
        out_shape=jax.ShapeDtypeStruct((bsz, seq, HY_WIDTH), BF16),
        compiler_params=_params("parallel", "parallel"),
        name="hy_inv",
    )(g_sym, g_anti, s, hy)


ROW_PARTS = D_MODEL // LANES


def _token_tile(t):
    return pl.ds(pl.multiple_of(t * ROW_PARTS, ROW_PARTS), ROW_PARTS)


def _load_token_rows(ref, n_tok):
    return jnp.concatenate([ref[pl.ds(j, n_tok, stride=ROW_PARTS), :] for j in range(ROW_PARTS)], axis=-1)


def _store_token_rows(ref, val):
    for j in range(ROW_PARTS):
        ref[pl.ds(j, val.shape[0], stride=ROW_PARTS), :] = val[:, j * LANES:(j + 1) * LANES]


POST_TILE = 512


def _post_kernel(of_ref, ob_ref, og_ref, yh_ref, ma_ref, mb_ref, x_ref, g1_ref, sc2_ref, sh2_ref,
                 gn_ref, nf_ref, wa_ref, wb_ref, wo_ref, wr_ref, br_ref,
                 x1_ref, h2_ref, idx_ref, wts_ref):
    o = of_ref[0].astype(F32) + ob_ref[0].astype(F32)
    gn = gn_ref[...]
    heads = [_rms(o[:, h * GLA_HEAD_V:(h + 1) * GLA_HEAD_V], gn) for h in range(GLA_HEADS)]
    og = og_ref[0].astype(F32)
    a_in = jnp.concatenate(heads, axis=-1) * (og * _sigmoid(og))
    branch_a = _dot(a_in.astype(BF16), wa_ref[...])
    branch_b = _dot(yh_ref[0], wb_ref[...])
    y = _sigmoid(ma_ref[0].astype(F32)) * branch_a + _sigmoid(mb_ref[0].astype(F32)) * branch_b
    x1 = x_ref[0] + g1_ref[0] * _dot(y.astype(BF16), wo_ref[...])
    x1_ref[0] = x1
    h2 = _rms(x1, nf_ref[...]) * (1.0 + sc2_ref[0]) + sh2_ref[0]
    _store_token_rows(h2_ref.at[0], h2)

    logits = _dot3(h2, wr_ref[...]) + br_ref[...]
    lane = lax.broadcasted_iota(jnp.int32, logits.shape, 1)
    slot = lax.broadcasted_iota(jnp.int32, (logits.shape[0], TOP_K), 1)
    work = logits
    tops = []
    idx = jnp.zeros((logits.shape[0], TOP_K), jnp.int32)
    for k in range(TOP_K):
        m = jnp.max(work, axis=-1, keepdims=True)
        first = jnp.min(jnp.where(work == m, lane, N_EXPERTS), axis=-1, keepdims=True)
        tops.append(m)
        idx = jnp.where(slot == k, first, idx)
        work = jnp.where(lane == first, -jnp.inf, work)
    es = [jnp.exp(m - tops[0]) for m in tops]
    inv = 1.0 / functools.reduce(lambda a, b: a + b, es)
    wts = jnp.zeros((logits.shape[0], TOP_K), F32)
    for k, e in enumerate(es):
        wts = jnp.where(slot == k, e * inv, wts)
    idx_ref[0] = idx
    wts_ref[0] = wts


def _post(o_f, o_b, og, y_h, m_a, m_b, x, mods, gla_norm, norm_ffn, w_a, w_b, w_o, w_router, b_router):
    bsz, seq, _ = x.shape
    tok = lambda n: pl.BlockSpec((1, POST_TILE, n), lambda b, i: (b, i, 0))
    mod = lambda which: pl.BlockSpec((1, 1, D_MODEL), lambda b, i: (b, 0, which))
    in_specs = [tok(D_MODEL)] * 7 + [mod(MOD_G1), mod(MOD_SC2), mod(MOD_SH2),
                                     _resident((1, GLA_HEAD_V)), _resident((1, D_MODEL)),
                                     _resident(w_a.shape), _resident(w_b.shape), _resident(w_o.shape),
                                     _resident(w_router.shape), _resident((1, N_EXPERTS))]
    return pl.pallas_call(
        _post_kernel,
        grid=(bsz, seq // POST_TILE),
        in_specs=in_specs,
        out_specs=[tok(D_MODEL), pl.BlockSpec((1, POST_TILE * ROW_PARTS, LANES), lambda b, i: (b, i, 0)),
                   tok(TOP_K), tok(TOP_K)],
        out_shape=[jax.ShapeDtypeStruct((bsz, seq, D_MODEL), F32),
                   jax.ShapeDtypeStruct((bsz, seq * ROW_PARTS, LANES), F32),
                   jax.ShapeDtypeStruct((bsz, seq, TOP_K), jnp.int32), jax.ShapeDtypeStruct((bsz, seq, TOP_K), F32)],
        compiler_params=_params("parallel", "parallel"),
        name="post",
    )(o_f, o_b, og, y_h, m_a, m_b, x, mods, mods, mods, gla_norm.reshape(1, -1), norm_ffn.reshape(1, -1),
      w_a, w_b, w_o, w_router, b_router.reshape(1, -1))


UP_GROUP = 2 * LANES
N_UP_GROUPS = 2 * D_MODEL // UP_GROUP


def _deinterleave_kernel(w_ref, o_ref):
    r = lax.broadcasted_iota(jnp.int32, (UP_GROUP, UP_GROUP), 0)
    c = lax.broadcasted_iota(jnp.int32, (UP_GROUP, UP_GROUP), 1)
    perm = jnp.where(c == (r % 2) * LANES + r // 2, 1.0, 0.0).astype(BF16)
    for g in range(w_ref.shape[2] // UP_GROUP):
        cols = slice(g * UP_GROUP, (g + 1) * UP_GROUP)
        o_ref[0, :, cols] = _dot(w_ref[0, :, cols].astype(BF16), perm).astype(BF16)


def _deinterleave_up(w_up):
    n_exp, d_in, d_up = w_up.shape
    blk = pl.BlockSpec((1, d_in, d_up // 2), lambda e, j: (e, 0, j))
    return pl.pallas_call(
        _deinterleave_kernel,
        grid=(n_exp, 2),
        in_specs=[blk], out_specs=blk,
        out_shape=jax.ShapeDtypeStruct(w_up.shape, BF16),
        compiler_params=_params("parallel", "parallel"),
        name="deinterleave",
    )(w_up)


ROUTE_TILE = 512
EXPERT_TILE = 512
DOWN_COLS_PER_DOT = 256
COMBINE_TILE = 256


def _exact_count_dot(a, m):
    a0 = a.astype(BF16)
    r1 = a - a0.astype(F32)
    a1 = r1.astype(BF16)
    a2 = (r1 - a1.astype(F32)).astype(BF16)
    return _dot(a0, m) + (_dot(a1, m) + _dot(a2, m))


def _route_kernel(idx_ref, pos_ref, cnt_ref, counts, base, offs):
    p = pl.program_id(0)
    i = pl.program_id(1)
    n_tok = idx_ref.shape[0]
    idx = idx_ref[...]
    lane = lax.broadcasted_iota(jnp.int32, (n_tok, N_EXPERTS), 1)
    hot = jnp.zeros((n_tok, N_EXPERTS), F32)
    for k in range(TOP_K):
        hot = hot + jnp.where(lane == idx[:, k:k + 1], 1.0, 0.0)
    colsum = jnp.sum(hot, axis=0, keepdims=True)

    @pl.when((p == 0) & (i == 0))
    def _():
        counts[...] = jnp.zeros_like(counts)

    @pl.when(p == 0)
    def _():
        counts[...] += colsum

    @pl.when((p == 1) & (i == 0))
    def _():
        r = lax.broadcasted_iota(jnp.int32, (N_EXPERTS, N_EXPERTS), 0)
        c = lax.broadcasted_iota(jnp.int32, (N_EXPERTS, N_EXPERTS), 1)
        before = jnp.where(r < c, 1.0, 0.0).astype(BF16)
        offs[...] = _exact_count_dot(jnp.broadcast_to(counts[...], (8, N_EXPERTS)), before)[0:1]
        base[...] = jnp.zeros_like(base)

    @pl.when(p == 1)
    def _():
        r = lax.broadcasted_iota(jnp.int32, (n_tok, n_tok), 0)
        c = lax.broadcasted_iota(jnp.int32, (n_tok, n_tok), 1)
        earlier = jnp.where(c < r, 1.0, 0.0).astype(BF16)
        dense = offs[...] + base[...] + _dot(earlier, hot.astype(BF16))
        slot = lax.broadcasted_iota(jnp.int32, (n_tok, TOP_K), 1)
        pos = jnp.zeros((n_tok, TOP_K), F32)
        for k in range(TOP_K):
            mine = jnp.sum(jnp.where(lane == idx[:, k:k + 1], dense, 0.0), axis=-1, keepdims=True)
            pos = jnp.where(slot == k, mine, pos)
        pos_ref[...] = pos.astype(jnp.int32)
        base[...] += colsum

    cnt_ref[...] = counts[...].astype(jnp.int32)


def _route(idx):
    n_tok = idx.shape[0]
    return pl.pallas_call(
        _route_kernel,
        grid=(2, n_tok // ROUTE_TILE),
        in_specs=[pl.BlockSpec((ROUTE_TILE, TOP_K), lambda p, i: (i, 0))],
        out_specs=[pl.BlockSpec((ROUTE_TILE, TOP_K), lambda p, i: (i * p, 0)),
                   pl.BlockSpec((1, N_EXPERTS), lambda p, i: (0, 0))],
        out_shape=[jax.ShapeDtypeStruct((n_tok, TOP_K), jnp.int32), jax.ShapeDtypeStruct((1, N_EXPERTS), jnp.int32)],
        scratch_shapes=[pltpu.VMEM((1, N_EXPERTS), F32)] * 3,
        compiler_params=_params("arbitrary", "arbitrary"),
        name="route",
    )(idx)


def _start_rows(n_tok, copy):
    def issue(t, carry):
        for k in range(TOP_K):
            copy(t, k).start(priority=k % 2)
        return carry

    lax.fori_loop(0, n_tok, issue, 0, unroll=4)


def _wait_rows(n_tok, copy):
    def drain(t, carry):
        for k in range(TOP_K):
            copy(t, k).wait()
        return carry

    lax.fori_loop(0, n_tok, drain, 0, unroll=4)


def _rows_one_step_ahead(n_tok, copies):
    i = pl.program_id(0)

    @pl.when(i == 0)
    def _():
        _start_rows(n_tok, copies(i, False))

    @pl.when(i + 1 < pl.num_programs(0))
    def _():
        _start_rows(n_tok, copies(i + 1, True))

    _wait_rows(n_tok, copies(i, False))


def _dispatch_kernel(pos_ref, h_ref, xs_ref, sem):
    def copy(t, k):
        return pltpu.make_async_copy(h_ref.at[_token_tile(t), :], xs_ref.at[_token_tile(pos_ref[t * TOP_K + k]), :], sem)

    _start_rows(ROUTE_TILE, copy)
    _wait_rows(ROUTE_TILE, copy)


def _index_blocks(tile, n_steps):
    blk = (tile * TOP_K,)
    return [pl.BlockSpec(blk, lambda i: (i,), memory_space=pltpu.SMEM),
            pl.BlockSpec(blk, lambda i: (jnp.minimum(i + 1, n_steps - 1),), memory_space=pltpu.SMEM)]


def _dispatch(pos_flat, h2):
    n_tok = h2.shape[0] // ROW_PARTS
    n_steps = n_tok // ROUTE_TILE
    return pl.pallas_call(
        _dispatch_kernel,
        grid=(n_steps,),
        in_specs=[_index_blocks(ROUTE_TILE, n_steps)[0],
                  pl.BlockSpec((ROUTE_TILE * ROW_PARTS, LANES), lambda i: (i, 0))],
        out_specs=pl.BlockSpec(memory_space=pl.ANY),
        out_shape=jax.ShapeDtypeStruct((n_tok * TOP_K * ROW_PARTS, LANES), F32),
        scratch_shapes=[pltpu.SemaphoreType.DMA],
        compiler_params=_params("arbitrary"),
        name="dispatch",
    )(pos_flat, h2)


def _experts_kernel(tile_ref, exp_ref, lo_ref, hi_ref, x_ref, wu_ref, bu_ref, wd_ref, bd_ref, o_ref, acc_ref):
    w = pl.program_id(0)
    lo = lo_ref[w]
    hi = hi_ref[w]

    @pl.when(w == 0)
    def _():
        acc_ref[...] = jnp.zeros_like(acc_ref)

    @pl.when(hi > lo)
    def _():
        h = _load_token_rows(x_ref, EXPERT_TILE).astype(BF16)
        acts = []
        for g in range(N_UP_GROUPS):
            cols = slice(g * UP_GROUP, (g + 1) * UP_GROUP)
            u = _dot(h, wu_ref[0, :, cols]) + bu_ref[0, :, cols]
            glu = jnp.minimum(u[:, :LANES], SWIGLU_LIMIT)
            lin = jnp.clip(u[:, LANES:], -SWIGLU_LIMIT, SWIGLU_LIMIT)
            acts.append((glu * _sigmoid(SWIGLU_ALPHA * glu) * (lin + 1.0)).astype(BF16))
        a = jnp.concatenate(acts, axis=-1)
        row = tile_ref[w] * EXPERT_TILE + lax.broadcasted_iota(jnp.int32, (EXPERT_TILE, 1), 0)
        mine = (row >= lo) & (row < hi)
        for c0 in range(0, D_MODEL, DOWN_COLS_PER_DOT):
            cols = slice(c0, c0 + DOWN_COLS_PER_DOT)
            kept = jnp.where(mine, _dot(a, wd_ref[0, :, cols]) + bd_ref[0, :, cols], acc_ref[:, cols])
            acc_ref[:, cols] = kept
            for j in range(DOWN_COLS_PER_DOT // LANES):
                part = pl.ds(c0 // LANES + j, EXPERT_TILE, stride=ROW_PARTS)
                o_ref[part, :] = kept[:, j * LANES:(j + 1) * LANES]


def _work_items(counts, n_rows):
    n_tiles = n_rows // EXPERT_TILE
    ends = jnp.cumsum(counts)
    tile_ends = jnp.arange(1, n_tiles + 1, dtype=jnp.int32) * EXPERT_TILE
    n_items = n_tiles + N_EXPERTS
    count = lambda cond: jnp.sum(cond.astype(jnp.int32), axis=1)
    slot_t = jnp.arange(n_tiles, dtype=jnp.int32) + count(ends[None, :] < tile_ends[:, None])
    slot_e = jnp.arange(N_EXPERTS, dtype=jnp.int32) + count(tile_ends[None, :] <= ends[:, None])
    slots = jnp.concatenate([slot_t, slot_e])
    vals = jnp.concatenate([tile_ends, ends])
    item = jnp.arange(n_items, dtype=jnp.int32)
    hi = jnp.sum(jnp.where(slots[None, :] == item[:, None], vals[None, :], 0), axis=1)
    lo = jnp.concatenate([jnp.zeros((1,), jnp.int32), hi[:-1]])
    tile_id = jnp.minimum(lo // EXPERT_TILE, n_tiles - 1)
    exp_id = jnp.minimum(count(ends[None, :] <= lo[:, None]), N_EXPERTS - 1)
    return tile_id, exp_id, lo, hi


def _experts(items, xs, w_up_g, b_up_g, w_down, b_down):
    n_items = items[0].shape[0]
    rows = lambda w, t, e, lo, hi: (t[w], 0)
    per_expert = lambda r, c: pl.BlockSpec((1, r, c), lambda w, t, e, lo, hi: (e[w], 0, 0))
    return pl.pallas_call(
        _experts_kernel,
        grid_spec=pltpu.PrefetchScalarGridSpec(
            num_scalar_prefetch=4,
            grid=(n_items,),
            in_specs=[pl.BlockSpec((EXPERT_TILE * ROW_PARTS, LANES), rows),
                      per_expert(D_MODEL, 2 * D_MODEL), per_expert(1, 2 * D_MODEL),
                      per_expert(D_MODEL, D_MODEL), per_expert(1, D_MODEL)],
            out_specs=pl.BlockSpec((EXPERT_TILE * ROW_PARTS, LANES), rows),
            scratch_shapes=[pltpu.VMEM((EXPERT_TILE, D_MODEL), F32)]),
        out_shape=jax.ShapeDtypeStruct(xs.shape, F32),
        compiler_params=_params("arbitrary"),
        name="experts",
    )(*items, xs, w_up_g, b_up_g, w_down, b_down)


def _combine_kernel(pos_ref, nxt_ref, ys_ref, wts_ref, x1_ref, g2_ref, nf_ref, o_ref, buf, sem):
    def copies(step, use_next):
        p_ref = nxt_ref if use_next else pos_ref

        def copy(t, k):
            src = ys_ref.at[_token_tile(p_ref[t * TOP_K + k]), :]
            return pltpu.make_async_copy(src, buf.at[step % 2, k, _token_tile(t), :], sem.at[step % 2])
        return copy

    n_tok = x1_ref.shape[0]
    _rows_one_step_ahead(n_tok, copies)
    mine = buf.at[pl.program_id(0) % 2]
    wts = wts_ref[...]
    acc = wts[:, 0:1] * _load_token_rows(mine.at[0], n_tok)
    for k in range(1, TOP_K):
        acc = acc + wts[:, k:k + 1] * _load_token_rows(mine.at[k], n_tok)
    o_ref[...] = _rms(x1_ref[...] + g2_ref[0] * acc, nf_ref[...])


def _combine(pos_flat, ys, wts, x1, mods, norm_final, seq):
    n_tok = x1.shape[0]
    tiles_per_sample = seq // COMBINE_TILE
    tok = lambda n: pl.BlockSpec((COMBINE_TILE, n), lambda i: (i, 0))
    n_steps = n_tok // COMBINE_TILE
    return pl.pallas_call(
        _combine_kernel,
        grid=(n_steps,),
        in_specs=_index_blocks(COMBINE_TILE, n_steps) + [
            pl.BlockSpec(memory_space=pl.ANY), tok(TOP_K), tok(D_MODEL),
            pl.BlockSpec((1, 1, D_MODEL), lambda i: (i // tiles_per_sample, 0, MOD_G2)),
            pl.BlockSpec((1, D_MODEL), lambda i: (0, 0))],
        out_specs=tok(D_MODEL),
        out_shape=jax.ShapeDtypeStruct((n_tok, D_MODEL), F32),
        scratch_shapes=[pltpu.VMEM((2, TOP_K, COMBINE_TILE * ROW_PARTS, LANES), F32),
                        pltpu.SemaphoreType.DMA((2,))],
        compiler_params=_params("arbitrary"),
        name="combine",
    )(pos_flat, pos_flat, ys, wts, x1, mods, norm_final.reshape(1, -1))


def _moe(h2, idx, wts, w_up_g, b_up_g, w_down, b_down, x1, mods, norm_final):
    bsz, seq, _ = x1.shape
    n_tok = bsz * seq
    pos, counts = _route(idx.reshape(n_tok, TOP_K))
    pos_flat = pos.reshape(-1)
    xs = _dispatch(pos_flat, h2.reshape(n_tok * ROW_PARTS, LANES))
    ys = _experts(_work_items(counts.reshape(-1), n_tok * TOP_K), xs, w_up_g, b_up_g, w_down, b_down)
    out = _combine(pos_flat, ys, wts.reshape(n_tok, TOP_K), x1.reshape(n_tok, D_MODEL), mods, norm_final, seq)
    return out.reshape(bsz, seq, D_MODEL)


def kernel(x, c, ctx, c_ctx, w_ada, b_ada, norm_mix, norm_ffn, w_in, w_gk_f, b_gk_f, w_gk_b, b_gk_b, gla_norm, w_gla_out, hy_conv_w, hy_conv_b, hy_f_w1, hy_f_b1, hy_f_w2, hy_f_b2, hy_f_w3, hy_f_b3, hy_f_w4, hy_sin_freq, hy_bias, w_hy_out, w_out, w_router, b_router, w_up, b_up, w_down, b_down, norm_final):
    depth = w_ada.shape[0]
    assert depth == 1, "single-layer block: the context stream only feeds later layers"
    bsz, seq, _ = x.shape

    n_rows = -(-(bsz + 1) // 8) * 8
    cc = jnp.zeros((n_rows, D_MODEL), F32).at[:bsz].set(c).at[bsz].set(c_ctx)
    mods = _ada(cc, w_ada[0], b_ada[0]).reshape(n_rows, 1, N_MOD * D_MODEL)

    sizes = (GLA_DK, GLA_DK, GLA_DV, GLA_DV, GLA_GATE_RANK, GLA_GATE_RANK, 3 * HY_WIDTH, D_MODEL, D_MODEL)
    offs = np.concatenate([[0], np.cumsum(sizes)])
    w_in_b = w_in[0].astype(BF16)
    wq, wk, wv, wog, wrf, wrb, why, wma, wmb = [w_in_b[:, offs[i]:offs[i + 1]] for i in range(len(sizes))]
    q, k, v, og, r_f, r_b, hy, m_a, m_b = _inproj(
        x, mods, lambda b: b, norm_mix[0], [wq, wk, wv, wog, wrf, wrb, why, wma, wmb],
        [BF16, BF16, BF16, BF16, F32, F32, BF16, BF16, BF16], tl=512, conv=(6, hy_conv_w[0], hy_conv_b[0]))
    k_c, v_c, rc_f, rc_b = _inproj(
        ctx, mods, lambda b: bsz, norm_mix[0], [wk, wv, wrf, wrb], [BF16, BF16, F32, F32], tl=ctx.shape[1])

    w_gk = jnp.stack([w_gk_f[0], w_gk_b[0]])
    b_gk = jnp.stack([b_gk_f[0], b_gk_b[0]])[:, None, :]
    o_f, o_b = _gla(q, k, v, r_f, r_b, k_c, v_c, rc_f, rc_b, w_gk, b_gk)

    t_sym, t_anti = [jnp.asarray(t).astype(BF16) for t in _dft_tables(seq)]
    hs, hd, pn = _hyena_filter(seq, hy_f_w1[0], hy_f_b1[0], hy_f_w2[0], hy_f_b2[0], hy_f_w3[0], hy_f_b3[0],
                               hy_f_w4[0], hy_sin_freq[0])
    p_spec, q_spec, k_nyq = _hyena_spectrum(t_sym, t_anti, hs, hd, pn, hy_bias[0])
    y_h = _hyena_conv(hy, (t_sym, t_anti, t_sym.T, t_anti.T), p_spec, q_spec, k_nyq)

    x1, h2, idx, wts = _post(o_f, o_b, og, y_h, m_a, m_b, x, mods, gla_norm[0], norm_ffn[0],
                             w_gla_out[0].astype(BF16), w_hy_out[0].astype(BF16), w_out[0].astype(BF16),
                             w_router[0], b_router[0])

    b_up_g = b_up[0].reshape(N_EXPERTS, N_UP_GROUPS, LANES, 2).transpose(0, 1, 3, 2).reshape(N_EXPERTS, 1, -1)
    return _moe(h2, idx, wts, _deinterleave_up(w_up[0]), b_up_g, w_down[0].astype(BF16), b_down[0][:, None, :],
                x1, mods, norm_final)
```

```python
import functools
import math

import jax
import jax.numpy as jnp
import numpy as np
from jax import lax
from jax.experimental import pallas as pl
from jax.experimental.pallas import tpu as pltpu

F32 = jnp.float32
BF16 = jnp.bfloat16

D_MODEL = 1024
GRID_W = 64
EPS = 1e-6
N_MOD = 6

GLA_HEADS = 4
GLA_HEAD_K = 128
GLA_HEAD_V = 256
GLA_DK = GLA_HEADS * GLA_HEAD_K
GLA_DV = GLA_HEADS * GLA_HEAD_V
GLA_GATE_RANK = 16
GLA_GATE_NORM = 16.0
GLA_CHUNK = 64

HY_WIDTH = D_MODEL
HY_EMB = 33
HY_FAST_DECAY = 0.3
HY_SLOW_DECAY = 1.5
HY_TARGET = 1e-2

N_EXPERTS = 32
TOP_K = 4
SWIGLU_LIMIT = 7.0
SWIGLU_ALPHA = 1.702

V7X_VMEM_BYTES = 64 * 1024 * 1024
VMEM_LIMIT = V7X_VMEM_BYTES - 8 * 1024 * 1024
LANES = 128

MOD_SH1, MOD_SC1, MOD_G1, MOD_SH2, MOD_SC2, MOD_G2 = range(N_MOD)


def _params(*sem):
    return pltpu.CompilerParams(dimension_semantics=sem, vmem_limit_bytes=VMEM_LIMIT)


def _resident(shape):
    return pl.BlockSpec(shape, lambda *_: (0,) * len(shape), pipeline_mode=pl.Buffered(1))


def _dot(a, b):
    return jnp.dot(a, b, preferred_element_type=F32)


def _dot_nt(a, b):
    return lax.dot_general(a, b, (((1,), (1,)), ((), ())), preferred_element_type=F32)


def _dot_tn(a, b):
    return lax.dot_general(a, b, (((0,), (0,)), ((), ())), preferred_element_type=F32)


def _split(a):
    hi = a.astype(BF16)
    lo = (a - hi.astype(F32)).astype(BF16)
    return hi, lo


def _dot3(a, b):
    ah, al = _split(a)
    bh, bl = _split(b)
    return _dot(ah, bh) + (_dot(ah, bl) + _dot(al, bh))


def _sigmoid(x):
    return 1.0 / (1.0 + jnp.exp(-x))


def _log_sigmoid(x):
    return jnp.minimum(x, 0.0) - jnp.log(1.0 + jnp.exp(-jnp.abs(x)))


def _rms(x, w):
    return x * lax.rsqrt(jnp.mean(x * x, axis=-1, keepdims=True) + EPS) * w


def _ada_kernel(c_ref, w_ref, b_ref, o_ref):
    c = c_ref[...]
    o_ref[...] = _dot3(c * _sigmoid(c), w_ref[...]) + b_ref[...]


def _ada(cc, w_ada, b_ada):
    rows = cc.shape[0]
    return pl.pallas_call(
        _ada_kernel,
        grid=(N_MOD,),
        in_specs=[pl.BlockSpec((rows, D_MODEL), lambda j: (0, 0)),
                  pl.BlockSpec((D_MODEL, D_MODEL), lambda j: (0, j)),
                  pl.BlockSpec((1, D_MODEL), lambda j: (0, j))],
        out_specs=pl.BlockSpec((rows, D_MODEL), lambda j: (0, j)),
        out_shape=jax.ShapeDtypeStruct((rows, N_MOD * D_MODEL), F32),
        compiler_params=_params("parallel"),
        name="ada",
    )(cc, w_ada, b_ada.reshape(1, -1))


INPROJ_COL_CHUNK = 512


def _inproj_kernel(n_out, conv_out, x_ref, sc_ref, sh_ref, nw_ref, *refs):
    n_in = n_out + (0 if conv_out is None else 2)
    w_refs, o_refs = refs[:n_out], refs[n_in:]
    h = _rms(x_ref[0], nw_ref[...]) * (1.0 + sc_ref[0]) + sh_ref[0]
    hb = h.astype(BF16)
    for i, (w_ref, o_ref) in enumerate(zip(w_refs, o_refs)):
        n = w_ref.shape[1]
        for c0 in range(0, n, INPROJ_COL_CHUNK):
            c1 = min(c0 + INPROJ_COL_CHUNK, n)
            y = _dot(hb, w_ref[:, c0:c1])
            if i == conv_out:
                cw_ref, cb_ref = refs[n_out:n_in]
                y = _short_conv(y, cw_ref[:, c0:c1], cb_ref[:, c0:c1])
            o_ref[0, :, c0:c1] = y.astype(o_ref.dtype)


def _inproj(x, mods, mod_row, norm_w, weights, out_dtypes, tl, conv=None):
    bsz, seq, _ = x.shape
    n_out = len(weights)
    assert tl % GRID_W == 0
    in_specs = [pl.BlockSpec((1, tl, D_MODEL), lambda b, i: (b, i, 0)),
                pl.BlockSpec((1, 1, D_MODEL), lambda b, i: (mod_row(b), 0, MOD_SC1)),
                pl.BlockSpec((1, 1, D_MODEL), lambda b, i: (mod_row(b), 0, MOD_SH1)),
                _resident((1, D_MODEL))]
    in_specs += [_resident(w.shape) for w in weights]
    extra = []
    if conv is not None:
        extra = [conv[1], conv[2].reshape(1, -1)]
        in_specs += [_resident(a.shape) for a in extra]
    out_specs = [pl.BlockSpec((1, tl, w.shape[1]), lambda b, i: (b, i, 0)) for w in weights]
    out_shape = [jax.ShapeDtypeStruct((bsz, seq, w.shape[1]), dt) for w, dt in zip(weights, out_dtypes)]
    return pl.pallas_call(
        functools.partial(_inproj_kernel, n_out, None if conv is None else conv[0]),
        grid=(bsz, seq // tl),
        in_specs=in_specs, out_specs=out_specs, out_shape=out_shape,
        compiler_params=_params("parallel", "parallel"),
        name="inproj",
    )(x, mods, mods, norm_w.reshape(1, -1), *weights, *extra)


GLA_TILE = 256
GLA_NCH = GLA_TILE // GLA_CHUNK
GLA_SCALE = GLA_HEAD_K ** -0.5
GLA_HPS = 4
GLA_KW = GLA_HPS * GLA_HEAD_K
GLA_VW = GLA_HPS * GLA_HEAD_V


@functools.lru_cache(maxsize=None)
def _chunk_triangles():
    i = np.arange(GLA_TILE)
    same = (i[:, None] // GLA_CHUNK) == (i[None, :] // GLA_CHUNK)
    return np.stack([same & (i[None, :] <= i[:, None]), same & (i[None, :] >= i[:, None])]).astype(np.float32)


def _gla_kernel(tri_ref, qf_ref, kf_ref, vf_ref, rf_ref, qb_ref, kb_ref, vb_ref, rb_ref,
                kc_ref, vc_ref, rcf_ref, rcb_ref, wgk_ref, bgk_ref,
                of_ref, ob_ref, sf_ref, sb_ref):
    t = pl.program_id(2)
    crow = lax.broadcasted_iota(jnp.int32, (GLA_CHUNK, GLA_CHUNK), 0)
    ccol = lax.broadcasted_iota(jnp.int32, (GLA_CHUNK, GLA_CHUNK), 1)
    heads = range(GLA_HPS)

    def rows(c):
        return slice(c * GLA_CHUNK, (c + 1) * GLA_CHUNK)

    def hk(h):
        return slice(h * GLA_HEAD_K, (h + 1) * GLA_HEAD_K)

    def hv(h):
        return slice(h * GLA_HEAD_V, (h + 1) * GLA_HEAD_V)

    def order(fwd):
        return range(GLA_NCH) if fwd else range(GLA_NCH - 1, -1, -1)

    def cum_decay(r, d):
        z = _dot3(r, wgk_ref[d]) + bgk_ref[d]
        g = _log_sigmoid(z) * (1.0 / GLA_GATE_NORM)
        gh, gl = _split(g)
        return _dot(tri_ref[d], gh) + _dot(tri_ref[d], gl)

    def updates(k_ref, v_ref, b, fwd):
        out = {}
        for c in range(GLA_NCH):
            i = c * GLA_CHUNK + (GLA_CHUNK - 1 if fwd else 0)
            total = b[i:i + 1, :]
            k_upd = (k_ref[0, rows(c), :].astype(F32) * jnp.exp(total - b[rows(c)])).astype(BF16)
            dec = jnp.exp(total)
            for h in heads:
                out[c, h] = (dec[:, hk(h)], _dot_tn(v_ref[0, rows(c), hv(h)], k_upd[:, hk(h)]))
        return out

    def context_state(s_ref, r_ref, d, fwd):
        upd = updates(kc_ref, vc_ref, cum_decay(r_ref[0], d), fwd)
        for h in heads:
            st = jnp.zeros((GLA_HEAD_V, GLA_HEAD_K), F32)
            for c in order(fwd):
                dec, inc = upd[c, h]
                st = st * dec + inc
            s_ref[h] = st

    def scan(s_ref, q_ref, k_ref, v_ref, r_ref, o_ref, d, fwd):
        mask = (ccol <= crow) if fwd else (ccol >= crow)
        b = cum_decay(r_ref[0], d)
        q_dec = (q_ref[0].astype(F32) * (jnp.exp(b) * GLA_SCALE)).astype(BF16)
        k_inv = (k_ref[0].astype(F32) * jnp.exp(-b)).astype(BF16)
        upd = updates(k_ref, v_ref, b, fwd)
        intra = {}
        for c in range(GLA_NCH):
            for h in heads:
                att = jnp.where(mask, _dot_nt(q_dec[rows(c), hk(h)], k_inv[rows(c), hk(h)]), 0.0).astype(BF16)
                intra[c, h] = _dot(att, v_ref[0, rows(c), hv(h)])
        enter = {}
        for h in heads:
            st = s_ref[h]
            for c in order(fwd):
                enter[c, h] = st.astype(BF16)
                dec, inc = upd[c, h]
                st = st * dec + inc
            s_ref[h] = st
        for c in range(GLA_NCH):
            for h in heads:
                o = intra[c, h] + _dot_nt(q_dec[rows(c), hk(h)], enter[c, h])
                o_ref[0, rows(c), hv(h)] = o.astype(o_ref.dtype)

    @pl.when(t == 0)
    def _():
        context_state(sf_ref, rcf_ref, 0, True)
        context_state(sb_ref, rcb_ref, 1, False)

    scan(sf_ref, qf_ref, kf_ref, vf_ref, rf_ref, of_ref, 0, True)
    scan(sb_ref, qb_ref, kb_ref, vb_ref, rb_ref, ob_ref, 1, False)


def _gla(q, k, v, r_f, r_b, k_c, v_c, rc_f, rc_b, w_gk, b_gk):
    bsz, seq, _ = q.shape
    assert k_c.shape[1] == GLA_TILE
    nt = seq // GLA_TILE
    tri = jnp.asarray(_chunk_triangles()).astype(BF16)
    fwd = lambda b, h, t: (b, t, h)
    bwd = lambda b, h, t: (b, nt - 1 - t, h)
    fwd0 = lambda b, h, t: (b, t, 0)
    bwd0 = lambda b, h, t: (b, nt - 1 - t, 0)
    ctx = lambda b, h, t: (b, 0, h)
    ctx0 = lambda b, h, t: (b, 0, 0)
    kblk = (1, GLA_TILE, GLA_KW)
    vblk = (1, GLA_TILE, GLA_VW)
    rblk = (1, GLA_TILE, GLA_GATE_RANK)
    in_specs = [_resident(tri.shape),
                pl.BlockSpec(kblk, fwd), pl.BlockSpec(kblk, fwd), pl.BlockSpec(vblk, fwd), pl.BlockSpec(rblk, fwd0),
                pl.BlockSpec(kblk, bwd), pl.BlockSpec(kblk, bwd), pl.BlockSpec(vblk, bwd), pl.BlockSpec(rblk, bwd0),
                pl.BlockSpec(kblk, ctx), pl.BlockSpec(vblk, ctx), pl.BlockSpec(rblk, ctx0), pl.BlockSpec(rblk, ctx0),
                pl.BlockSpec((2, GLA_GATE_RANK, GLA_KW), lambda b, h, t: (0, 0, h)),
                pl.BlockSpec((2, 1, GLA_KW), lambda b, h, t: (0, 0, h))]
    out_specs = [pl.BlockSpec(vblk, fwd), pl.BlockSpec(vblk, bwd)]
    out_shape = [jax.ShapeDtypeStruct((bsz, seq, GLA_DV), BF16)] * 2
    return pl.pallas_call(
        _gla_kernel,
        grid=(bsz, GLA_HEADS // GLA_HPS, nt),
        in_specs=in_specs, out_specs=out_specs, out_shape=out_shape,
        scratch_shapes=[pltpu.VMEM((GLA_HPS, GLA_HEAD_V, GLA_HEAD_K), F32)] * 2,
        compiler_params=_params("parallel", "parallel", "arbitrary"),
        name="gla",
    )(tri, q, k, v, r_f, q, k, v, r_b, k_c, v_c, rc_f, rc_b, w_gk, b_gk)


HY_CW = 256
HY_ROW_CHUNK = 512
MLP_PAD = 128


@functools.lru_cache(maxsize=None)
def _dft_tables(seq):
    n, half = 2 * seq, seq // 2
    m = np.arange(half)[:, None]
    s = np.arange(half)[None, :]
    ang = 2.0 * np.pi * np.arange(n) / n
    cos = lambda k: np.cos(ang)[(k * s) % n]
    sin = lambda k: np.sin(ang)[(k * s) % n]
    sym = np.concatenate([cos(2 * m), sin(2 * m + 1)], axis=0)
    anti = np.concatenate([sin(2 * m), cos(2 * m + 1)], axis=0)
    return sym.astype(np.float32), anti.astype(np.float32)


@functools.lru_cache(maxsize=None)
def _filter_features(seq):
    bands = (HY_EMB - 1) // 2
    pos = np.arange(seq, dtype=np.float64)[:, None]
    t = pos / max(seq - 1, 1)
    f = np.linspace(1e-4, bands - 1, bands)[None]
    ang = (2.0 * math.pi / seq) * pos * f
    z = np.concatenate([t, np.cos(ang), -np.sin(ang)], axis=-1)
    out = np.zeros((seq, MLP_PAD), np.float32)
    out[:, :HY_EMB] = z
    deltas = np.abs(np.linspace(math.log(HY_TARGET) / HY_SLOW_DECAY, math.log(HY_TARGET) / HY_FAST_DECAY, HY_WIDTH))
    return out, deltas.astype(np.float32)[None]


def _filter_kernel(z_ref, w1_ref, b1_ref, w2_ref, b2_ref, w3_ref, b3_ref, fr_ref, w4f_ref, w4b_ref, dl_ref,
                   hs_ref, hd_ref, pn_ref, hid_ref):
    z = z_ref[...]

    @pl.when(pl.program_id(0) == 0)
    def _():
        fr = fr_ref[...]
        h = jnp.sin(fr * (_dot3(z, w1_ref[...]) + b1_ref[...]))
        h = jnp.sin(fr * (_dot3(h, w2_ref[...]) + b2_ref[...]))
        hid_ref[...] = jnp.sin(fr * (_dot3(h, w3_ref[...]) + b3_ref[...]))

    h = hid_ref[...]
    window = jnp.exp(-z[:, 0:1] * dl_ref[...])
    h_f = _dot3(h, w4f_ref[...]) * window
    h_b = _dot3(h, w4b_ref[...]) * window
    pos = lax.broadcasted_iota(jnp.int32, (z.shape[0], 1), 0)
    h_b = jnp.where(pos == 0, 0.0, h_b)
    hs = h_f + h_b
    hs_ref[...] = hs.astype(BF16)
    hd_ref[...] = (h_b - h_f).astype(BF16)
    sign = jnp.where(pos % 2 == 0, 1.0, -1.0)
    pn_ref[...] = jnp.sum(hs * sign, axis=0, keepdims=True)


def _pad2(a, rows, cols):
    return jnp.zeros((rows, cols), F32).at[:a.shape[0], :a.shape[1]].set(a.astype(F32))


def _hyena_filter(seq, w1, b1, w2, b2, w3, b3, w4, freq):
    z_np, deltas_np = _filter_features(seq)
    p = MLP_PAD
    args = [jnp.asarray(z_np), _pad2(w1, p, p), _pad2(b1[None], 1, p), _pad2(w2, p, p), _pad2(b2[None], 1, p),
            _pad2(w3, p, p), _pad2(b3[None], 1, p), _pad2(freq[None], 1, p),
            _pad2(w4[:, :HY_WIDTH], p, HY_WIDTH), _pad2(w4[:, HY_WIDTH:], p, HY_WIDTH), jnp.asarray(deltas_np)]
    full = lambda shape: pl.BlockSpec(shape, lambda j: (0, 0))
    colblk = lambda rows: pl.BlockSpec((rows, HY_CW), lambda j: (0, j))
    in_specs = [full((seq, p)), full((p, p)), full((1, p)), full((p, p)), full((1, p)), full((p, p)), full((1, p)),
                full((1, p)), colblk(p), colblk(p), colblk(1)]
    return pl.pallas_call(
        _filter_kernel,
        grid=(HY_WIDTH // HY_CW,),
        in_specs=in_specs,
        out_specs=[colblk(seq), colblk(seq), colblk(1)],
        out_shape=[jax.ShapeDtypeStruct((seq, HY_WIDTH), BF16), jax.ShapeDtypeStruct((seq, HY_WIDTH), BF16),
                   jax.ShapeDtypeStruct((1, HY_WIDTH), F32)],
        scratch_shapes=[pltpu.VMEM((seq, p), F32)],
        compiler_params=_params("arbitrary"),
        name="hy_filter",
    )(*args)


REV_BLOCK = 256


def _negate_index(x):
    n = x.shape[0]
    r = lax.broadcasted_iota(jnp.int32, (REV_BLOCK, REV_BLOCK), 0)
    c = lax.broadcasted_iota(jnp.int32, (REV_BLOCK, REV_BLOCK), 1)
    flip = jnp.where(r + c == REV_BLOCK - 1, 1.0, 0.0).astype(BF16)
    if x.dtype == BF16:
        pieces = [x]
    else:
        x0 = x.astype(BF16)
        r1 = x - x0.astype(F32)
        x1 = r1.astype(BF16)
        pieces = [x0, x1, (r1 - x1.astype(F32)).astype(BF16)]
    blocks = []
    for i in range(n // REV_BLOCK - 1, -1, -1):
        rows = slice(i * REV_BLOCK, (i + 1) * REV_BLOCK)
        acc = _dot(flip, pieces[0][rows])
        for p in pieces[1:]:
            acc = acc + _dot(flip, p[rows])
        blocks.append(acc)
    return pltpu.roll(jnp.concatenate(blocks, axis=0), 1, 0)


def _fold(x):
    half = x.shape[0] // 2
    lo = x[:half].astype(F32)
    hi = _negate_index(x[half:])
    pos = lax.broadcasted_iota(jnp.int32, (half, 1), 0)
    mid = hi[0:1]
    hi = jnp.where(pos == 0, 0.0, hi)
    return (lo + hi).astype(BF16), (lo - hi).astype(BF16), mid


def _alternating(n):
    pos = lax.broadcasted_iota(jnp.int32, (n, 1), 0)
    return jnp.where(pos % 2 == 0, 1.0, -1.0)


def _spectrum_kernel(ts_ref, ta_ref, hs_ref, hd_ref, pn_ref, bias_ref, p_ref, q_ref, kn_ref):
    half = hs_ref.shape[0] // 2
    bias = bias_ref[...]
    sgn = _alternating(half)
    s_sym, s_anti, s_mid = _fold(hs_ref[...])
    d_sym, d_anti, d_mid = _fold(hd_ref[...])
    p_ref[:half, :] = _dot(ts_ref[:half, :], s_sym) + sgn * s_mid + bias
    p_ref[half:, :] = _dot(ta_ref[half:, :], s_anti) + bias
    q_ref[:half, :] = _dot(ta_ref[:half, :], d_anti)
    q_ref[half:, :] = _dot(ts_ref[half:, :], d_sym) + sgn * d_mid
    kn_ref[...] = pn_ref[...] + bias


def _hyena_spectrum(t_sym, t_anti, hs, hd, pn, bias):
    seq = hs.shape[0]
    colblk = lambda rows: pl.BlockSpec((rows, HY_CW), lambda j: (0, j))
    return pl.pallas_call(
        _spectrum_kernel,
        grid=(HY_WIDTH // HY_CW,),
        in_specs=[_resident(t_sym.shape), _resident(t_anti.shape), colblk(seq), colblk(seq), colblk(1), colblk(1)],
        out_specs=[colblk(seq), colblk(seq), colblk(1)],
        out_shape=[jax.ShapeDtypeStruct((seq, HY_WIDTH), F32)] * 2 + [jax.ShapeDtypeStruct((1, HY_WIDTH), F32)],
        compiler_params=_params("parallel"),
        name="hy_spectrum",
    )(t_sym, t_anti, hs, hd, pn, bias.reshape(1, -1))


def _short_conv(u, w, b):
    n = u.shape[0]
    pos = lax.broadcasted_iota(jnp.int32, (n, 1), 0) % GRID_W
    up = jnp.where(pos == 0, 0.0, pltpu.roll(u, 1, 0))
    dn = jnp.where(pos == GRID_W - 1, 0.0, pltpu.roll(u, n - 1, 0))
    return up * w[0:1] + u * w[1:2] + dn * w[2:3] + b


def _hy_fwd_kernel(ts_ref, ta_ref, x1_ref, v_ref, p_ref, q_ref, kn_ref, s_ref):
    seq = x1_ref.shape[1]
    n, half = 2 * seq, seq // 2
    u_sym, u_anti, u_mid = _fold((v_ref[0].astype(F32) * x1_ref[0].astype(F32)).astype(BF16))
    sgn = _alternating(half)
    nyq = (jnp.sum(sgn * u_sym.astype(F32), axis=0, keepdims=True) + u_mid) * kn_ref[...] * (1.0 / n)
    for r0 in range(0, half, HY_ROW_CHUNK):
        r1 = r0 + HY_ROW_CHUNK
        mid = sgn[r0:r1] * u_mid
        re_e = _dot(ts_ref[r0:r1, :], u_sym) + mid
        im_o = _dot(ts_ref[half + r0:half + r1, :], u_sym) + mid
        im_e = _dot(ta_ref[r0:r1, :], u_anti)
        re_o = _dot(ta_ref[half + r0:half + r1, :], u_anti)
        pe, po = p_ref[r0:r1, :], p_ref[half + r0:half + r1, :]
        qe, qo = q_ref[r0:r1, :], q_ref[half + r0:half + r1, :]
        s_re_e = (re_e * pe + im_e * qe) * (2.0 / n)
        s_im_e = (im_e * pe - re_e * qe) * (2.0 / n)
        if r0 == 0:
            pos = lax.broadcasted_iota(jnp.int32, (HY_ROW_CHUNK, 1), 0)
            s_re_e = jnp.where(pos == 0, re_e * pe * (1.0 / n), s_re_e)
            s_im_e = jnp.where(pos == 0, nyq, s_im_e)
        s_ref[0, r0:r1, :] = s_re_e.astype(BF16)
        s_ref[0, half + r0:half + r1, :] = ((im_o * po - re_o * qo) * (2.0 / n)).astype(BF16)
        s_ref[0, seq + r0:seq + r1, :] = s_im_e.astype(BF16)
        s_ref[0, seq + half + r0:seq + half + r1, :] = ((re_o * po + im_o * qo) * (2.0 / n)).astype(BF16)


def _hy_inv_kernel(gs_ref, ga_ref, s_ref, x0_ref, o_ref):
    seq = x0_ref.shape[1]
    half = seq // 2
    x0 = x0_ref[0].astype(F32)
    sgn = _alternating(half)
    nyq = sgn * s_ref[0, seq:seq + 1, :].astype(F32)
    mirrored = []
    for r0 in range(0, half, HY_ROW_CHUNK):
        r1 = r0 + HY_ROW_CHUNK
        sym = _dot(gs_ref[r0:r1, :], s_ref[0, :seq, :]) + nyq[r0:r1]
        anti = _dot(ga_ref[r0:r1, :], s_ref[0, seq:, :])
        o_ref[0, r0:r1, :] = (x0[r0:r1] * (sym + anti)).astype(o_ref.dtype)
        mirrored.append(sym - anti)
    centre = (jnp.sum(sgn * (s_ref[0, :half, :].astype(F32) + s_ref[0, half:seq, :].astype(F32)), axis=0, keepdims=True)
              + nyq[0:1])
    pos = lax.broadcasted_iota(jnp.int32, (half, 1), 0)
    upper = _negate_index(jnp.where(pos == 0, centre, jnp.concatenate(mirrored, axis=0)).astype(BF16))
    o_ref[0, half:, :] = (x0[half:] * upper).astype(o_ref.dtype)


def _hyena_conv(hy, tabs, p_spec, q_spec, k_nyq):
    bsz, seq, _ = hy.shape
    nc = HY_WIDTH // HY_CW
    t_sym, t_anti, g_sym, g_anti = tabs
    sig = lambda part: pl.BlockSpec((1, seq, HY_CW), lambda j, b: (b, 0, part * nc + j))
    spec = lambda rows: pl.BlockSpec((rows, HY_CW), lambda j, b: (0, j))
    s = pl.pallas_call(
        _hy_fwd_kernel,
        grid=(nc, bsz),
        in_specs=[_resident(t_sym.shape), _resident(t_anti.shape), sig(1), sig(2),
                  spec(seq), spec(seq), spec(1)],
        out_specs=pl.BlockSpec((1, 2 * seq, HY_CW), lambda j, b: (b, 0, j)),
        out_shape=jax.ShapeDtypeStruct((bsz, 2 * seq, HY_WIDTH), BF16),
        compiler_params=_params("parallel", "parallel"),
        name="hy_fwd",
    )(t_sym, t_anti, hy, hy, p_spec, q_spec, k_nyq)
    return pl.pallas_call(
        _hy_inv_kernel,
        grid=(nc, bsz),
        in_specs=[_resident(g_sym.shape), _resident(g_anti.shape),
                  pl.BlockSpec((1, 2 * seq, HY_CW), lambda j, b: (b, 0, j)), sig(0)],
        out_specs=pl.BlockSpec((1, seq, HY_CW), lambda j, b: (b, 0, j)),
        out_shape=jax.ShapeDtypeStruct((bsz, seq, HY_WIDTH), BF16),
        compiler_params=_params("parallel", "parallel"),
        name="hy_inv",
    )(g_sym, g_anti, s, hy)


ROW_PARTS = D_MODEL // LANES


def _token_tile(t):
    return pl.ds(pl.multiple_of(t * ROW_PARTS, ROW_PARTS), ROW_PARTS)


def _load_token_rows(ref, n_tok):
    return jnp.concatenate([ref[pl.ds(j, n_tok, stride=ROW_PARTS), :] for j in range(ROW_PARTS)], axis=-1)


def _store_token_rows(ref, val):
    for j in range(ROW_PARTS):
        ref[pl.ds(j, val.shape[0], stride=ROW_PARTS), :] = val[:, j * LANES:(j + 1) * LANES]


POST_TILE = 512


def _post_kernel(of_ref, ob_ref, og_ref, yh_ref, ma_ref, mb_ref, x_ref, g1_ref, sc2_ref, sh2_ref,
                 gn_ref, nf_ref, wa_ref, wb_ref, wo_ref, wr_ref, br_ref,
                 x1_ref, h2_ref, idx_ref, wts_ref):
    o = of_ref[0].astype(F32) + ob_ref[0].astype(F32)
    gn = gn_ref[...]
    heads = [_rms(o[:, h * GLA_HEAD_V:(h + 1) * GLA_HEAD_V], gn) for h in range(GLA_HEADS)]
    og = og_ref[0].astype(F32)
    a_in = jnp.concatenate(heads, axis=-1) * (og * _sigmoid(og))
    branch_a = _dot(a_in.astype(BF16), wa_ref[...])
    branch_b = _dot(yh_ref[0], wb_ref[...])
    y = _sigmoid(ma_ref[0].astype(F32)) * branch_a + _sigmoid(mb_ref[0].astype(F32)) * branch_b
    x1 = x_ref[0] + g1_ref[0] * _dot(y.astype(BF16), wo_ref[...])
    x1_ref[0] = x1
    h2 = _rms(x1, nf_ref[...]) * (1.0 + sc2_ref[0]) + sh2_ref[0]
    _store_token_rows(h2_ref.at[0], h2)

    logits = _dot3(h2, wr_ref[...]) + br_ref[...]
    lane = lax.broadcasted_iota(jnp.int32, logits.shape, 1)
    slot = lax.broadcasted_iota(jnp.int32, (logits.shape[0], TOP_K), 1)
    work = logits
    tops = []
    idx = jnp.zeros((logits.shape[0], TOP_K), jnp.int32)
    for k in range(TOP_K):
        m = jnp.max(work, axis=-1, keepdims=True)
        first = jnp.min(jnp.where(work == m, lane, N_EXPERTS), axis=-1, keepdims=True)
        tops.append(m)
        idx = jnp.where(slot == k, first, idx)
        work = jnp.where(lane == first, -jnp.inf, work)
    es = [jnp.exp(m - tops[0]) for m in tops]
    inv = 1.0 / functools.reduce(lambda a, b: a + b, es)
    wts = jnp.zeros((logits.shape[0], TOP_K), F32)
    for k, e in enumerate(es):
        wts = jnp.where(slot == k, e * inv, wts)
    idx_ref[0] = idx
    wts_ref[0] = wts


def _post(o_f, o_b, og, y_h, m_a, m_b, x, mods, gla_norm, norm_ffn, w_a, w_b, w_o, w_router, b_router):
    bsz, seq, _ = x.shape
    tok = lambda n: pl.BlockSpec((1, POST_TILE, n), lambda b, i: (b, i, 0))
    mod = lambda which: pl.BlockSpec((1, 1, D_MODEL), lambda b, i: (b, 0, which))
    in_specs = [tok(D_MODEL)] * 7 + [mod(MOD_G1), mod(MOD_SC2), mod(MOD_SH2),
                                     _resident((1, GLA_HEAD_V)), _resident((1, D_MODEL)),
                                     _resident(w_a.shape), _resident(w_b.shape), _resident(w_o.shape),
                                     _resident(w_router.shape), _resident((1, N_EXPERTS))]
    return pl.pallas_call(
        _post_kernel,
        grid=(bsz, seq // POST_TILE),
        in_specs=in_specs,
        out_specs=[tok(D_MODEL), pl.BlockSpec((1, POST_TILE * ROW_PARTS, LANES), lambda b, i: (b, i, 0)),
                   tok(TOP_K), tok(TOP_K)],
        out_shape=[jax.ShapeDtypeStruct((bsz, seq, D_MODEL), F32),
                   jax.ShapeDtypeStruct((bsz, seq * ROW_PARTS, LANES), F32),
                   jax.ShapeDtypeStruct((bsz, seq, TOP_K), jnp.int32), jax.ShapeDtypeStruct((bsz, seq, TOP_K), F32)],
        compiler_params=_params("parallel", "parallel"),
        name="post",
    )(o_f, o_b, og, y_h, m_a, m_b, x, mods, mods, mods, gla_norm.reshape(1, -1), norm_ffn.reshape(1, -1),
      w_a, w_b, w_o, w_router, b_router.reshape(1, -1))


UP_GROUP = 2 * LANES
N_UP_GROUPS = 2 * D_MODEL // UP_GROUP


def _deinterleave_kernel(w_ref, o_ref):
    r = lax.broadcasted_iota(jnp.int32, (UP_GROUP, UP_GROUP), 0)
    c = lax.broadcasted_iota(jnp.int32, (UP_GROUP, UP_GROUP), 1)
    perm = jnp.where(c == (r % 2) * LANES + r // 2, 1.0, 0.0).astype(BF16)
    for g in range(w_ref.shape[2] // UP_GROUP):
        cols = slice(g * UP_GROUP, (g + 1) * UP_GROUP)
        o_ref[0, :, cols] = _dot(w_ref[0, :, cols].astype(BF16), perm).astype(BF16)


def _deinterleave_up(w_up):
    n_exp, d_in, d_up = w_up.shape
    blk = pl.BlockSpec((1, d_in, d_up // 2), lambda e, j: (e, 0, j))
    return pl.pallas_call(
        _deinterleave_kernel,
        grid=(n_exp, 2),
        in_specs=[blk], out_specs=blk,
        out_shape=jax.ShapeDtypeStruct(w_up.shape, BF16),
        compiler_params=_params("parallel", "parallel"),
        name="deinterleave",
    )(w_up)


RANK_TILE = 1024
ROUTE_TILE = 512
EXPERT_TILE = 1024
DOWN_COLS_PER_DOT = 256
COMBINE_TILE = 256


def _exact_count_dot(a, m):
    a0 = a.astype(BF16)
    r1 = a - a0.astype(F32)
    a1 = r1.astype(BF16)
    a2 = (r1 - a1.astype(F32)).astype(BF16)
    return _dot(a0, m) + (_dot(a1, m) + _dot(a2, m))


def _route_kernel(idx_ref, pos_ref, cnt_ref, counts, base, offs):
    p = pl.program_id(0)
    i = pl.program_id(1)
    n_tok = idx_ref.shape[0]
    idx = idx_ref[...]
    lane = lax.broadcasted_iota(jnp.int32, (n_tok, N_EXPERTS), 1)
    hot = jnp.zeros((n_tok, N_EXPERTS), F32)
    for k in range(TOP_K):
        hot = hot + jnp.where(lane == idx[:, k:k + 1], 1.0, 0.0)
    colsum = jnp.sum(hot, axis=0, keepdims=True)

    @pl.when((p == 0) & (i == 0))
    def _():
        counts[...] = jnp.zeros_like(counts)

    @pl.when(p == 0)
    def _():
        counts[...] += colsum

    @pl.when((p == 1) & (i == 0))
    def _():
        r = lax.broadcasted_iota(jnp.int32, (N_EXPERTS, N_EXPERTS), 0)
        c = lax.broadcasted_iota(jnp.int32, (N_EXPERTS, N_EXPERTS), 1)
        before = jnp.where(r < c, 1.0, 0.0).astype(BF16)
        offs[...] = _exact_count_dot(jnp.broadcast_to(counts[...], (8, N_EXPERTS)), before)[0:1]
        base[...] = jnp.zeros_like(base)

    @pl.when(p == 1)
    def _():
        r = lax.broadcasted_iota(jnp.int32, (n_tok, n_tok), 0)
        c = lax.broadcasted_iota(jnp.int32, (n_tok, n_tok), 1)
        earlier = jnp.where(c < r, 1.0, 0.0).astype(BF16)
        dense = offs[...] + base[...] + _dot(earlier, hot.astype(BF16))
        slot = lax.broadcasted_iota(jnp.int32, (n_tok, TOP_K), 1)
        pos = jnp.zeros((n_tok, TOP_K), F32)
        for k in range(TOP_K):
            mine = jnp.sum(jnp.where(lane == idx[:, k:k + 1], dense, 0.0), axis=-1, keepdims=True)
            pos = jnp.where(slot == k, mine, pos)
        pos_ref[...] = pos.astype(jnp.int32)
        base[...] += colsum

    cnt_ref[...] = counts[...].astype(jnp.int32)


def _route(idx):
    n_tok = idx.shape[0]
    return pl.pallas_call(
        _route_kernel,
        grid=(2, n_tok // RANK_TILE),
        in_specs=[pl.BlockSpec((RANK_TILE, TOP_K), lambda p, i: (i, 0))],
        out_specs=[pl.BlockSpec((RANK_TILE, TOP_K), lambda p, i: (i * p, 0)),
                   pl.BlockSpec((1, N_EXPERTS), lambda p, i: (0, 0))],
        out_shape=[jax.ShapeDtypeStruct((n_tok, TOP_K), jnp.int32), jax.ShapeDtypeStruct((1, N_EXPERTS), jnp.int32)],
        scratch_shapes=[pltpu.VMEM((1, N_EXPERTS), F32)] * 3,
        compiler_params=_params("arbitrary", "arbitrary"),
        name="route",
    )(idx)


def _start_rows(n_tok, copy):
    def issue(t, carry):
        for k in range(TOP_K):
            copy(t, k).start(priority=k % 2)
        return carry

    lax.fori_loop(0, n_tok, issue, 0, unroll=4)


def _wait_rows(n_tok, copy):
    def drain(t, carry):
        for k in range(TOP_K):
            copy(t, k).wait()
        return carry

    lax.fori_loop(0, n_tok, drain, 0, unroll=4)


def _dispatch_kernel(pos_ref, h_ref, xs_ref, sem):
    def copy(t, k):
        return pltpu.make_async_copy(h_ref.at[_token_tile(t), :], xs_ref.at[_token_tile(pos_ref[t * TOP_K + k]), :], sem)

    _start_rows(ROUTE_TILE, copy)
    _wait_rows(ROUTE_TILE, copy)


def _index_blocks(tile, n_steps):
    blk = (tile * TOP_K,)
    return [pl.BlockSpec(blk, lambda i: (i,), memory_space=pltpu.SMEM),
            pl.BlockSpec(blk, lambda i: (jnp.minimum(i + 1, n_steps - 1),), memory_space=pltpu.SMEM)]


def _dispatch(pos_flat, h2):
    n_tok = h2.shape[0] // ROW_PARTS
    n_steps = n_tok // ROUTE_TILE
    return pl.pallas_call(
        _dispatch_kernel,
        grid=(n_steps,),
        in_specs=[_index_blocks(ROUTE_TILE, n_steps)[0],
                  pl.BlockSpec((ROUTE_TILE * ROW_PARTS, LANES), lambda i: (i, 0))],
        out_specs=pl.BlockSpec(memory_space=pl.ANY),
        out_shape=jax.ShapeDtypeStruct((n_tok * TOP_K * ROW_PARTS, LANES), F32),
        scratch_shapes=[pltpu.SemaphoreType.DMA],
        compiler_params=_params("arbitrary"),
        name="dispatch",
    )(pos_flat, h2)


def _experts_kernel(tile_ref, exp_ref, lo_ref, hi_ref, x_ref, wu_ref, bu_ref, wd_ref, bd_ref, o_ref, acc_ref):
    w = pl.program_id(0)
    lo = lo_ref[w]
    hi = hi_ref[w]

    @pl.when(w == 0)
    def _():
        acc_ref[...] = jnp.zeros_like(acc_ref)

    @pl.when(hi > lo)
    def _():
        h = _load_token_rows(x_ref, EXPERT_TILE).astype(BF16)
        acts = []
        for g in range(N_UP_GROUPS):
            cols = slice(g * UP_GROUP, (g + 1) * UP_GROUP)
            u = _dot(h, wu_ref[0, :, cols]) + bu_ref[0, :, cols]
            glu = jnp.minimum(u[:, :LANES], SWIGLU_LIMIT)
            lin = jnp.clip(u[:, LANES:], -SWIGLU_LIMIT, SWIGLU_LIMIT)
            acts.append((glu * _sigmoid(SWIGLU_ALPHA * glu) * (lin + 1.0)).astype(BF16))
        a = jnp.concatenate(acts, axis=-1)
        row = tile_ref[w] * EXPERT_TILE + lax.broadcasted_iota(jnp.int32, (EXPERT_TILE, 1), 0)
        mine = (row >= lo) & (row < hi)
        for c0 in range(0, D_MODEL, DOWN_COLS_PER_DOT):
            cols = slice(c0, c0 + DOWN_COLS_PER_DOT)
            kept = jnp.where(mine, _dot(a, wd_ref[0, :, cols]) + bd_ref[0, :, cols], acc_ref[:, cols])
            acc_ref[:, cols] = kept
            for j in range(DOWN_COLS_PER_DOT // LANES):
                part = pl.ds(c0 // LANES + j, EXPERT_TILE, stride=ROW_PARTS)
                o_ref[part, :] = kept[:, j * LANES:(j + 1) * LANES]


def _work_items(counts, n_rows):
    n_tiles = n_rows // EXPERT_TILE
    ends = jnp.cumsum(counts)
    tile_ends = jnp.arange(1, n_tiles + 1, dtype=jnp.int32) * EXPERT_TILE
    n_items = n_tiles + N_EXPERTS
    count = lambda cond: jnp.sum(cond.astype(jnp.int32), axis=1)
    slot_t = jnp.arange(n_tiles, dtype=jnp.int32) + count(ends[None, :] < tile_ends[:, None])
    slot_e = jnp.arange(N_EXPERTS, dtype=jnp.int32) + count(tile_ends[None, :] <= ends[:, None])
    slots = jnp.concatenate([slot_t, slot_e])
    vals = jnp.concatenate([tile_ends, ends])
    item = jnp.arange(n_items, dtype=jnp.int32)
    hi = jnp.sum(jnp.where(slots[None, :] == item[:, None], vals[None, :], 0), axis=1)
    lo = jnp.concatenate([jnp.zeros((1,), jnp.int32), hi[:-1]])
    tile_id = jnp.minimum(lo // EXPERT_TILE, n_tiles - 1)
    exp_id = jnp.minimum(count(ends[None, :] <= lo[:, None]), N_EXPERTS - 1)
    return tile_id, exp_id, lo, hi


def _experts(items, xs, w_up_g, b_up_g, w_down, b_down):
    n_items = items[0].shape[0]
    rows = lambda w, t, e, lo, hi: (t[w], 0)
    per_expert = lambda r, c: pl.BlockSpec((1, r, c), lambda w, t, e, lo, hi: (e[w], 0, 0))
    return pl.pallas_call(
        _experts_kernel,
        grid_spec=pltpu.PrefetchScalarGridSpec(
            num_scalar_prefetch=4,
            grid=(n_items,),
            in_specs=[pl.BlockSpec((EXPERT_TILE * ROW_PARTS, LANES), rows),
                      per_expert(D_MODEL, 2 * D_MODEL), per_expert(1, 2 * D_MODEL),
                      per_expert(D_MODEL, D_MODEL), per_expert(1, D_MODEL)],
            out_specs=pl.BlockSpec((EXPERT_TILE * ROW_PARTS, LANES), rows),
            scratch_shapes=[pltpu.VMEM((EXPERT_TILE, D_MODEL), F32)]),
        out_shape=jax.ShapeDtypeStruct(xs.shape, F32),
        compiler_params=_params("arbitrary"),
        name="experts",
    )(*items, xs, w_up_g, b_up_g, w_down, b_down)


def _combine_kernel(pos_ref, nxt_ref, ys_ref, wts_ref, x1_ref, g2_ref, nf_ref, o_ref, buf_a, buf_b, sem):
    i = pl.program_id(0)
    sub = COMBINE_TILE

    def gather(p_ref, first_tok, buf, s):
        def copy(t, k):
            src = ys_ref.at[_token_tile(p_ref[(first_tok + t) * TOP_K + k]), :]
            return pltpu.make_async_copy(src, buf.at[k, _token_tile(t), :], sem.at[s])
        return copy

    this_a = gather(pos_ref, 0, buf_a, 0)
    this_b = gather(pos_ref, sub, buf_b, 1)
    next_a = gather(nxt_ref, 0, buf_a, 0)

    def reduce_rows(buf, half, start_other):
        rows = slice(half * sub, (half + 1) * sub)
        wts = wts_ref[rows, :]
        parts = []
        for j in range(ROW_PARTS):
            for t in range(j * sub // ROW_PARTS, (j + 1) * sub // ROW_PARTS):
                for k in range(TOP_K):
                    start_other(t, k).start(priority=k % 2)
            part = wts[:, 0:1] * buf[0, pl.ds(j, sub, stride=ROW_PARTS), :]
            for k in range(1, TOP_K):
                part = part + wts[:, k:k + 1] * buf[k, pl.ds(j, sub, stride=ROW_PARTS), :]
            parts.append(part)
        acc = jnp.concatenate(parts, axis=-1)
        o_ref[rows, :] = _rms(x1_ref[rows, :] + g2_ref[0] * acc, nf_ref[...])

    @pl.when(i == 0)
    def _():
        _start_rows(sub, this_a)

    _wait_rows(sub, this_a)
    reduce_rows(buf_a, 0, this_b)
    _wait_rows(sub, this_b)
    reduce_rows(buf_b, 1, next_a)

    @pl.when(i == pl.num_programs(0) - 1)
    def _():
        _wait_rows(sub, next_a)


def _combine(pos_flat, ys, wts, x1, mods, norm_final, seq):
    n_tok = x1.shape[0]
    step_tok = 2 * COMBINE_TILE
    steps_per_sample = seq // step_tok
    tok = lambda n: pl.BlockSpec((step_tok, n), lambda i: (i, 0))
    n_steps = n_tok // step_tok
    gather_buf = pltpu.VMEM((TOP_K, COMBINE_TILE * ROW_PARTS, LANES), F32)
    return pl.pallas_call(
        _combine_kernel,
        grid=(n_steps,),
        in_specs=_index_blocks(step_tok, n_steps) + [
            pl.BlockSpec(memory_space=pl.ANY), tok(TOP_K), tok(D_MODEL),
            pl.BlockSpec((1, 1, D_MODEL), lambda i: (i // steps_per_sample, 0, MOD_G2)),
            pl.BlockSpec((1, D_MODEL), lambda i: (0, 0))],
        out_specs=tok(D_MODEL),
        out_shape=jax.ShapeDtypeStruct((n_tok, D_MODEL), F32),
        scratch_shapes=[gather_buf, gather_buf, pltpu.SemaphoreType.DMA((2,))],
        compiler_params=_params("arbitrary"),
        name="combine",
    )(pos_flat, pos_flat, ys, wts, x1, mods, norm_final.reshape(1, -1))


def _moe(h2, idx, wts, w_up_g, b_up_g, w_down, b_down, x1, mods, norm_final):
    bsz, seq, _ = x1.shape
    n_tok = bsz * seq
    pos, counts = _route(idx.reshape(n_tok, TOP_K))
    pos_flat = pos.reshape(-1)
    xs = _dispatch(pos_flat, h2.reshape(n_tok * ROW_PARTS, LANES))
    ys = _experts(_work_items(counts.reshape(-1), n_tok * TOP_K), xs, w_up_g, b_up_g, w_down, b_down)
    out = _combine(pos_flat, ys, wts.reshape(n_tok, TOP_K), x1.reshape(n_tok, D_MODEL), mods, norm_final, seq)
    return out.reshape(bsz, seq, D_MODEL)


def kernel(x, c, ctx, c_ctx, w_ada, b_ada, norm_mix, norm_ffn, w_in, w_gk_f, b_gk_f, w_gk_b, b_gk_b, gla_norm, w_gla_out, hy_conv_w, hy_conv_b, hy_f_w1, hy_f_b1, hy_f_w2, hy_f_b2, hy_f_w3, hy_f_b3, hy_f_w4, hy_sin_freq, hy_bias, w_hy_out, w_out, w_router, b_router, w_up, b_up, w_down, b_down, norm_final):
    depth = w_ada.shape[0]
    assert depth == 1, "single-layer block: the context stream only feeds later layers"
    bsz, seq, _ = x.shape

    n_rows = -(-(bsz + 1) // 8) * 8
    cc = jnp.zeros((n_rows, D_MODEL), F32).at[:bsz].set(c).at[bsz].set(c_ctx)
    mods = _ada(cc, w_ada[0], b_ada[0]).reshape(n_rows, 1, N_MOD * D_MODEL)

    sizes = (GLA_DK, GLA_DK, GLA_DV, GLA_DV, GLA_GATE_RANK, GLA_GATE_RANK, 3 * HY_WIDTH, D_MODEL, D_MODEL)
    offs = np.concatenate([[0], np.cumsum(sizes)])
    w_in_b = w_in[0].astype(BF16)
    wq, wk, wv, wog, wrf, wrb, why, wma, wmb = [w_in_b[:, offs[i]:offs[i + 1]] for i in range(len(sizes))]
    q, k, v, og, r_f, r_b, hy, m_a, m_b = _inproj(
        x, mods, lambda b: b, norm_mix[0], [wq, wk, wv, wog, wrf, wrb, why, wma, wmb],
        [BF16, BF16, BF16, BF16, F32, F32, BF16, BF16, BF16], tl=512, conv=(6, hy_conv_w[0], hy_conv_b[0]))
    k_c, v_c, rc_f, rc_b = _inproj(
        ctx, mods, lambda b: bsz, norm_mix[0], [wk, wv, wrf, wrb], [BF16, BF16, F32, F32], tl=ctx.shape[1])

    w_gk = jnp.stack([w_gk_f[0], w_gk_b[0]])
    b_gk = jnp.stack([b_gk_f[0], b_gk_b[0]])[:, None, :]
    o_f, o_b = _gla(q, k, v, r_f, r_b, k_c, v_c, rc_f, rc_b, w_gk, b_gk)

    t_sym, t_anti = [jnp.asarray(t).astype(BF16) for t in _dft_tables(seq)]
    hs, hd, pn = _hyena_filter(seq, hy_f_w1[0], hy_f_b1[0], hy_f_w2[0], hy_f_b2[0], hy_f_w3[0], hy_f_b3[0],
                               hy_f_w4[0], hy_sin_freq[0])
    p_spec, q_spec, k_nyq = _hyena_spectrum(t_sym, t_anti, hs, hd, pn, hy_bias[0])
    y_h = _hyena_conv(hy, (t_sym, t_anti, t_sym.T, t_anti.T), p_spec, q_spec, k_nyq)

    x1, h2, idx, wts = _post(o_f, o_b, og, y_h, m_a, m_b, x, mods, gla_norm[0], norm_ffn[0],
                             w_gla_out[0].astype(BF16), w_hy_out[0].astype(BF16), w_out[0].astype(BF16),
                             w_router[0], b_router[0])

    b_up_g = b_up[0].reshape(N_EXPERTS, N_UP_GROUPS, LANES, 2).transpose(0, 1, 3, 2).reshape(N_EXPERTS, 1, -1)
    return _moe(h2, idx, wts, _deinterleave_up(w_up[0]), b_up_g, w_down[0].astype(BF16), b_down[0][:, None, :],
                x1, mods, norm_final)
```

```python
import functools
import math

import jax
import jax.numpy as jnp
import numpy as np
from jax import lax
from jax.experimental import pallas as pl
from jax.experimental.pallas import tpu as pltpu

F32 = jnp.float32
BF16 = jnp.bfloat16

D_MODEL = 1024
GRID_W = 64
EPS = 1e-6
N_MOD = 6

GLA_HEADS = 4
GLA_HEAD_K = 128
GLA_HEAD_V = 256
GLA_DK = GLA_HEADS * GLA_HEAD_K
GLA_DV = GLA_HEADS * GLA_HEAD_V
GLA_GATE_RANK = 16
GLA_GATE_NORM = 16.0
GLA_CHUNK = 64

HY_WIDTH = D_MODEL
HY_EMB = 33
HY_FAST_DECAY = 0.3
HY_SLOW_DECAY = 1.5
HY_TARGET = 1e-2

N_EXPERTS = 32
TOP_K = 4
SWIGLU_LIMIT = 7.0
SWIGLU_ALPHA = 1.702

V7X_VMEM_BYTES = 64 * 1024 * 1024
VMEM_LIMIT = V7X_VMEM_BYTES - 8 * 1024 * 1024
LANES = 128

MOD_SH1, MOD_SC1, MOD_G1, MOD_SH2, MOD_SC2, MOD_G2 = range(N_MOD)


def _params(*sem):
    return pltpu.CompilerParams(dimension_semantics=sem, vmem_limit_bytes=VMEM_LIMIT)


def _resident(shape):
    return pl.BlockSpec(shape, lambda *_: (0,) * len(shape), pipeline_mode=pl.Buffered(1))


def _dot(a, b):
    return jnp.dot(a, b, preferred_element_type=F32)


def _dot_nt(a, b):
    return lax.dot_general(a, b, (((1,), (1,)), ((), ())), preferred_element_type=F32)


def _dot_tn(a, b):
    return lax.dot_general(a, b, (((0,), (0,)), ((), ())), preferred_element_type=F32)


def _split(a):
    hi = a.astype(BF16)
    lo = (a - hi.astype(F32)).astype(BF16)
    return hi, lo


def _dot3(a, b):
    ah, al = _split(a)
    bh, bl = _split(b)
    return _dot(ah, bh) + (_dot(ah, bl) + _dot(al, bh))


def _sigmoid(x):
    return 1.0 / (1.0 + jnp.exp(-x))


def _log_sigmoid(x):
    return jnp.minimum(x, 0.0) - jnp.log(1.0 + jnp.exp(-jnp.abs(x)))


def _rms(x, w):
    return x * lax.rsqrt(jnp.mean(x * x, axis=-1, keepdims=True) + EPS) * w


def _ada_kernel(c_ref, w_ref, b_ref, o_ref):
    c = c_ref[...]
    o_ref[...] = _dot3(c * _sigmoid(c), w_ref[...]) + b_ref[...]


def _ada(cc, w_ada, b_ada):
    rows = cc.shape[0]
    return pl.pallas_call(
        _ada_kernel,
        grid=(N_MOD,),
        in_specs=[pl.BlockSpec((rows, D_MODEL), lambda j: (0, 0)),
                  pl.BlockSpec((D_MODEL, D_MODEL), lambda j: (0, j)),
                  pl.BlockSpec((1, D_MODEL), lambda j: (0, j))],
        out_specs=pl.BlockSpec((rows, D_MODEL), lambda j: (0, j)),
        out_shape=jax.ShapeDtypeStruct((rows, N_MOD * D_MODEL), F32),
        compiler_params=_params("parallel"),
        name="ada",
    )(cc, w_ada, b_ada.reshape(1, -1))


INPROJ_COL_CHUNK = 512


def _inproj_kernel(n_out, conv_out, x_ref, sc_ref, sh_ref, nw_ref, *refs):
    n_in = n_out + (0 if conv_out is None else 2)
    w_refs, o_refs = refs[:n_out], refs[n_in:]
    h = _rms(x_ref[0], nw_ref[...]) * (1.0 + sc_ref[0]) + sh_ref[0]
    hb = h.astype(BF16)
    for i, (w_ref, o_ref) in enumerate(zip(w_refs, o_refs)):
        n = w_ref.shape[1]
        for c0 in range(0, n, INPROJ_COL_CHUNK):
            c1 = min(c0 + INPROJ_COL_CHUNK, n)
            y = _dot(hb, w_ref[:, c0:c1])
            if i == conv_out:
                cw_ref, cb_ref = refs[n_out:n_in]
                y = _short_conv(y, cw_ref[:, c0:c1], cb_ref[:, c0:c1])
            o_ref[0, :, c0:c1] = y.astype(o_ref.dtype)


def _inproj(x, mods, mod_row, norm_w, weights, out_dtypes, tl, conv=None):
    bsz, seq, _ = x.shape
    n_out = len(weights)
    assert tl % GRID_W == 0
    in_specs = [pl.BlockSpec((1, tl, D_MODEL), lambda b, i: (b, i, 0)),
                pl.BlockSpec((1, 1, D_MODEL), lambda b, i: (mod_row(b), 0, MOD_SC1)),
                pl.BlockSpec((1, 1, D_MODEL), lambda b, i: (mod_row(b), 0, MOD_SH1)),
                _resident((1, D_MODEL))]
    in_specs += [_resident(w.shape) for w in weights]
    extra = []
    if conv is not None:
        extra = [conv[1], conv[2].reshape(1, -1)]
        in_specs += [_resident(a.shape) for a in extra]
    out_specs = [pl.BlockSpec((1, tl, w.shape[1]), lambda b, i: (b, i, 0)) for w in weights]
    out_shape = [jax.ShapeDtypeStruct((bsz, seq, w.shape[1]), dt) for w, dt in zip(weights, out_dtypes)]
    return pl.pallas_call(
        functools.partial(_inproj_kernel, n_out, None if conv is None else conv[0]),
        grid=(bsz, seq // tl),
        in_specs=in_specs, out_specs=out_specs, out_shape=out_shape,
        compiler_params=_params("parallel", "parallel"),
        name="inproj",
    )(x, mods, mods, norm_w.reshape(1, -1), *weights, *extra)


GLA_TILE = 256
GLA_NCH = GLA_TILE // GLA_CHUNK
GLA_SCALE = GLA_HEAD_K ** -0.5
GLA_HPS = 4
GLA_KW = GLA_HPS * GLA_HEAD_K
GLA_VW = GLA_HPS * GLA_HEAD_V


@functools.lru_cache(maxsize=None)
def _chunk_triangles():
    i = np.arange(GLA_TILE)
    same = (i[:, None] // GLA_CHUNK) == (i[None, :] // GLA_CHUNK)
    return np.stack([same & (i[None, :] <= i[:, None]), same & (i[None, :] >= i[:, None])]).astype(np.float32)


def _gla_kernel(tri_ref, qf_ref, kf_ref, vf_ref, rf_ref, qb_ref, kb_ref, vb_ref, rb_ref,
                kc_ref, vc_ref, rcf_ref, rcb_ref, wgk_ref, bgk_ref,
                of_ref, ob_ref, sf_ref, sb_ref):
    t = pl.program_id(2)
    crow = lax.broadcasted_iota(jnp.int32, (GLA_CHUNK, GLA_CHUNK), 0)
    ccol = lax.broadcasted_iota(jnp.int32, (GLA_CHUNK, GLA_CHUNK), 1)
    heads = range(GLA_HPS)

    def rows(c):
        return slice(c * GLA_CHUNK, (c + 1) * GLA_CHUNK)

    def hk(h):
        return slice(h * GLA_HEAD_K, (h + 1) * GLA_HEAD_K)

    def hv(h):
        return slice(h * GLA_HEAD_V, (h + 1) * GLA_HEAD_V)

    def order(fwd):
        return range(GLA_NCH) if fwd else range(GLA_NCH - 1, -1, -1)

    def cum_decay(r, d):
        z = _dot3(r, wgk_ref[d]) + bgk_ref[d]
        g = _log_sigmoid(z) * (1.0 / GLA_GATE_NORM)
        gh, gl = _split(g)
        return _dot(tri_ref[d], gh) + _dot(tri_ref[d], gl)

    def updates(k_ref, v_ref, b, fwd):
        out = {}
        for c in range(GLA_NCH):
            i = c * GLA_CHUNK + (GLA_CHUNK - 1 if fwd else 0)
            total = b[i:i + 1, :]
            k_upd = (k_ref[0, rows(c), :].astype(F32) * jnp.exp(total - b[rows(c)])).astype(BF16)
            dec = jnp.exp(total)
            for h in heads:
                out[c, h] = (dec[:, hk(h)], _dot_tn(v_ref[0, rows(c), hv(h)], k_upd[:, hk(h)]))
        return out

    def context_state(s_ref, r_ref, d, fwd):
        upd = updates(kc_ref, vc_ref, cum_decay(r_ref[0], d), fwd)
        for h in heads:
            st = jnp.zeros((GLA_HEAD_V, GLA_HEAD_K), F32)
            for c in order(fwd):
                dec, inc = upd[c, h]
                st = st * dec + inc
            s_ref[h] = st

    def scan(s_ref, q_ref, k_ref, v_ref, r_ref, o_ref, d, fwd):
        mask = (ccol <= crow) if fwd else (ccol >= crow)
        b = cum_decay(r_ref[0], d)
        q_dec = (q_ref[0].astype(F32) * (jnp.exp(b) * GLA_SCALE)).astype(BF16)
        k_inv = (k_ref[0].astype(F32) * jnp.exp(-b)).astype(BF16)
        upd = updates(k_ref, v_ref, b, fwd)
        intra = {}
        for c in range(GLA_NCH):
            for h in heads:
                att = jnp.where(mask, _dot_nt(q_dec[rows(c), hk(h)], k_inv[rows(c), hk(h)]), 0.0).astype(BF16)
                intra[c, h] = _dot(att, v_ref[0, rows(c), hv(h)])
        enter = {}
        for h in heads:
            st = s_ref[h]
            for c in order(fwd):
                enter[c, h] = st.astype(BF16)
                dec, inc = upd[c, h]
                st = st * dec + inc
            s_ref[h] = st
        for c in range(GLA_NCH):
            for h in heads:
                o = intra[c, h] + _dot_nt(q_dec[rows(c), hk(h)], enter[c, h])
                o_ref[0, rows(c), hv(h)] = o.astype(o_ref.dtype)

    @pl.when(t == 0)
    def _():
        context_state(sf_ref, rcf_ref, 0, True)
        context_state(sb_ref, rcb_ref, 1, False)

    scan(sf_ref, qf_ref, kf_ref, vf_ref, rf_ref, of_ref, 0, True)
    scan(sb_ref, qb_ref, kb_ref, vb_ref, rb_ref, ob_ref, 1, False)


def _gla(q, k, v, r_f, r_b, k_c, v_c, rc_f, rc_b, w_gk, b_gk):
    bsz, seq, _ = q.shape
    assert k_c.shape[1] == GLA_TILE
    nt = seq // GLA_TILE
    tri = jnp.asarray(_chunk_triangles()).astype(BF16)
    fwd = lambda b, h, t: (b, t, h)
    bwd = lambda b, h, t: (b, nt - 1 - t, h)
    fwd0 = lambda b, h, t: (b, t, 0)
    bwd0 = lambda b, h, t: (b, nt - 1 - t, 0)
    ctx = lambda b, h, t: (b, 0, h)
    ctx0 = lambda b, h, t: (b, 0, 0)
    kblk = (1, GLA_TILE, GLA_KW)
    vblk = (1, GLA_TILE, GLA_VW)
    rblk = (1, GLA_TILE, GLA_GATE_RANK)
    in_specs = [_resident(tri.shape),
                pl.BlockSpec(kblk, fwd), pl.BlockSpec(kblk, fwd), pl.BlockSpec(vblk, fwd), pl.BlockSpec(rblk, fwd0),
                pl.BlockSpec(kblk, bwd), pl.BlockSpec(kblk, bwd), pl.BlockSpec(vblk, bwd), pl.BlockSpec(rblk, bwd0),
                pl.BlockSpec(kblk, ctx), pl.BlockSpec(vblk, ctx), pl.BlockSpec(rblk, ctx0), pl.BlockSpec(rblk, ctx0),
                pl.BlockSpec((2, GLA_GATE_RANK, GLA_KW), lambda b, h, t: (0, 0, h)),
                pl.BlockSpec((2, 1, GLA_KW), lambda b, h, t: (0, 0, h))]
    out_specs = [pl.BlockSpec(vblk, fwd), pl.BlockSpec(vblk, bwd)]
    out_shape = [jax.ShapeDtypeStruct((bsz, seq, GLA_DV), BF16)] * 2
    return pl.pallas_call(
        _gla_kernel,
        grid=(bsz, GLA_HEADS // GLA_HPS, nt),
        in_specs=in_specs, out_specs=out_specs, out_shape=out_shape,
        scratch_shapes=[pltpu.VMEM((GLA_HPS, GLA_HEAD_V, GLA_HEAD_K), F32)] * 2,
        compiler_params=_params("parallel", "parallel", "arbitrary"),
        name="gla",
    )(tri, q, k, v, r_f, q, k, v, r_b, k_c, v_c, rc_f, rc_b, w_gk, b_gk)


HY_CW = 256
HY_ROW_CHUNK = 512
MLP_PAD = 128


@functools.lru_cache(maxsize=None)
def _dft_tables(seq):
    n, half = 2 * seq, seq // 2
    m = np.arange(half)[:, None]
    s = np.arange(half)[None, :]
    ang = 2.0 * np.pi * np.arange(n) / n
    cos = lambda k: np.cos(ang)[(k * s) % n]
    sin = lambda k: np.sin(ang)[(k * s) % n]
    sym = np.concatenate([cos(2 * m), sin(2 * m + 1)], axis=0)
    anti = np.concatenate([sin(2 * m), cos(2 * m + 1)], axis=0)
    return sym.astype(np.float32), anti.astype(np.float32)


@functools.lru_cache(maxsize=None)
def _filter_features(seq):
    bands = (HY_EMB - 1) // 2
    pos = np.arange(seq, dtype=np.float64)[:, None]
    t = pos / max(seq - 1, 1)
    f = np.linspace(1e-4, bands - 1, bands)[None]
    ang = (2.0 * math.pi / seq) * pos * f
    z = np.concatenate([t, np.cos(ang), -np.sin(ang)], axis=-1)
    out = np.zeros((seq, MLP_PAD), np.float32)
    out[:, :HY_EMB] = z
    deltas = np.abs(np.linspace(math.log(HY_TARGET) / HY_SLOW_DECAY, math.log(HY_TARGET) / HY_FAST_DECAY, HY_WIDTH))
    return out, deltas.astype(np.float32)[None]


def _filter_kernel(z_ref, w1_ref, b1_ref, w2_ref, b2_ref, w3_ref, b3_ref, fr_ref, w4f_ref, w4b_ref, dl_ref,
                   hs_ref, hd_ref, pn_ref, hid_ref):
    z = z_ref[...]

    @pl.when(pl.program_id(0) == 0)
    def _():
        fr = fr_ref[...]
        h = jnp.sin(fr * (_dot3(z, w1_ref[...]) + b1_ref[...]))
        h = jnp.sin(fr * (_dot3(h, w2_ref[...]) + b2_ref[...]))
        hid_ref[...] = jnp.sin(fr * (_dot3(h, w3_ref[...]) + b3_ref[...]))

    h = hid_ref[...]
    window = jnp.exp(-z[:, 0:1] * dl_ref[...])
    h_f = _dot3(h, w4f_ref[...]) * window
    h_b = _dot3(h, w4b_ref[...]) * window
    pos = lax.broadcasted_iota(jnp.int32, (z.shape[0], 1), 0)
    h_b = jnp.where(pos == 0, 0.0, h_b)
    hs = h_f + h_b
    hs_ref[...] = hs.astype(BF16)
    hd_ref[...] = (h_b - h_f).astype(BF16)
    sign = jnp.where(pos % 2 == 0, 1.0, -1.0)
    pn_ref[...] = jnp.sum(hs * sign, axis=0, keepdims=True)


def _pad2(a, rows, cols):
    return jnp.zeros((rows, cols), F32).at[:a.shape[0], :a.shape[1]].set(a.astype(F32))


def _hyena_filter(seq, w1, b1, w2, b2, w3, b3, w4, freq):
    z_np, deltas_np = _filter_features(seq)
    p = MLP_PAD
    args = [jnp.asarray(z_np), _pad2(w1, p, p), _pad2(b1[None], 1, p), _pad2(w2, p, p), _pad2(b2[None], 1, p),
            _pad2(w3, p, p), _pad2(b3[None], 1, p), _pad2(freq[None], 1, p),
            _pad2(w4[:, :HY_WIDTH], p, HY_WIDTH), _pad2(w4[:, HY_WIDTH:], p, HY_WIDTH), jnp.asarray(deltas_np)]
    full = lambda shape: pl.BlockSpec(shape, lambda j: (0, 0))
    colblk = lambda rows: pl.BlockSpec((rows, HY_CW), lambda j: (0, j))
    in_specs = [full((seq, p)), full((p, p)), full((1, p)), full((p, p)), full((1, p)), full((p, p)), full((1, p)),
                full((1, p)), colblk(p), colblk(p), colblk(1)]
    return pl.pallas_call(
        _filter_kernel,
        grid=(HY_WIDTH // HY_CW,),
        in_specs=in_specs,
        out_specs=[colblk(seq), colblk(seq), colblk(1)],
        out_shape=[jax.ShapeDtypeStruct((seq, HY_WIDTH), BF16), jax.ShapeDtypeStruct((seq, HY_WIDTH), BF16),
                   jax.ShapeDtypeStruct((1, HY_WIDTH), F32)],
        scratch_shapes=[pltpu.VMEM((seq, p), F32)],
        compiler_params=_params("arbitrary"),
        name="hy_filter",
    )(*args)


REV_BLOCK = 256


def _negate_index(x):
    n = x.shape[0]
    r = lax.broadcasted_iota(jnp.int32, (REV_BLOCK, REV_BLOCK), 0)
    c = lax.broadcasted_iota(jnp.int32, (REV_BLOCK, REV_BLOCK), 1)
    flip = jnp.where(r + c == REV_BLOCK - 1, 1.0, 0.0).astype(BF16)
    if x.dtype == BF16:
        pieces = [x]
    else:
        x0 = x.astype(BF16)
        r1 = x - x0.astype(F32)
        x1 = r1.astype(BF16)
        pieces = [x0, x1, (r1 - x1.astype(F32)).astype(BF16)]
    blocks = []
    for i in range(n // REV_BLOCK - 1, -1, -1):
        rows = slice(i * REV_BLOCK, (i + 1) * REV_BLOCK)
        acc = _dot(flip, pieces[0][rows])
        for p in pieces[1:]:
            acc = acc + _dot(flip, p[rows])
        blocks.append(acc)
    return pltpu.roll(jnp.concatenate(blocks, axis=0), 1, 0)


def _fold(x):
    half = x.shape[0] // 2
    lo = x[:half].astype(F32)
    hi = _negate_index(x[half:])
    pos = lax.broadcasted_iota(jnp.int32, (half, 1), 0)
    mid = hi[0:1]
    hi = jnp.where(pos == 0, 0.0, hi)
    return (lo + hi).astype(BF16), (lo - hi).astype(BF16), mid


def _alternating(n):
    pos = lax.broadcasted_iota(jnp.int32, (n, 1), 0)
    return jnp.where(pos % 2 == 0, 1.0, -1.0)


def _spectrum_kernel(ts_ref, ta_ref, hs_ref, hd_ref, pn_ref, bias_ref, p_ref, q_ref, kn_ref):
    half = hs_ref.shape[0] // 2
    bias = bias_ref[...]
    sgn = _alternating(half)
    s_sym, s_anti, s_mid = _fold(hs_ref[...])
    d_sym, d_anti, d_mid = _fold(hd_ref[...])
    p_ref[:half, :] = _dot(ts_ref[:half, :], s_sym) + sgn * s_mid + bias
    p_ref[half:, :] = _dot(ta_ref[half:, :], s_anti) + bias
    q_ref[:half, :] = _dot(ta_ref[:half, :], d_anti)
    q_ref[half:, :] = _dot(ts_ref[half:, :], d_sym) + sgn * d_mid
    kn_ref[...] = pn_ref[...] + bias


def _hyena_spectrum(t_sym, t_anti, hs, hd, pn, bias):
    seq = hs.shape[0]
    colblk = lambda rows: pl.BlockSpec((rows, HY_CW), lambda j: (0, j))
    return pl.pallas_call(
        _spectrum_kernel,
        grid=(HY_WIDTH // HY_CW,),
        in_specs=[_resident(t_sym.shape), _resident(t_anti.shape), colblk(seq), colblk(seq), colblk(1), colblk(1)],
        out_specs=[colblk(seq), colblk(seq), colblk(1)],
        out_shape=[jax.ShapeDtypeStruct((seq, HY_WIDTH), F32)] * 2 + [jax.ShapeDtypeStruct((1, HY_WIDTH), F32)],
        compiler_params=_params("parallel"),
        name="hy_spectrum",
    )(t_sym, t_anti, hs, hd, pn, bias.reshape(1, -1))


def _short_conv(u, w, b):
    n = u.shape[0]
    pos = lax.broadcasted_iota(jnp.int32, (n, 1), 0) % GRID_W
    up = jnp.where(pos == 0, 0.0, pltpu.roll(u, 1, 0))
    dn = jnp.where(pos == GRID_W - 1, 0.0, pltpu.roll(u, n - 1, 0))
    return up * w[0:1] + u * w[1:2] + dn * w[2:3] + b


def _hy_fwd_kernel(ts_ref, ta_ref, x1_ref, v_ref, p_ref, q_ref, kn_ref, s_ref):
    seq = x1_ref.shape[1]
    n, half = 2 * seq, seq // 2
    u_sym, u_anti, u_mid = _fold((v_ref[0].astype(F32) * x1_ref[0].astype(F32)).astype(BF16))
    sgn = _alternating(half)
    nyq = (jnp.sum(sgn * u_sym.astype(F32), axis=0, keepdims=True) + u_mid) * kn_ref[...] * (1.0 / n)
    for r0 in range(0, half, HY_ROW_CHUNK):
        r1 = r0 + HY_ROW_CHUNK
        mid = sgn[r0:r1] * u_mid
        re_e = _dot(ts_ref[r0:r1, :], u_sym) + mid
        im_o = _dot(ts_ref[half + r0:half + r1, :], u_sym) + mid
        im_e = _dot(ta_ref[r0:r1, :], u_anti)
        re_o = _dot(ta_ref[half + r0:half + r1, :], u_anti)
        pe, po = p_ref[r0:r1, :], p_ref[half + r0:half + r1, :]
        qe, qo = q_ref[r0:r1, :], q_ref[half + r0:half + r1, :]
        s_re_e = (re_e * pe + im_e * qe) * (2.0 / n)
        s_im_e = (im_e * pe - re_e * qe) * (2.0 / n)
        if r0 == 0:
            pos = lax.broadcasted_iota(jnp.int32, (HY_ROW_CHUNK, 1), 0)
            s_re_e = jnp.where(pos == 0, re_e * pe * (1.0 / n), s_re_e)
            s_im_e = jnp.where(pos == 0, nyq, s_im_e)
        s_ref[0, r0:r1, :] = s_re_e.astype(BF16)
        s_ref[0, half + r0:half + r1, :] = ((im_o * po - re_o * qo) * (2.0 / n)).astype(BF16)
        s_ref[0, seq + r0:seq + r1, :] = s_im_e.astype(BF16)
        s_ref[0, seq + half + r0:seq + half + r1, :] = ((re_o * po + im_o * qo) * (2.0 / n)).astype(BF16)


def _hy_inv_kernel(gs_ref, ga_ref, s_ref, x0_ref, o_ref):
    seq = x0_ref.shape[1]
    half = seq // 2
    x0 = x0_ref[0].astype(F32)
    sgn = _alternating(half)
    nyq = sgn * s_ref[0, seq:seq + 1, :].astype(F32)
    mirrored = []
    for r0 in range(0, half, HY_ROW_CHUNK):
        r1 = r0 + HY_ROW_CHUNK
        sym = _dot(gs_ref[r0:r1, :], s_ref[0, :seq, :]) + nyq[r0:r1]
        anti = _dot(ga_ref[r0:r1, :], s_ref[0, seq:, :])
        o_ref[0, r0:r1, :] = (x0[r0:r1] * (sym + anti)).astype(o_ref.dtype)
        mirrored.append(sym - anti)
    centre = (jnp.sum(sgn * (s_ref[0, :half, :].astype(F32) + s_ref[0, half:seq, :].astype(F32)), axis=0, keepdims=True)
              + nyq[0:1])
    pos = lax.broadcasted_iota(jnp.int32, (half, 1), 0)
    upper = _negate_index(jnp.where(pos == 0, centre, jnp.concatenate(mirrored, axis=0)).astype(BF16))
    o_ref[0, half:, :] = (x0[half:] * upper).astype(o_ref.dtype)


def _hyena_conv(hy, tabs, p_spec, q_spec, k_nyq):
    bsz, seq, _ = hy.shape
    nc = HY_WIDTH // HY_CW
    t_sym, t_anti, g_sym, g_anti = tabs
    sig = lambda part: pl.BlockSpec((1, seq, HY_CW), lambda j, b: (b, 0, part * nc + j))
    spec = lambda rows: pl.BlockSpec((rows, HY_CW), lambda j, b: (0, j))
    s = pl.pallas_call(
        _hy_fwd_kernel,
        grid=(nc, bsz),
        in_specs=[_resident(t_sym.shape), _resident(t_anti.shape), sig(1), sig(2),
                  spec(seq), spec(seq), spec(1)],
        out_specs=pl.BlockSpec((1, 2 * seq, HY_CW), lambda j, b: (b, 0, j)),
        out_shape=jax.ShapeDtypeStruct((bsz, 2 * seq, HY_WIDTH), BF16),
        compiler_params=_params("parallel", "parallel"),
        name="hy_fwd",
    )(t_sym, t_anti, hy, hy, p_spec, q_spec, k_nyq)
    return pl.pallas_call(
        _hy_inv_kernel,
        grid=(nc, bsz),
        in_specs=[_resident(g_sym.shape), _resident(g_anti.shape),
                  pl.BlockSpec((1, 2 * seq, HY_CW), lambda j, b: (b, 0, j)), sig(0)],
        out_specs=pl.BlockSpec((1, seq, HY_CW), lambda j, b: (b, 0, j)),
        out_shape=jax.ShapeDtypeStruct((bsz, seq, HY_WIDTH), BF16),
        compiler_params=_params("parallel", "parallel"),
        name="hy_inv",
    )(g_sym, g_anti, s, hy)


ROW_PARTS = D_MODEL // LANES


def _token_tile(t):
    return pl.ds(pl.multiple_of(t * ROW_PARTS, ROW_PARTS), ROW_PARTS)


def _load_token_rows(ref, n_tok):
    return jnp.concatenate([ref[pl.ds(j, n_tok, stride=ROW_PARTS), :] for j in range(ROW_PARTS)], axis=-1)


def _store_token_rows(ref, val):
    for j in range(ROW_PARTS):
        ref[pl.ds(j, val.shape[0], stride=ROW_PARTS), :] = val[:, j * LANES:(j + 1) * LANES]


POST_TILE = 512


def _post_kernel(of_ref, ob_ref, og_ref, yh_ref, ma_ref, mb_ref, x_ref, g1_ref, sc2_ref, sh2_ref,
                 gn_ref, nf_ref, wa_ref, wb_ref, wo_ref, wr_ref, br_ref,
                 x1_ref, h2_ref, idx_ref, wts_ref):
    o = of_ref[0].astype(F32) + ob_ref[0].astype(F32)
    gn = gn_ref[...]
    heads = [_rms(o[:, h * GLA_HEAD_V:(h + 1) * GLA_HEAD_V], gn) for h in range(GLA_HEADS)]
    og = og_ref[0].astype(F32)
    a_in = jnp.concatenate(heads, axis=-1) * (og * _sigmoid(og))
    branch_a = _dot(a_in.astype(BF16), wa_ref[...])
    branch_b = _dot(yh_ref[0], wb_ref[...])
    y = _sigmoid(ma_ref[0].astype(F32)) * branch_a + _sigmoid(mb_ref[0].astype(F32)) * branch_b
    x1 = x_ref[0] + g1_ref[0] * _dot(y.astype(BF16), wo_ref[...])
    x1_ref[0] = x1
    h2 = _rms(x1, nf_ref[...]) * (1.0 + sc2_ref[0]) + sh2_ref[0]
    _store_token_rows(h2_ref.at[0], h2)

    logits = _dot3(h2, wr_ref[...]) + br_ref[...]
    lane = lax.broadcasted_iota(jnp.int32, logits.shape, 1)
    slot = lax.broadcasted_iota(jnp.int32, (logits.shape[0], TOP_K), 1)
    work = logits
    tops = []
    idx = jnp.zeros((logits.shape[0], TOP_K), jnp.int32)
    for k in range(TOP_K):
        m = jnp.max(work, axis=-1, keepdims=True)
        first = jnp.min(jnp.where(work == m, lane, N_EXPERTS), axis=-1, keepdims=True)
        tops.append(m)
        idx = jnp.where(slot == k, first, idx)
        work = jnp.where(lane == first, -jnp.inf, work)
    es = [jnp.exp(m - tops[0]) for m in tops]
    inv = 1.0 / functools.reduce(lambda a, b: a + b, es)
    wts = jnp.zeros((logits.shape[0], TOP_K), F32)
    for k, e in enumerate(es):
        wts = jnp.where(slot == k, e * inv, wts)
    idx_ref[0] = idx
    wts_ref[0] = wts


def _post(o_f, o_b, og, y_h, m_a, m_b, x, mods, gla_norm, norm_ffn, w_a, w_b, w_o, w_router, b_router):
    bsz, seq, _ = x.shape
    tok = lambda n: pl.BlockSpec((1, POST_TILE, n), lambda b, i: (b, i, 0))
    mod = lambda which: pl.BlockSpec((1, 1, D_MODEL), lambda b, i: (b, 0, which))
    in_specs = [tok(D_MODEL)] * 7 + [mod(MOD_G1), mod(MOD_SC2), mod(MOD_SH2),
                                     _resident((1, GLA_HEAD_V)), _resident((1, D_MODEL)),
                                     _resident(w_a.shape), _resident(w_b.shape), _resident(w_o.shape),
                                     _resident(w_router.shape), _resident((1, N_EXPERTS))]
    return pl.pallas_call(
        _post_kernel,
        grid=(bsz, seq // POST_TILE),
        in_specs=in_specs,
        out_specs=[tok(D_MODEL), pl.BlockSpec((1, POST_TILE * ROW_PARTS, LANES), lambda b, i: (b, i, 0)),
                   tok(TOP_K), tok(TOP_K)],
        out_shape=[jax.ShapeDtypeStruct((bsz, seq, D_MODEL), F32),
                   jax.ShapeDtypeStruct((bsz, seq * ROW_PARTS, LANES), F32),
                   jax.ShapeDtypeStruct((bsz, seq, TOP_K), jnp.int32), jax.ShapeDtypeStruct((bsz, seq, TOP_K), F32)],
        compiler_params=_params("parallel", "parallel"),
        name="post",
    )(o_f, o_b, og, y_h, m_a, m_b, x, mods, mods, mods, gla_norm.reshape(1, -1), norm_ffn.reshape(1, -1),
      w_a, w_b, w_o, w_router, b_router.reshape(1, -1))


UP_GROUP = 2 * LANES
N_UP_GROUPS = 2 * D_MODEL // UP_GROUP


def _deinterleave_kernel(w_ref, o_ref):
    r = lax.broadcasted_iota(jnp.int32, (UP_GROUP, UP_GROUP), 0)
    c = lax.broadcasted_iota(jnp.int32, (UP_GROUP, UP_GROUP), 1)
    perm = jnp.where(c == (r % 2) * LANES + r // 2, 1.0, 0.0).astype(BF16)
    for g in range(w_ref.shape[2] // UP_GROUP):
        cols = slice(g * UP_GROUP, (g + 1) * UP_GROUP)
        o_ref[0, :, cols] = _dot(w_ref[0, :, cols].astype(BF16), perm).astype(BF16)


def _deinterleave_up(w_up):
    n_exp, d_in, d_up = w_up.shape
    blk = pl.BlockSpec((1, d_in, d_up // 2), lambda e, j: (e, 0, j))
    return pl.pallas_call(
        _deinterleave_kernel,
        grid=(n_exp, 2),
        in_specs=[blk], out_specs=blk,
        out_shape=jax.ShapeDtypeStruct(w_up.shape, BF16),
        compiler_params=_params("parallel", "parallel"),
        name="deinterleave",
    )(w_up)


RANK_TILE = 1024
ROUTE_TILE = 512
EXPERT_TILE = 1024
EXPERT_BLOCK = 512
DOWN_COLS_PER_DOT = 256
COMBINE_TILE = 256


def _exact_count_dot(a, m):
    a0 = a.astype(BF16)
    r1 = a - a0.astype(F32)
    a1 = r1.astype(BF16)
    a2 = (r1 - a1.astype(F32)).astype(BF16)
    return _dot(a0, m) + (_dot(a1, m) + _dot(a2, m))


def _route_kernel(idx_ref, pos_ref, cnt_ref, counts, base, offs):
    p = pl.program_id(0)
    i = pl.program_id(1)
    n_tok = idx_ref.shape[0]
    idx = idx_ref[...]
    lane = lax.broadcasted_iota(jnp.int32, (n_tok, N_EXPERTS), 1)
    hot = jnp.zeros((n_tok, N_EXPERTS), F32)
    for k in range(TOP_K):
        hot = hot + jnp.where(lane == idx[:, k:k + 1], 1.0, 0.0)
    colsum = jnp.sum(hot, axis=0, keepdims=True)

    @pl.when((p == 0) & (i == 0))
    def _():
        counts[...] = jnp.zeros_like(counts)

    @pl.when(p == 0)
    def _():
        counts[...] += colsum

    @pl.when((p == 1) & (i == 0))
    def _():
        r = lax.broadcasted_iota(jnp.int32, (N_EXPERTS, N_EXPERTS), 0)
        c = lax.broadcasted_iota(jnp.int32, (N_EXPERTS, N_EXPERTS), 1)
        before = jnp.where(r < c, 1.0, 0.0).astype(BF16)
        offs[...] = _exact_count_dot(jnp.broadcast_to(counts[...], (8, N_EXPERTS)), before)[0:1]
        base[...] = jnp.zeros_like(base)

    @pl.when(p == 1)
    def _():
        r = lax.broadcasted_iota(jnp.int32, (n_tok, n_tok), 0)
        c = lax.broadcasted_iota(jnp.int32, (n_tok, n_tok), 1)
        earlier = jnp.where(c < r, 1.0, 0.0).astype(BF16)
        dense = offs[...] + base[...] + _dot(earlier, hot.astype(BF16))
        slot = lax.broadcasted_iota(jnp.int32, (n_tok, TOP_K), 1)
        pos = jnp.zeros((n_tok, TOP_K), F32)
        for k in range(TOP_K):
            mine = jnp.sum(jnp.where(lane == idx[:, k:k + 1], dense, 0.0), axis=-1, keepdims=True)
            pos = jnp.where(slot == k, mine, pos)
        pos_ref[...] = pos.astype(jnp.int32)
        base[...] += colsum

    cnt_ref[...] = counts[...].astype(jnp.int32)


def _route(idx):
    n_tok = idx.shape[0]
    return pl.pallas_call(
        _route_kernel,
        grid=(2, n_tok // RANK_TILE),
        in_specs=[pl.BlockSpec((RANK_TILE, TOP_K), lambda p, i: (i, 0))],
        out_specs=[pl.BlockSpec((RANK_TILE, TOP_K), lambda p, i: (i * p, 0)),
                   pl.BlockSpec((1, N_EXPERTS), lambda p, i: (0, 0))],
        out_shape=[jax.ShapeDtypeStruct((n_tok, TOP_K), jnp.int32), jax.ShapeDtypeStruct((1, N_EXPERTS), jnp.int32)],
        scratch_shapes=[pltpu.VMEM((1, N_EXPERTS), F32)] * 3,
        compiler_params=_params("arbitrary", "arbitrary"),
        name="route",
    )(idx)


def _start_rows(n_tok, copy):
    def issue(t, carry):
        for k in range(TOP_K):
            copy(t, k).start(priority=k % 2)
        return carry

    lax.fori_loop(0, n_tok, issue, 0, unroll=4)


def _wait_rows(n_tok, copy):
    def drain(t, carry):
        for k in range(TOP_K):
            copy(t, k).wait()
        return carry

    lax.fori_loop(0, n_tok, drain, 0, unroll=4)


def _dispatch_kernel(pos_ref, h_ref, xs_ref, sem):
    def copy(t, k):
        return pltpu.make_async_copy(h_ref.at[_token_tile(t), :], xs_ref.at[_token_tile(pos_ref[t * TOP_K + k]), :], sem)

    _start_rows(ROUTE_TILE, copy)
    _wait_rows(ROUTE_TILE, copy)


def _index_blocks(tile, n_steps):
    blk = (tile * TOP_K,)
    return [pl.BlockSpec(blk, lambda i: (i,), memory_space=pltpu.SMEM),
            pl.BlockSpec(blk, lambda i: (jnp.minimum(i + 1, n_steps - 1),), memory_space=pltpu.SMEM)]


def _dispatch(pos_flat, h2):
    n_tok = h2.shape[0] // ROW_PARTS
    n_steps = n_tok // ROUTE_TILE
    return pl.pallas_call(
        _dispatch_kernel,
        grid=(n_steps,),
        in_specs=[_index_blocks(ROUTE_TILE, n_steps)[0],
                  pl.BlockSpec((ROUTE_TILE * ROW_PARTS, LANES), lambda i: (i, 0))],
        out_specs=pl.BlockSpec(memory_space=pl.ANY),
        out_shape=jax.ShapeDtypeStruct((n_tok * TOP_K * ROW_PARTS, LANES), F32),
        scratch_shapes=[pltpu.SemaphoreType.DMA],
        compiler_params=_params("arbitrary"),
        name="dispatch",
    )(pos_flat, h2)


def _experts_kernel(tile_ref, exp_ref, lo_ref, hi_ref, x_ref, wu_ref, bu_ref, wd_ref, bd_ref, o_ref, acc_ref):
    w = pl.program_id(0)
    lo = lo_ref[w]
    hi = hi_ref[w]

    @pl.when(w == 0)
    def _():
        acc_ref[...] = jnp.zeros_like(acc_ref)

    def block(r0):
        first = tile_ref[w] * EXPERT_TILE + r0
        rows = slice(r0, r0 + EXPERT_BLOCK)

        @pl.when((hi > lo) & (lo < first + EXPERT_BLOCK) & (hi > first))
        def _():
            x_rows = x_ref.at[pl.ds(r0 * ROW_PARTS, EXPERT_BLOCK * ROW_PARTS), :]
            h = _load_token_rows(x_rows, EXPERT_BLOCK).astype(BF16)
            acts = []
            for g in range(N_UP_GROUPS):
                cols = slice(g * UP_GROUP, (g + 1) * UP_GROUP)
                u = _dot(h, wu_ref[0, :, cols]) + bu_ref[0, :, cols]
                glu = jnp.minimum(u[:, :LANES], SWIGLU_LIMIT)
                lin = jnp.clip(u[:, LANES:], -SWIGLU_LIMIT, SWIGLU_LIMIT)
                acts.append((glu * _sigmoid(SWIGLU_ALPHA * glu) * (lin + 1.0)).astype(BF16))
            a = jnp.concatenate(acts, axis=-1)
            row = first + lax.broadcasted_iota(jnp.int32, (EXPERT_BLOCK, 1), 0)
            mine = (row >= lo) & (row < hi)
            for c0 in range(0, D_MODEL, DOWN_COLS_PER_DOT):
                cols = slice(c0, c0 + DOWN_COLS_PER_DOT)
                kept = jnp.where(mine, _dot(a, wd_ref[0, :, cols]) + bd_ref[0, :, cols], acc_ref[rows, cols])
                acc_ref[rows, cols] = kept
                for j in range(DOWN_COLS_PER_DOT // LANES):
                    part = pl.ds(r0 * ROW_PARTS + c0 // LANES + j, EXPERT_BLOCK, stride=ROW_PARTS)
                    o_ref[part, :] = kept[:, j * LANES:(j + 1) * LANES]

    for r0 in range(0, EXPERT_TILE, EXPERT_BLOCK):
        block(r0)


def _work_items(counts, n_rows):
    n_tiles = n_rows // EXPERT_TILE
    ends = jnp.cumsum(counts)
    tile_ends = jnp.arange(1, n_tiles + 1, dtype=jnp.int32) * EXPERT_TILE
    n_items = n_tiles + N_EXPERTS
    count = lambda cond: jnp.sum(cond.astype(jnp.int32), axis=1)
    slot_t = jnp.arange(n_tiles, dtype=jnp.int32) + count(ends[None, :] < tile_ends[:, None])
    slot_e = jnp.arange(N_EXPERTS, dtype=jnp.int32) + count(tile_ends[None, :] <= ends[:, None])
    slots = jnp.concatenate([slot_t, slot_e])
    vals = jnp.concatenate([tile_ends, ends])
    item = jnp.arange(n_items, dtype=jnp.int32)
    hi = jnp.sum(jnp.where(slots[None, :] == item[:, None], vals[None, :], 0), axis=1)
    lo = jnp.concatenate([jnp.zeros((1,), jnp.int32), hi[:-1]])
    tile_id = jnp.minimum(lo // EXPERT_TILE, n_tiles - 1)
    exp_id = jnp.minimum(count(ends[None, :] <= lo[:, None]), N_EXPERTS - 1)
    return tile_id, exp_id, lo, hi


def _experts(items, xs, w_up_g, b_up_g, w_down, b_down):
    n_items = items[0].shape[0]
    rows = lambda w, t, e, lo, hi: (t[w], 0)
    per_expert = lambda r, c: pl.BlockSpec((1, r, c), lambda w, t, e, lo, hi: (e[w], 0, 0))
    return pl.pallas_call(
        _experts_kernel,
        grid_spec=pltpu.PrefetchScalarGridSpec(
            num_scalar_prefetch=4,
            grid=(n_items,),
            in_specs=[pl.BlockSpec((EXPERT_TILE * ROW_PARTS, LANES), rows),
                      per_expert(D_MODEL, 2 * D_MODEL), per_expert(1, 2 * D_MODEL),
                      per_expert(D_MODEL, D_MODEL), per_expert(1, D_MODEL)],
            out_specs=pl.BlockSpec((EXPERT_TILE * ROW_PARTS, LANES), rows),
            scratch_shapes=[pltpu.VMEM((EXPERT_TILE, D_MODEL), F32)]),
        out_shape=jax.ShapeDtypeStruct(xs.shape, F32),
        compiler_params=_params("arbitrary"),
        name="experts",
    )(*items, xs, w_up_g, b_up_g, w_down, b_down)


def _combine_kernel(pos_ref, nxt_ref, ys_ref, wts_ref, x1_ref, g2_ref, nf_ref, o_ref, buf_a, buf_b, sem):
    i = pl.program_id(0)
    sub = COMBINE_TILE

    def gather(p_ref, first_tok, buf, s):
        def copy(t, k):
            src = ys_ref.at[_token_tile(p_ref[(first_tok + t) * TOP_K + k]), :]
            return pltpu.make_async_copy(src, buf.at[k, _token_tile(t), :], sem.at[s])
        return copy

    this_a = gather(pos_ref, 0, buf_a, 0)
    this_b = gather(pos_ref, sub, buf_b, 1)
    next_a = gather(nxt_ref, 0, buf_a, 0)

    def reduce_rows(buf, half, start_other):
        rows = slice(half * sub, (half + 1) * sub)
        wts = wts_ref[rows, :]
        parts = []
        for j in range(ROW_PARTS):
            for t in range(j * sub // ROW_PARTS, (j + 1) * sub // ROW_PARTS):
                for k in range(TOP_K):
                    start_other(t, k).start(priority=k % 2)
            part = wts[:, 0:1] * buf[0, pl.ds(j, sub, stride=ROW_PARTS), :]
            for k in range(1, TOP_K):
                part = part + wts[:, k:k + 1] * buf[k, pl.ds(j, sub, stride=ROW_PARTS), :]
            parts.append(part)
        acc = jnp.concatenate(parts, axis=-1)
        o_ref[rows, :] = _rms(x1_ref[rows, :] + g2_ref[0] * acc, nf_ref[...])

    @pl.when(i == 0)
    def _():
        _start_rows(sub, this_a)

    _wait_rows(sub, this_a)
    reduce_rows(buf_a, 0, this_b)
    _wait_rows(sub, this_b)
    reduce_rows(buf_b, 1, next_a)

    @pl.when(i == pl.num_programs(0) - 1)
    def _():
        _wait_rows(sub, next_a)


def _combine(pos_flat, ys, wts, x1, mods, norm_final, seq):
    n_tok = x1.shape[0]
    step_tok = 2 * COMBINE_TILE
    steps_per_sample = seq // step_tok
    tok = lambda n: pl.BlockSpec((step_tok, n), lambda i: (i, 0))
    n_steps = n_tok // step_tok
    gather_buf = pltpu.VMEM((TOP_K, COMBINE_TILE * ROW_PARTS, LANES), F32)
    return pl.pallas_call(
        _combine_kernel,
        grid=(n_steps,),
        in_specs=_index_blocks(step_tok, n_steps) + [
            pl.BlockSpec(memory_space=pl.ANY), tok(TOP_K), tok(D_MODEL),
            pl.BlockSpec((1, 1, D_MODEL), lambda i: (i // steps_per_sample, 0, MOD_G2)),
            pl.BlockSpec((1, D_MODEL), lambda i: (0, 0))],
        out_specs=tok(D_MODEL),
        out_shape=jax.ShapeDtypeStruct((n_tok, D_MODEL), F32),
        scratch_shapes=[gather_buf, gather_buf, pltpu.SemaphoreType.DMA((2,))],
        compiler_params=_params("arbitrary"),
        name="combine",
    )(pos_flat, pos_flat, ys, wts, x1, mods, norm_final.reshape(1, -1))


def _moe(h2, idx, wts, w_up_g, b_up_g, w_down, b_down, x1, mods, norm_final):
    bsz, seq, _ = x1.shape
    n_tok = bsz * seq
    pos, counts = _route(idx.reshape(n_tok, TOP_K))
    pos_flat = pos.reshape(-1)
    xs = _dispatch(pos_flat, h2.reshape(n_tok * ROW_PARTS, LANES))
    ys = _experts(_work_items(counts.reshape(-1), n_tok * TOP_K), xs, w_up_g, b_up_g, w_down, b_down)
    out = _combine(pos_flat, ys, wts.reshape(n_tok, TOP_K), x1.reshape(n_tok, D_MODEL), mods, norm_final, seq)
    return out.reshape(bsz, seq, D_MODEL)


def kernel(x, c, ctx, c_ctx, w_ada, b_ada, norm_mix, norm_ffn, w_in, w_gk_f, b_gk_f, w_gk_b, b_gk_b, gla_norm, w_gla_out, hy_conv_w, hy_conv_b, hy_f_w1, hy_f_b1, hy_f_w2, hy_f_b2, hy_f_w3, hy_f_b3, hy_f_w4, hy_sin_freq, hy_bias, w_hy_out, w_out, w_router, b_router, w_up, b_up, w_down, b_down, norm_final):
    depth = w_ada.shape[0]
    assert depth == 1, "single-layer block: the context stream only feeds later layers"
    bsz, seq, _ = x.shape

    n_rows = -(-(bsz + 1) // 8) * 8
    cc = jnp.zeros((n_rows, D_MODEL), F32).at[:bsz].set(c).at[bsz].set(c_ctx)
    mods = _ada(cc, w_ada[0], b_ada[0]).reshape(n_rows, 1, N_MOD * D_MODEL)

    sizes = (GLA_DK, GLA_DK, GLA_DV, GLA_DV, GLA_GATE_RANK, GLA_GATE_RANK, 3 * HY_WIDTH, D_MODEL, D_MODEL)
    offs = np.concatenate([[0], np.cumsum(sizes)])
    w_in_b = w_in[0].astype(BF16)
    wq, wk, wv, wog, wrf, wrb, why, wma, wmb = [w_in_b[:, offs[i]:offs[i + 1]] for i in range(len(sizes))]
    q, k, v, og, r_f, r_b, hy, m_a, m_b = _inproj(
        x, mods, lambda b: b, norm_mix[0], [wq, wk, wv, wog, wrf, wrb, why, wma, wmb],
        [BF16, BF16, BF16, BF16, F32, F32, BF16, BF16, BF16], tl=512, conv=(6, hy_conv_w[0], hy_conv_b[0]))
    k_c, v_c, rc_f, rc_b = _inproj(
        ctx, mods, lambda b: bsz, norm_mix[0], [wk, wv, wrf, wrb], [BF16, BF16, F32, F32], tl=ctx.shape[1])

    w_gk = jnp.stack([w_gk_f[0], w_gk_b[0]])
    b_gk = jnp.stack([b_gk_f[0], b_gk_b[0]])[:, None, :]
    o_f, o_b = _gla(q, k, v, r_f, r_b, k_c, v_c, rc_f, rc_b, w_gk, b_gk)

    t_sym, t_anti = [jnp.asarray(t).astype(BF16) for t in _dft_tables(seq)]
    hs, hd, pn = _hyena_filter(seq, hy_f_w1[0], hy_f_b1[0], hy_f_w2[0], hy_f_b2[0], hy_f_w3[0], hy_f_b3[0],
                               hy_f_w4[0], hy_sin_freq[0])
    p_spec, q_spec, k_nyq = _hyena_spectrum(t_sym, t_anti, hs, hd, pn, hy_bias[0])
    y_h = _hyena_conv(hy, (t_sym, t_anti, t_sym.T, t_anti.T), p_spec, q_spec, k_nyq)

    x1, h2, idx, wts = _post(o_f, o_b, og, y_h, m_a, m_b, x, mods, gla_norm[0], norm_ffn[0],
                             w_gla_out[0].astype(BF16), w_hy_out[0].astype(BF16), w_out[0].astype(BF16),
                             w_router[0], b_router[0])

    b_up_g = b_up[0].reshape(N_EXPERTS, N_UP_GROUPS, LANES, 2).transpose(0, 1, 3, 2).reshape(N_EXPERTS, 1, -1)
    return _moe(h2, idx, wts, _deinterleave_up(w_up[0]), b_up_g, w_down[0].astype(BF16), b_down[0][:, None, :],
                x1, mods, norm_final)
```

```python
import functools
import math

import jax
import jax.numpy as jnp
import numpy as np
from jax import lax
from jax.experimental import pallas as pl
from jax.experimental.pallas import tpu as pltpu

F32 = jnp.float32
BF16 = jnp.bfloat16

D_MODEL = 1024
GRID_W = 64
EPS = 1e-6
N_MOD = 6

GLA_HEADS = 4
GLA_HEAD_K = 128
GLA_HEAD_V = 256
GLA_DK = GLA_HEADS * GLA_HEAD_K
GLA_DV = GLA_HEADS * GLA_HEAD_V
GLA_GATE_RANK = 16
GLA_GATE_NORM = 16.0
GLA_CHUNK = 64

HY_WIDTH = D_MODEL
HY_EMB = 33
HY_FAST_DECAY = 0.3
HY_SLOW_DECAY = 1.5
HY_TARGET = 1e-2

N_EXPERTS = 32
TOP_K = 4
SWIGLU_LIMIT = 7.0
SWIGLU_ALPHA = 1.702

V7X_VMEM_BYTES = 64 * 1024 * 1024
VMEM_LIMIT = V7X_VMEM_BYTES - 8 * 1024 * 1024
LANES = 128

MOD_SH1, MOD_SC1, MOD_G1, MOD_SH2, MOD_SC2, MOD_G2 = range(N_MOD)


def _params(*sem):
    return pltpu.CompilerParams(dimension_semantics=sem, vmem_limit_bytes=VMEM_LIMIT)


def _resident(shape):
    return pl.BlockSpec(shape, lambda *_: (0,) * len(shape), pipeline_mode=pl.Buffered(1))


def _dot(a, b):
    return jnp.dot(a, b, preferred_element_type=F32)


def _dot_nt(a, b):
    return lax.dot_general(a, b, (((1,), (1,)), ((), ())), preferred_element_type=F32)


def _dot_tn(a, b):
    return lax.dot_general(a, b, (((0,), (0,)), ((), ())), preferred_element_type=F32)


def _split(a):
    hi = a.astype(BF16)
    lo = (a - hi.astype(F32)).astype(BF16)
    return hi, lo


def _dot3(a, b):
    ah, al = _split(a)
    bh, bl = _split(b)
    return _dot(ah, bh) + (_dot(ah, bl) + _dot(al, bh))


def _sigmoid(x):
    return 1.0 / (1.0 + jnp.exp(-x))


def _log_sigmoid(x):
    return jnp.minimum(x, 0.0) - jnp.log(1.0 + jnp.exp(-jnp.abs(x)))


def _rms(x, w):
    return x * lax.rsqrt(jnp.mean(x * x, axis=-1, keepdims=True) + EPS) * w


def _ada_kernel(c_ref, w_ref, b_ref, o_ref):
    c = c_ref[...]
    o_ref[...] = _dot3(c * _sigmoid(c), w_ref[...]) + b_ref[...]


def _ada(cc, w_ada, b_ada):
    rows = cc.shape[0]
    return pl.pallas_call(
        _ada_kernel,
        grid=(N_MOD,),
        in_specs=[pl.BlockSpec((rows, D_MODEL), lambda j: (0, 0)),
                  pl.BlockSpec((D_MODEL, D_MODEL), lambda j: (0, j)),
                  pl.BlockSpec((1, D_MODEL), lambda j: (0, j))],
        out_specs=pl.BlockSpec((rows, D_MODEL), lambda j: (0, j)),
        out_shape=jax.ShapeDtypeStruct((rows, N_MOD * D_MODEL), F32),
        compiler_params=_params("parallel"),
        name="ada",
    )(cc, w_ada, b_ada.reshape(1, -1))


INPROJ_COL_CHUNK = 512


def _inproj_kernel(n_out, conv_out, x_ref, sc_ref, sh_ref, nw_ref, *refs):
    n_in = n_out + (0 if conv_out is None else 2)
    w_refs, o_refs = refs[:n_out], refs[n_in:]
    h = _rms(x_ref[0], nw_ref[...]) * (1.0 + sc_ref[0]) + sh_ref[0]
    hb = h.astype(BF16)
    for i, (w_ref, o_ref) in enumerate(zip(w_refs, o_refs)):
        n = w_ref.shape[1]
        for c0 in range(0, n, INPROJ_COL_CHUNK):
            c1 = min(c0 + INPROJ_COL_CHUNK, n)
            y = _dot(hb, w_ref[:, c0:c1])
            if i == conv_out:
                cw_ref, cb_ref = refs[n_out:n_in]
                y = _short_conv(y, cw_ref[:, c0:c1], cb_ref[:, c0:c1])
            o_ref[0, :, c0:c1] = y.astype(o_ref.dtype)


def _inproj(x, mods, mod_row, norm_w, weights, out_dtypes, tl, conv=None):
    bsz, seq, _ = x.shape
    n_out = len(weights)
    assert tl % GRID_W == 0
    in_specs = [pl.BlockSpec((1, tl, D_MODEL), lambda b, i: (b, i, 0)),
                pl.BlockSpec((1, 1, D_MODEL), lambda b, i: (mod_row(b), 0, MOD_SC1)),
                pl.BlockSpec((1, 1, D_MODEL), lambda b, i: (mod_row(b), 0, MOD_SH1)),
                _resident((1, D_MODEL))]
    in_specs += [_resident(w.shape) for w in weights]
    extra = []
    if conv is not None:
        extra = [conv[1], conv[2].reshape(1, -1)]
        in_specs += [_resident(a.shape) for a in extra]
    out_specs = [pl.BlockSpec((1, tl, w.shape[1]), lambda b, i: (b, i, 0)) for w in weights]
    out_shape = [jax.ShapeDtypeStruct((bsz, seq, w.shape[1]), dt) for w, dt in zip(weights, out_dtypes)]
    return pl.pallas_call(
        functools.partial(_inproj_kernel, n_out, None if conv is None else conv[0]),
        grid=(bsz, seq // tl),
        in_specs=in_specs, out_specs=out_specs, out_shape=out_shape,
        compiler_params=_params("parallel", "parallel"),
        name="inproj",
    )(x, mods, mods, norm_w.reshape(1, -1), *weights, *extra)


GLA_TILE = 256
GLA_NCH = GLA_TILE // GLA_CHUNK
GLA_SCALE = GLA_HEAD_K ** -0.5
GLA_HPS = 4
GLA_KW = GLA_HPS * GLA_HEAD_K
GLA_VW = GLA_HPS * GLA_HEAD_V


@functools.lru_cache(maxsize=None)
def _chunk_triangles():
    i = np.arange(GLA_TILE)
    same = (i[:, None] // GLA_CHUNK) == (i[None, :] // GLA_CHUNK)
    return np.stack([same & (i[None, :] <= i[:, None]), same & (i[None, :] >= i[:, None])]).astype(np.float32)


def _gla_kernel(tri_ref, qf_ref, kf_ref, vf_ref, rf_ref, qb_ref, kb_ref, vb_ref, rb_ref,
                kc_ref, vc_ref, rcf_ref, rcb_ref, wgk_ref, bgk_ref,
                of_ref, ob_ref, sf_ref, sb_ref):
    t = pl.program_id(2)
    crow = lax.broadcasted_iota(jnp.int32, (GLA_CHUNK, GLA_CHUNK), 0)
    ccol = lax.broadcasted_iota(jnp.int32, (GLA_CHUNK, GLA_CHUNK), 1)
    heads = range(GLA_HPS)

    def rows(c):
        return slice(c * GLA_CHUNK, (c + 1) * GLA_CHUNK)

    def hk(h):
        return slice(h * GLA_HEAD_K, (h + 1) * GLA_HEAD_K)

    def hv(h):
        return slice(h * GLA_HEAD_V, (h + 1) * GLA_HEAD_V)

    def order(fwd):
        return range(GLA_NCH) if fwd else range(GLA_NCH - 1, -1, -1)

    def cum_decay(r, d):
        z = _dot3(r, wgk_ref[d]) + bgk_ref[d]
        g = _log_sigmoid(z) * (1.0 / GLA_GATE_NORM)
        gh, gl = _split(g)
        return _dot(tri_ref[d], gh) + _dot(tri_ref[d], gl)

    def updates(k_ref, v_ref, b, fwd):
        out = {}
        for c in range(GLA_NCH):
            i = c * GLA_CHUNK + (GLA_CHUNK - 1 if fwd else 0)
            total = b[i:i + 1, :]
            k_upd = (k_ref[0, rows(c), :].astype(F32) * jnp.exp(total - b[rows(c)])).astype(BF16)
            dec = jnp.exp(total)
            for h in heads:
                out[c, h] = (dec[:, hk(h)], _dot_tn(v_ref[0, rows(c), hv(h)], k_upd[:, hk(h)]))
        return out

    def context_state(s_ref, r_ref, d, fwd):
        upd = updates(kc_ref, vc_ref, cum_decay(r_ref[0], d), fwd)
        for h in heads:
            st = jnp.zeros((GLA_HEAD_V, GLA_HEAD_K), F32)
            for c in order(fwd):
                dec, inc = upd[c, h]
                st = st * dec + inc
            s_ref[h] = st

    def scan(s_ref, q_ref, k_ref, v_ref, r_ref, o_ref, d, fwd):
        mask = (ccol <= crow) if fwd else (ccol >= crow)
        b = cum_decay(r_ref[0], d)
        q_dec = (q_ref[0].astype(F32) * (jnp.exp(b) * GLA_SCALE)).astype(BF16)
        k_inv = (k_ref[0].astype(F32) * jnp.exp(-b)).astype(BF16)
        upd = updates(k_ref, v_ref, b, fwd)
        intra = {}
        for c in range(GLA_NCH):
            for h in heads:
                att = jnp.where(mask, _dot_nt(q_dec[rows(c), hk(h)], k_inv[rows(c), hk(h)]), 0.0).astype(BF16)
                intra[c, h] = _dot(att, v_ref[0, rows(c), hv(h)])
        enter = {}
        for h in heads:
            st = s_ref[h]
            for c in order(fwd):
                enter[c, h] = st.astype(BF16)
                dec, inc = upd[c, h]
                st = st * dec + inc
            s_ref[h] = st
        for c in range(GLA_NCH):
            for h in heads:
                o = intra[c, h] + _dot_nt(q_dec[rows(c), hk(h)], enter[c, h])
                o_ref[0, rows(c), hv(h)] = o.astype(o_ref.dtype)

    @pl.when(t == 0)
    def _():
        context_state(sf_ref, rcf_ref, 0, True)
        context_state(sb_ref, rcb_ref, 1, False)

    scan(sf_ref, qf_ref, kf_ref, vf_ref, rf_ref, of_ref, 0, True)
    scan(sb_ref, qb_ref, kb_ref, vb_ref, rb_ref, ob_ref, 1, False)


def _gla(q, k, v, r_f, r_b, k_c, v_c, rc_f, rc_b, w_gk, b_gk):
    bsz, seq, _ = q.shape
    assert k_c.shape[1] == GLA_TILE
    nt = seq // GLA_TILE
    tri = jnp.asarray(_chunk_triangles()).astype(BF16)
    fwd = lambda b, h, t: (b, t, h)
    bwd = lambda b, h, t: (b, nt - 1 - t, h)
    fwd0 = lambda b, h, t: (b, t, 0)
    bwd0 = lambda b, h, t: (b, nt - 1 - t, 0)
    ctx = lambda b, h, t: (b, 0, h)
    ctx0 = lambda b, h, t: (b, 0, 0)
    kblk = (1, GLA_TILE, GLA_KW)
    vblk = (1, GLA_TILE, GLA_VW)
    rblk = (1, GLA_TILE, GLA_GATE_RANK)
    in_specs = [_resident(tri.shape),
                pl.BlockSpec(kblk, fwd), pl.BlockSpec(kblk, fwd), pl.BlockSpec(vblk, fwd), pl.BlockSpec(rblk, fwd0),
                pl.BlockSpec(kblk, bwd), pl.BlockSpec(kblk, bwd), pl.BlockSpec(vblk, bwd), pl.BlockSpec(rblk, bwd0),
                pl.BlockSpec(kblk, ctx), pl.BlockSpec(vblk, ctx), pl.BlockSpec(rblk, ctx0), pl.BlockSpec(rblk, ctx0),
                pl.BlockSpec((2, GLA_GATE_RANK, GLA_KW), lambda b, h, t: (0, 0, h)),
                pl.BlockSpec((2, 1, GLA_KW), lambda b, h, t: (0, 0, h))]
    out_specs = [pl.BlockSpec(vblk, fwd), pl.BlockSpec(vblk, bwd)]
    out_shape = [jax.ShapeDtypeStruct((bsz, seq, GLA_DV), BF16)] * 2
    return pl.pallas_call(
        _gla_kernel,
        grid=(bsz, GLA_HEADS // GLA_HPS, nt),
        in_specs=in_specs, out_specs=out_specs, out_shape=out_shape,
        scratch_shapes=[pltpu.VMEM((GLA_HPS, GLA_HEAD_V, GLA_HEAD_K), F32)] * 2,
        compiler_params=_params("parallel", "parallel", "arbitrary"),
        name="gla",
    )(tri, q, k, v, r_f, q, k, v, r_b, k_c, v_c, rc_f, rc_b, w_gk, b_gk)


HY_CW = 256
HY_ROW_CHUNK = 512
MLP_PAD = 128


@functools.lru_cache(maxsize=None)
def _dft_tables(seq):
    n, half = 2 * seq, seq // 2
    m = np.arange(half)[:, None]
    s = np.arange(half)[None, :]
    ang = 2.0 * np.pi * np.arange(n) / n
    cos = lambda k: np.cos(ang)[(k * s) % n]
    sin = lambda k: np.sin(ang)[(k * s) % n]
    sym = np.concatenate([cos(2 * m), sin(2 * m + 1)], axis=0)
    anti = np.concatenate([sin(2 * m), cos(2 * m + 1)], axis=0)
    return sym.astype(np.float32), anti.astype(np.float32)


@functools.lru_cache(maxsize=None)
def _filter_features(seq):
    bands = (HY_EMB - 1) // 2
    pos = np.arange(seq, dtype=np.float64)[:, None]
    t = pos / max(seq - 1, 1)
    f = np.linspace(1e-4, bands - 1, bands)[None]
    ang = (2.0 * math.pi / seq) * pos * f
    z = np.concatenate([t, np.cos(ang), -np.sin(ang)], axis=-1)
    out = np.zeros((seq, MLP_PAD), np.float32)
    out[:, :HY_EMB] = z
    deltas = np.abs(np.linspace(math.log(HY_TARGET) / HY_SLOW_DECAY, math.log(HY_TARGET) / HY_FAST_DECAY, HY_WIDTH))
    return out, deltas.astype(np.float32)[None]


def _filter_kernel(z_ref, w1_ref, b1_ref, w2_ref, b2_ref, w3_ref, b3_ref, fr_ref, w4f_ref, w4b_ref, dl_ref,
                   hs_ref, hd_ref, pn_ref, hid_ref):
    z = z_ref[...]

    @pl.when(pl.program_id(0) == 0)
    def _():
        fr = fr_ref[...]
        h = jnp.sin(fr * (_dot3(z, w1_ref[...]) + b1_ref[...]))
        h = jnp.sin(fr * (_dot3(h, w2_ref[...]) + b2_ref[...]))
        hid_ref[...] = jnp.sin(fr * (_dot3(h, w3_ref[...]) + b3_ref[...]))

    h = hid_ref[...]
    window = jnp.exp(-z[:, 0:1] * dl_ref[...])
    h_f = _dot3(h, w4f_ref[...]) * window
    h_b = _dot3(h, w4b_ref[...]) * window
    pos = lax.broadcasted_iota(jnp.int32, (z.shape[0], 1), 0)
    h_b = jnp.where(pos == 0, 0.0, h_b)
    hs = h_f + h_b
    hs_ref[...] = hs.astype(BF16)
    hd_ref[...] = (h_b - h_f).astype(BF16)
    sign = jnp.where(pos % 2 == 0, 1.0, -1.0)
    pn_ref[...] = jnp.sum(hs * sign, axis=0, keepdims=True)


def _pad2(a, rows, cols):
    return jnp.zeros((rows, cols), F32).at[:a.shape[0], :a.shape[1]].set(a.astype(F32))


def _hyena_filter(seq, w1, b1, w2, b2, w3, b3, w4, freq):
    z_np, deltas_np = _filter_features(seq)
    p = MLP_PAD
    args = [jnp.asarray(z_np), _pad2(w1, p, p), _pad2(b1[None], 1, p), _pad2(w2, p, p), _pad2(b2[None], 1, p),
            _pad2(w3, p, p), _pad2(b3[None], 1, p), _pad2(freq[None], 1, p),
            _pad2(w4[:, :HY_WIDTH], p, HY_WIDTH), _pad2(w4[:, HY_WIDTH:], p, HY_WIDTH), jnp.asarray(deltas_np)]
    full = lambda shape: pl.BlockSpec(shape, lambda j: (0, 0))
    colblk = lambda rows: pl.BlockSpec((rows, HY_CW), lambda j: (0, j))
    in_specs = [full((seq, p)), full((p, p)), full((1, p)), full((p, p)), full((1, p)), full((p, p)), full((1, p)),
                full((1, p)), colblk(p), colblk(p), colblk(1)]
    return pl.pallas_call(
        _filter_kernel,
        grid=(HY_WIDTH // HY_CW,),
        in_specs=in_specs,
        out_specs=[colblk(seq), colblk(seq), colblk(1)],
        out_shape=[jax.ShapeDtypeStruct((seq, HY_WIDTH), BF16), jax.ShapeDtypeStruct((seq, HY_WIDTH), BF16),
                   jax.ShapeDtypeStruct((1, HY_WIDTH), F32)],
        scratch_shapes=[pltpu.VMEM((seq, p), F32)],
        compiler_params=_params("arbitrary"),
        name="hy_filter",
    )(*args)


REV_BLOCK = 256


def _negate_index(x):
    n = x.shape[0]
    r = lax.broadcasted_iota(jnp.int32, (REV_BLOCK, REV_BLOCK), 0)
    c = lax.broadcasted_iota(jnp.int32, (REV_BLOCK, REV_BLOCK), 1)
    flip = jnp.where(r + c == REV_BLOCK - 1, 1.0, 0.0).astype(BF16)
    if x.dtype == BF16:
        pieces = [x]
    else:
        x0 = x.astype(BF16)
        r1 = x - x0.astype(F32)
        x1 = r1.astype(BF16)
        pieces = [x0, x1, (r1 - x1.astype(F32)).astype(BF16)]
    blocks = []
    for i in range(n // REV_BLOCK - 1, -1, -1):
        rows = slice(i * REV_BLOCK, (i + 1) * REV_BLOCK)
        acc = _dot(flip, pieces[0][rows])
        for p in pieces[1:]:
            acc = acc + _dot(flip, p[rows])
        blocks.append(acc)
    return pltpu.roll(jnp.concatenate(blocks, axis=0), 1, 0)


def _fold(x):
    half = x.shape[0] // 2
    lo = x[:half].astype(F32)
    hi = _negate_index(x[half:])
    pos = lax.broadcasted_iota(jnp.int32, (half, 1), 0)
    mid = hi[0:1]
    hi = jnp.where(pos == 0, 0.0, hi)
    return (lo + hi).astype(BF16), (lo - hi).astype(BF16), mid


def _alternating(n):
    pos = lax.broadcasted_iota(jnp.int32, (n, 1), 0)
    return jnp.where(pos % 2 == 0, 1.0, -1.0)


def _spectrum_kernel(ts_ref, ta_ref, hs_ref, hd_ref, pn_ref, bias_ref, p_ref, q_ref, kn_ref):
    half = hs_ref.shape[0] // 2
    bias = bias_ref[...]
    sgn = _alternating(half)
    s_sym, s_anti, s_mid = _fold(hs_ref[...])
    d_sym, d_anti, d_mid = _fold(hd_ref[...])
    p_ref[:half, :] = _dot(ts_ref[:half, :], s_sym) + sgn * s_mid + bias
    p_ref[half:, :] = _dot(ta_ref[half:, :], s_anti) + bias
    q_ref[:half, :] = _dot(ta_ref[:half, :], d_anti)
    q_ref[half:, :] = _dot(ts_ref[half:, :], d_sym) + sgn * d_mid
    kn_ref[...] = pn_ref[...] + bias


def _hyena_spectrum(t_sym, t_anti, hs, hd, pn, bias):
    seq = hs.shape[0]
    colblk = lambda rows: pl.BlockSpec((rows, HY_CW), lambda j: (0, j))
    return pl.pallas_call(
        _spectrum_kernel,
        grid=(HY_WIDTH // HY_CW,),
        in_specs=[_resident(t_sym.shape), _resident(t_anti.shape), colblk(seq), colblk(seq), colblk(1), colblk(1)],
        out_specs=[colblk(seq), colblk(seq), colblk(1)],
        out_shape=[jax.ShapeDtypeStruct((seq, HY_WIDTH), F32)] * 2 + [jax.ShapeDtypeStruct((1, HY_WIDTH), F32)],
        compiler_params=_params("parallel"),
        name="hy_spectrum",
    )(t_sym, t_anti, hs, hd, pn, bias.reshape(1, -1))


def _short_conv(u, w, b):
    n = u.shape[0]
    pos = lax.broadcasted_iota(jnp.int32, (n, 1), 0) % GRID_W
    up = jnp.where(pos == 0, 0.0, pltpu.roll(u, 1, 0))
    dn = jnp.where(pos == GRID_W - 1, 0.0, pltpu.roll(u, n - 1, 0))
    return up * w[0:1] + u * w[1:2] + dn * w[2:3] + b


def _hy_fwd_kernel(ts_ref, ta_ref, x1_ref, v_ref, p_ref, q_ref, kn_ref, s_ref):
    seq = x1_ref.shape[1]
    n, half = 2 * seq, seq // 2
    u_sym, u_anti, u_mid = _fold((v_ref[0].astype(F32) * x1_ref[0].astype(F32)).astype(BF16))
    sgn = _alternating(half)
    nyq = (jnp.sum(sgn * u_sym.astype(F32), axis=0, keepdims=True) + u_mid) * kn_ref[...] * (1.0 / n)
    for r0 in range(0, half, HY_ROW_CHUNK):
        r1 = r0 + HY_ROW_CHUNK
        mid = sgn[r0:r1] * u_mid
        re_e = _dot(ts_ref[r0:r1, :], u_sym) + mid
        im_o = _dot(ts_ref[half + r0:half + r1, :], u_sym) + mid
        im_e = _dot(ta_ref[r0:r1, :], u_anti)
        re_o = _dot(ta_ref[half + r0:half + r1, :], u_anti)
        pe, po = p_ref[r0:r1, :], p_ref[half + r0:half + r1, :]
        qe, qo = q_ref[r0:r1, :], q_ref[half + r0:half + r1, :]
        s_re_e = (re_e * pe + im_e * qe) * (2.0 / n)
        s_im_e = (im_e * pe - re_e * qe) * (2.0 / n)
        if r0 == 0:
            pos = lax.broadcasted_iota(jnp.int32, (HY_ROW_CHUNK, 1), 0)
            s_re_e = jnp.where(pos == 0, re_e * pe * (1.0 / n), s_re_e)
            s_im_e = jnp.where(pos == 0, nyq, s_im_e)
        s_ref[0, r0:r1, :] = s_re_e.astype(BF16)
        s_ref[0, half + r0:half + r1, :] = ((im_o * po - re_o * qo) * (2.0 / n)).astype(BF16)
        s_ref[0, seq + r0:seq + r1, :] = s_im_e.astype(BF16)
        s_ref[0, seq + half + r0:seq + half + r1, :] = ((re_o * po + im_o * qo) * (2.0 / n)).astype(BF16)


def _hy_inv_kernel(gs_ref, ga_ref, s_ref, x0_ref, o_ref):
    seq = x0_ref.shape[1]
    half = seq // 2
    x0 = x0_ref[0].astype(F32)
    sgn = _alternating(half)
    nyq = sgn * s_ref[0, seq:seq + 1, :].astype(F32)
    mirrored = []
    for r0 in range(0, half, HY_ROW_CHUNK):
        r1 = r0 + HY_ROW_CHUNK
        sym = _dot(gs_ref[r0:r1, :], s_ref[0, :seq, :]) + nyq[r0:r1]
        anti = _dot(ga_ref[r0:r1, :], s_ref[0, seq:, :])
        o_ref[0, r0:r1, :] = (x0[r0:r1] * (sym + anti)).astype(o_ref.dtype)
        mirrored.append(sym - anti)
    centre = (jnp.sum(sgn * (s_ref[0, :half, :].astype(F32) + s_ref[0, half:seq, :].astype(F32)), axis=0, keepdims=True)
              + nyq[0:1])
    pos = lax.broadcasted_iota(jnp.int32, (half, 1), 0)
    upper = _negate_index(jnp.where(pos == 0, centre, jnp.concatenate(mirrored, axis=0)).astype(BF16))
    o_ref[0, half:, :] = (x0[half:] * upper).astype(o_ref.dtype)


def _hyena_conv(hy, tabs, p_spec, q_spec, k_nyq):
    bsz, seq, _ = hy.shape
    nc = HY_WIDTH // HY_CW
    t_sym, t_anti, g_sym, g_anti = tabs
    sig = lambda part: pl.BlockSpec((1, seq, HY_CW), lambda j, b: (b, 0, part * nc + j))
    spec = lambda rows: pl.BlockSpec((rows, HY_CW), lambda j, b: (0, j))
    s = pl.pallas_call(
        _hy_fwd_kernel,
        grid=(nc, bsz),
        in_specs=[_resident(t_sym.shape), _resident(t_anti.shape), sig(1), sig(2),
                  spec(seq), spec(seq), spec(1)],
        out_specs=pl.BlockSpec((1, 2 * seq, HY_CW), lambda j, b: (b, 0, j)),
        out_shape=jax.ShapeDtypeStruct((bsz, 2 * seq, HY_WIDTH), BF16),
        compiler_params=_params("parallel", "parallel"),
        name="hy_fwd",
    )(t_sym, t_anti, hy, hy, p_spec, q_spec, k_nyq)
    return pl.pallas_call(
        _hy_inv_kernel,
        grid=(nc, bsz),
        in_specs=[_resident(g_sym.shape), _resident(g_anti.shape),
                  pl.BlockSpec((1, 2 * seq, HY_CW), lambda j, b: (b, 0, j)), sig(0)],
        out_specs=pl.BlockSpec((1, seq, HY_CW), lambda j, b: (b, 0, j)),
        out_shape=jax.ShapeDtypeStruct((bsz, seq, HY_WIDTH), BF16),
        compiler_params=_params("parallel", "parallel"),
        name="hy_inv",
    )(g_sym, g_anti, s, hy)


ROW_PARTS = D_MODEL // LANES


def _token_tile(t):
    return pl.ds(pl.multiple_of(t * ROW_PARTS, ROW_PARTS), ROW_PARTS)


def _load_token_rows(ref, n_tok):
    return jnp.concatenate([ref[pl.ds(j, n_tok, stride=ROW_PARTS), :] for j in range(ROW_PARTS)], axis=-1)


def _store_token_rows(ref, val):
    for j in range(ROW_PARTS):
        ref[pl.ds(j, val.shape[0], stride=ROW_PARTS), :] = val[:, j * LANES:(j + 1) * LANES]


POST_TILE = 512


def _post_kernel(of_ref, ob_ref, og_ref, yh_ref, ma_ref, mb_ref, x_ref, g1_ref, sc2_ref, sh2_ref,
                 gn_ref, nf_ref, wa_ref, wb_ref, wo_ref, wr_ref, br_ref,
                 x1_ref, h2_ref, idx_ref, wts_ref):
    o = of_ref[0].astype(F32) + ob_ref[0].astype(F32)
    gn = gn_ref[...]
    heads = [_rms(o[:, h * GLA_HEAD_V:(h + 1) * GLA_HEAD_V], gn) for h in range(GLA_HEADS)]
    og = og_ref[0].astype(F32)
    a_in = jnp.concatenate(heads, axis=-1) * (og * _sigmoid(og))
    branch_a = _dot(a_in.astype(BF16), wa_ref[...])
    branch_b = _dot(yh_ref[0], wb_ref[...])
    y = _sigmoid(ma_ref[0].astype(F32)) * branch_a + _sigmoid(mb_ref[0].astype(F32)) * branch_b
    x1 = x_ref[0] + g1_ref[0] * _dot(y.astype(BF16), wo_ref[...])
    x1_ref[0] = x1
    h2 = _rms(x1, nf_ref[...]) * (1.0 + sc2_ref[0]) + sh2_ref[0]
    _store_token_rows(h2_ref.at[0], h2)

    logits = _dot3(h2, wr_ref[...]) + br_ref[...]
    lane = lax.broadcasted_iota(jnp.int32, logits.shape, 1)
    slot = lax.broadcasted_iota(jnp.int32, (logits.shape[0], TOP_K), 1)
    work = logits
    tops = []
    idx = jnp.zeros((logits.shape[0], TOP_K), jnp.int32)
    for k in range(TOP_K):
        m = jnp.max(work, axis=-1, keepdims=True)
        first = jnp.min(jnp.where(work == m, lane, N_EXPERTS), axis=-1, keepdims=True)
        tops.append(m)
        idx = jnp.where(slot == k, first, idx)
        work = jnp.where(lane == first, -jnp.inf, work)
    es = [jnp.exp(m - tops[0]) for m in tops]
    inv = 1.0 / functools.reduce(lambda a, b: a + b, es)
    wts = jnp.zeros((logits.shape[0], TOP_K), F32)
    for k, e in enumerate(es):
        wts = jnp.where(slot == k, e * inv, wts)
    idx_ref[0] = idx
    wts_ref[0] = wts


def _post(o_f, o_b, og, y_h, m_a, m_b, x, mods, gla_norm, norm_ffn, w_a, w_b, w_o, w_router, b_router):
    bsz, seq, _ = x.shape
    tok = lambda n: pl.BlockSpec((1, POST_TILE, n), lambda b, i: (b, i, 0))
    mod = lambda which: pl.BlockSpec((1, 1, D_MODEL), lambda b, i: (b, 0, which))
    in_specs = [tok(D_MODEL)] * 7 + [mod(MOD_G1), mod(MOD_SC2), mod(MOD_SH2),
                                     _resident((1, GLA_HEAD_V)), _resident((1, D_MODEL)),
                                     _resident(w_a.shape), _resident(w_b.shape), _resident(w_o.shape),
                                     _resident(w_router.shape), _resident((1, N_EXPERTS))]
    return pl.pallas_call(
        _post_kernel,
        grid=(bsz, seq // POST_TILE),
        in_specs=in_specs,
        out_specs=[tok(D_MODEL), pl.BlockSpec((1, POST_TILE * ROW_PARTS, LANES), lambda b, i: (b, i, 0)),
                   tok(TOP_K), tok(TOP_K)],
        out_shape=[jax.ShapeDtypeStruct((bsz, seq, D_MODEL), F32),
                   jax.ShapeDtypeStruct((bsz, seq * ROW_PARTS, LANES), F32),
                   jax.ShapeDtypeStruct((bsz, seq, TOP_K), jnp.int32), jax.ShapeDtypeStruct((bsz, seq, TOP_K), F32)],
        compiler_params=_params("parallel", "parallel"),
        name="post",
    )(o_f, o_b, og, y_h, m_a, m_b, x, mods, mods, mods, gla_norm.reshape(1, -1), norm_ffn.reshape(1, -1),
      w_a, w_b, w_o, w_router, b_router.reshape(1, -1))


UP_GROUP = 2 * LANES
N_UP_GROUPS = 2 * D_MODEL // UP_GROUP


def _stage_expert_weights(wu_ref, wd_ref, wu_s, wd_s):
    r = lax.broadcasted_iota(jnp.int32, (UP_GROUP, UP_GROUP), 0)
    c = lax.broadcasted_iota(jnp.int32, (UP_GROUP, UP_GROUP), 1)
    perm = jnp.where(c == (r % 2) * LANES + r // 2, 1.0, 0.0).astype(BF16)
    for g in range(N_UP_GROUPS):
        cols = slice(g * UP_GROUP, (g + 1) * UP_GROUP)
        wu_s[:, cols] = _dot(wu_ref[0, :, cols].astype(BF16), perm).astype(BF16)
    wd_s[...] = wd_ref[0].astype(BF16)


RANK_TILE = 1024
ROUTE_TILE = 512
EXPERT_TILE = 512
EXPERT_BLOCK = 512
DOWN_COLS_PER_DOT = 256
COMBINE_TILE = 256


def _exact_count_dot(a, m):
    a0 = a.astype(BF16)
    r1 = a - a0.astype(F32)
    a1 = r1.astype(BF16)
    a2 = (r1 - a1.astype(F32)).astype(BF16)
    return _dot(a0, m) + (_dot(a1, m) + _dot(a2, m))


def _route_kernel(idx_ref, pos_ref, cnt_ref, counts, base, offs):
    p = pl.program_id(0)
    i = pl.program_id(1)
    n_tok = idx_ref.shape[0]
    idx = idx_ref[...]
    lane = lax.broadcasted_iota(jnp.int32, (n_tok, N_EXPERTS), 1)
    hot = jnp.zeros((n_tok, N_EXPERTS), F32)
    for k in range(TOP_K):
        hot = hot + jnp.where(lane == idx[:, k:k + 1], 1.0, 0.0)
    colsum = jnp.sum(hot, axis=0, keepdims=True)

    @pl.when((p == 0) & (i == 0))
    def _():
        counts[...] = jnp.zeros_like(counts)

    @pl.when(p == 0)
    def _():
        counts[...] += colsum

    @pl.when((p == 1) & (i == 0))
    def _():
        r = lax.broadcasted_iota(jnp.int32, (N_EXPERTS, N_EXPERTS), 0)
        c = lax.broadcasted_iota(jnp.int32, (N_EXPERTS, N_EXPERTS), 1)
        before = jnp.where(r < c, 1.0, 0.0).astype(BF16)
        offs[...] = _exact_count_dot(jnp.broadcast_to(counts[...], (8, N_EXPERTS)), before)[0:1]
        base[...] = jnp.zeros_like(base)

    @pl.when(p == 1)
    def _():
        r = lax.broadcasted_iota(jnp.int32, (n_tok, n_tok), 0)
        c = lax.broadcasted_iota(jnp.int32, (n_tok, n_tok), 1)
        earlier = jnp.where(c < r, 1.0, 0.0).astype(BF16)
        dense = offs[...] + base[...] + _dot(earlier, hot.astype(BF16))
        slot = lax.broadcasted_iota(jnp.int32, (n_tok, TOP_K), 1)
        pos = jnp.zeros((n_tok, TOP_K), F32)
        for k in range(TOP_K):
            mine = jnp.sum(jnp.where(lane == idx[:, k:k + 1], dense, 0.0), axis=-1, keepdims=True)
            pos = jnp.where(slot == k, mine, pos)
        pos_ref[...] = pos.astype(jnp.int32)
        base[...] += colsum

    cnt_ref[...] = counts[...].astype(jnp.int32)


def _route(idx):
    n_tok = idx.shape[0]
    return pl.pallas_call(
        _route_kernel,
        grid=(2, n_tok // RANK_TILE),
        in_specs=[pl.BlockSpec((RANK_TILE, TOP_K), lambda p, i: (i, 0))],
        out_specs=[pl.BlockSpec((RANK_TILE, TOP_K), lambda p, i: (i * p, 0)),
                   pl.BlockSpec((1, N_EXPERTS), lambda p, i: (0, 0))],
        out_shape=[jax.ShapeDtypeStruct((n_tok, TOP_K), jnp.int32), jax.ShapeDtypeStruct((1, N_EXPERTS), jnp.int32)],
        scratch_shapes=[pltpu.VMEM((1, N_EXPERTS), F32)] * 3,
        compiler_params=_params("arbitrary", "arbitrary"),
        name="route",
    )(idx)


def _start_rows(n_tok, copy):
    def issue(t, carry):
        for k in range(TOP_K):
            copy(t, k).start(priority=k % 2)
        return carry

    lax.fori_loop(0, n_tok, issue, 0, unroll=4)


def _wait_rows(n_tok, copy):
    def drain(t, carry):
        for k in range(TOP_K):
            copy(t, k).wait()
        return carry

    lax.fori_loop(0, n_tok, drain, 0, unroll=4)


def _dispatch_kernel(pos_ref, h_ref, xs_ref, sem):
    def copy(t, k):
        return pltpu.make_async_copy(h_ref.at[_token_tile(t), :], xs_ref.at[_token_tile(pos_ref[t * TOP_K + k]), :], sem)

    _start_rows(ROUTE_TILE, copy)
    _wait_rows(ROUTE_TILE, copy)


def _index_blocks(tile, n_steps):
    blk = (tile * TOP_K,)
    return [pl.BlockSpec(blk, lambda i: (i,), memory_space=pltpu.SMEM),
            pl.BlockSpec(blk, lambda i: (jnp.minimum(i + 1, n_steps - 1),), memory_space=pltpu.SMEM)]


def _dispatch(pos_flat, h2):
    n_tok = h2.shape[0] // ROW_PARTS
    n_steps = n_tok // ROUTE_TILE
    return pl.pallas_call(
        _dispatch_kernel,
        grid=(n_steps,),
        in_specs=[_index_blocks(ROUTE_TILE, n_steps)[0],
                  pl.BlockSpec((ROUTE_TILE * ROW_PARTS, LANES), lambda i: (i, 0))],
        out_specs=pl.BlockSpec(memory_space=pl.ANY),
        out_shape=jax.ShapeDtypeStruct((n_tok * TOP_K * ROW_PARTS, LANES), F32),
        scratch_shapes=[pltpu.SemaphoreType.DMA],
        compiler_params=_params("arbitrary"),
        name="dispatch",
    )(pos_flat, h2)


def _experts_kernel(tile_ref, exp_ref, lo_ref, hi_ref, x_ref, wu_ref, bu_ref, wd_ref, bd_ref, o_ref,
                    acc_ref, wu_s, wd_s):
    w = pl.program_id(0)
    lo = lo_ref[w]
    hi = hi_ref[w]

    @pl.when(w == 0)
    def _():
        acc_ref[...] = jnp.zeros_like(acc_ref)

    @pl.when((w == 0) | (exp_ref[w] != exp_ref[jnp.maximum(w - 1, 0)]))
    def _():
        _stage_expert_weights(wu_ref, wd_ref, wu_s, wd_s)

    def block(r0):
        first = tile_ref[w] * EXPERT_TILE + r0
        rows = slice(r0, r0 + EXPERT_BLOCK)

        @pl.when((hi > lo) & (lo < first + EXPERT_BLOCK) & (hi > first))
        def _():
            x_rows = x_ref.at[pl.ds(r0 * ROW_PARTS, EXPERT_BLOCK * ROW_PARTS), :]
            h = _load_token_rows(x_rows, EXPERT_BLOCK).astype(BF16)
            acts = []
            for g in range(N_UP_GROUPS):
                cols = slice(g * UP_GROUP, (g + 1) * UP_GROUP)
                u = _dot(h, wu_s[:, cols]) + bu_ref[0, :, cols]
                glu = jnp.minimum(u[:, :LANES], SWIGLU_LIMIT)
                lin = jnp.clip(u[:, LANES:], -SWIGLU_LIMIT, SWIGLU_LIMIT)
                acts.append((glu * _sigmoid(SWIGLU_ALPHA * glu) * (lin + 1.0)).astype(BF16))
            a = jnp.concatenate(acts, axis=-1)
            row = first + lax.broadcasted_iota(jnp.int32, (EXPERT_BLOCK, 1), 0)
            mine = (row >= lo) & (row < hi)
            for c0 in range(0, D_MODEL, DOWN_COLS_PER_DOT):
                cols = slice(c0, c0 + DOWN_COLS_PER_DOT)
                kept = jnp.where(mine, _dot(a, wd_s[:, cols]) + bd_ref[0, :, cols], acc_ref[rows, cols])
                acc_ref[rows, cols] = kept
                for j in range(DOWN_COLS_PER_DOT // LANES):
                    part = pl.ds(r0 * ROW_PARTS + c0 // LANES + j, EXPERT_BLOCK, stride=ROW_PARTS)
                    o_ref[part, :] = kept[:, j * LANES:(j + 1) * LANES]

    for r0 in range(0, EXPERT_TILE, EXPERT_BLOCK):
        block(r0)


def _work_items(counts, n_rows):
    n_tiles = n_rows // EXPERT_TILE
    ends = jnp.cumsum(counts)
    tile_ends = jnp.arange(1, n_tiles + 1, dtype=jnp.int32) * EXPERT_TILE
    n_items = n_tiles + N_EXPERTS
    count = lambda cond: jnp.sum(cond.astype(jnp.int32), axis=1)
    slot_t = jnp.arange(n_tiles, dtype=jnp.int32) + count(ends[None, :] < tile_ends[:, None])
    slot_e = jnp.arange(N_EXPERTS, dtype=jnp.int32) + count(tile_ends[None, :] <= ends[:, None])
    slots = jnp.concatenate([slot_t, slot_e])
    vals = jnp.concatenate([tile_ends, ends])
    item = jnp.arange(n_items, dtype=jnp.int32)
    hi = jnp.sum(jnp.where(slots[None, :] == item[:, None], vals[None, :], 0), axis=1)
    lo = jnp.concatenate([jnp.zeros((1,), jnp.int32), hi[:-1]])
    tile_id = jnp.minimum(lo // EXPERT_TILE, n_tiles - 1)
    exp_id = jnp.minimum(count(ends[None, :] <= lo[:, None]), N_EXPERTS - 1)
    return tile_id, exp_id, lo, hi


def _experts(items, xs, w_up, b_up_g, w_down, b_down):
    n_items = items[0].shape[0]
    rows = lambda w, t, e, lo, hi: (t[w], 0)
    per_expert = lambda r, c: pl.BlockSpec((1, r, c), lambda w, t, e, lo, hi: (e[w], 0, 0))
    return pl.pallas_call(
        _experts_kernel,
        grid_spec=pltpu.PrefetchScalarGridSpec(
            num_scalar_prefetch=4,
            grid=(n_items,),
            in_specs=[pl.BlockSpec((EXPERT_TILE * ROW_PARTS, LANES), rows),
                      per_expert(D_MODEL, 2 * D_MODEL), per_expert(1, 2 * D_MODEL),
                      per_expert(D_MODEL, D_MODEL), per_expert(1, D_MODEL)],
            out_specs=pl.BlockSpec((EXPERT_TILE * ROW_PARTS, LANES), rows),
            scratch_shapes=[pltpu.VMEM((EXPERT_TILE, D_MODEL), F32), pltpu.VMEM((D_MODEL, 2 * D_MODEL), BF16),
                            pltpu.VMEM((D_MODEL, D_MODEL), BF16)]),
        out_shape=jax.ShapeDtypeStruct(xs.shape, F32),
        compiler_params=_params("arbitrary"),
        name="experts",
    )(*items, xs, w_up, b_up_g, w_down, b_down)


def _combine_kernel(pos_ref, nxt_ref, ys_ref, wts_ref, x1_ref, g2_ref, nf_ref, o_ref, buf_a, buf_b, sem):
    i = pl.program_id(0)
    sub = COMBINE_TILE

    def gather(p_ref, first_tok, buf, s):
        def copy(t, k):
            src = ys_ref.at[_token_tile(p_ref[(first_tok + t) * TOP_K + k]), :]
            return pltpu.make_async_copy(src, buf.at[k, _token_tile(t), :], sem.at[s])
        return copy

    this_a = gather(pos_ref, 0, buf_a, 0)
    this_b = gather(pos_ref, sub, buf_b, 1)
    next_a = gather(nxt_ref, 0, buf_a, 0)

    def reduce_rows(buf, half, start_other):
        rows = slice(half * sub, (half + 1) * sub)
        wts = wts_ref[rows, :]
        parts = []
        for j in range(ROW_PARTS):
            for t in range(j * sub // ROW_PARTS, (j + 1) * sub // ROW_PARTS):
                for k in range(TOP_K):
                    start_other(t, k).start(priority=k % 2)
            part = wts[:, 0:1] * buf[0, pl.ds(j, sub, stride=ROW_PARTS), :]
            for k in range(1, TOP_K):
                part = part + wts[:, k:k + 1] * buf[k, pl.ds(j, sub, stride=ROW_PARTS), :]
            parts.append(part)
        acc = jnp.concatenate(parts, axis=-1)
        o_ref[rows, :] = _rms(x1_ref[rows, :] + g2_ref[0] * acc, nf_ref[...])

    @pl.when(i == 0)
    def _():
        _start_rows(sub, this_a)

    _wait_rows(sub, this_a)
    reduce_rows(buf_a, 0, this_b)
    _wait_rows(sub, this_b)
    reduce_rows(buf_b, 1, next_a)

    @pl.when(i == pl.num_programs(0) - 1)
    def _():
        _wait_rows(sub, next_a)


def _combine(pos_flat, ys, wts, x1, mods, norm_final, seq):
    n_tok = x1.shape[0]
    step_tok = 2 * COMBINE_TILE
    steps_per_sample = seq // step_tok
    tok = lambda n: pl.BlockSpec((step_tok, n), lambda i: (i, 0))
    n_steps = n_tok // step_tok
    gather_buf = pltpu.VMEM((TOP_K, COMBINE_TILE * ROW_PARTS, LANES), F32)
    return pl.pallas_call(
        _combine_kernel,
        grid=(n_steps,),
        in_specs=_index_blocks(step_tok, n_steps) + [
            pl.BlockSpec(memory_space=pl.ANY), tok(TOP_K), tok(D_MODEL),
            pl.BlockSpec((1, 1, D_MODEL), lambda i: (i // steps_per_sample, 0, MOD_G2)),
            pl.BlockSpec((1, D_MODEL), lambda i: (0, 0))],
        out_specs=tok(D_MODEL),
        out_shape=jax.ShapeDtypeStruct((n_tok, D_MODEL), F32),
        scratch_shapes=[gather_buf, gather_buf, pltpu.SemaphoreType.DMA((2,))],
        compiler_params=_params("arbitrary"),
        name="combine",
    )(pos_flat, pos_flat, ys, wts, x1, mods, norm_final.reshape(1, -1))


def _moe(h2, idx, wts, w_up, b_up_g, w_down, b_down, x1, mods, norm_final):
    bsz, seq, _ = x1.shape
    n_tok = bsz * seq
    pos, counts = _route(idx.reshape(n_tok, TOP_K))
    pos_flat = pos.reshape(-1)
    xs = _dispatch(pos_flat, h2.reshape(n_tok * ROW_PARTS, LANES))
    ys = _experts(_work_items(counts.reshape(-1), n_tok * TOP_K), xs, w_up, b_up_g, w_down, b_down)
    out = _combine(pos_flat, ys, wts.reshape(n_tok, TOP_K), x1.reshape(n_tok, D_MODEL), mods, norm_final, seq)
    return out.reshape(bsz, seq, D_MODEL)


def kernel(x, c, ctx, c_ctx, w_ada, b_ada, norm_mix, norm_ffn, w_in, w_gk_f, b_gk_f, w_gk_b, b_gk_b, gla_norm, w_gla_out, hy_conv_w, hy_conv_b, hy_f_w1, hy_f_b1, hy_f_w2, hy_f_b2, hy_f_w3, hy_f_b3, hy_f_w4, hy_sin_freq, hy_bias, w_hy_out, w_out, w_router, b_router, w_up, b_up, w_down, b_down, norm_final):
    depth = w_ada.shape[0]
    assert depth == 1, "single-layer block: the context stream only feeds later layers"
    bsz, seq, _ = x.shape

    n_rows = -(-(bsz + 1) // 8) * 8
    cc = jnp.zeros((n_rows, D_MODEL), F32).at[:bsz].set(c).at[bsz].set(c_ctx)
    mods = _ada(cc, w_ada[0], b_ada[0]).reshape(n_rows, 1, N_MOD * D_MODEL)

    sizes = (GLA_DK, GLA_DK, GLA_DV, GLA_DV, GLA_GATE_RANK, GLA_GATE_RANK, 3 * HY_WIDTH, D_MODEL, D_MODEL)
    offs = np.concatenate([[0], np.cumsum(sizes)])
    w_in_b = w_in[0].astype(BF16)
    wq, wk, wv, wog, wrf, wrb, why, wma, wmb = [w_in_b[:, offs[i]:offs[i + 1]] for i in range(len(sizes))]
    q, k, v, og, r_f, r_b, hy, m_a, m_b = _inproj(
        x, mods, lambda b: b, norm_mix[0], [wq, wk, wv, wog, wrf, wrb, why, wma, wmb],
        [BF16, BF16, BF16, BF16, F32, F32, BF16, BF16, BF16], tl=512, conv=(6, hy_conv_w[0], hy_conv_b[0]))
    k_c, v_c, rc_f, rc_b = _inproj(
        ctx, mods, lambda b: bsz, norm_mix[0], [wk, wv, wrf, wrb], [BF16, BF16, F32, F32], tl=ctx.shape[1])

    w_gk = jnp.stack([w_gk_f[0], w_gk_b[0]])
    b_gk = jnp.stack([b_gk_f[0], b_gk_b[0]])[:, None, :]
    o_f, o_b = _gla(q, k, v, r_f, r_b, k_c, v_c, rc_f, rc_b, w_gk, b_gk)

    t_sym, t_anti = [jnp.asarray(t).astype(BF16) for t in _dft_tables(seq)]
    hs, hd, pn = _hyena_filter(seq, hy_f_w1[0], hy_f_b1[0], hy_f_w2[0], hy_f_b2[0], hy_f_w3[0], hy_f_b3[0],
                               hy_f_w4[0], hy_sin_freq[0])
    p_spec, q_spec, k_nyq = _hyena_spectrum(t_sym, t_anti, hs, hd, pn, hy_bias[0])
    y_h = _hyena_conv(hy, (t_sym, t_anti, t_sym.T, t_anti.T), p_spec, q_spec, k_nyq)

    x1, h2, idx, wts = _post(o_f, o_b, og, y_h, m_a, m_b, x, mods, gla_norm[0], norm_ffn[0],
                             w_gla_out[0].astype(BF16), w_hy_out[0].astype(BF16), w_out[0].astype(BF16),
                             w_router[0], b_router[0])

    b_up_g = b_up[0].reshape(N_EXPERTS, N_UP_GROUPS, LANES, 2).transpose(0, 1, 3, 2).reshape(N_EXPERTS, 1, -1)
    return _moe(h2, idx, wts, w_up[0], b_up_g, w_down[0], b_down[0][:, None, :],
                x1, mods, norm_final)
```

```python
import functools
import math

import jax
import jax.numpy as jnp
import numpy as np
from jax import lax
from jax.experimental import pallas as pl
from jax.experimental.pallas import tpu as pltpu

F32 = jnp.float32
BF16 = jnp.bfloat16

D_MODEL = 1024
GRID_W = 64
EPS = 1e-6
N_MOD = 6

GLA_HEADS = 4
GLA_HEAD_K = 128
GLA_HEAD_V = 256
GLA_DK = GLA_HEADS * GLA_HEAD_K
GLA_DV = GLA_HEADS * GLA_HEAD_V
GLA_GATE_RANK = 16
GLA_GATE_NORM = 16.0
GLA_CHUNK = 64

HY_WIDTH = D_MODEL
HY_EMB = 33
HY_FAST_DECAY = 0.3
HY_SLOW_DECAY = 1.5
HY_TARGET = 1e-2

N_EXPERTS = 32
TOP_K = 4
SWIGLU_LIMIT = 7.0
SWIGLU_ALPHA = 1.702

V7X_VMEM_BYTES = 64 * 1024 * 1024
VMEM_LIMIT = V7X_VMEM_BYTES - 8 * 1024 * 1024
LANES = 128

MOD_SH1, MOD_SC1, MOD_G1, MOD_SH2, MOD_SC2, MOD_G2 = range(N_MOD)


def _params(*sem):
    return pltpu.CompilerParams(dimension_semantics=sem, vmem_limit_bytes=VMEM_LIMIT)


def _resident(shape):
    return pl.BlockSpec(shape, lambda *_: (0,) * len(shape), pipeline_mode=pl.Buffered(1))


def _dot(a, b):
    return jnp.dot(a, b, preferred_element_type=F32)


def _dot_nt(a, b):
    return lax.dot_general(a, b, (((1,), (1,)), ((), ())), preferred_element_type=F32)


def _dot_tn(a, b):
    return lax.dot_general(a, b, (((0,), (0,)), ((), ())), preferred_element_type=F32)


def _split(a):
    hi = a.astype(BF16)
    lo = (a - hi.astype(F32)).astype(BF16)
    return hi, lo


def _dot3(a, b):
    ah, al = _split(a)
    bh, bl = _split(b)
    return _dot(ah, bh) + (_dot(ah, bl) + _dot(al, bh))


def _sigmoid(x):
    return 1.0 / (1.0 + jnp.exp(-x))


def _log_sigmoid(x):
    return jnp.minimum(x, 0.0) - jnp.log(1.0 + jnp.exp(-jnp.abs(x)))


def _rms(x, w):
    return x * lax.rsqrt(jnp.mean(x * x, axis=-1, keepdims=True) + EPS) * w


def _ada_kernel(c_ref, w_ref, b_ref, o_ref):
    c = c_ref[...]
    o_ref[...] = _dot3(c * _sigmoid(c), w_ref[...]) + b_ref[...]


def _ada(cc, w_ada, b_ada):
    rows = cc.shape[0]
    return pl.pallas_call(
        _ada_kernel,
        grid=(N_MOD,),
        in_specs=[pl.BlockSpec((rows, D_MODEL), lambda j: (0, 0)),
                  pl.BlockSpec((D_MODEL, D_MODEL), lambda j: (0, j)),
                  pl.BlockSpec((1, D_MODEL), lambda j: (0, j))],
        out_specs=pl.BlockSpec((rows, D_MODEL), lambda j: (0, j)),
        out_shape=jax.ShapeDtypeStruct((rows, N_MOD * D_MODEL), F32),
        compiler_params=_params("parallel"),
        name="ada",
    )(cc, w_ada, b_ada.reshape(1, -1))


INPROJ_COL_CHUNK = 512


def _inproj_kernel(n_out, conv_out, x_ref, sc_ref, sh_ref, nw_ref, *refs):
    n_in = n_out + (0 if conv_out is None else 2)
    w_refs, o_refs = refs[:n_out], refs[n_in:]
    h = _rms(x_ref[0], nw_ref[...]) * (1.0 + sc_ref[0]) + sh_ref[0]
    hb = h.astype(BF16)
    for i, (w_ref, o_ref) in enumerate(zip(w_refs, o_refs)):
        n = w_ref.shape[1]
        for c0 in range(0, n, INPROJ_COL_CHUNK):
            c1 = min(c0 + INPROJ_COL_CHUNK, n)
            y = _dot(hb, w_ref[:, c0:c1])
            if i == conv_out:
                cw_ref, cb_ref = refs[n_out:n_in]
                y = _short_conv(y, cw_ref[:, c0:c1], cb_ref[:, c0:c1])
            o_ref[0, :, c0:c1] = y.astype(o_ref.dtype)


def _inproj(x, mods, mod_row, norm_w, weights, out_dtypes, tl, conv=None):
    bsz, seq, _ = x.shape
    n_out = len(weights)
    assert tl % GRID_W == 0
    in_specs = [pl.BlockSpec((1, tl, D_MODEL), lambda b, i: (b, i, 0)),
                pl.BlockSpec((1, 1, D_MODEL), lambda b, i: (mod_row(b), 0, MOD_SC1)),
                pl.BlockSpec((1, 1, D_MODEL), lambda b, i: (mod_row(b), 0, MOD_SH1)),
                _resident((1, D_MODEL))]
    in_specs += [_resident(w.shape) for w in weights]
    extra = []
    if conv is not None:
        extra = [conv[1], conv[2].reshape(1, -1)]
        in_specs += [_resident(a.shape) for a in extra]
    out_specs = [pl.BlockSpec((1, tl, w.shape[1]), lambda b, i: (b, i, 0)) for w in weights]
    out_shape = [jax.ShapeDtypeStruct((bsz, seq, w.shape[1]), dt) for w, dt in zip(weights, out_dtypes)]
    return pl.pallas_call(
        functools.partial(_inproj_kernel, n_out, None if conv is None else conv[0]),
        grid=(bsz, seq // tl),
        in_specs=in_specs, out_specs=out_specs, out_shape=out_shape,
        compiler_params=_params("parallel", "parallel"),
        name="inproj",
    )(x, mods, mods, norm_w.reshape(1, -1), *weights, *extra)


GLA_TILE = 256
GLA_NCH = GLA_TILE // GLA_CHUNK
GLA_SCALE = GLA_HEAD_K ** -0.5
GLA_HPS = 4
GLA_KW = GLA_HPS * GLA_HEAD_K
GLA_VW = GLA_HPS * GLA_HEAD_V


@functools.lru_cache(maxsize=None)
def _chunk_triangles():
    i = np.arange(GLA_TILE)
    same = (i[:, None] // GLA_CHUNK) == (i[None, :] // GLA_CHUNK)
    return np.stack([same & (i[None, :] <= i[:, None]), same & (i[None, :] >= i[:, None])]).astype(np.float32)


def _gla_kernel(tri_ref, qf_ref, kf_ref, vf_ref, rf_ref, qb_ref, kb_ref, vb_ref, rb_ref,
                kc_ref, vc_ref, rcf_ref, rcb_ref, wgk_ref, bgk_ref,
                of_ref, ob_ref, sf_ref, sb_ref):
    t = pl.program_id(2)
    crow = lax.broadcasted_iota(jnp.int32, (GLA_CHUNK, GLA_CHUNK), 0)
    ccol = lax.broadcasted_iota(jnp.int32, (GLA_CHUNK, GLA_CHUNK), 1)
    heads = range(GLA_HPS)

    def rows(c):
        return slice(c * GLA_CHUNK, (c + 1) * GLA_CHUNK)

    def hk(h):
        return slice(h * GLA_HEAD_K, (h + 1) * GLA_HEAD_K)

    def hv(h):
        return slice(h * GLA_HEAD_V, (h + 1) * GLA_HEAD_V)

    def order(fwd):
        return range(GLA_NCH) if fwd else range(GLA_NCH - 1, -1, -1)

    def cum_decay(r, d):
        z = _dot3(r, wgk_ref[d]) + bgk_ref[d]
        g = _log_sigmoid(z) * (1.0 / GLA_GATE_NORM)
        gh, gl = _split(g)
        return _dot(tri_ref[d], gh) + _dot(tri_ref[d], gl)

    def updates(k_ref, v_ref, b, fwd):
        out = {}
        for c in range(GLA_NCH):
            i = c * GLA_CHUNK + (GLA_CHUNK - 1 if fwd else 0)
            total = b[i:i + 1, :]
            k_upd = (k_ref[0, rows(c), :].astype(F32) * jnp.exp(total - b[rows(c)])).astype(BF16)
            dec = jnp.exp(total)
            for h in heads:
                out[c, h] = (dec[:, hk(h)], _dot_tn(v_ref[0, rows(c), hv(h)], k_upd[:, hk(h)]))
        return out

    def context_state(s_ref, r_ref, d, fwd):
        upd = updates(kc_ref, vc_ref, cum_decay(r_ref[0], d), fwd)
        for h in heads:
            st = jnp.zeros((GLA_HEAD_V, GLA_HEAD_K), F32)
            for c in order(fwd):
                dec, inc = upd[c, h]
                st = st * dec + inc
            s_ref[h] = st

    def scan(s_ref, q_ref, k_ref, v_ref, r_ref, o_ref, d, fwd):
        mask = (ccol <= crow) if fwd else (ccol >= crow)
        b = cum_decay(r_ref[0], d)
        q_dec = (q_ref[0].astype(F32) * (jnp.exp(b) * GLA_SCALE)).astype(BF16)
        k_inv = (k_ref[0].astype(F32) * jnp.exp(-b)).astype(BF16)
        upd = updates(k_ref, v_ref, b, fwd)
        intra = {}
        for c in range(GLA_NCH):
            for h in heads:
                att = jnp.where(mask, _dot_nt(q_dec[rows(c), hk(h)], k_inv[rows(c), hk(h)]), 0.0).astype(BF16)
                intra[c, h] = _dot(att, v_ref[0, rows(c), hv(h)])
        enter = {}
        for h in heads:
            st = s_ref[h]
            for c in order(fwd):
                enter[c, h] = st.astype(BF16)
                dec, inc = upd[c, h]
                st = st * dec + inc
            s_ref[h] = st
        for c in range(GLA_NCH):
            for h in heads:
                o = intra[c, h] + _dot_nt(q_dec[rows(c), hk(h)], enter[c, h])
                o_ref[0, rows(c), hv(h)] = o.astype(o_ref.dtype)

    @pl.when(t == 0)
    def _():
        context_state(sf_ref, rcf_ref, 0, True)
        context_state(sb_ref, rcb_ref, 1, False)

    scan(sf_ref, qf_ref, kf_ref, vf_ref, rf_ref, of_ref, 0, True)
    scan(sb_ref, qb_ref, kb_ref, vb_ref, rb_ref, ob_ref, 1, False)


def _gla(q, k, v, r_f, r_b, k_c, v_c, rc_f, rc_b, w_gk, b_gk):
    bsz, seq, _ = q.shape
    assert k_c.shape[1] == GLA_TILE
    nt = seq // GLA_TILE
    tri = jnp.asarray(_chunk_triangles()).astype(BF16)
    fwd = lambda b, h, t: (b, t, h)
    bwd = lambda b, h, t: (b, nt - 1 - t, h)
    fwd0 = lambda b, h, t: (b, t, 0)
    bwd0 = lambda b, h, t: (b, nt - 1 - t, 0)
    ctx = lambda b, h, t: (b, 0, h)
    ctx0 = lambda b, h, t: (b, 0, 0)
    kblk = (1, GLA_TILE, GLA_KW)
    vblk = (1, GLA_TILE, GLA_VW)
    rblk = (1, GLA_TILE, GLA_GATE_RANK)
    in_specs = [_resident(tri.shape),
                pl.BlockSpec(kblk, fwd), pl.BlockSpec(kblk, fwd), pl.BlockSpec(vblk, fwd), pl.BlockSpec(rblk, fwd0),
                pl.BlockSpec(kblk, bwd), pl.BlockSpec(kblk, bwd), pl.BlockSpec(vblk, bwd), pl.BlockSpec(rblk, bwd0),
                pl.BlockSpec(kblk, ctx), pl.BlockSpec(vblk, ctx), pl.BlockSpec(rblk, ctx0), pl.BlockSpec(rblk, ctx0),
                pl.BlockSpec((2, GLA_GATE_RANK, GLA_KW), lambda b, h, t: (0, 0, h)),
                pl.BlockSpec((2, 1, GLA_KW), lambda b, h, t: (0, 0, h))]
    out_specs = [pl.BlockSpec(vblk, fwd), pl.BlockSpec(vblk, bwd)]
    out_shape = [jax.ShapeDtypeStruct((bsz, seq, GLA_DV), BF16)] * 2
    return pl.pallas_call(
        _gla_kernel,
        grid=(bsz, GLA_HEADS // GLA_HPS, nt),
        in_specs=in_specs, out_specs=out_specs, out_shape=out_shape,
        scratch_shapes=[pltpu.VMEM((GLA_HPS, GLA_HEAD_V, GLA_HEAD_K), F32)] * 2,
        compiler_params=_params("parallel", "parallel", "arbitrary"),
        name="gla",
    )(tri, q, k, v, r_f, q, k, v, r_b, k_c, v_c, rc_f, rc_b, w_gk, b_gk)


HY_CW = 512
HY_ROW_CHUNK = 512
MLP_PAD = 128


@functools.lru_cache(maxsize=None)
def _dft_tables(seq):
    n, half = 2 * seq, seq // 2
    m = np.arange(half)[:, None]
    s = np.arange(half)[None, :]
    ang = 2.0 * np.pi * np.arange(n) / n
    cos = lambda k: np.cos(ang)[(k * s) % n]
    sin = lambda k: np.sin(ang)[(k * s) % n]
    sym = np.concatenate([cos(2 * m), sin(2 * m + 1)], axis=0)
    anti = np.concatenate([sin(2 * m), cos(2 * m + 1)], axis=0)
    tables = (sym, anti, sym.T, anti.T)
    return tuple(np.ascontiguousarray(t, dtype=np.float32) for t in tables)


@functools.lru_cache(maxsize=None)
def _filter_features(seq):
    bands = (HY_EMB - 1) // 2
    pos = np.arange(seq, dtype=np.float64)[:, None]
    t = pos / max(seq - 1, 1)
    f = np.linspace(1e-4, bands - 1, bands)[None]
    ang = (2.0 * math.pi / seq) * pos * f
    z = np.concatenate([t, np.cos(ang), -np.sin(ang)], axis=-1)
    out = np.zeros((seq, MLP_PAD), np.float32)
    out[:, :HY_EMB] = z
    deltas = np.abs(np.linspace(math.log(HY_TARGET) / HY_SLOW_DECAY, math.log(HY_TARGET) / HY_FAST_DECAY, HY_WIDTH))
    return out, deltas.astype(np.float32)[None]


def _filter_kernel(z_ref, w1_ref, b1_ref, w2_ref, b2_ref, w3_ref, b3_ref, fr_ref, w4f_ref, w4b_ref, dl_ref,
                   hs_ref, hd_ref, pn_ref, hid_ref):
    z = z_ref[...]

    @pl.when(pl.program_id(0) == 0)
    def _():
        fr = fr_ref[...]
        h = jnp.sin(fr * (_dot3(z, w1_ref[...]) + b1_ref[...]))
        h = jnp.sin(fr * (_dot3(h, w2_ref[...]) + b2_ref[...]))
        hid_ref[...] = jnp.sin(fr * (_dot3(h, w3_ref[...]) + b3_ref[...]))

    h = hid_ref[...]
    window = jnp.exp(-z[:, 0:1] * dl_ref[...])
    h_f = _dot3(h, w4f_ref[...]) * window
    h_b = _dot3(h, w4b_ref[...]) * window
    pos = lax.broadcasted_iota(jnp.int32, (z.shape[0], 1), 0)
    h_b = jnp.where(pos == 0, 0.0, h_b)
    hs = h_f + h_b
    hs_ref[...] = hs.astype(BF16)
    hd_ref[...] = (h_b - h_f).astype(BF16)
    sign = jnp.where(pos % 2 == 0, 1.0, -1.0)
    pn_ref[...] = jnp.sum(hs * sign, axis=0, keepdims=True)


def _pad2(a, rows, cols):
    return jnp.zeros((rows, cols), F32).at[:a.shape[0], :a.shape[1]].set(a.astype(F32))


def _hyena_filter(seq, w1, b1, w2, b2, w3, b3, w4, freq):
    z_np, deltas_np = _filter_features(seq)
    p = MLP_PAD
    args = [jnp.asarray(z_np), _pad2(w1, p, p), _pad2(b1[None], 1, p), _pad2(w2, p, p), _pad2(b2[None], 1, p),
            _pad2(w3, p, p), _pad2(b3[None], 1, p), _pad2(freq[None], 1, p),
            _pad2(w4[:, :HY_WIDTH], p, HY_WIDTH), _pad2(w4[:, HY_WIDTH:], p, HY_WIDTH), jnp.asarray(deltas_np)]
    full = lambda shape: pl.BlockSpec(shape, lambda j: (0, 0))
    colblk = lambda rows: pl.BlockSpec((rows, HY_CW), lambda j: (0, j))
    in_specs = [full((seq, p)), full((p, p)), full((1, p)), full((p, p)), full((1, p)), full((p, p)), full((1, p)),
                full((1, p)), colblk(p), colblk(p), colblk(1)]
    return pl.pallas_call(
        _filter_kernel,
        grid=(HY_WIDTH // HY_CW,),
        in_specs=in_specs,
        out_specs=[colblk(seq), colblk(seq), colblk(1)],
        out_shape=[jax.ShapeDtypeStruct((seq, HY_WIDTH), BF16), jax.ShapeDtypeStruct((seq, HY_WIDTH), BF16),
                   jax.ShapeDtypeStruct((1, HY_WIDTH), F32)],
        scratch_shapes=[pltpu.VMEM((seq, p), F32)],
        compiler_params=_params("arbitrary"),
        name="hy_filter",
    )(*args)


REV_BLOCK = 256


def _negate_index(x):
    n = x.shape[0]
    r = lax.broadcasted_iota(jnp.int32, (REV_BLOCK, REV_BLOCK), 0)
    c = lax.broadcasted_iota(jnp.int32, (REV_BLOCK, REV_BLOCK), 1)
    flip = jnp.where(r + c == REV_BLOCK - 1, 1.0, 0.0).astype(BF16)
    if x.dtype == BF16:
        pieces = [x]
    else:
        x0 = x.astype(BF16)
        r1 = x - x0.astype(F32)
        x1 = r1.astype(BF16)
        pieces = [x0, x1, (r1 - x1.astype(F32)).astype(BF16)]
    blocks = []
    for i in range(n // REV_BLOCK - 1, -1, -1):
        rows = slice(i * REV_BLOCK, (i + 1) * REV_BLOCK)
        acc = _dot(flip, pieces[0][rows])
        for p in pieces[1:]:
            acc = acc + _dot(flip, p[rows])
        blocks.append(acc)
    return pltpu.roll(jnp.concatenate(blocks, axis=0), 1, 0)


def _fold(x):
    half = x.shape[0] // 2
    lo = x[:half].astype(F32)
    hi = _negate_index(x[half:])
    pos = lax.broadcasted_iota(jnp.int32, (half, 1), 0)
    mid = hi[0:1]
    hi = jnp.where(pos == 0, 0.0, hi)
    return (lo + hi).astype(BF16), (lo - hi).astype(BF16), mid


def _alternating(n):
    pos = lax.broadcasted_iota(jnp.int32, (n, 1), 0)
    return jnp.where(pos % 2 == 0, 1.0, -1.0)


def _spectrum_kernel(ts_ref, ta_ref, hs_ref, hd_ref, pn_ref, bias_ref, p_ref, q_ref, kn_ref):
    half = hs_ref.shape[0] // 2
    bias = bias_ref[...]
    sgn = _alternating(half)
    s_sym, s_anti, s_mid = _fold(hs_ref[...])
    d_sym, d_anti, d_mid = _fold(hd_ref[...])
    p_ref[:half, :] = _dot(ts_ref[:half, :], s_sym) + sgn * s_mid + bias
    p_ref[half:, :] = _dot(ta_ref[half:, :], s_anti) + bias
    q_ref[:half, :] = _dot(ta_ref[:half, :], d_anti)
    q_ref[half:, :] = _dot(ts_ref[half:, :], d_sym) + sgn * d_mid
    kn_ref[...] = pn_ref[...] + bias


def _hyena_spectrum(t_sym, t_anti, hs, hd, pn, bias):
    seq = hs.shape[0]
    colblk = lambda rows: pl.BlockSpec((rows, HY_CW), lambda j: (0, j))
    return pl.pallas_call(
        _spectrum_kernel,
        grid=(HY_WIDTH // HY_CW,),
        in_specs=[_resident(t_sym.shape), _resident(t_anti.shape), colblk(seq), colblk(seq), colblk(1), colblk(1)],
        out_specs=[colblk(seq), colblk(seq), colblk(1)],
        out_shape=[jax.ShapeDtypeStruct((seq, HY_WIDTH), F32)] * 2 + [jax.ShapeDtypeStruct((1, HY_WIDTH), F32)],
        compiler_params=_params("parallel"),
        name="hy_spectrum",
    )(t_sym, t_anti, hs, hd, pn, bias.reshape(1, -1))


def _short_conv(u, w, b):
    n = u.shape[0]
    pos = lax.broadcasted_iota(jnp.int32, (n, 1), 0) % GRID_W
    up = jnp.where(pos == 0, 0.0, pltpu.roll(u, 1, 0))
    dn = jnp.where(pos == GRID_W - 1, 0.0, pltpu.roll(u, n - 1, 0))
    return up * w[0:1] + u * w[1:2] + dn * w[2:3] + b


def _hy_fwd_kernel(ts_ref, ta_ref, x1_ref, v_ref, p_ref, q_ref, kn_ref, s_ref):
    seq = x1_ref.shape[1]
    n, half = 2 * seq, seq // 2
    u_sym, u_anti, u_mid = _fold((v_ref[0].astype(F32) * x1_ref[0].astype(F32)).astype(BF16))
    sgn = _alternating(half)
    nyq = (jnp.sum(sgn * u_sym.astype(F32), axis=0, keepdims=True) + u_mid) * kn_ref[...] * (1.0 / n)
    for r0 in range(0, half, HY_ROW_CHUNK):
        r1 = r0 + HY_ROW_CHUNK
        mid = sgn[r0:r1] * u_mid
        re_e = _dot(ts_ref[r0:r1, :], u_sym) + mid
        im_o = _dot(ts_ref[half + r0:half + r1, :], u_sym) + mid
        im_e = _dot(ta_ref[r0:r1, :], u_anti)
        re_o = _dot(ta_ref[half + r0:half + r1, :], u_anti)
        pe, po = p_ref[r0:r1, :], p_ref[half + r0:half + r1, :]
        qe, qo = q_ref[r0:r1, :], q_ref[half + r0:half + r1, :]
        s_re_e = (re_e * pe + im_e * qe) * (2.0 / n)
        s_im_e = (im_e * pe - re_e * qe) * (2.0 / n)
        if r0 == 0:
            pos = lax.broadcasted_iota(jnp.int32, (HY_ROW_CHUNK, 1), 0)
            s_re_e = jnp.where(pos == 0, re_e * pe * (1.0 / n), s_re_e)
            s_im_e = jnp.where(pos == 0, nyq, s_im_e)
        s_ref[0, r0:r1, :] = s_re_e.astype(BF16)
        s_ref[0, half + r0:half + r1, :] = ((im_o * po - re_o * qo) * (2.0 / n)).astype(BF16)
        s_ref[0, seq + r0:seq + r1, :] = s_im_e.astype(BF16)
        s_ref[0, seq + half + r0:seq + half + r1, :] = ((re_o * po + im_o * qo) * (2.0 / n)).astype(BF16)


def _hy_inv_kernel(gs_ref, ga_ref, s_ref, x0_ref, o_ref):
    seq = x0_ref.shape[1]
    half = seq // 2
    x0 = x0_ref[0].astype(F32)
    sgn = _alternating(half)
    nyq = sgn * s_ref[0, seq:seq + 1, :].astype(F32)
    mirrored = []
    for r0 in range(0, half, HY_ROW_CHUNK):
        r1 = r0 + HY_ROW_CHUNK
        sym = _dot(gs_ref[r0:r1, :], s_ref[0, :seq, :]) + nyq[r0:r1]
        anti = _dot(ga_ref[r0:r1, :], s_ref[0, seq:, :])
        o_ref[0, r0:r1, :] = (x0[r0:r1] * (sym + anti)).astype(o_ref.dtype)
        mirrored.append(sym - anti)
    centre = (jnp.sum(sgn * (s_ref[0, :half, :].astype(F32) + s_ref[0, half:seq, :].astype(F32)), axis=0, keepdims=True)
              + nyq[0:1])
    pos = lax.broadcasted_iota(jnp.int32, (half, 1), 0)
    upper = _negate_index(jnp.where(pos == 0, centre, jnp.concatenate(mirrored, axis=0)).astype(BF16))
    o_ref[0, half:, :] = (x0[half:] * upper).astype(o_ref.dtype)


def _hyena_conv(hy, tabs, p_spec, q_spec, k_nyq):
    bsz, seq, _ = hy.shape
    nc = HY_WIDTH // HY_CW
    t_sym, t_anti, g_sym, g_anti = tabs
    sig = lambda part: pl.BlockSpec((1, seq, HY_CW), lambda j, b: (b, 0, part * nc + j))
    spec = lambda rows: pl.BlockSpec((rows, HY_CW), lambda j, b: (0, j))
    s = pl.pallas_call(
        _hy_fwd_kernel,
        grid=(nc, bsz),
        in_specs=[_resident(t_sym.shape), _resident(t_anti.shape), sig(1), sig(2),
                  spec(seq), spec(seq), spec(1)],
        out_specs=pl.BlockSpec((1, 2 * seq, HY_CW), lambda j, b: (b, 0, j)),
        out_shape=jax.ShapeDtypeStruct((bsz, 2 * seq, HY_WIDTH), BF16),
        compiler_params=_params("parallel", "parallel"),
        name="hy_fwd",
    )(t_sym, t_anti, hy, hy, p_spec, q_spec, k_nyq)
    return pl.pallas_call(
        _hy_inv_kernel,
        grid=(nc, bsz),
        in_specs=[_resident(g_sym.shape), _resident(g_anti.shape),
                  pl.BlockSpec((1, 2 * seq, HY_CW), lambda j, b: (b, 0, j)), sig(0)],
        out_specs=pl.BlockSpec((1, seq, HY_CW), lambda j, b: (b, 0, j)),
        out_shape=jax.ShapeDtypeStruct((bsz, seq, HY_WIDTH), BF16),
        compiler_params=_params("parallel", "parallel"),
        name="hy_inv",
    )(g_sym, g_anti, s, hy)


ROW_PARTS = D_MODEL // LANES


def _token_tile(t):
    return pl.ds(pl.multiple_of(t * ROW_PARTS, ROW_PARTS), ROW_PARTS)


def _load_token_rows(ref, n_tok):
    return jnp.concatenate([ref[pl.ds(j, n_tok, stride=ROW_PARTS), :] for j in range(ROW_PARTS)], axis=-1)


def _store_token_rows(ref, val):
    for j in range(ROW_PARTS):
        ref[pl.ds(j, val.shape[0], stride=ROW_PARTS), :] = val[:, j * LANES:(j + 1) * LANES]


POST_TILE = 512


def _post_kernel(of_ref, ob_ref, og_ref, yh_ref, ma_ref, mb_ref, x_ref, g1_ref, sc2_ref, sh2_ref,
                 gn_ref, nf_ref, wa_ref, wb_ref, wo_ref, wr_ref, br_ref,
                 x1_ref, h2_ref, idx_ref, wts_ref, cnt_ref):
    o = of_ref[0].astype(F32) + ob_ref[0].astype(F32)
    gn = gn_ref[...]
    heads = [_rms(o[:, h * GLA_HEAD_V:(h + 1) * GLA_HEAD_V], gn) for h in range(GLA_HEADS)]
    og = og_ref[0].astype(F32)
    a_in = jnp.concatenate(heads, axis=-1) * (og * _sigmoid(og))
    branch_a = _dot(a_in.astype(BF16), wa_ref[...])
    branch_b = _dot(yh_ref[0], wb_ref[...])
    y = _sigmoid(ma_ref[0].astype(F32)) * branch_a + _sigmoid(mb_ref[0].astype(F32)) * branch_b
    x1 = x_ref[0] + g1_ref[0] * _dot(y.astype(BF16), wo_ref[...])
    x1_ref[0] = x1
    h2 = _rms(x1, nf_ref[...]) * (1.0 + sc2_ref[0]) + sh2_ref[0]
    _store_token_rows(h2_ref.at[0], h2)

    h_hi, h_lo = _split(h2)
    w_hi, w_lo = _split(wr_ref[...])
    both = _dot(h_hi, jnp.concatenate([w_hi, w_lo], axis=-1))
    logits = both[:, :N_EXPERTS] + (both[:, N_EXPERTS:] + _dot(h_lo, w_hi)) + br_ref[...]
    lane = lax.broadcasted_iota(jnp.int32, logits.shape, 1)
    slot = lax.broadcasted_iota(jnp.int32, (logits.shape[0], TOP_K), 1)
    work = logits
    tops = []
    idx = jnp.zeros((logits.shape[0], TOP_K), jnp.int32)
    chosen = jnp.zeros(logits.shape, F32)
    for k in range(TOP_K):
        m = jnp.max(work, axis=-1, keepdims=True)
        first = jnp.min(jnp.where(work == m, lane, N_EXPERTS), axis=-1, keepdims=True)
        tops.append(m)
        idx = jnp.where(slot == k, first, idx)
        chosen = jnp.where(lane == first, 1.0, chosen)
        work = jnp.where(lane == first, -jnp.inf, work)
    es = [jnp.exp(m - tops[0]) for m in tops]
    inv = 1.0 / functools.reduce(lambda a, b: a + b, es)
    wts = jnp.zeros((logits.shape[0], TOP_K), F32)
    for k, e in enumerate(es):
        wts = jnp.where(slot == k, e * inv, wts)
    idx_ref[0] = idx
    wts_ref[0] = wts

    @pl.when((pl.program_id(0) == 0) & (pl.program_id(1) == 0))
    def _():
        cnt_ref[...] = jnp.zeros_like(cnt_ref)

    cnt_ref[...] += jnp.sum(chosen, axis=0, keepdims=True)


def _post(o_f, o_b, og, y_h, m_a, m_b, x, mods, gla_norm, norm_ffn, w_a, w_b, w_o, w_router, b_router):
    bsz, seq, _ = x.shape
    tok = lambda n: pl.BlockSpec((1, POST_TILE, n), lambda b, i: (b, i, 0))
    mod = lambda which: pl.BlockSpec((1, 1, D_MODEL), lambda b, i: (b, 0, which))
    in_specs = [tok(D_MODEL)] * 7 + [mod(MOD_G1), mod(MOD_SC2), mod(MOD_SH2),
                                     _resident((1, GLA_HEAD_V)), _resident((1, D_MODEL)),
                                     _resident(w_a.shape), _resident(w_b.shape), _resident(w_o.shape),
                                     _resident(w_router.shape), _resident((1, N_EXPERTS))]
    return pl.pallas_call(
        _post_kernel,
        grid=(bsz, seq // POST_TILE),
        in_specs=in_specs,
        out_specs=[tok(D_MODEL), pl.BlockSpec((1, POST_TILE * ROW_PARTS, LANES), lambda b, i: (b, i, 0)),
                   tok(TOP_K), tok(TOP_K), pl.BlockSpec((1, N_EXPERTS), lambda b, i: (0, 0))],
        out_shape=[jax.ShapeDtypeStruct((bsz, seq, D_MODEL), F32),
                   jax.ShapeDtypeStruct((bsz, seq * ROW_PARTS, LANES), F32),
                   jax.ShapeDtypeStruct((bsz, seq, TOP_K), jnp.int32), jax.ShapeDtypeStruct((bsz, seq, TOP_K), F32),
                   jax.ShapeDtypeStruct((1, N_EXPERTS), F32)],
        compiler_params=_params("arbitrary", "arbitrary"),
        name="post",
    )(o_f, o_b, og, y_h, m_a, m_b, x, mods, mods, mods, gla_norm.reshape(1, -1), norm_ffn.reshape(1, -1),
      w_a, w_b, w_o, w_router, b_router.reshape(1, -1))


UP_GROUP = 2 * LANES
N_UP_GROUPS = 2 * D_MODEL // UP_GROUP


def _stage_expert_weights(wu_ref, wd_ref, wu_s, wd_s):
    r = lax.broadcasted_iota(jnp.int32, (UP_GROUP, UP_GROUP), 0)
    c = lax.broadcasted_iota(jnp.int32, (UP_GROUP, UP_GROUP), 1)
    perm = jnp.where(c == (r % 2) * LANES + r // 2, 1.0, 0.0).astype(BF16)
    for g in range(N_UP_GROUPS):
        cols = slice(g * UP_GROUP, (g + 1) * UP_GROUP)
        wu_s[:, cols] = _dot(wu_ref[0, :, cols].astype(BF16), perm).astype(BF16)
    wd_s[...] = wd_ref[0].astype(BF16)


RANK_TILE = 1024
ROUTE_TILE = 512
EXPERT_TILE = 512
EXPERT_BLOCK = 512
DOWN_COLS_PER_DOT = 256
COMBINE_TILE = 256


def _exact_count_dot(a, m):
    a0 = a.astype(BF16)
    r1 = a - a0.astype(F32)
    a1 = r1.astype(BF16)
    a2 = (r1 - a1.astype(F32)).astype(BF16)
    return _dot(a0, m) + (_dot(a1, m) + _dot(a2, m))


def _route_kernel(idx_ref, cnt_ref, pos_ref, base, offs):
    i = pl.program_id(0)
    n_tok = idx_ref.shape[0]
    idx = idx_ref[...]
    lane = lax.broadcasted_iota(jnp.int32, (n_tok, N_EXPERTS), 1)
    hot = jnp.zeros((n_tok, N_EXPERTS), F32)
    for k in range(TOP_K):
        hot = hot + jnp.where(lane == idx[:, k:k + 1], 1.0, 0.0)

    @pl.when(i == 0)
    def _():
        r = lax.broadcasted_iota(jnp.int32, (N_EXPERTS, N_EXPERTS), 0)
        c = lax.broadcasted_iota(jnp.int32, (N_EXPERTS, N_EXPERTS), 1)
        before = jnp.where(r < c, 1.0, 0.0).astype(BF16)
        offs[...] = _exact_count_dot(jnp.broadcast_to(cnt_ref[...], (8, N_EXPERTS)), before)[0:1]
        base[...] = jnp.zeros_like(base)

    r = lax.broadcasted_iota(jnp.int32, (n_tok, n_tok), 0)
    c = lax.broadcasted_iota(jnp.int32, (n_tok, n_tok), 1)
    earlier = jnp.where(c < r, 1.0, 0.0).astype(BF16)
    dense = offs[...] + base[...] + _dot(earlier, hot.astype(BF16))
    slot = lax.broadcasted_iota(jnp.int32, (n_tok, TOP_K), 1)
    pos = jnp.zeros((n_tok, TOP_K), F32)
    for k in range(TOP_K):
        mine = jnp.sum(jnp.where(lane == idx[:, k:k + 1], dense, 0.0), axis=-1, keepdims=True)
        pos = jnp.where(slot == k, mine, pos)
    pos_ref[...] = pos.astype(jnp.int32)
    base[...] += jnp.sum(hot, axis=0, keepdims=True)


def _route(idx, counts):
    n_tok = idx.shape[0]
    return pl.pallas_call(
        _route_kernel,
        grid=(n_tok // RANK_TILE,),
        in_specs=[pl.BlockSpec((RANK_TILE, TOP_K), lambda i: (i, 0)),
                  pl.BlockSpec((1, N_EXPERTS), lambda i: (0, 0))],
        out_specs=pl.BlockSpec((RANK_TILE, TOP_K), lambda i: (i, 0)),
        out_shape=jax.ShapeDtypeStruct((n_tok, TOP_K), jnp.int32),
        scratch_shapes=[pltpu.VMEM((1, N_EXPERTS), F32)] * 2,
        compiler_params=_params("arbitrary"),
        name="route",
    )(idx, counts)


def _start_rows(n_tok, copy):
    def issue(t, carry):
        for k in range(TOP_K):
            copy(t, k).start(priority=k % 2)
        return carry

    lax.fori_loop(0, n_tok, issue, 0, unroll=4)


def _wait_rows(n_tok, copy):
    def drain(t, carry):
        for k in range(TOP_K):
            copy(t, k).wait()
        return carry

    lax.fori_loop(0, n_tok, drain, 0, unroll=4)


def _dispatch_kernel(pos_ref, h_ref, xs_ref, sem):
    def copy(t, k):
        return pltpu.make_async_copy(h_ref.at[_token_tile(t), :], xs_ref.at[_token_tile(pos_ref[t * TOP_K + k]), :], sem)

    _start_rows(ROUTE_TILE, copy)
    _wait_rows(ROUTE_TILE, copy)


def _index_blocks(tile, n_steps):
    blk = (tile * TOP_K,)
    return [pl.BlockSpec(blk, lambda i: (i,), memory_space=pltpu.SMEM),
            pl.BlockSpec(blk, lambda i: (jnp.minimum(i + 1, n_steps - 1),), memory_space=pltpu.SMEM)]


def _dispatch(pos_flat, h2):
    n_tok = h2.shape[0] // ROW_PARTS
    n_steps = n_tok // ROUTE_TILE
    return pl.pallas_call(
        _dispatch_kernel,
        grid=(n_steps,),
        in_specs=[_index_blocks(ROUTE_TILE, n_steps)[0],
                  pl.BlockSpec((ROUTE_TILE * ROW_PARTS, LANES), lambda i: (i, 0))],
        out_specs=pl.BlockSpec(memory_space=pl.ANY),
        out_shape=jax.ShapeDtypeStruct((n_tok * TOP_K * ROW_PARTS, LANES), F32),
        scratch_shapes=[pltpu.SemaphoreType.DMA],
        compiler_params=_params("arbitrary"),
        name="dispatch",
    )(pos_flat, h2)


def _experts_kernel(tile_ref, exp_ref, lo_ref, hi_ref, x_ref, wu_ref, bu_ref, wd_ref, bd_ref, o_ref,
                    acc_ref, wu_s, wd_s):
    w = pl.program_id(0)
    lo = lo_ref[w]
    hi = hi_ref[w]

    @pl.when(w == 0)
    def _():
        acc_ref[...] = jnp.zeros_like(acc_ref)

    @pl.when((w == 0) | (exp_ref[w] != exp_ref[jnp.maximum(w - 1, 0)]))
    def _():
        _stage_expert_weights(wu_ref, wd_ref, wu_s, wd_s)

    def block(r0):
        first = tile_ref[w] * EXPERT_TILE + r0
        rows = slice(r0, r0 + EXPERT_BLOCK)

        @pl.when((hi > lo) & (lo < first + EXPERT_BLOCK) & (hi > first))
        def _():
            x_rows = x_ref.at[pl.ds(r0 * ROW_PARTS, EXPERT_BLOCK * ROW_PARTS), :]
            h = _load_token_rows(x_rows, EXPERT_BLOCK).astype(BF16)
            acts = []
            for g in range(N_UP_GROUPS):
                cols = slice(g * UP_GROUP, (g + 1) * UP_GROUP)
                u = _dot(h, wu_s[:, cols]) + bu_ref[0, :, cols]
                glu = jnp.minimum(u[:, :LANES], SWIGLU_LIMIT)
                lin = jnp.clip(u[:, LANES:], -SWIGLU_LIMIT, SWIGLU_LIMIT)
                acts.append((glu * _sigmoid(SWIGLU_ALPHA * glu) * (lin + 1.0)).astype(BF16))
            a = jnp.concatenate(acts, axis=-1)
            row = first + lax.broadcasted_iota(jnp.int32, (EXPERT_BLOCK, 1), 0)
            mine = (row >= lo) & (row < hi)
            for c0 in range(0, D_MODEL, DOWN_COLS_PER_DOT):
                cols = slice(c0, c0 + DOWN_COLS_PER_DOT)
                kept = jnp.where(mine, _dot(a, wd_s[:, cols]) + bd_ref[0, :, cols], acc_ref[rows, cols])
                acc_ref[rows, cols] = kept
                for j in range(DOWN_COLS_PER_DOT // LANES):
                    part = pl.ds(r0 * ROW_PARTS + c0 // LANES + j, EXPERT_BLOCK, stride=ROW_PARTS)
                    o_ref[part, :] = kept[:, j * LANES:(j + 1) * LANES]

    for r0 in range(0, EXPERT_TILE, EXPERT_BLOCK):
        block(r0)


def _work_items(counts, n_rows):
    n_tiles = n_rows // EXPERT_TILE
    ends = jnp.cumsum(counts)
    tile_ends = jnp.arange(1, n_tiles + 1, dtype=jnp.int32) * EXPERT_TILE
    n_items = n_tiles + N_EXPERTS
    count = lambda cond: jnp.sum(cond.astype(jnp.int32), axis=1)
    slot_t = jnp.arange(n_tiles, dtype=jnp.int32) + count(ends[None, :] < tile_ends[:, None])
    slot_e = jnp.arange(N_EXPERTS, dtype=jnp.int32) + count(tile_ends[None, :] <= ends[:, None])
    slots = jnp.concatenate([slot_t, slot_e])
    vals = jnp.concatenate([tile_ends, ends])
    item = jnp.arange(n_items, dtype=jnp.int32)
    hi = jnp.sum(jnp.where(slots[None, :] == item[:, None], vals[None, :], 0), axis=1)
    lo = jnp.concatenate([jnp.zeros((1,), jnp.int32), hi[:-1]])
    tile_id = jnp.minimum(lo // EXPERT_TILE, n_tiles - 1)
    exp_id = jnp.minimum(count(ends[None, :] <= lo[:, None]), N_EXPERTS - 1)
    return tile_id, exp_id, lo, hi


def _experts(items, xs, w_up, b_up_g, w_down, b_down):
    n_items = items[0].shape[0]
    rows = lambda w, t, e, lo, hi: (t[w], 0)
    per_expert = lambda r, c: pl.BlockSpec((1, r, c), lambda w, t, e, lo, hi: (e[w], 0, 0))
    return pl.pallas_call(
        _experts_kernel,
        grid_spec=pltpu.PrefetchScalarGridSpec(
            num_scalar_prefetch=4,
            grid=(n_items,),
            in_specs=[pl.BlockSpec((EXPERT_TILE * ROW_PARTS, LANES), rows),
                      per_expert(D_MODEL, 2 * D_MODEL), per_expert(1, 2 * D_MODEL),
                      per_expert(D_MODEL, D_MODEL), per_expert(1, D_MODEL)],
            out_specs=pl.BlockSpec((EXPERT_TILE * ROW_PARTS, LANES), rows),
            scratch_shapes=[pltpu.VMEM((EXPERT_TILE, D_MODEL), F32), pltpu.VMEM((D_MODEL, 2 * D_MODEL), BF16),
                            pltpu.VMEM((D_MODEL, D_MODEL), BF16)]),
        out_shape=jax.ShapeDtypeStruct(xs.shape, F32),
        compiler_params=_params("arbitrary"),
        name="experts",
    )(*items, xs, w_up, b_up_g, w_down, b_down)


def _combine_kernel(pos_ref, nxt_ref, ys_ref, wts_ref, x1_ref, g2_ref, nf_ref, o_ref, buf_a, buf_b, sem):
    i = pl.program_id(0)
    sub = COMBINE_TILE

    def gather(p_ref, first_tok, buf, s):
        def copy(t, k):
            src = ys_ref.at[_token_tile(p_ref[(first_tok + t) * TOP_K + k]), :]
            return pltpu.make_async_copy(src, buf.at[k, _token_tile(t), :], sem.at[s])
        return copy

    this_a = gather(pos_ref, 0, buf_a, 0)
    this_b = gather(pos_ref, sub, buf_b, 1)
    next_a = gather(nxt_ref, 0, buf_a, 0)

    def reduce_rows(buf, half, start_other):
        rows = slice(half * sub, (half + 1) * sub)
        wts = wts_ref[rows, :]
        parts = []
        for j in range(ROW_PARTS):
            for t in range(j * sub // ROW_PARTS, (j + 1) * sub // ROW_PARTS):
                for k in range(TOP_K):
                    start_other(t, k).start(priority=k % 2)
            part = wts[:, 0:1] * buf[0, pl.ds(j, sub, stride=ROW_PARTS), :]
            for k in range(1, TOP_K):
                part = part + wts[:, k:k + 1] * buf[k, pl.ds(j, sub, stride=ROW_PARTS), :]
            parts.append(part)
        acc = jnp.concatenate(parts, axis=-1)
        o_ref[rows, :] = _rms(x1_ref[rows, :] + g2_ref[0] * acc, nf_ref[...])

    @pl.when(i == 0)
    def _():
        _start_rows(sub, this_a)

    _wait_rows(sub, this_a)
    reduce_rows(buf_a, 0, this_b)
    _wait_rows(sub, this_b)
    reduce_rows(buf_b, 1, next_a)

    @pl.when(i == pl.num_programs(0) - 1)
    def _():
        _wait_rows(sub, next_a)


def _combine(pos_flat, ys, wts, x1, mods, norm_final, seq):
    n_tok = x1.shape[0]
    step_tok = 2 * COMBINE_TILE
    steps_per_sample = seq // step_tok
    tok = lambda n: pl.BlockSpec((step_tok, n), lambda i: (i, 0))
    n_steps = n_tok // step_tok
    gather_buf = pltpu.VMEM((TOP_K, COMBINE_TILE * ROW_PARTS, LANES), F32)
    return pl.pallas_call(
        _combine_kernel,
        grid=(n_steps,),
        in_specs=_index_blocks(step_tok, n_steps) + [
            pl.BlockSpec(memory_space=pl.ANY), tok(TOP_K), tok(D_MODEL),
            pl.BlockSpec((1, 1, D_MODEL), lambda i: (i // steps_per_sample, 0, MOD_G2)),
            pl.BlockSpec((1, D_MODEL), lambda i: (0, 0))],
        out_specs=tok(D_MODEL),
        out_shape=jax.ShapeDtypeStruct((n_tok, D_MODEL), F32),
        scratch_shapes=[gather_buf, gather_buf, pltpu.SemaphoreType.DMA((2,))],
        compiler_params=_params("arbitrary"),
        name="combine",
    )(pos_flat, pos_flat, ys, wts, x1, mods, norm_final.reshape(1, -1))


def _moe(h2, idx, wts, counts, w_up, b_up_g, w_down, b_down, x1, mods, norm_final):
    bsz, seq, _ = x1.shape
    n_tok = bsz * seq
    pos_flat = _route(idx.reshape(n_tok, TOP_K), counts).reshape(-1)
    xs = _dispatch(pos_flat, h2.reshape(n_tok * ROW_PARTS, LANES))
    items = _work_items(counts.reshape(-1).astype(jnp.int32), n_tok * TOP_K)
    ys = _experts(items, xs, w_up, b_up_g, w_down, b_down)
    out = _combine(pos_flat, ys, wts.reshape(n_tok, TOP_K), x1.reshape(n_tok, D_MODEL), mods, norm_final, seq)
    return out.reshape(bsz, seq, D_MODEL)


def kernel(x, c, ctx, c_ctx, w_ada, b_ada, norm_mix, norm_ffn, w_in, w_gk_f, b_gk_f, w_gk_b, b_gk_b, gla_norm, w_gla_out, hy_conv_w, hy_conv_b, hy_f_w1, hy_f_b1, hy_f_w2, hy_f_b2, hy_f_w3, hy_f_b3, hy_f_w4, hy_sin_freq, hy_bias, w_hy_out, w_out, w_router, b_router, w_up, b_up, w_down, b_down, norm_final):
    depth = w_ada.shape[0]
    assert depth == 1, "single-layer block: the context stream only feeds later layers"
    bsz, seq, _ = x.shape

    n_rows = -(-(bsz + 1) // 8) * 8
    cc = jnp.zeros((n_rows, D_MODEL), F32).at[:bsz].set(c).at[bsz].set(c_ctx)
    mods = _ada(cc, w_ada[0], b_ada[0]).reshape(n_rows, 1, N_MOD * D_MODEL)

    sizes = (GLA_DK, GLA_DK, GLA_DV, GLA_DV, GLA_GATE_RANK, GLA_GATE_RANK, 3 * HY_WIDTH, D_MODEL, D_MODEL)
    offs = np.concatenate([[0], np.cumsum(sizes)])
    w_in_b = w_in[0].astype(BF16)
    wq, wk, wv, wog, wrf, wrb, why, wma, wmb = [w_in_b[:, offs[i]:offs[i + 1]] for i in range(len(sizes))]
    q, k, v, og, r_f, r_b, hy, m_a, m_b = _inproj(
        x, mods, lambda b: b, norm_mix[0], [wq, wk, wv, wog, wrf, wrb, why, wma, wmb],
        [BF16, BF16, BF16, BF16, F32, F32, BF16, BF16, BF16], tl=512, conv=(6, hy_conv_w[0], hy_conv_b[0]))
    k_c, v_c, rc_f, rc_b = _inproj(
        ctx, mods, lambda b: bsz, norm_mix[0], [wk, wv, wrf, wrb], [BF16, BF16, F32, F32], tl=ctx.shape[1])

    w_gk = jnp.stack([w_gk_f[0], w_gk_b[0]])
    b_gk = jnp.stack([b_gk_f[0], b_gk_b[0]])[:, None, :]
    o_f, o_b = _gla(q, k, v, r_f, r_b, k_c, v_c, rc_f, rc_b, w_gk, b_gk)

    tabs = [jnp.asarray(t).astype(BF16) for t in _dft_tables(seq)]
    t_sym, t_anti = tabs[:2]
    hs, hd, pn = _hyena_filter(seq, hy_f_w1[0], hy_f_b1[0], hy_f_w2[0], hy_f_b2[0], hy_f_w3[0], hy_f_b3[0],
                               hy_f_w4[0], hy_sin_freq[0])
    p_spec, q_spec, k_nyq = _hyena_spectrum(t_sym, t_anti, hs, hd, pn, hy_bias[0])
    y_h = _hyena_conv(hy, tabs, p_spec, q_spec, k_nyq)

    x1, h2, idx, wts, counts = _post(o_f, o_b, og, y_h, m_a, m_b, x, mods, gla_norm[0], norm_ffn[0],
                                     w_gla_out[0].astype(BF16), w_hy_out[0].astype(BF16), w_out[0].astype(BF16),
                                     w_router[0], b_router[0])

    b_up_g = b_up[0].reshape(N_EXPERTS, N_UP_GROUPS, LANES, 2).transpose(0, 1, 3, 2).reshape(N_EXPERTS, 1, -1)
    return _moe(h2, idx, wts, counts, w_up[0], b_up_g, w_down[0], b_down[0][:, None, :],
                x1, mods, norm_final)
```

```python
import functools
import math

import jax
import jax.numpy as jnp
import numpy as np
from jax import lax
from jax.experimental import pallas as pl
from jax.experimental.pallas import tpu as pltpu

F32 = jnp.float32
BF16 = jnp.bfloat16

D_MODEL = 1024
GRID_W = 64
EPS = 1e-6
N_MOD = 6

GLA_HEADS = 4
GLA_HEAD_K = 128
GLA_HEAD_V = 256
GLA_DK = GLA_HEADS * GLA_HEAD_K
GLA_DV = GLA_HEADS * GLA_HEAD_V
GLA_GATE_RANK = 16
GLA_GATE_NORM = 16.0
GLA_CHUNK = 64

HY_WIDTH = D_MODEL
HY_EMB = 33
HY_FAST_DECAY = 0.3
HY_SLOW_DECAY = 1.5
HY_TARGET = 1e-2

N_EXPERTS = 32
TOP_K = 4
SWIGLU_LIMIT = 7.0
SWIGLU_ALPHA = 1.702

V7X_VMEM_BYTES = 64 * 1024 * 1024
VMEM_LIMIT = V7X_VMEM_BYTES - 8 * 1024 * 1024
LANES = 128

MOD_SH1, MOD_SC1, MOD_G1, MOD_SH2, MOD_SC2, MOD_G2 = range(N_MOD)


def _params(*sem):
    return pltpu.CompilerParams(dimension_semantics=sem, vmem_limit_bytes=VMEM_LIMIT)


def _resident(shape):
    return pl.BlockSpec(shape, lambda *_: (0,) * len(shape), pipeline_mode=pl.Buffered(1))


def _dot(a, b):
    return jnp.dot(a, b, preferred_element_type=F32)


def _dot_nt(a, b):
    return lax.dot_general(a, b, (((1,), (1,)), ((), ())), preferred_element_type=F32)


def _dot_tn(a, b):
    return lax.dot_general(a, b, (((0,), (0,)), ((), ())), preferred_element_type=F32)


def _split(a):
    hi = a.astype(BF16)
    lo = (a - hi.astype(F32)).astype(BF16)
    return hi, lo


def _dot3(a, b):
    ah, al = _split(a)
    bh, bl = _split(b)
    return _dot(ah, bh) + (_dot(ah, bl) + _dot(al, bh))


def _sigmoid(x):
    return 1.0 / (1.0 + jnp.exp(-x))


def _log_sigmoid(x):
    return jnp.minimum(x, 0.0) - jnp.log(1.0 + jnp.exp(-jnp.abs(x)))


def _rms(x, w):
    return x * lax.rsqrt(jnp.mean(x * x, axis=-1, keepdims=True) + EPS) * w


def _ada_kernel(c_ref, w_ref, b_ref, o_ref):
    c = c_ref[...]
    o_ref[...] = _dot3(c * _sigmoid(c), w_ref[...]) + b_ref[...]


def _ada(cc, w_ada, b_ada):
    rows = cc.shape[0]
    return pl.pallas_call(
        _ada_kernel,
        grid=(N_MOD,),
        in_specs=[pl.BlockSpec((rows, D_MODEL), lambda j: (0, 0)),
                  pl.BlockSpec((D_MODEL, D_MODEL), lambda j: (0, j)),
                  pl.BlockSpec((1, D_MODEL), lambda j: (0, j))],
        out_specs=pl.BlockSpec((rows, D_MODEL), lambda j: (0, j)),
        out_shape=jax.ShapeDtypeStruct((rows, N_MOD * D_MODEL), F32),
        compiler_params=_params("parallel"),
        name="ada",
    )(cc, w_ada, b_ada.reshape(1, -1))


INPROJ_COL_CHUNK = 512


def _inproj_kernel(n_out, conv_out, x_ref, sc_ref, sh_ref, nw_ref, *refs):
    n_in = n_out + (0 if conv_out is None else 2)
    w_refs, o_refs = refs[:n_out], refs[n_in:]
    h = _rms(x_ref[0], nw_ref[...]) * (1.0 + sc_ref[0]) + sh_ref[0]
    hb = h.astype(BF16)
    for i, (w_ref, o_ref) in enumerate(zip(w_refs, o_refs)):
        n = w_ref.shape[1]
        for c0 in range(0, n, INPROJ_COL_CHUNK):
            c1 = min(c0 + INPROJ_COL_CHUNK, n)
            y = _dot(hb, w_ref[:, c0:c1])
            if i == conv_out:
                cw_ref, cb_ref = refs[n_out:n_in]
                y = _short_conv(y, cw_ref[:, c0:c1], cb_ref[:, c0:c1])
            o_ref[0, :, c0:c1] = y.astype(o_ref.dtype)


def _inproj(x, mods, mod_row, norm_w, weights, out_dtypes, tl, conv=None):
    bsz, seq, _ = x.shape
    n_out = len(weights)
    assert tl % GRID_W == 0
    in_specs = [pl.BlockSpec((1, tl, D_MODEL), lambda b, i: (b, i, 0)),
                pl.BlockSpec((1, 1, D_MODEL), lambda b, i: (mod_row(b), 0, MOD_SC1)),
                pl.BlockSpec((1, 1, D_MODEL), lambda b, i: (mod_row(b), 0, MOD_SH1)),
                _resident((1, D_MODEL))]
    in_specs += [_resident(w.shape) for w in weights]
    extra = []
    if conv is not None:
        extra = [conv[1], conv[2].reshape(1, -1)]
        in_specs += [_resident(a.shape) for a in extra]
    out_specs = [pl.BlockSpec((1, tl, w.shape[1]), lambda b, i: (b, i, 0)) for w in weights]
    out_shape = [jax.ShapeDtypeStruct((bsz, seq, w.shape[1]), dt) for w, dt in zip(weights, out_dtypes)]
    return pl.pallas_call(
        functools.partial(_inproj_kernel, n_out, None if conv is None else conv[0]),
        grid=(bsz, seq // tl),
        in_specs=in_specs, out_specs=out_specs, out_shape=out_shape,
        compiler_params=_params("parallel", "parallel"),
        name="inproj",
    )(x, mods, mods, norm_w.reshape(1, -1), *weights, *extra)


GLA_TILE = 256
GLA_NCH = GLA_TILE // GLA_CHUNK
GLA_SCALE = GLA_HEAD_K ** -0.5
GLA_HPS = 4
GLA_KW = GLA_HPS * GLA_HEAD_K
GLA_VW = GLA_HPS * GLA_HEAD_V


@functools.lru_cache(maxsize=None)
def _chunk_triangles():
    i = np.arange(GLA_TILE)
    same = (i[:, None] // GLA_CHUNK) == (i[None, :] // GLA_CHUNK)
    return np.stack([same & (i[None, :] <= i[:, None]), same & (i[None, :] >= i[:, None])]).astype(np.float32)


def _gla_kernel(tri_ref, qf_ref, kf_ref, vf_ref, rf_ref, qb_ref, kb_ref, vb_ref, rb_ref,
                kc_ref, vc_ref, rcf_ref, rcb_ref, wgk_ref, bgk_ref,
                of_ref, ob_ref, sf_ref, sb_ref):
    t = pl.program_id(2)
    crow = lax.broadcasted_iota(jnp.int32, (GLA_CHUNK, GLA_CHUNK), 0)
    ccol = lax.broadcasted_iota(jnp.int32, (GLA_CHUNK, GLA_CHUNK), 1)
    heads = range(GLA_HPS)

    def rows(c):
        return slice(c * GLA_CHUNK, (c + 1) * GLA_CHUNK)

    def hk(h):
        return slice(h * GLA_HEAD_K, (h + 1) * GLA_HEAD_K)

    def hv(h):
        return slice(h * GLA_HEAD_V, (h + 1) * GLA_HEAD_V)

    def order(fwd):
        return range(GLA_NCH) if fwd else range(GLA_NCH - 1, -1, -1)

    def cum_decay(r, d):
        z = _dot3(r, wgk_ref[d]) + bgk_ref[d]
        g = _log_sigmoid(z) * (1.0 / GLA_GATE_NORM)
        gh, gl = _split(g)
        return _dot(tri_ref[d], gh) + _dot(tri_ref[d], gl)

    def updates(k_ref, v_ref, b, fwd):
        out = {}
        for c in range(GLA_NCH):
            i = c * GLA_CHUNK + (GLA_CHUNK - 1 if fwd else 0)
            total = b[i:i + 1, :]
            k_upd = (k_ref[0, rows(c), :].astype(F32) * jnp.exp(total - b[rows(c)])).astype(BF16)
            dec = jnp.exp(total)
            for h in heads:
                out[c, h] = (dec[:, hk(h)], _dot_tn(v_ref[0, rows(c), hv(h)], k_upd[:, hk(h)]))
        return out

    def context_state(s_ref, r_ref, d, fwd):
        upd = updates(kc_ref, vc_ref, cum_decay(r_ref[0], d), fwd)
        for h in heads:
            st = jnp.zeros((GLA_HEAD_V, GLA_HEAD_K), F32)
            for c in order(fwd):
                dec, inc = upd[c, h]
                st = st * dec + inc
            s_ref[h] = st

    def scan(s_ref, q_ref, k_ref, v_ref, r_ref, o_ref, d, fwd):
        mask = (ccol <= crow) if fwd else (ccol >= crow)
        b = cum_decay(r_ref[0], d)
        q_dec = (q_ref[0].astype(F32) * (jnp.exp(b) * GLA_SCALE)).astype(BF16)
        k_inv = (k_ref[0].astype(F32) * jnp.exp(-b)).astype(BF16)
        upd = updates(k_ref, v_ref, b, fwd)
        intra = {}
        for c in range(GLA_NCH):
            for h in heads:
                att = jnp.where(mask, _dot_nt(q_dec[rows(c), hk(h)], k_inv[rows(c), hk(h)]), 0.0).astype(BF16)
                intra[c, h] = _dot(att, v_ref[0, rows(c), hv(h)])
        enter = {}
        for h in heads:
            st = s_ref[h]
            for c in order(fwd):
                enter[c, h] = st.astype(BF16)
                dec, inc = upd[c, h]
                st = st * dec + inc
            s_ref[h] = st
        for c in range(GLA_NCH):
            for h in heads:
                o = intra[c, h] + _dot_nt(q_dec[rows(c), hk(h)], enter[c, h])
                o_ref[0, rows(c), hv(h)] = o.astype(o_ref.dtype)

    @pl.when(t == 0)
    def _():
        context_state(sf_ref, rcf_ref, 0, True)
        context_state(sb_ref, rcb_ref, 1, False)

    scan(sf_ref, qf_ref, kf_ref, vf_ref, rf_ref, of_ref, 0, True)
    scan(sb_ref, qb_ref, kb_ref, vb_ref, rb_ref, ob_ref, 1, False)


def _gla(q, k, v, r_f, r_b, k_c, v_c, rc_f, rc_b, w_gk, b_gk):
    bsz, seq, _ = q.shape
    assert k_c.shape[1] == GLA_TILE
    nt = seq // GLA_TILE
    tri = jnp.asarray(_chunk_triangles()).astype(BF16)
    fwd = lambda b, h, t: (b, t, h)
    bwd = lambda b, h, t: (b, nt - 1 - t, h)
    fwd0 = lambda b, h, t: (b, t, 0)
    bwd0 = lambda b, h, t: (b, nt - 1 - t, 0)
    ctx = lambda b, h, t: (b, 0, h)
    ctx0 = lambda b, h, t: (b, 0, 0)
    kblk = (1, GLA_TILE, GLA_KW)
    vblk = (1, GLA_TILE, GLA_VW)
    rblk = (1, GLA_TILE, GLA_GATE_RANK)
    in_specs = [_resident(tri.shape),
                pl.BlockSpec(kblk, fwd), pl.BlockSpec(kblk, fwd), pl.BlockSpec(vblk, fwd), pl.BlockSpec(rblk, fwd0),
                pl.BlockSpec(kblk, bwd), pl.BlockSpec(kblk, bwd), pl.BlockSpec(vblk, bwd), pl.BlockSpec(rblk, bwd0),
                pl.BlockSpec(kblk, ctx), pl.BlockSpec(vblk, ctx), pl.BlockSpec(rblk, ctx0), pl.BlockSpec(rblk, ctx0),
                pl.BlockSpec((2, GLA_GATE_RANK, GLA_KW), lambda b, h, t: (0, 0, h)),
                pl.BlockSpec((2, 1, GLA_KW), lambda b, h, t: (0, 0, h))]
    out_specs = [pl.BlockSpec(vblk, fwd), pl.BlockSpec(vblk, bwd)]
    out_shape = [jax.ShapeDtypeStruct((bsz, seq, GLA_DV), BF16)] * 2
    return pl.pallas_call(
        _gla_kernel,
        grid=(bsz, GLA_HEADS // GLA_HPS, nt),
        in_specs=in_specs, out_specs=out_specs, out_shape=out_shape,
        scratch_shapes=[pltpu.VMEM((GLA_HPS, GLA_HEAD_V, GLA_HEAD_K), F32)] * 2,
        compiler_params=_params("parallel", "parallel", "arbitrary"),
        name="gla",
    )(tri, q, k, v, r_f, q, k, v, r_b, k_c, v_c, rc_f, rc_b, w_gk, b_gk)


HY_CW = 512
HY_ROW_CHUNK = 512
MLP_PAD = 128


@functools.lru_cache(maxsize=None)
def _dft_tables(seq):
    n, half = 2 * seq, seq // 2
    m = np.arange(half)[:, None]
    s = np.arange(half)[None, :]
    ang = 2.0 * np.pi * np.arange(n) / n
    cos = lambda k: np.cos(ang)[(k * s) % n]
    sin = lambda k: np.sin(ang)[(k * s) % n]
    sym = np.concatenate([cos(2 * m), sin(2 * m + 1)], axis=0)
    anti = np.concatenate([sin(2 * m), cos(2 * m + 1)], axis=0)
    tables = (sym, anti, sym.T, anti.T)
    return tuple(np.ascontiguousarray(t, dtype=np.float32) for t in tables)


@functools.lru_cache(maxsize=None)
def _filter_features(seq):
    bands = (HY_EMB - 1) // 2
    pos = np.arange(seq, dtype=np.float64)[:, None]
    t = pos / max(seq - 1, 1)
    f = np.linspace(1e-4, bands - 1, bands)[None]
    ang = (2.0 * math.pi / seq) * pos * f
    z = np.concatenate([t, np.cos(ang), -np.sin(ang)], axis=-1)
    out = np.zeros((seq, MLP_PAD), np.float32)
    out[:, :HY_EMB] = z
    deltas = np.abs(np.linspace(math.log(HY_TARGET) / HY_SLOW_DECAY, math.log(HY_TARGET) / HY_FAST_DECAY, HY_WIDTH))
    return out, deltas.astype(np.float32)[None]


def _filter_kernel(z_ref, w1_ref, b1_ref, w2_ref, b2_ref, w3_ref, b3_ref, fr_ref, w4f_ref, w4b_ref, dl_ref,
                   hs_ref, hd_ref, pn_ref, hid_ref):
    z = z_ref[...]

    @pl.when(pl.program_id(0) == 0)
    def _():
        fr = fr_ref[...]
        h = jnp.sin(fr * (_dot3(z, w1_ref[...]) + b1_ref[...]))
        h = jnp.sin(fr * (_dot3(h, w2_ref[...]) + b2_ref[...]))
        hid_ref[...] = jnp.sin(fr * (_dot3(h, w3_ref[...]) + b3_ref[...]))

    h = hid_ref[...]
    window = jnp.exp(-z[:, 0:1] * dl_ref[...])
    h_f = _dot3(h, w4f_ref[...]) * window
    h_b = _dot3(h, w4b_ref[...]) * window
    pos = lax.broadcasted_iota(jnp.int32, (z.shape[0], 1), 0)
    h_b = jnp.where(pos == 0, 0.0, h_b)
    hs = h_f + h_b
    hs_ref[...] = hs.astype(BF16)
    hd_ref[...] = (h_b - h_f).astype(BF16)
    sign = jnp.where(pos % 2 == 0, 1.0, -1.0)
    pn_ref[...] = jnp.sum(hs * sign, axis=0, keepdims=True)


def _pad2(a, rows, cols):
    return jnp.zeros((rows, cols), F32).at[:a.shape[0], :a.shape[1]].set(a.astype(F32))


def _hyena_filter(seq, w1, b1, w2, b2, w3, b3, w4, freq):
    z_np, deltas_np = _filter_features(seq)
    p = MLP_PAD
    args = [jnp.asarray(z_np), _pad2(w1, p, p), _pad2(b1[None], 1, p), _pad2(w2, p, p), _pad2(b2[None], 1, p),
            _pad2(w3, p, p), _pad2(b3[None], 1, p), _pad2(freq[None], 1, p),
            _pad2(w4[:, :HY_WIDTH], p, HY_WIDTH), _pad2(w4[:, HY_WIDTH:], p, HY_WIDTH), jnp.asarray(deltas_np)]
    full = lambda shape: pl.BlockSpec(shape, lambda j: (0, 0))
    colblk = lambda rows: pl.BlockSpec((rows, HY_CW), lambda j: (0, j))
    in_specs = [full((seq, p)), full((p, p)), full((1, p)), full((p, p)), full((1, p)), full((p, p)), full((1, p)),
                full((1, p)), colblk(p), colblk(p), colblk(1)]
    return pl.pallas_call(
        _filter_kernel,
        grid=(HY_WIDTH // HY_CW,),
        in_specs=in_specs,
        out_specs=[colblk(seq), colblk(seq), colblk(1)],
        out_shape=[jax.ShapeDtypeStruct((seq, HY_WIDTH), BF16), jax.ShapeDtypeStruct((seq, HY_WIDTH), BF16),
                   jax.ShapeDtypeStruct((1, HY_WIDTH), F32)],
        scratch_shapes=[pltpu.VMEM((seq, p), F32)],
        compiler_params=_params("arbitrary"),
        name="hy_filter",
    )(*args)


REV_BLOCK = 256


def _negate_index(x):
    n = x.shape[0]
    r = lax.broadcasted_iota(jnp.int32, (REV_BLOCK, REV_BLOCK), 0)
    c = lax.broadcasted_iota(jnp.int32, (REV_BLOCK, REV_BLOCK), 1)
    flip = jnp.where(r + c == REV_BLOCK - 1, 1.0, 0.0).astype(BF16)
    if x.dtype == BF16:
        pieces = [x]
    else:
        x0 = x.astype(BF16)
        r1 = x - x0.astype(F32)
        x1 = r1.astype(BF16)
        pieces = [x0, x1, (r1 - x1.astype(F32)).astype(BF16)]
    blocks = []
    for i in range(n // REV_BLOCK - 1, -1, -1):
        rows = slice(i * REV_BLOCK, (i + 1) * REV_BLOCK)
        acc = _dot(flip, pieces[0][rows])
        for p in pieces[1:]:
            acc = acc + _dot(flip, p[rows])
        blocks.append(acc)
    return pltpu.roll(jnp.concatenate(blocks, axis=0), 1, 0)


def _fold(x):
    half = x.shape[0] // 2
    lo = x[:half].astype(F32)
    hi = _negate_index(x[half:])
    pos = lax.broadcasted_iota(jnp.int32, (half, 1), 0)
    mid = hi[0:1]
    hi = jnp.where(pos == 0, 0.0, hi)
    return (lo + hi).astype(BF16), (lo - hi).astype(BF16), mid


def _alternating(n):
    pos = lax.broadcasted_iota(jnp.int32, (n, 1), 0)
    return jnp.where(pos % 2 == 0, 1.0, -1.0)


def _spectrum_kernel(ts_ref, ta_ref, hs_ref, hd_ref, pn_ref, bias_ref, p_ref, q_ref, kn_ref):
    half = hs_ref.shape[0] // 2
    bias = bias_ref[...]
    sgn = _alternating(half)
    s_sym, s_anti, s_mid = _fold(hs_ref[...])
    d_sym, d_anti, d_mid = _fold(hd_ref[...])
    p_ref[:half, :] = _dot(ts_ref[:half, :], s_sym) + sgn * s_mid + bias
    p_ref[half:, :] = _dot(ta_ref[half:, :], s_anti) + bias
    q_ref[:half, :] = _dot(ta_ref[:half, :], d_anti)
    q_ref[half:, :] = _dot(ts_ref[half:, :], d_sym) + sgn * d_mid
    kn_ref[...] = pn_ref[...] + bias


def _hyena_spectrum(t_sym, t_anti, hs, hd, pn, bias):
    seq = hs.shape[0]
    colblk = lambda rows: pl.BlockSpec((rows, HY_CW), lambda j: (0, j))
    return pl.pallas_call(
        _spectrum_kernel,
        grid=(HY_WIDTH // HY_CW,),
        in_specs=[_resident(t_sym.shape), _resident(t_anti.shape), colblk(seq), colblk(seq), colblk(1), colblk(1)],
        out_specs=[colblk(seq), colblk(seq), colblk(1)],
        out_shape=[jax.ShapeDtypeStruct((seq, HY_WIDTH), F32)] * 2 + [jax.ShapeDtypeStruct((1, HY_WIDTH), F32)],
        compiler_params=_params("parallel"),
        name="hy_spectrum",
    )(t_sym, t_anti, hs, hd, pn, bias.reshape(1, -1))


def _short_conv(u, w, b):
    n = u.shape[0]
    pos = lax.broadcasted_iota(jnp.int32, (n, 1), 0) % GRID_W
    up = jnp.where(pos == 0, 0.0, pltpu.roll(u, 1, 0))
    dn = jnp.where(pos == GRID_W - 1, 0.0, pltpu.roll(u, n - 1, 0))
    return up * w[0:1] + u * w[1:2] + dn * w[2:3] + b


def _hy_fwd_kernel(ts_ref, ta_ref, x1_ref, v_ref, p_ref, q_ref, kn_ref, s_ref):
    seq = x1_ref.shape[1]
    n, half = 2 * seq, seq // 2
    u_sym, u_anti, u_mid = _fold((v_ref[0].astype(F32) * x1_ref[0].astype(F32)).astype(BF16))
    sgn = _alternating(half)
    nyq = (jnp.sum(sgn * u_sym.astype(F32), axis=0, keepdims=True) + u_mid) * kn_ref[...] * (1.0 / n)
    for r0 in range(0, half, HY_ROW_CHUNK):
        r1 = r0 + HY_ROW_CHUNK
        mid = sgn[r0:r1] * u_mid
        re_e = _dot(ts_ref[r0:r1, :], u_sym) + mid
        im_o = _dot(ts_ref[half + r0:half + r1, :], u_sym) + mid
        im_e = _dot(ta_ref[r0:r1, :], u_anti)
        re_o = _dot(ta_ref[half + r0:half + r1, :], u_anti)
        pe, po = p_ref[r0:r1, :], p_ref[half + r0:half + r1, :]
        qe, qo = q_ref[r0:r1, :], q_ref[half + r0:half + r1, :]
        s_re_e = (re_e * pe + im_e * qe) * (2.0 / n)
        s_im_e = (im_e * pe - re_e * qe) * (2.0 / n)
        if r0 == 0:
            pos = lax.broadcasted_iota(jnp.int32, (HY_ROW_CHUNK, 1), 0)
            s_re_e = jnp.where(pos == 0, re_e * pe * (1.0 / n), s_re_e)
            s_im_e = jnp.where(pos == 0, nyq, s_im_e)
        s_ref[0, r0:r1, :] = s_re_e.astype(BF16)
        s_ref[0, half + r0:half + r1, :] = ((im_o * po - re_o * qo) * (2.0 / n)).astype(BF16)
        s_ref[0, seq + r0:seq + r1, :] = s_im_e.astype(BF16)
        s_ref[0, seq + half + r0:seq + half + r1, :] = ((re_o * po + im_o * qo) * (2.0 / n)).astype(BF16)


def _hy_inv_kernel(gs_ref, ga_ref, s_ref, x0_ref, o_ref):
    seq = x0_ref.shape[1]
    half = seq // 2
    x0 = x0_ref[0].astype(F32)
    sgn = _alternating(half)
    nyq = sgn * s_ref[0, seq:seq + 1, :].astype(F32)
    mirrored = []
    for r0 in range(0, half, HY_ROW_CHUNK):
        r1 = r0 + HY_ROW_CHUNK
        sym = _dot(gs_ref[r0:r1, :], s_ref[0, :seq, :]) + nyq[r0:r1]
        anti = _dot(ga_ref[r0:r1, :], s_ref[0, seq:, :])
        o_ref[0, r0:r1, :] = (x0[r0:r1] * (sym + anti)).astype(o_ref.dtype)
        mirrored.append(sym - anti)
    centre = (jnp.sum(sgn * (s_ref[0, :half, :].astype(F32) + s_ref[0, half:seq, :].astype(F32)), axis=0, keepdims=True)
              + nyq[0:1])
    pos = lax.broadcasted_iota(jnp.int32, (half, 1), 0)
    upper = _negate_index(jnp.where(pos == 0, centre, jnp.concatenate(mirrored, axis=0)).astype(BF16))
    o_ref[0, half:, :] = (x0[half:] * upper).astype(o_ref.dtype)


def _hyena_conv(hy, tabs, p_spec, q_spec, k_nyq):
    bsz, seq, _ = hy.shape
    nc = HY_WIDTH // HY_CW
    t_sym, t_anti, g_sym, g_anti = tabs
    sig = lambda part: pl.BlockSpec((1, seq, HY_CW), lambda j, b: (b, 0, part * nc + j))
    spec = lambda rows: pl.BlockSpec((rows, HY_CW), lambda j, b: (0, j))
    s = pl.pallas_call(
        _hy_fwd_kernel,
        grid=(nc, bsz),
        in_specs=[_resident(t_sym.shape), _resident(t_anti.shape), sig(1), sig(2),
                  spec(seq), spec(seq), spec(1)],
        out_specs=pl.BlockSpec((1, 2 * seq, HY_CW), lambda j, b: (b, 0, j)),
        out_shape=jax.ShapeDtypeStruct((bsz, 2 * seq, HY_WIDTH), BF16),
        compiler_params=_params("parallel", "parallel"),
        name="hy_fwd",
    )(t_sym, t_anti, hy, hy, p_spec, q_spec, k_nyq)
    return pl.pallas_call(
        _hy_inv_kernel,
        grid=(nc, bsz),
        in_specs=[_resident(g_sym.shape), _resident(g_anti.shape),
                  pl.BlockSpec((1, 2 * seq, HY_CW), lambda j, b: (b, 0, j)), sig(0)],
        out_specs=pl.BlockSpec((1, seq, HY_CW), lambda j, b: (b, 0, j)),
        out_shape=jax.ShapeDtypeStruct((bsz, seq, HY_WIDTH), BF16),
        compiler_params=_params("parallel", "parallel"),
        name="hy_inv",
    )(g_sym, g_anti, s, hy)


ROW_PARTS = D_MODEL // LANES


def _token_tile(t):
    return pl.ds(pl.multiple_of(t * ROW_PARTS, ROW_PARTS), ROW_PARTS)


def _load_token_rows(ref, n_tok):
    return jnp.concatenate([ref[pl.ds(j, n_tok, stride=ROW_PARTS), :] for j in range(ROW_PARTS)], axis=-1)


def _store_token_rows(ref, val):
    for j in range(ROW_PARTS):
        ref[pl.ds(j, val.shape[0], stride=ROW_PARTS), :] = val[:, j * LANES:(j + 1) * LANES]


POST_TILE = 512


def _post_kernel(of_ref, ob_ref, og_ref, yh_ref, ma_ref, mb_ref, x_ref, g1_ref, sc2_ref, sh2_ref,
                 gn_ref, nf_ref, wa_ref, wb_ref, wo_ref, wr_ref, br_ref,
                 x1_ref, h2_ref, idx_ref, wts_ref, cnt_ref):
    o = of_ref[0].astype(F32) + ob_ref[0].astype(F32)
    gn = gn_ref[...]
    heads = [_rms(o[:, h * GLA_HEAD_V:(h + 1) * GLA_HEAD_V], gn) for h in range(GLA_HEADS)]
    og = og_ref[0].astype(F32)
    a_in = jnp.concatenate(heads, axis=-1) * (og * _sigmoid(og))
    branch_a = _dot(a_in.astype(BF16), wa_ref[...])
    branch_b = _dot(yh_ref[0], wb_ref[...])
    y = _sigmoid(ma_ref[0].astype(F32)) * branch_a + _sigmoid(mb_ref[0].astype(F32)) * branch_b
    x1 = x_ref[0] + g1_ref[0] * _dot(y.astype(BF16), wo_ref[...])
    x1_ref[0] = x1
    h2 = _rms(x1, nf_ref[...]) * (1.0 + sc2_ref[0]) + sh2_ref[0]
    _store_token_rows(h2_ref.at[0], h2)

    h_hi, h_lo = _split(h2)
    w_hi, w_lo = _split(wr_ref[...])
    both = _dot(h_hi, jnp.concatenate([w_hi, w_lo], axis=-1))
    logits = both[:, :N_EXPERTS] + (both[:, N_EXPERTS:] + _dot(h_lo, w_hi)) + br_ref[...]
    lane = lax.broadcasted_iota(jnp.int32, logits.shape, 1)
    slot = lax.broadcasted_iota(jnp.int32, (logits.shape[0], TOP_K), 1)
    work = logits
    tops = []
    idx = jnp.zeros((logits.shape[0], TOP_K), jnp.int32)
    chosen = jnp.zeros(logits.shape, F32)
    for k in range(TOP_K):
        m = jnp.max(work, axis=-1, keepdims=True)
        first = jnp.min(jnp.where(work == m, lane, N_EXPERTS), axis=-1, keepdims=True)
        tops.append(m)
        idx = jnp.where(slot == k, first, idx)
        chosen = jnp.where(lane == first, 1.0, chosen)
        work = jnp.where(lane == first, -jnp.inf, work)
    es = [jnp.exp(m - tops[0]) for m in tops]
    inv = 1.0 / functools.reduce(lambda a, b: a + b, es)
    wts = jnp.zeros((logits.shape[0], TOP_K), F32)
    for k, e in enumerate(es):
        wts = jnp.where(slot == k, e * inv, wts)
    idx_ref[0] = idx
    wts_ref[0] = wts

    @pl.when((pl.program_id(0) == 0) & (pl.program_id(1) == 0))
    def _():
        cnt_ref[...] = jnp.zeros_like(cnt_ref)

    cnt_ref[...] += jnp.sum(chosen, axis=0, keepdims=True)


def _post(o_f, o_b, og, y_h, m_a, m_b, x, mods, gla_norm, norm_ffn, w_a, w_b, w_o, w_router, b_router):
    bsz, seq, _ = x.shape
    tok = lambda n: pl.BlockSpec((1, POST_TILE, n), lambda b, i: (b, i, 0))
    mod = lambda which: pl.BlockSpec((1, 1, D_MODEL), lambda b, i: (b, 0, which))
    in_specs = [tok(D_MODEL)] * 7 + [mod(MOD_G1), mod(MOD_SC2), mod(MOD_SH2),
                                     _resident((1, GLA_HEAD_V)), _resident((1, D_MODEL)),
                                     _resident(w_a.shape), _resident(w_b.shape), _resident(w_o.shape),
                                     _resident(w_router.shape), _resident((1, N_EXPERTS))]
    return pl.pallas_call(
        _post_kernel,
        grid=(bsz, seq // POST_TILE),
        in_specs=in_specs,
        out_specs=[tok(D_MODEL), pl.BlockSpec((1, POST_TILE * ROW_PARTS, LANES), lambda b, i: (b, i, 0)),
                   tok(TOP_K), tok(TOP_K), pl.BlockSpec((1, N_EXPERTS), lambda b, i: (0, 0))],
        out_shape=[jax.ShapeDtypeStruct((bsz, seq, D_MODEL), F32),
                   jax.ShapeDtypeStruct((bsz, seq * ROW_PARTS, LANES), F32),
                   jax.ShapeDtypeStruct((bsz, seq, TOP_K), jnp.int32), jax.ShapeDtypeStruct((bsz, seq, TOP_K), F32),
                   jax.ShapeDtypeStruct((1, N_EXPERTS), F32)],
        compiler_params=_params("arbitrary", "arbitrary"),
        name="post",
    )(o_f, o_b, og, y_h, m_a, m_b, x, mods, mods, mods, gla_norm.reshape(1, -1), norm_ffn.reshape(1, -1),
      w_a, w_b, w_o, w_router, b_router.reshape(1, -1))


UP_GROUP = 2 * LANES
N_UP_GROUPS = 2 * D_MODEL // UP_GROUP


def _stage_expert_weights(wu_ref, wd_ref, wu_s, wd_s):
    r = lax.broadcasted_iota(jnp.int32, (UP_GROUP, UP_GROUP), 0)
    c = lax.broadcasted_iota(jnp.int32, (UP_GROUP, UP_GROUP), 1)
    perm = jnp.where(c == (r % 2) * LANES + r // 2, 1.0, 0.0).astype(BF16)
    for g in range(N_UP_GROUPS):
        cols = slice(g * UP_GROUP, (g + 1) * UP_GROUP)
        wu_s[:, cols] = _dot(wu_ref[:, cols].astype(BF16), perm).astype(BF16)
    wd_s[...] = wd_ref[...].astype(BF16)


RANK_TILE = 1024
ROUTE_TILE = 512
EXPERT_TILE = 512
EXPERT_BLOCK = 512
DOWN_COLS_PER_DOT = 256
COMBINE_TILE = 256


def _exact_count_dot(a, m):
    a0 = a.astype(BF16)
    r1 = a - a0.astype(F32)
    a1 = r1.astype(BF16)
    a2 = (r1 - a1.astype(F32)).astype(BF16)
    return _dot(a0, m) + (_dot(a1, m) + _dot(a2, m))


def _route_kernel(idx_ref, cnt_ref, pos_ref, base, offs):
    i = pl.program_id(0)
    n_tok = idx_ref.shape[0]
    idx = idx_ref[...]
    lane = lax.broadcasted_iota(jnp.int32, (n_tok, N_EXPERTS), 1)
    hot = jnp.zeros((n_tok, N_EXPERTS), F32)
    for k in range(TOP_K):
        hot = hot + jnp.where(lane == idx[:, k:k + 1], 1.0, 0.0)

    @pl.when(i == 0)
    def _():
        r = lax.broadcasted_iota(jnp.int32, (N_EXPERTS, N_EXPERTS), 0)
        c = lax.broadcasted_iota(jnp.int32, (N_EXPERTS, N_EXPERTS), 1)
        before = jnp.where(r < c, 1.0, 0.0).astype(BF16)
        offs[...] = _exact_count_dot(jnp.broadcast_to(cnt_ref[...], (8, N_EXPERTS)), before)[0:1]
        base[...] = jnp.zeros_like(base)

    r = lax.broadcasted_iota(jnp.int32, (n_tok, n_tok), 0)
    c = lax.broadcasted_iota(jnp.int32, (n_tok, n_tok), 1)
    earlier = jnp.where(c < r, 1.0, 0.0).astype(BF16)
    dense = offs[...] + base[...] + _dot(earlier, hot.astype(BF16))
    slot = lax.broadcasted_iota(jnp.int32, (n_tok, TOP_K), 1)
    pos = jnp.zeros((n_tok, TOP_K), F32)
    for k in range(TOP_K):
        mine = jnp.sum(jnp.where(lane == idx[:, k:k + 1], dense, 0.0), axis=-1, keepdims=True)
        pos = jnp.where(slot == k, mine, pos)
    pos_ref[...] = pos.astype(jnp.int32)
    base[...] += jnp.sum(hot, axis=0, keepdims=True)


def _route(idx, counts):
    n_tok = idx.shape[0]
    return pl.pallas_call(
        _route_kernel,
        grid=(n_tok // RANK_TILE,),
        in_specs=[pl.BlockSpec((RANK_TILE, TOP_K), lambda i: (i, 0)),
                  pl.BlockSpec((1, N_EXPERTS), lambda i: (0, 0))],
        out_specs=pl.BlockSpec((RANK_TILE, TOP_K), lambda i: (i, 0)),
        out_shape=jax.ShapeDtypeStruct((n_tok, TOP_K), jnp.int32),
        scratch_shapes=[pltpu.VMEM((1, N_EXPERTS), F32)] * 2,
        compiler_params=_params("arbitrary"),
        name="route",
    )(idx, counts)


def _start_rows(n_tok, copy):
    def issue(t, carry):
        for k in range(TOP_K):
            copy(t, k).start(priority=k % 2)
        return carry

    lax.fori_loop(0, n_tok, issue, 0, unroll=4)


def _wait_rows(n_tok, copy):
    def drain(t, carry):
        for k in range(TOP_K):
            copy(t, k).wait()
        return carry

    lax.fori_loop(0, n_tok, drain, 0, unroll=4)


def _dispatch_kernel(pos_ref, h_ref, xs_ref, sem):
    def copy(t, k):
        return pltpu.make_async_copy(h_ref.at[_token_tile(t), :], xs_ref.at[_token_tile(pos_ref[t * TOP_K + k]), :], sem)

    _start_rows(ROUTE_TILE, copy)
    _wait_rows(ROUTE_TILE, copy)


def _index_blocks(tile, n_steps):
    blk = (tile * TOP_K,)
    return [pl.BlockSpec(blk, lambda i: (i,), memory_space=pltpu.SMEM),
            pl.BlockSpec(blk, lambda i: (jnp.minimum(i + 1, n_steps - 1),), memory_space=pltpu.SMEM)]


def _dispatch(pos_flat, h2):
    n_tok = h2.shape[0] // ROW_PARTS
    n_steps = n_tok // ROUTE_TILE
    return pl.pallas_call(
        _dispatch_kernel,
        grid=(n_steps,),
        in_specs=[_index_blocks(ROUTE_TILE, n_steps)[0],
                  pl.BlockSpec((ROUTE_TILE * ROW_PARTS, LANES), lambda i: (i, 0))],
        out_specs=pl.BlockSpec(memory_space=pl.ANY),
        out_shape=jax.ShapeDtypeStruct((n_tok * TOP_K * ROW_PARTS, LANES), F32),
        scratch_shapes=[pltpu.SemaphoreType.DMA],
        compiler_params=_params("arbitrary"),
        name="dispatch",
    )(pos_flat, h2)


def _experts_kernel(tile_ref, exp_ref, nxt_ref, lo_ref, hi_ref, x_ref, wu_hbm, bu_ref, wd_hbm, bd_ref, o_ref,
                    acc_ref, wu_s, wd_s, wu_f, wd_f, sem):
    w = pl.program_id(0)
    lo = lo_ref[w]
    hi = hi_ref[w]

    def fetch(e):
        return (pltpu.make_async_copy(wu_hbm.at[e], wu_f, sem.at[0]),
                pltpu.make_async_copy(wd_hbm.at[e], wd_f, sem.at[1]))

    @pl.when(w == 0)
    def _():
        acc_ref[...] = jnp.zeros_like(acc_ref)
        for copy in fetch(exp_ref[0]):
            copy.start()

    @pl.when((w == 0) | (exp_ref[w] != exp_ref[jnp.maximum(w - 1, 0)]))
    def _():
        for copy in fetch(exp_ref[w]):
            copy.wait()
        _stage_expert_weights(wu_f, wd_f, wu_s, wd_s)

        @pl.when(nxt_ref[w] >= 0)
        def _():
            for copy in fetch(nxt_ref[w]):
                copy.start()

    def block(r0):
        first = tile_ref[w] * EXPERT_TILE + r0
        rows = slice(r0, r0 + EXPERT_BLOCK)

        @pl.when((hi > lo) & (lo < first + EXPERT_BLOCK) & (hi > first))
        def _():
            x_rows = x_ref.at[pl.ds(r0 * ROW_PARTS, EXPERT_BLOCK * ROW_PARTS), :]
            h = _load_token_rows(x_rows, EXPERT_BLOCK).astype(BF16)
            acts = []
            for g in range(N_UP_GROUPS):
                cols = slice(g * UP_GROUP, (g + 1) * UP_GROUP)
                u = _dot(h, wu_s[:, cols]) + bu_ref[0, :, cols]
                glu = jnp.minimum(u[:, :LANES], SWIGLU_LIMIT)
                lin = jnp.clip(u[:, LANES:], -SWIGLU_LIMIT, SWIGLU_LIMIT)
                acts.append((glu * _sigmoid(SWIGLU_ALPHA * glu) * (lin + 1.0)).astype(BF16))
            a = jnp.concatenate(acts, axis=-1)
            row = first + lax.broadcasted_iota(jnp.int32, (EXPERT_BLOCK, 1), 0)
            mine = (row >= lo) & (row < hi)
            for c0 in range(0, D_MODEL, DOWN_COLS_PER_DOT):
                cols = slice(c0, c0 + DOWN_COLS_PER_DOT)
                kept = jnp.where(mine, _dot(a, wd_s[:, cols]) + bd_ref[0, :, cols], acc_ref[rows, cols])
                acc_ref[rows, cols] = kept
                for j in range(DOWN_COLS_PER_DOT // LANES):
                    part = pl.ds(r0 * ROW_PARTS + c0 // LANES + j, EXPERT_BLOCK, stride=ROW_PARTS)
                    o_ref[part, :] = kept[:, j * LANES:(j + 1) * LANES]

    for r0 in range(0, EXPERT_TILE, EXPERT_BLOCK):
        block(r0)


def _work_items(counts, n_rows):
    n_tiles = n_rows // EXPERT_TILE
    ends = jnp.cumsum(counts)
    tile_ends = jnp.arange(1, n_tiles + 1, dtype=jnp.int32) * EXPERT_TILE
    n_items = n_tiles + N_EXPERTS
    count = lambda cond: jnp.sum(cond.astype(jnp.int32), axis=1)
    slot_t = jnp.arange(n_tiles, dtype=jnp.int32) + count(ends[None, :] < tile_ends[:, None])
    slot_e = jnp.arange(N_EXPERTS, dtype=jnp.int32) + count(tile_ends[None, :] <= ends[:, None])
    slots = jnp.concatenate([slot_t, slot_e])
    vals = jnp.concatenate([tile_ends, ends])
    item = jnp.arange(n_items, dtype=jnp.int32)
    hi = jnp.sum(jnp.where(slots[None, :] == item[:, None], vals[None, :], 0), axis=1)
    lo = jnp.concatenate([jnp.zeros((1,), jnp.int32), hi[:-1]])
    tile_id = jnp.minimum(lo // EXPERT_TILE, n_tiles - 1)
    exp_id = jnp.minimum(count(ends[None, :] <= lo[:, None]), N_EXPERTS - 1)
    later = jnp.where(exp_id[None, :] > exp_id[:, None], exp_id[None, :], N_EXPERTS)
    nxt_id = jnp.min(later, axis=1)
    nxt_id = jnp.where(nxt_id == N_EXPERTS, -1, nxt_id)
    return tile_id, exp_id, nxt_id, lo, hi


def _experts(items, xs, w_up, b_up_g, w_down, b_down):
    n_items = items[0].shape[0]
    rows = lambda w, t, e, nxt, lo, hi: (t[w], 0)
    bias = lambda c: pl.BlockSpec((1, 1, c), lambda w, t, e, nxt, lo, hi: (e[w], 0, 0))
    hbm = pl.BlockSpec(memory_space=pl.ANY)
    return pl.pallas_call(
        _experts_kernel,
        grid_spec=pltpu.PrefetchScalarGridSpec(
            num_scalar_prefetch=5,
            grid=(n_items,),
            in_specs=[pl.BlockSpec((EXPERT_TILE * ROW_PARTS, LANES), rows),
                      hbm, bias(2 * D_MODEL), hbm, bias(D_MODEL)],
            out_specs=pl.BlockSpec((EXPERT_TILE * ROW_PARTS, LANES), rows),
            scratch_shapes=[pltpu.VMEM((EXPERT_TILE, D_MODEL), F32),
                            pltpu.VMEM((D_MODEL, 2 * D_MODEL), BF16), pltpu.VMEM((D_MODEL, D_MODEL), BF16),
                            pltpu.VMEM((D_MODEL, 2 * D_MODEL), F32), pltpu.VMEM((D_MODEL, D_MODEL), F32),
                            pltpu.SemaphoreType.DMA((2,))]),
        out_shape=jax.ShapeDtypeStruct(xs.shape, F32),
        compiler_params=_params("arbitrary"),
        name="experts",
    )(*items, xs, w_up, b_up_g, w_down, b_down)


def _combine_kernel(pos_ref, nxt_ref, ys_ref, wts_ref, x1_ref, g2_ref, nf_ref, o_ref, buf_a, buf_b, sem):
    i = pl.program_id(0)
    sub = COMBINE_TILE

    def gather(p_ref, first_tok, buf, s):
        def copy(t, k):
            src = ys_ref.at[_token_tile(p_ref[(first_tok + t) * TOP_K + k]), :]
            return pltpu.make_async_copy(src, buf.at[k, _token_tile(t), :], sem.at[s])
        return copy

    this_a = gather(pos_ref, 0, buf_a, 0)
    this_b = gather(pos_ref, sub, buf_b, 1)
    next_a = gather(nxt_ref, 0, buf_a, 0)

    def reduce_rows(buf, half, start_other):
        rows = slice(half * sub, (half + 1) * sub)
        wts = wts_ref[rows, :]
        parts = []
        for j in range(ROW_PARTS):
            for t in range(j * sub // ROW_PARTS, (j + 1) * sub // ROW_PARTS):
                for k in range(TOP_K):
                    start_other(t, k).start(priority=k % 2)
            part = wts[:, 0:1] * buf[0, pl.ds(j, sub, stride=ROW_PARTS), :]
            for k in range(1, TOP_K):
                part = part + wts[:, k:k + 1] * buf[k, pl.ds(j, sub, stride=ROW_PARTS), :]
            parts.append(part)
        acc = jnp.concatenate(parts, axis=-1)
        o_ref[rows, :] = _rms(x1_ref[rows, :] + g2_ref[0] * acc, nf_ref[...])

    @pl.when(i == 0)
    def _():
        _start_rows(sub, this_a)

    _wait_rows(sub, this_a)
    reduce_rows(buf_a, 0, this_b)
    _wait_rows(sub, this_b)
    reduce_rows(buf_b, 1, next_a)

    @pl.when(i == pl.num_programs(0) - 1)
    def _():
        _wait_rows(sub, next_a)


def _combine(pos_flat, ys, wts, x1, mods, norm_final, seq):
    n_tok = x1.shape[0]
    step_tok = 2 * COMBINE_TILE
    steps_per_sample = seq // step_tok
    tok = lambda n: pl.BlockSpec((step_tok, n), lambda i: (i, 0))
    n_steps = n_tok // step_tok
    gather_buf = pltpu.VMEM((TOP_K, COMBINE_TILE * ROW_PARTS, LANES), F32)
    return pl.pallas_call(
        _combine_kernel,
        grid=(n_steps,),
        in_specs=_index_blocks(step_tok, n_steps) + [
            pl.BlockSpec(memory_space=pl.ANY), tok(TOP_K), tok(D_MODEL),
            pl.BlockSpec((1, 1, D_MODEL), lambda i: (i // steps_per_sample, 0, MOD_G2)),
            pl.BlockSpec((1, D_MODEL), lambda i: (0, 0))],
        out_specs=tok(D_MODEL),
        out_shape=jax.ShapeDtypeStruct((n_tok, D_MODEL), F32),
        scratch_shapes=[gather_buf, gather_buf, pltpu.SemaphoreType.DMA((2,))],
        compiler_params=_params("arbitrary"),
        name="combine",
    )(pos_flat, pos_flat, ys, wts, x1, mods, norm_final.reshape(1, -1))


def _moe(h2, idx, wts, counts, w_up, b_up_g, w_down, b_down, x1, mods, norm_final):
    bsz, seq, _ = x1.shape
    n_tok = bsz * seq
    pos_flat = _route(idx.reshape(n_tok, TOP_K), counts).reshape(-1)
    xs = _dispatch(pos_flat, h2.reshape(n_tok * ROW_PARTS, LANES))
    items = _work_items(counts.reshape(-1).astype(jnp.int32), n_tok * TOP_K)
    ys = _experts(items, xs, w_up, b_up_g, w_down, b_down)
    out = _combine(pos_flat, ys, wts.reshape(n_tok, TOP_K), x1.reshape(n_tok, D_MODEL), mods, norm_final, seq)
    return out.reshape(bsz, seq, D_MODEL)


def kernel(x, c, ctx, c_ctx, w_ada, b_ada, norm_mix, norm_ffn, w_in, w_gk_f, b_gk_f, w_gk_b, b_gk_b, gla_norm, w_gla_out, hy_conv_w, hy_conv_b, hy_f_w1, hy_f_b1, hy_f_w2, hy_f_b2, hy_f_w3, hy_f_b3, hy_f_w4, hy_sin_freq, hy_bias, w_hy_out, w_out, w_router, b_router, w_up, b_up, w_down, b_down, norm_final):
    depth = w_ada.shape[0]
    assert depth == 1, "single-layer block: the context stream only feeds later layers"
    bsz, seq, _ = x.shape

    n_rows = -(-(bsz + 1) // 8) * 8
    cc = jnp.zeros((n_rows, D_MODEL), F32).at[:bsz].set(c).at[bsz].set(c_ctx)
    mods = _ada(cc, w_ada[0], b_ada[0]).reshape(n_rows, 1, N_MOD * D_MODEL)

    sizes = (GLA_DK, GLA_DK, GLA_DV, GLA_DV, GLA_GATE_RANK, GLA_GATE_RANK, 3 * HY_WIDTH, D_MODEL, D_MODEL)
    offs = np.concatenate([[0], np.cumsum(sizes)])
    w_in_b = w_in[0].astype(BF16)
    wq, wk, wv, wog, wrf, wrb, why, wma, wmb = [w_in_b[:, offs[i]:offs[i + 1]] for i in range(len(sizes))]
    q, k, v, og, r_f, r_b, hy, m_a, m_b = _inproj(
        x, mods, lambda b: b, norm_mix[0], [wq, wk, wv, wog, wrf, wrb, why, wma, wmb],
        [BF16, BF16, BF16, BF16, F32, F32, BF16, BF16, BF16], tl=512, conv=(6, hy_conv_w[0], hy_conv_b[0]))
    k_c, v_c, rc_f, rc_b = _inproj(
        ctx, mods, lambda b: bsz, norm_mix[0], [wk, wv, wrf, wrb], [BF16, BF16, F32, F32], tl=ctx.shape[1])

    w_gk = jnp.stack([w_gk_f[0], w_gk_b[0]])
    b_gk = jnp.stack([b_gk_f[0], b_gk_b[0]])[:, None, :]
    o_f, o_b = _gla(q, k, v, r_f, r_b, k_c, v_c, rc_f, rc_b, w_gk, b_gk)

    tabs = [jnp.asarray(t).astype(BF16) for t in _dft_tables(seq)]
    t_sym, t_anti = tabs[:2]
    hs, hd, pn = _hyena_filter(seq, hy_f_w1[0], hy_f_b1[0], hy_f_w2[0], hy_f_b2[0], hy_f_w3[0], hy_f_b3[0],
                               hy_f_w4[0], hy_sin_freq[0])
    p_spec, q_spec, k_nyq = _hyena_spectrum(t_sym, t_anti, hs, hd, pn, hy_bias[0])
    y_h = _hyena_conv(hy, tabs, p_spec, q_spec, k_nyq)

    x1, h2, idx, wts, counts = _post(o_f, o_b, og, y_h, m_a, m_b, x, mods, gla_norm[0], norm_ffn[0],
                                     w_gla_out[0].astype(BF16), w_hy_out[0].astype(BF16), w_out[0].astype(BF16),
                                     w_router[0], b_router[0])

    b_up_g = b_up[0].reshape(N_EXPERTS, N_UP_GROUPS, LANES, 2).transpose(0, 1, 3, 2).reshape(N_EXPERTS, 1, -1)
    return _moe(h2, idx, wts, counts, w_up[0], b_up_g, w_down[0], b_down[0][:, None, :],
                x1, mods, norm_final)
```

```python
import functools
import math

import jax
import jax.numpy as jnp
import numpy as np
from jax import lax
from jax.experimental import pallas as pl
from jax.experimental.pallas import tpu as pltpu

F32 = jnp.float32
BF16 = jnp.bfloat16

D_MODEL = 1024
GRID_W = 64
EPS = 1e-6
N_MOD = 6

GLA_HEADS = 4
GLA_HEAD_K = 128
GLA_HEAD_V = 256
GLA_DK = GLA_HEADS * GLA_HEAD_K
GLA_DV = GLA_HEADS * GLA_HEAD_V
GLA_GATE_RANK = 16
GLA_GATE_NORM = 16.0
GLA_CHUNK = 64

HY_WIDTH = D_MODEL
HY_EMB = 33
HY_FAST_DECAY = 0.3
HY_SLOW_DECAY = 1.5
HY_TARGET = 1e-2

N_EXPERTS = 32
TOP_K = 4
SWIGLU_LIMIT = 7.0
SWIGLU_ALPHA = 1.702

V7X_VMEM_BYTES = 64 * 1024 * 1024
VMEM_LIMIT = V7X_VMEM_BYTES - 8 * 1024 * 1024
LANES = 128

MOD_SH1, MOD_SC1, MOD_G1, MOD_SH2, MOD_SC2, MOD_G2 = range(N_MOD)


def _params(*sem):
    return pltpu.CompilerParams(dimension_semantics=sem, vmem_limit_bytes=VMEM_LIMIT)


def _resident(shape):
    return pl.BlockSpec(shape, lambda *_: (0,) * len(shape), pipeline_mode=pl.Buffered(1))


def _dot(a, b):
    return jnp.dot(a, b, preferred_element_type=F32)


def _dot_nt(a, b):
    return lax.dot_general(a, b, (((1,), (1,)), ((), ())), preferred_element_type=F32)


def _dot_tn(a, b):
    return lax.dot_general(a, b, (((0,), (0,)), ((), ())), preferred_element_type=F32)


def _split(a):
    hi = a.astype(BF16)
    lo = (a - hi.astype(F32)).astype(BF16)
    return hi, lo


def _dot3(a, b):
    ah, al = _split(a)
    bh, bl = _split(b)
    return _dot(ah, bh) + (_dot(ah, bl) + _dot(al, bh))


def _sigmoid(x):
    return 1.0 / (1.0 + jnp.exp(-x))


def _log_sigmoid(x):
    return jnp.minimum(x, 0.0) - jnp.log(1.0 + jnp.exp(-jnp.abs(x)))


def _rms(x, w):
    return x * lax.rsqrt(jnp.mean(x * x, axis=-1, keepdims=True) + EPS) * w


def _ada_kernel(c_ref, w_ref, b_ref, o_ref):
    c = c_ref[...]
    o_ref[...] = _dot3(c * _sigmoid(c), w_ref[...]) + b_ref[...]


def _ada(cc, w_ada, b_ada):
    rows = cc.shape[0]
    return pl.pallas_call(
        _ada_kernel,
        grid=(N_MOD,),
        in_specs=[pl.BlockSpec((rows, D_MODEL), lambda j: (0, 0)),
                  pl.BlockSpec((D_MODEL, D_MODEL), lambda j: (0, j)),
                  pl.BlockSpec((1, D_MODEL), lambda j: (0, j))],
        out_specs=pl.BlockSpec((rows, D_MODEL), lambda j: (0, j)),
        out_shape=jax.ShapeDtypeStruct((rows, N_MOD * D_MODEL), F32),
        compiler_params=_params("parallel"),
        name="ada",
    )(cc, w_ada, b_ada.reshape(1, -1))


INPROJ_COL_CHUNK = 512


def _inproj_kernel(n_out, conv_out, x_ref, sc_ref, sh_ref, nw_ref, *refs):
    n_in = n_out + (0 if conv_out is None else 2)
    w_refs, o_refs = refs[:n_out], refs[n_in:]
    h = _rms(x_ref[0], nw_ref[...]) * (1.0 + sc_ref[0]) + sh_ref[0]
    hb = h.astype(BF16)
    for i, (w_ref, o_ref) in enumerate(zip(w_refs, o_refs)):
        n = w_ref.shape[1]
        for c0 in range(0, n, INPROJ_COL_CHUNK):
            c1 = min(c0 + INPROJ_COL_CHUNK, n)
            y = _dot(hb, w_ref[:, c0:c1])
            if i == conv_out:
                cw_ref, cb_ref = refs[n_out:n_in]
                y = _short_conv(y, cw_ref[:, c0:c1], cb_ref[:, c0:c1])
            o_ref[0, :, c0:c1] = y.astype(o_ref.dtype)


def _inproj(x, mods, mod_row, norm_w, weights, out_dtypes, tl, conv=None):
    bsz, seq, _ = x.shape
    n_out = len(weights)
    assert tl % GRID_W == 0
    in_specs = [pl.BlockSpec((1, tl, D_MODEL), lambda b, i: (b, i, 0)),
                pl.BlockSpec((1, 1, D_MODEL), lambda b, i: (mod_row(b), 0, MOD_SC1)),
                pl.BlockSpec((1, 1, D_MODEL), lambda b, i: (mod_row(b), 0, MOD_SH1)),
                _resident((1, D_MODEL))]
    in_specs += [_resident(w.shape) for w in weights]
    extra = []
    if conv is not None:
        extra = [conv[1], conv[2].reshape(1, -1)]
        in_specs += [_resident(a.shape) for a in extra]
    out_specs = [pl.BlockSpec((1, tl, w.shape[1]), lambda b, i: (b, i, 0)) for w in weights]
    out_shape = [jax.ShapeDtypeStruct((bsz, seq, w.shape[1]), dt) for w, dt in zip(weights, out_dtypes)]
    return pl.pallas_call(
        functools.partial(_inproj_kernel, n_out, None if conv is None else conv[0]),
        grid=(bsz, seq // tl),
        in_specs=in_specs, out_specs=out_specs, out_shape=out_shape,
        compiler_params=_params("parallel", "parallel"),
        name="inproj",
    )(x, mods, mods, norm_w.reshape(1, -1), *weights, *extra)


GLA_TILE = 256
GLA_NCH = GLA_TILE // GLA_CHUNK
GLA_SCALE = GLA_HEAD_K ** -0.5
GLA_HPS = 4
GLA_KW = GLA_HPS * GLA_HEAD_K
GLA_VW = GLA_HPS * GLA_HEAD_V


@functools.lru_cache(maxsize=None)
def _chunk_triangles():
    i = np.arange(GLA_TILE)
    same = (i[:, None] // GLA_CHUNK) == (i[None, :] // GLA_CHUNK)
    return np.stack([same & (i[None, :] <= i[:, None]), same & (i[None, :] >= i[:, None])]).astype(np.float32)


def _gla_kernel(tri_ref, qf_ref, kf_ref, vf_ref, rf_ref, qb_ref, kb_ref, vb_ref, rb_ref,
                kc_ref, vc_ref, rcf_ref, rcb_ref, wgk_ref, bgk_ref,
                of_ref, ob_ref, sf_ref, sb_ref):
    t = pl.program_id(2)
    crow = lax.broadcasted_iota(jnp.int32, (GLA_CHUNK, GLA_CHUNK), 0)
    ccol = lax.broadcasted_iota(jnp.int32, (GLA_CHUNK, GLA_CHUNK), 1)
    heads = range(GLA_HPS)

    def rows(c):
        return slice(c * GLA_CHUNK, (c + 1) * GLA_CHUNK)

    def hk(h):
        return slice(h * GLA_HEAD_K, (h + 1) * GLA_HEAD_K)

    def hv(h):
        return slice(h * GLA_HEAD_V, (h + 1) * GLA_HEAD_V)

    def order(fwd):
        return range(GLA_NCH) if fwd else range(GLA_NCH - 1, -1, -1)

    def cum_decay(r, d):
        rh, rl = _split(r)
        wh, wl = _split(wgk_ref[d])
        z = _dot(jnp.concatenate([rh, rl, rh], axis=1), jnp.concatenate([wh, wh, wl], axis=0)) + bgk_ref[d]
        g = _log_sigmoid(z) * (1.0 / GLA_GATE_NORM)
        gh, gl = _split(g)
        return _dot(tri_ref[d], gh) + _dot(tri_ref[d], gl)

    def updates(k_ref, v_ref, b, fwd):
        out = {}
        for c in range(GLA_NCH):
            i = c * GLA_CHUNK + (GLA_CHUNK - 1 if fwd else 0)
            total = b[i:i + 1, :]
            k_upd = (k_ref[0, rows(c), :].astype(F32) * jnp.exp(total - b[rows(c)])).astype(BF16)
            dec = jnp.exp(total)
            for h in heads:
                out[c, h] = (dec[:, hk(h)], _dot_tn(v_ref[0, rows(c), hv(h)], k_upd[:, hk(h)]))
        return out

    def context_state(s_ref, r_ref, d, fwd):
        upd = updates(kc_ref, vc_ref, cum_decay(r_ref[0], d), fwd)
        for h in heads:
            st = jnp.zeros((GLA_HEAD_V, GLA_HEAD_K), F32)
            for c in order(fwd):
                dec, inc = upd[c, h]
                st = st * dec + inc
            s_ref[h] = st

    def scan(s_ref, q_ref, k_ref, v_ref, r_ref, o_ref, d, fwd):
        mask = (ccol <= crow) if fwd else (ccol >= crow)
        b = cum_decay(r_ref[0], d)
        q_dec = (q_ref[0].astype(F32) * (jnp.exp(b) * GLA_SCALE)).astype(BF16)
        k_inv = (k_ref[0].astype(F32) * jnp.exp(-b)).astype(BF16)
        upd = updates(k_ref, v_ref, b, fwd)
        intra = {}
        for c in range(GLA_NCH):
            for h in heads:
                att = jnp.where(mask, _dot_nt(q_dec[rows(c), hk(h)], k_inv[rows(c), hk(h)]), 0.0).astype(BF16)
                intra[c, h] = _dot(att, v_ref[0, rows(c), hv(h)])
        enter = {}
        for h in heads:
            st = s_ref[h]
            for c in order(fwd):
                enter[c, h] = st.astype(BF16)
                dec, inc = upd[c, h]
                st = st * dec + inc
            s_ref[h] = st
        for c in range(GLA_NCH):
            for h in heads:
                o = intra[c, h] + _dot_nt(q_dec[rows(c), hk(h)], enter[c, h])
                o_ref[0, rows(c), hv(h)] = o.astype(o_ref.dtype)

    @pl.when(t == 0)
    def _():
        context_state(sf_ref, rcf_ref, 0, True)
        context_state(sb_ref, rcb_ref, 1, False)

    scan(sf_ref, qf_ref, kf_ref, vf_ref, rf_ref, of_ref, 0, True)
    scan(sb_ref, qb_ref, kb_ref, vb_ref, rb_ref, ob_ref, 1, False)


def _gla(q, k, v, r_f, r_b, k_c, v_c, rc_f, rc_b, w_gk, b_gk):
    bsz, seq, _ = q.shape
    assert k_c.shape[1] == GLA_TILE
    nt = seq // GLA_TILE
    tri = jnp.asarray(_chunk_triangles()).astype(BF16)
    fwd = lambda b, h, t: (b, t, h)
    bwd = lambda b, h, t: (b, nt - 1 - t, h)
    fwd0 = lambda b, h, t: (b, t, 0)
    bwd0 = lambda b, h, t: (b, nt - 1 - t, 0)
    ctx = lambda b, h, t: (b, 0, h)
    ctx0 = lambda b, h, t: (b, 0, 0)
    kblk = (1, GLA_TILE, GLA_KW)
    vblk = (1, GLA_TILE, GLA_VW)
    rblk = (1, GLA_TILE, GLA_GATE_RANK)
    in_specs = [_resident(tri.shape),
                pl.BlockSpec(kblk, fwd), pl.BlockSpec(kblk, fwd), pl.BlockSpec(vblk, fwd), pl.BlockSpec(rblk, fwd0),
                pl.BlockSpec(kblk, bwd), pl.BlockSpec(kblk, bwd), pl.BlockSpec(vblk, bwd), pl.BlockSpec(rblk, bwd0),
                pl.BlockSpec(kblk, ctx), pl.BlockSpec(vblk, ctx), pl.BlockSpec(rblk, ctx0), pl.BlockSpec(rblk, ctx0),
                pl.BlockSpec((2, GLA_GATE_RANK, GLA_KW), lambda b, h, t: (0, 0, h)),
                pl.BlockSpec((2, 1, GLA_KW), lambda b, h, t: (0, 0, h))]
    out_specs = [pl.BlockSpec(vblk, fwd), pl.BlockSpec(vblk, bwd)]
    out_shape = [jax.ShapeDtypeStruct((bsz, seq, GLA_DV), BF16)] * 2
    return pl.pallas_call(
        _gla_kernel,
        grid=(bsz, GLA_HEADS // GLA_HPS, nt),
        in_specs=in_specs, out_specs=out_specs, out_shape=out_shape,
        scratch_shapes=[pltpu.VMEM((GLA_HPS, GLA_HEAD_V, GLA_HEAD_K), F32)] * 2,
        compiler_params=_params("parallel", "parallel", "arbitrary"),
        name="gla",
    )(tri, q, k, v, r_f, q, k, v, r_b, k_c, v_c, rc_f, rc_b, w_gk, b_gk)


HY_CW = 512
HY_ROW_CHUNK = 512
MLP_PAD = 128


@functools.lru_cache(maxsize=None)
def _dft_tables(seq):
    n, half = 2 * seq, seq // 2
    m = np.arange(half)[:, None]
    s = np.arange(half)[None, :]
    ang = 2.0 * np.pi * np.arange(n) / n
    cos = lambda k: np.cos(ang)[(k * s) % n]
    sin = lambda k: np.sin(ang)[(k * s) % n]
    sym = np.concatenate([cos(2 * m), sin(2 * m + 1)], axis=0)
    anti = np.concatenate([sin(2 * m), cos(2 * m + 1)], axis=0)
    tables = (sym, anti, sym.T, anti.T)
    return tuple(np.ascontiguousarray(t, dtype=np.float32) for t in tables)


@functools.lru_cache(maxsize=None)
def _filter_features(seq):
    bands = (HY_EMB - 1) // 2
    pos = np.arange(seq, dtype=np.float64)[:, None]
    t = pos / max(seq - 1, 1)
    f = np.linspace(1e-4, bands - 1, bands)[None]
    ang = (2.0 * math.pi / seq) * pos * f
    z = np.concatenate([t, np.cos(ang), -np.sin(ang)], axis=-1)
    out = np.zeros((seq, MLP_PAD), np.float32)
    out[:, :HY_EMB] = z
    deltas = np.abs(np.linspace(math.log(HY_TARGET) / HY_SLOW_DECAY, math.log(HY_TARGET) / HY_FAST_DECAY, HY_WIDTH))
    return out, deltas.astype(np.float32)[None]


def _filter_kernel(z_ref, w1_ref, b1_ref, w2_ref, b2_ref, w3_ref, b3_ref, fr_ref, w4f_ref, w4b_ref, dl_ref,
                   hs_ref, hd_ref, pn_ref, hid_ref):
    z = z_ref[...]

    @pl.when(pl.program_id(0) == 0)
    def _():
        fr = fr_ref[...]
        h = jnp.sin(fr * (_dot3(z, w1_ref[...]) + b1_ref[...]))
        h = jnp.sin(fr * (_dot3(h, w2_ref[...]) + b2_ref[...]))
        hid_ref[...] = jnp.sin(fr * (_dot3(h, w3_ref[...]) + b3_ref[...]))

    h = hid_ref[...]
    window = jnp.exp(-z[:, 0:1] * dl_ref[...])
    h_f = _dot3(h, w4f_ref[...]) * window
    h_b = _dot3(h, w4b_ref[...]) * window
    pos = lax.broadcasted_iota(jnp.int32, (z.shape[0], 1), 0)
    h_b = jnp.where(pos == 0, 0.0, h_b)
    hs = h_f + h_b
    hs_ref[...] = hs.astype(BF16)
    hd_ref[...] = (h_b - h_f).astype(BF16)
    sign = jnp.where(pos % 2 == 0, 1.0, -1.0)
    pn_ref[...] = jnp.sum(hs * sign, axis=0, keepdims=True)


def _pad2(a, rows, cols):
    return jnp.zeros((rows, cols), F32).at[:a.shape[0], :a.shape[1]].set(a.astype(F32))


def _hyena_filter(seq, w1, b1, w2, b2, w3, b3, w4, freq):
    z_np, deltas_np = _filter_features(seq)
    p = MLP_PAD
    args = [jnp.asarray(z_np), _pad2(w1, p, p), _pad2(b1[None], 1, p), _pad2(w2, p, p), _pad2(b2[None], 1, p),
            _pad2(w3, p, p), _pad2(b3[None], 1, p), _pad2(freq[None], 1, p),
            _pad2(w4[:, :HY_WIDTH], p, HY_WIDTH), _pad2(w4[:, HY_WIDTH:], p, HY_WIDTH), jnp.asarray(deltas_np)]
    full = lambda shape: pl.BlockSpec(shape, lambda j: (0, 0))
    colblk = lambda rows: pl.BlockSpec((rows, HY_CW), lambda j: (0, j))
    in_specs = [full((seq, p)), full((p, p)), full((1, p)), full((p, p)), full((1, p)), full((p, p)), full((1, p)),
                full((1, p)), colblk(p), colblk(p), colblk(1)]
    return pl.pallas_call(
        _filter_kernel,
        grid=(HY_WIDTH // HY_CW,),
        in_specs=in_specs,
        out_specs=[colblk(seq), colblk(seq), colblk(1)],
        out_shape=[jax.ShapeDtypeStruct((seq, HY_WIDTH), BF16), jax.ShapeDtypeStruct((seq, HY_WIDTH), BF16),
                   jax.ShapeDtypeStruct((1, HY_WIDTH), F32)],
        scratch_shapes=[pltpu.VMEM((seq, p), F32)],
        compiler_params=_params("arbitrary"),
        name="hy_filter",
    )(*args)


REV_BLOCK = 256


def _negate_index(x):
    n = x.shape[0]
    r = lax.broadcasted_iota(jnp.int32, (REV_BLOCK, REV_BLOCK), 0)
    c = lax.broadcasted_iota(jnp.int32, (REV_BLOCK, REV_BLOCK), 1)
    flip = jnp.where(r + c == REV_BLOCK - 1, 1.0, 0.0).astype(BF16)
    if x.dtype == BF16:
        pieces = [x]
    else:
        x0 = x.astype(BF16)
        r1 = x - x0.astype(F32)
        x1 = r1.astype(BF16)
        pieces = [x0, x1, (r1 - x1.astype(F32)).astype(BF16)]
    blocks = []
    for i in range(n // REV_BLOCK - 1, -1, -1):
        rows = slice(i * REV_BLOCK, (i + 1) * REV_BLOCK)
        acc = _dot(flip, pieces[0][rows])
        for p in pieces[1:]:
            acc = acc + _dot(flip, p[rows])
        blocks.append(acc)
    return pltpu.roll(jnp.concatenate(blocks, axis=0), 1, 0)


def _fold(x):
    half = x.shape[0] // 2
    lo = x[:half].astype(F32)
    hi = _negate_index(x[half:])
    pos = lax.broadcasted_iota(jnp.int32, (half, 1), 0)
    mid = hi[0:1]
    hi = jnp.where(pos == 0, 0.0, hi)
    return (lo + hi).astype(BF16), (lo - hi).astype(BF16), mid


def _alternating(n):
    pos = lax.broadcasted_iota(jnp.int32, (n, 1), 0)
    return jnp.where(pos % 2 == 0, 1.0, -1.0)


def _spectrum_kernel(ts_ref, ta_ref, hs_ref, hd_ref, pn_ref, bias_ref, p_ref, q_ref, kn_ref):
    half = hs_ref.shape[0] // 2
    bias = bias_ref[...]
    sgn = _alternating(half)
    s_sym, s_anti, s_mid = _fold(hs_ref[...])
    d_sym, d_anti, d_mid = _fold(hd_ref[...])
    p_ref[:half, :] = _dot(ts_ref[:half, :], s_sym) + sgn * s_mid + bias
    p_ref[half:, :] = _dot(ta_ref[half:, :], s_anti) + bias
    q_ref[:half, :] = _dot(ta_ref[:half, :], d_anti)
    q_ref[half:, :] = _dot(ts_ref[half:, :], d_sym) + sgn * d_mid
    kn_ref[...] = pn_ref[...] + bias


def _hyena_spectrum(t_sym, t_anti, hs, hd, pn, bias):
    seq = hs.shape[0]
    colblk = lambda rows: pl.BlockSpec((rows, HY_CW), lambda j: (0, j))
    return pl.pallas_call(
        _spectrum_kernel,
        grid=(HY_WIDTH // HY_CW,),
        in_specs=[_resident(t_sym.shape), _resident(t_anti.shape), colblk(seq), colblk(seq), colblk(1), colblk(1)],
        out_specs=[colblk(seq), colblk(seq), colblk(1)],
        out_shape=[jax.ShapeDtypeStruct((seq, HY_WIDTH), F32)] * 2 + [jax.ShapeDtypeStruct((1, HY_WIDTH), F32)],
        compiler_params=_params("parallel"),
        name="hy_spectrum",
    )(t_sym, t_anti, hs, hd, pn, bias.reshape(1, -1))


def _short_conv(u, w, b):
    n = u.shape[0]
    pos = lax.broadcasted_iota(jnp.int32, (n, 1), 0) % GRID_W
    up = jnp.where(pos == 0, 0.0, pltpu.roll(u, 1, 0))
    dn = jnp.where(pos == GRID_W - 1, 0.0, pltpu.roll(u, n - 1, 0))
    return up * w[0:1] + u * w[1:2] + dn * w[2:3] + b


def _hy_fwd_kernel(ts_ref, ta_ref, x1_ref, v_ref, p_ref, q_ref, kn_ref, s_ref):
    seq = x1_ref.shape[1]
    n, half = 2 * seq, seq // 2
    u_sym, u_anti, u_mid = _fold((v_ref[0].astype(F32) * x1_ref[0].astype(F32)).astype(BF16))
    sgn = _alternating(half)
    nyq = (jnp.sum(sgn * u_sym.astype(F32), axis=0, keepdims=True) + u_mid) * kn_ref[...] * (1.0 / n)
    for r0 in range(0, half, HY_ROW_CHUNK):
        r1 = r0 + HY_ROW_CHUNK
        mid = sgn[r0:r1] * u_mid
        re_e = _dot(ts_ref[r0:r1, :], u_sym) + mid
        im_o = _dot(ts_ref[half + r0:half + r1, :], u_sym) + mid
        im_e = _dot(ta_ref[r0:r1, :], u_anti)
        re_o = _dot(ta_ref[half + r0:half + r1, :], u_anti)
        pe, po = p_ref[r0:r1, :], p_ref[half + r0:half + r1, :]
        qe, qo = q_ref[r0:r1, :], q_ref[half + r0:half + r1, :]
        s_re_e = (re_e * pe + im_e * qe) * (2.0 / n)
        s_im_e = (im_e * pe - re_e * qe) * (2.0 / n)
        if r0 == 0:
            pos = lax.broadcasted_iota(jnp.int32, (HY_ROW_CHUNK, 1), 0)
            s_re_e = jnp.where(pos == 0, re_e * pe * (1.0 / n), s_re_e)
            s_im_e = jnp.where(pos == 0, nyq, s_im_e)
        s_ref[0, r0:r1, :] = s_re_e.astype(BF16)
        s_ref[0, half + r0:half + r1, :] = ((im_o * po - re_o * qo) * (2.0 / n)).astype(BF16)
        s_ref[0, seq + r0:seq + r1, :] = s_im_e.astype(BF16)
        s_ref[0, seq + half + r0:seq + half + r1, :] = ((re_o * po + im_o * qo) * (2.0 / n)).astype(BF16)


def _hy_inv_kernel(gs_ref, ga_ref, s_ref, x0_ref, o_ref):
    seq = x0_ref.shape[1]
    half = seq // 2
    x0 = x0_ref[0].astype(F32)
    sgn = _alternating(half)
    nyq = sgn * s_ref[0, seq:seq + 1, :].astype(F32)
    mirrored = []
    for r0 in range(0, half, HY_ROW_CHUNK):
        r1 = r0 + HY_ROW_CHUNK
        sym = _dot(gs_ref[r0:r1, :], s_ref[0, :seq, :]) + nyq[r0:r1]
        anti = _dot(ga_ref[r0:r1, :], s_ref[0, seq:, :])
        o_ref[0, r0:r1, :] = (x0[r0:r1] * (sym + anti)).astype(o_ref.dtype)
        mirrored.append(sym - anti)
    centre = (jnp.sum(sgn * (s_ref[0, :half, :].astype(F32) + s_ref[0, half:seq, :].astype(F32)), axis=0, keepdims=True)
              + nyq[0:1])
    pos = lax.broadcasted_iota(jnp.int32, (half, 1), 0)
    upper = _negate_index(jnp.where(pos == 0, centre, jnp.concatenate(mirrored, axis=0)).astype(BF16))
    o_ref[0, half:, :] = (x0[half:] * upper).astype(o_ref.dtype)


def _hyena_conv(hy, tabs, p_spec, q_spec, k_nyq):
    bsz, seq, _ = hy.shape
    nc = HY_WIDTH // HY_CW
    t_sym, t_anti, g_sym, g_anti = tabs
    sig = lambda part: pl.BlockSpec((1, seq, HY_CW), lambda j, b: (b, 0, part * nc + j))
    spec = lambda rows: pl.BlockSpec((rows, HY_CW), lambda j, b: (0, j))
    s = pl.pallas_call(
        _hy_fwd_kernel,
        grid=(nc, bsz),
        in_specs=[_resident(t_sym.shape), _resident(t_anti.shape), sig(1), sig(2),
                  spec(seq), spec(seq), spec(1)],
        out_specs=pl.BlockSpec((1, 2 * seq, HY_CW), lambda j, b: (b, 0, j)),
        out_shape=jax.ShapeDtypeStruct((bsz, 2 * seq, HY_WIDTH), BF16),
        compiler_params=_params("parallel", "parallel"),
        name="hy_fwd",
    )(t_sym, t_anti, hy, hy, p_spec, q_spec, k_nyq)
    return pl.pallas_call(
        _hy_inv_kernel,
        grid=(nc, bsz),
        in_specs=[_resident(g_sym.shape), _resident(g_anti.shape),
                  pl.BlockSpec((1, 2 * seq, HY_CW), lambda j, b: (b, 0, j)), sig(0)],
        out_specs=pl.BlockSpec((1, seq, HY_CW), lambda j, b: (b, 0, j)),
        out_shape=jax.ShapeDtypeStruct((bsz, seq, HY_WIDTH), BF16),
        compiler_params=_params("parallel", "parallel"),
        name="hy_inv",
    )(g_sym, g_anti, s, hy)


ROW_PARTS = D_MODEL // LANES


def _token_tile(t):
    return pl.ds(pl.multiple_of(t * ROW_PARTS, ROW_PARTS), ROW_PARTS)


def _load_token_rows(ref, n_tok):
    return jnp.concatenate([ref[pl.ds(j, n_tok, stride=ROW_PARTS), :] for j in range(ROW_PARTS)], axis=-1)


def _store_token_rows(ref, val):
    for j in range(ROW_PARTS):
        ref[pl.ds(j, val.shape[0], stride=ROW_PARTS), :] = val[:, j * LANES:(j + 1) * LANES]


POST_TILE = 512


def _post_kernel(of_ref, ob_ref, og_ref, yh_ref, ma_ref, mb_ref, x_ref, g1_ref, sc2_ref, sh2_ref,
                 gn_ref, nf_ref, wa_ref, wb_ref, wo_ref, wr_ref, br_ref,
                 x1_ref, h2_ref, idx_ref, wts_ref, cnt_ref):
    o = of_ref[0].astype(F32) + ob_ref[0].astype(F32)
    gn = gn_ref[...]
    heads = [_rms(o[:, h * GLA_HEAD_V:(h + 1) * GLA_HEAD_V], gn) for h in range(GLA_HEADS)]
    og = og_ref[0].astype(F32)
    a_in = jnp.concatenate(heads, axis=-1) * (og * _sigmoid(og))
    branch_a = _dot(a_in.astype(BF16), wa_ref[...])
    branch_b = _dot(yh_ref[0], wb_ref[...])
    y = _sigmoid(ma_ref[0].astype(F32)) * branch_a + _sigmoid(mb_ref[0].astype(F32)) * branch_b
    x1 = x_ref[0] + g1_ref[0] * _dot(y.astype(BF16), wo_ref[...])
    x1_ref[0] = x1
    h2 = _rms(x1, nf_ref[...]) * (1.0 + sc2_ref[0]) + sh2_ref[0]
    _store_token_rows(h2_ref.at[0], h2)

    h_hi, h_lo = _split(h2)
    w_hi, w_lo = _split(wr_ref[...])
    both = _dot(h_hi, jnp.concatenate([w_hi, w_lo], axis=-1))
    logits = both[:, :N_EXPERTS] + (both[:, N_EXPERTS:] + _dot(h_lo, w_hi)) + br_ref[...]
    lane = lax.broadcasted_iota(jnp.int32, logits.shape, 1)
    slot = lax.broadcasted_iota(jnp.int32, (logits.shape[0], TOP_K), 1)
    work = logits
    tops = []
    idx = jnp.zeros((logits.shape[0], TOP_K), jnp.int32)
    chosen = jnp.zeros(logits.shape, F32)
    for k in range(TOP_K):
        m = jnp.max(work, axis=-1, keepdims=True)
        first = jnp.min(jnp.where(work == m, lane, N_EXPERTS), axis=-1, keepdims=True)
        tops.append(m)
        idx = jnp.where(slot == k, first, idx)
        chosen = jnp.where(lane == first, 1.0, chosen)
        work = jnp.where(lane == first, -jnp.inf, work)
    es = [jnp.exp(m - tops[0]) for m in tops]
    inv = 1.0 / functools.reduce(lambda a, b: a + b, es)
    wts = jnp.zeros((logits.shape[0], TOP_K), F32)
    for k, e in enumerate(es):
        wts = jnp.where(slot == k, e * inv, wts)
    idx_ref[0] = idx
    wts_ref[0] = wts

    @pl.when((pl.program_id(0) == 0) & (pl.program_id(1) == 0))
    def _():
        cnt_ref[...] = jnp.zeros_like(cnt_ref)

    cnt_ref[...] += jnp.sum(chosen, axis=0, keepdims=True)


def _post(o_f, o_b, og, y_h, m_a, m_b, x, mods, gla_norm, norm_ffn, w_a, w_b, w_o, w_router, b_router):
    bsz, seq, _ = x.shape
    tok = lambda n: pl.BlockSpec((1, POST_TILE, n), lambda b, i: (b, i, 0))
    mod = lambda which: pl.BlockSpec((1, 1, D_MODEL), lambda b, i: (b, 0, which))
    in_specs = [tok(D_MODEL)] * 7 + [mod(MOD_G1), mod(MOD_SC2), mod(MOD_SH2),
                                     _resident((1, GLA_HEAD_V)), _resident((1, D_MODEL)),
                                     _resident(w_a.shape), _resident(w_b.shape), _resident(w_o.shape),
                                     _resident(w_router.shape), _resident((1, N_EXPERTS))]
    return pl.pallas_call(
        _post_kernel,
        grid=(bsz, seq // POST_TILE),
        in_specs=in_specs,
        out_specs=[tok(D_MODEL), pl.BlockSpec((1, POST_TILE * ROW_PARTS, LANES), lambda b, i: (b, i, 0)),
                   tok(TOP_K), tok(TOP_K), pl.BlockSpec((1, N_EXPERTS), lambda b, i: (0, 0))],
        out_shape=[jax.ShapeDtypeStruct((bsz, seq, D_MODEL), F32),
                   jax.ShapeDtypeStruct((bsz, seq * ROW_PARTS, LANES), F32),
                   jax.ShapeDtypeStruct((bsz, seq, TOP_K), jnp.int32), jax.ShapeDtypeStruct((bsz, seq, TOP_K), F32),
                   jax.ShapeDtypeStruct((1, N_EXPERTS), F32)],
        compiler_params=_params("arbitrary", "arbitrary"),
        name="post",
    )(o_f, o_b, og, y_h, m_a, m_b, x, mods, mods, mods, gla_norm.reshape(1, -1), norm_ffn.reshape(1, -1),
      w_a, w_b, w_o, w_router, b_router.reshape(1, -1))


UP_GROUP = 2 * LANES
N_UP_GROUPS = 2 * D_MODEL // UP_GROUP


def _stage_expert_weights(wu_ref, wd_ref, wu_s, wd_s):
    r = lax.broadcasted_iota(jnp.int32, (UP_GROUP, UP_GROUP), 0)
    c = lax.broadcasted_iota(jnp.int32, (UP_GROUP, UP_GROUP), 1)
    perm = jnp.where(c == (r % 2) * LANES + r // 2, 1.0, 0.0).astype(BF16)
    for g in range(N_UP_GROUPS):
        cols = slice(g * UP_GROUP, (g + 1) * UP_GROUP)
        wu_s[:, cols] = _dot(wu_ref[:, cols].astype(BF16), perm).astype(BF16)
    wd_s[...] = wd_ref[...].astype(BF16)


RANK_TILE = 1024
ROUTE_TILE = 512
EXPERT_TILE = 512
EXPERT_BLOCK = 512
DOWN_COLS_PER_DOT = 256
COMBINE_TILE = 256


def _exact_count_dot(a, m):
    a0 = a.astype(BF16)
    r1 = a - a0.astype(F32)
    a1 = r1.astype(BF16)
    a2 = (r1 - a1.astype(F32)).astype(BF16)
    return _dot(a0, m) + (_dot(a1, m) + _dot(a2, m))


def _route_kernel(idx_ref, cnt_ref, pos_ref, base, offs):
    i = pl.program_id(0)
    n_tok = idx_ref.shape[0]
    idx = idx_ref[...]
    lane = lax.broadcasted_iota(jnp.int32, (n_tok, N_EXPERTS), 1)
    hot = jnp.zeros((n_tok, N_EXPERTS), F32)
    for k in range(TOP_K):
        hot = hot + jnp.where(lane == idx[:, k:k + 1], 1.0, 0.0)

    @pl.when(i == 0)
    def _():
        r = lax.broadcasted_iota(jnp.int32, (N_EXPERTS, N_EXPERTS), 0)
        c = lax.broadcasted_iota(jnp.int32, (N_EXPERTS, N_EXPERTS), 1)
        before = jnp.where(r < c, 1.0, 0.0).astype(BF16)
        offs[...] = _exact_count_dot(jnp.broadcast_to(cnt_ref[...], (8, N_EXPERTS)), before)[0:1]
        base[...] = jnp.zeros_like(base)

    r = lax.broadcasted_iota(jnp.int32, (n_tok, n_tok), 0)
    c = lax.broadcasted_iota(jnp.int32, (n_tok, n_tok), 1)
    earlier = jnp.where(c < r, 1.0, 0.0).astype(BF16)
    dense = offs[...] + base[...] + _dot(earlier, hot.astype(BF16))
    slot = lax.broadcasted_iota(jnp.int32, (n_tok, TOP_K), 1)
    pos = jnp.zeros((n_tok, TOP_K), F32)
    for k in range(TOP_K):
        mine = jnp.sum(jnp.where(lane == idx[:, k:k + 1], dense, 0.0), axis=-1, keepdims=True)
        pos = jnp.where(slot == k, mine, pos)
    pos_ref[...] = pos.astype(jnp.int32)
    base[...] += jnp.sum(hot, axis=0, keepdims=True)


def _route(idx, counts):
    n_tok = idx.shape[0]
    return pl.pallas_call(
        _route_kernel,
        grid=(n_tok // RANK_TILE,),
        in_specs=[pl.BlockSpec((RANK_TILE, TOP_K), lambda i: (i, 0)),
                  pl.BlockSpec((1, N_EXPERTS), lambda i: (0, 0))],
        out_specs=pl.BlockSpec((RANK_TILE, TOP_K), lambda i: (i, 0)),
        out_shape=jax.ShapeDtypeStruct((n_tok, TOP_K), jnp.int32),
        scratch_shapes=[pltpu.VMEM((1, N_EXPERTS), F32)] * 2,
        compiler_params=_params("arbitrary"),
        name="route",
    )(idx, counts)


def _start_rows(n_tok, copy):
    def issue(t, carry):
        for k in range(TOP_K):
            copy(t, k).start(priority=k % 2)
        return carry

    lax.fori_loop(0, n_tok, issue, 0, unroll=4)


def _wait_rows(n_tok, copy):
    def drain(t, carry):
        for k in range(TOP_K):
            copy(t, k).wait()
        return carry

    lax.fori_loop(0, n_tok, drain, 0, unroll=4)


def _dispatch_kernel(pos_ref, h_ref, xs_ref, sem):
    def copy(t, k):
        return pltpu.make_async_copy(h_ref.at[_token_tile(t), :], xs_ref.at[_token_tile(pos_ref[t * TOP_K + k]), :], sem)

    _start_rows(ROUTE_TILE, copy)
    _wait_rows(ROUTE_TILE, copy)


def _index_blocks(tile, n_steps):
    blk = (tile * TOP_K,)
    return [pl.BlockSpec(blk, lambda i: (i,), memory_space=pltpu.SMEM),
            pl.BlockSpec(blk, lambda i: (jnp.minimum(i + 1, n_steps - 1),), memory_space=pltpu.SMEM)]


def _dispatch(pos_flat, h2):
    n_tok = h2.shape[0] // ROW_PARTS
    n_steps = n_tok // ROUTE_TILE
    return pl.pallas_call(
        _dispatch_kernel,
        grid=(n_steps,),
        in_specs=[_index_blocks(ROUTE_TILE, n_steps)[0],
                  pl.BlockSpec((ROUTE_TILE * ROW_PARTS, LANES), lambda i: (i, 0))],
        out_specs=pl.BlockSpec(memory_space=pl.ANY),
        out_shape=jax.ShapeDtypeStruct((n_tok * TOP_K * ROW_PARTS, LANES), F32),
        scratch_shapes=[pltpu.SemaphoreType.DMA],
        compiler_params=_params("arbitrary"),
        name="dispatch",
    )(pos_flat, h2)


def _experts_kernel(tile_ref, exp_ref, nxt_ref, lo_ref, hi_ref, x_ref, wu_hbm, bu_ref, wd_hbm, bd_ref, o_ref,
                    acc_ref, wu_s, wd_s, wu_f, wd_f, sem):
    w = pl.program_id(0)
    lo = lo_ref[w]
    hi = hi_ref[w]

    def fetch(e):
        return (pltpu.make_async_copy(wu_hbm.at[e], wu_f, sem.at[0]),
                pltpu.make_async_copy(wd_hbm.at[e], wd_f, sem.at[1]))

    @pl.when(w == 0)
    def _():
        acc_ref[...] = jnp.zeros_like(acc_ref)
        for copy in fetch(exp_ref[0]):
            copy.start()

    @pl.when((w == 0) | (exp_ref[w] != exp_ref[jnp.maximum(w - 1, 0)]))
    def _():
        for copy in fetch(exp_ref[w]):
            copy.wait()
        _stage_expert_weights(wu_f, wd_f, wu_s, wd_s)

        @pl.when(nxt_ref[w] >= 0)
        def _():
            for copy in fetch(nxt_ref[w]):
                copy.start()

    def block(r0):
        first = tile_ref[w] * EXPERT_TILE + r0
        rows = slice(r0, r0 + EXPERT_BLOCK)

        @pl.when((hi > lo) & (lo < first + EXPERT_BLOCK) & (hi > first))
        def _():
            x_rows = x_ref.at[pl.ds(r0 * ROW_PARTS, EXPERT_BLOCK * ROW_PARTS), :]
            h = _load_token_rows(x_rows, EXPERT_BLOCK).astype(BF16)
            acts = []
            for g in range(N_UP_GROUPS):
                cols = slice(g * UP_GROUP, (g + 1) * UP_GROUP)
                u = _dot(h, wu_s[:, cols]) + bu_ref[0, :, cols]
                glu = jnp.minimum(u[:, :LANES], SWIGLU_LIMIT)
                lin = jnp.clip(u[:, LANES:], -SWIGLU_LIMIT, SWIGLU_LIMIT)
                acts.append((glu * _sigmoid(SWIGLU_ALPHA * glu) * (lin + 1.0)).astype(BF16))
            a = jnp.concatenate(acts, axis=-1)
            row = first + lax.broadcasted_iota(jnp.int32, (EXPERT_BLOCK, 1), 0)
            mine = (row >= lo) & (row < hi)
            for c0 in range(0, D_MODEL, DOWN_COLS_PER_DOT):
                cols = slice(c0, c0 + DOWN_COLS_PER_DOT)
                kept = jnp.where(mine, _dot(a, wd_s[:, cols]) + bd_ref[0, :, cols], acc_ref[rows, cols])
                acc_ref[rows, cols] = kept
                for j in range(DOWN_COLS_PER_DOT // LANES):
                    part = pl.ds(r0 * ROW_PARTS + c0 // LANES + j, EXPERT_BLOCK, stride=ROW_PARTS)
                    o_ref[part, :] = kept[:, j * LANES:(j + 1) * LANES]

    for r0 in range(0, EXPERT_TILE, EXPERT_BLOCK):
        block(r0)


def _work_items(counts, n_rows):
    n_tiles = n_rows // EXPERT_TILE
    ends = jnp.cumsum(counts)
    tile_ends = jnp.arange(1, n_tiles + 1, dtype=jnp.int32) * EXPERT_TILE
    n_items = n_tiles + N_EXPERTS
    count = lambda cond: jnp.sum(cond.astype(jnp.int32), axis=1)
    slot_t = jnp.arange(n_tiles, dtype=jnp.int32) + count(ends[None, :] < tile_ends[:, None])
    slot_e = jnp.arange(N_EXPERTS, dtype=jnp.int32) + count(tile_ends[None, :] <= ends[:, None])
    slots = jnp.concatenate([slot_t, slot_e])
    vals = jnp.concatenate([tile_ends, ends])
    item = jnp.arange(n_items, dtype=jnp.int32)
    hi = jnp.sum(jnp.where(slots[None, :] == item[:, None], vals[None, :], 0), axis=1)
    lo = jnp.concatenate([jnp.zeros((1,), jnp.int32), hi[:-1]])
    tile_id = jnp.minimum(lo // EXPERT_TILE, n_tiles - 1)
    exp_id = jnp.minimum(count(ends[None, :] <= lo[:, None]), N_EXPERTS - 1)
    later = jnp.where(exp_id[None, :] > exp_id[:, None], exp_id[None, :], N_EXPERTS)
    nxt_id = jnp.min(later, axis=1)
    nxt_id = jnp.where(nxt_id == N_EXPERTS, -1, nxt_id)
    return tile_id, exp_id, nxt_id, lo, hi


def _experts(items, xs, w_up, b_up_g, w_down, b_down):
    n_items = items[0].shape[0]
    rows = lambda w, t, e, nxt, lo, hi: (t[w], 0)
    bias = lambda c: pl.BlockSpec((1, 1, c), lambda w, t, e, nxt, lo, hi: (e[w], 0, 0))
    hbm = pl.BlockSpec(memory_space=pl.ANY)
    return pl.pallas_call(
        _experts_kernel,
        grid_spec=pltpu.PrefetchScalarGridSpec(
            num_scalar_prefetch=5,
            grid=(n_items,),
            in_specs=[pl.BlockSpec((EXPERT_TILE * ROW_PARTS, LANES), rows),
                      hbm, bias(2 * D_MODEL), hbm, bias(D_MODEL)],
            out_specs=pl.BlockSpec((EXPERT_TILE * ROW_PARTS, LANES), rows),
            scratch_shapes=[pltpu.VMEM((EXPERT_TILE, D_MODEL), F32),
                            pltpu.VMEM((D_MODEL, 2 * D_MODEL), BF16), pltpu.VMEM((D_MODEL, D_MODEL), BF16),
                            pltpu.VMEM((D_MODEL, 2 * D_MODEL), F32), pltpu.VMEM((D_MODEL, D_MODEL), F32),
                            pltpu.SemaphoreType.DMA((2,))]),
        out_shape=jax.ShapeDtypeStruct(xs.shape, F32),
        compiler_params=_params("arbitrary"),
        name="experts",
    )(*items, xs, w_up, b_up_g, w_down, b_down)


def _combine_kernel(pos_ref, nxt_ref, ys_ref, wts_ref, x1_ref, g2_ref, nf_ref, o_ref, buf_a, buf_b, sem):
    i = pl.program_id(0)
    sub = COMBINE_TILE

    def gather(p_ref, first_tok, buf, s):
        def copy(t, k):
            src = ys_ref.at[_token_tile(p_ref[(first_tok + t) * TOP_K + k]), :]
            return pltpu.make_async_copy(src, buf.at[k, _token_tile(t), :], sem.at[s])
        return copy

    this_a = gather(pos_ref, 0, buf_a, 0)
    this_b = gather(pos_ref, sub, buf_b, 1)
    next_a = gather(nxt_ref, 0, buf_a, 0)

    def reduce_rows(buf, half, start_other):
        rows = slice(half * sub, (half + 1) * sub)
        wts = wts_ref[rows, :]
        parts = []
        for j in range(ROW_PARTS):
            for t in range(j * sub // ROW_PARTS, (j + 1) * sub // ROW_PARTS):
                for k in range(TOP_K):
                    start_other(t, k).start(priority=k % 2)
            part = wts[:, 0:1] * buf[0, pl.ds(j, sub, stride=ROW_PARTS), :]
            for k in range(1, TOP_K):
                part = part + wts[:, k:k + 1] * buf[k, pl.ds(j, sub, stride=ROW_PARTS), :]
            parts.append(part)
        acc = jnp.concatenate(parts, axis=-1)
        o_ref[rows, :] = _rms(x1_ref[rows, :] + g2_ref[0] * acc, nf_ref[...])

    @pl.when(i == 0)
    def _():
        _start_rows(sub, this_a)

    _wait_rows(sub, this_a)
    reduce_rows(buf_a, 0, this_b)
    _wait_rows(sub, this_b)
    reduce_rows(buf_b, 1, next_a)

    @pl.when(i == pl.num_programs(0) - 1)
    def _():
        _wait_rows(sub, next_a)


def _combine(pos_flat, ys, wts, x1, mods, norm_final, seq):
    n_tok = x1.shape[0]
    step_tok = 2 * COMBINE_TILE
    steps_per_sample = seq // step_tok
    tok = lambda n: pl.BlockSpec((step_tok, n), lambda i: (i, 0))
    n_steps = n_tok // step_tok
    gather_buf = pltpu.VMEM((TOP_K, COMBINE_TILE * ROW_PARTS, LANES), F32)
    return pl.pallas_call(
        _combine_kernel,
        grid=(n_steps,),
        in_specs=_index_blocks(step_tok, n_steps) + [
            pl.BlockSpec(memory_space=pl.ANY), tok(TOP_K), tok(D_MODEL),
            pl.BlockSpec((1, 1, D_MODEL), lambda i: (i // steps_per_sample, 0, MOD_G2)),
            pl.BlockSpec((1, D_MODEL), lambda i: (0, 0))],
        out_specs=tok(D_MODEL),
        out_shape=jax.ShapeDtypeStruct((n_tok, D_MODEL), F32),
        scratch_shapes=[gather_buf, gather_buf, pltpu.SemaphoreType.DMA((2,))],
        compiler_params=_params("arbitrary"),
        name="combine",
    )(pos_flat, pos_flat, ys, wts, x1, mods, norm_final.reshape(1, -1))


def _moe(h2, idx, wts, counts, w_up, b_up_g, w_down, b_down, x1, mods, norm_final):
    bsz, seq, _ = x1.shape
    n_tok = bsz * seq
    pos_flat = _route(idx.reshape(n_tok, TOP_K), counts).reshape(-1)
    xs = _dispatch(pos_flat, h2.reshape(n_tok * ROW_PARTS, LANES))
    items = _work_items(counts.reshape(-1).astype(jnp.int32), n_tok * TOP_K)
    ys = _experts(items, xs, w_up, b_up_g, w_down, b_down)
    out = _combine(pos_flat, ys, wts.reshape(n_tok, TOP_K), x1.reshape(n_tok, D_MODEL), mods, norm_final, seq)
    return out.reshape(bsz, seq, D_MODEL)


def kernel(x, c, ctx, c_ctx, w_ada, b_ada, norm_mix, norm_ffn, w_in, w_gk_f, b_gk_f, w_gk_b, b_gk_b, gla_norm, w_gla_out, hy_conv_w, hy_conv_b, hy_f_w1, hy_f_b1, hy_f_w2, hy_f_b2, hy_f_w3, hy_f_b3, hy_f_w4, hy_sin_freq, hy_bias, w_hy_out, w_out, w_router, b_router, w_up, b_up, w_down, b_down, norm_final):
    depth = w_ada.shape[0]
    assert depth == 1, "single-layer block: the context stream only feeds later layers"
    bsz, seq, _ = x.shape

    n_rows = -(-(bsz + 1) // 8) * 8
    cc = jnp.zeros((n_rows, D_MODEL), F32).at[:bsz].set(c).at[bsz].set(c_ctx)
    mods = _ada(cc, w_ada[0], b_ada[0]).reshape(n_rows, 1, N_MOD * D_MODEL)

    sizes = (GLA_DK, GLA_DK, GLA_DV, GLA_DV, GLA_GATE_RANK, GLA_GATE_RANK, 3 * HY_WIDTH, D_MODEL, D_MODEL)
    offs = np.concatenate([[0], np.cumsum(sizes)])
    w_in_b = w_in[0].astype(BF16)
    wq, wk, wv, wog, wrf, wrb, why, wma, wmb = [w_in_b[:, offs[i]:offs[i + 1]] for i in range(len(sizes))]
    q, k, v, og, r_f, r_b, hy, m_a, m_b = _inproj(
        x, mods, lambda b: b, norm_mix[0], [wq, wk, wv, wog, wrf, wrb, why, wma, wmb],
        [BF16, BF16, BF16, BF16, F32, F32, BF16, BF16, BF16], tl=512, conv=(6, hy_conv_w[0], hy_conv_b[0]))
    k_c, v_c, rc_f, rc_b = _inproj(
        ctx, mods, lambda b: bsz, norm_mix[0], [wk, wv, wrf, wrb], [BF16, BF16, F32, F32], tl=ctx.shape[1])

    w_gk = jnp.stack([w_gk_f[0], w_gk_b[0]])
    b_gk = jnp.stack([b_gk_f[0], b_gk_b[0]])[:, None, :]
    o_f, o_b = _gla(q, k, v, r_f, r_b, k_c, v_c, rc_f, rc_b, w_gk, b_gk)

    tabs = [jnp.asarray(t).astype(BF16) for t in _dft_tables(seq)]
    t_sym, t_anti = tabs[:2]
    hs, hd, pn = _hyena_filter(seq, hy_f_w1[0], hy_f_b1[0], hy_f_w2[0], hy_f_b2[0], hy_f_w3[0], hy_f_b3[0],
                               hy_f_w4[0], hy_sin_freq[0])
    p_spec, q_spec, k_nyq = _hyena_spectrum(t_sym, t_anti, hs, hd, pn, hy_bias[0])
    y_h = _hyena_conv(hy, tabs, p_spec, q_spec, k_nyq)

    x1, h2, idx, wts, counts = _post(o_f, o_b, og, y_h, m_a, m_b, x, mods, gla_norm[0], norm_ffn[0],
                                     w_gla_out[0].astype(BF16), w_hy_out[0].astype(BF16), w_out[0].astype(BF16),
                                     w_router[0], b_router[0])

    b_up_g = b_up[0].reshape(N_EXPERTS, N_UP_GROUPS, LANES, 2).transpose(0, 1, 3, 2).reshape(N_EXPERTS, 1, -1)
    return _moe(h2, idx, wts, counts, w_up[0], b_up_g, w_down[0], b_down[0][:, None, :],
                x1, mods, norm_final)
```

```python
import functools
import math

import jax
import jax.numpy as jnp
import numpy as np
from jax import lax
from jax.experimental import pallas as pl
from jax.experimental.pallas import tpu as pltpu

F32 = jnp.float32
BF16 = jnp.bfloat16

D_MODEL = 1024
GRID_W = 64
EPS = 1e-6
N_MOD = 6

GLA_HEADS = 4
GLA_HEAD_K = 128
GLA_HEAD_V = 256
GLA_DK = GLA_HEADS * GLA_HEAD_K
GLA_DV = GLA_HEADS * GLA_HEAD_V
GLA_GATE_RANK = 16
GLA_GATE_NORM = 16.0
GLA_CHUNK = 64

HY_WIDTH = D_MODEL
HY_EMB = 33
HY_FAST_DECAY = 0.3
HY_SLOW_DECAY = 1.5
HY_TARGET = 1e-2

N_EXPERTS = 32
TOP_K = 4
SWIGLU_LIMIT = 7.0
SWIGLU_ALPHA = 1.702

V7X_VMEM_BYTES = 64 * 1024 * 1024
VMEM_LIMIT = V7X_VMEM_BYTES - 8 * 1024 * 1024
LANES = 128

MOD_SH1, MOD_SC1, MOD_G1, MOD_SH2, MOD_SC2, MOD_G2 = range(N_MOD)


def _params(*sem):
    return pltpu.CompilerParams(dimension_semantics=sem, vmem_limit_bytes=VMEM_LIMIT)


def _resident(shape):
    return pl.BlockSpec(shape, lambda *_: (0,) * len(shape), pipeline_mode=pl.Buffered(1))


def _dot(a, b):
    return jnp.dot(a, b, preferred_element_type=F32)


def _dot_nt(a, b):
    return lax.dot_general(a, b, (((1,), (1,)), ((), ())), preferred_element_type=F32)


def _dot_tn(a, b):
    return lax.dot_general(a, b, (((0,), (0,)), ((), ())), preferred_element_type=F32)


def _split(a):
    hi = a.astype(BF16)
    lo = (a - hi.astype(F32)).astype(BF16)
    return hi, lo


def _dot3(a, b):
    ah, al = _split(a)
    bh, bl = _split(b)
    return _dot(ah, bh) + (_dot(ah, bl) + _dot(al, bh))


def _sigmoid(x):
    return 1.0 / (1.0 + jnp.exp(-x))


def _log_sigmoid(x):
    return jnp.minimum(x, 0.0) - jnp.log(1.0 + jnp.exp(-jnp.abs(x)))


def _rms(x, w):
    return x * lax.rsqrt(jnp.mean(x * x, axis=-1, keepdims=True) + EPS) * w


def _ada_kernel(c_ref, w_ref, b_ref, o_ref):
    c = c_ref[...]
    o_ref[...] = _dot3(c * _sigmoid(c), w_ref[...]) + b_ref[...]


def _ada(cc, w_ada, b_ada):
    rows = cc.shape[0]
    return pl.pallas_call(
        _ada_kernel,
        grid=(N_MOD,),
        in_specs=[pl.BlockSpec((rows, D_MODEL), lambda j: (0, 0)),
                  pl.BlockSpec((D_MODEL, D_MODEL), lambda j: (0, j)),
                  pl.BlockSpec((1, D_MODEL), lambda j: (0, j))],
        out_specs=pl.BlockSpec((rows, D_MODEL), lambda j: (0, j)),
        out_shape=jax.ShapeDtypeStruct((rows, N_MOD * D_MODEL), F32),
        compiler_params=_params("parallel"),
        name="ada",
    )(cc, w_ada, b_ada.reshape(1, -1))


INPROJ_COL_CHUNK = 512


def _inproj_kernel(n_out, conv_out, x_ref, sc_ref, sh_ref, nw_ref, *refs):
    n_in = n_out + (0 if conv_out is None else 2)
    w_refs, o_refs = refs[:n_out], refs[n_in:]
    h = _rms(x_ref[0], nw_ref[...]) * (1.0 + sc_ref[0]) + sh_ref[0]
    hb = h.astype(BF16)
    for i, (w_ref, o_ref) in enumerate(zip(w_refs, o_refs)):
        n = w_ref.shape[1]
        for c0 in range(0, n, INPROJ_COL_CHUNK):
            c1 = min(c0 + INPROJ_COL_CHUNK, n)
            y = _dot(hb, w_ref[:, c0:c1])
            if i == conv_out:
                cw_ref, cb_ref = refs[n_out:n_in]
                y = _short_conv(y, cw_ref[:, c0:c1], cb_ref[:, c0:c1])
            o_ref[0, :, c0:c1] = y.astype(o_ref.dtype)


def _inproj(x, mods, mod_row, norm_w, weights, out_dtypes, tl, conv=None):
    bsz, seq, _ = x.shape
    n_out = len(weights)
    assert tl % GRID_W == 0
    in_specs = [pl.BlockSpec((1, tl, D_MODEL), lambda b, i: (b, i, 0)),
                pl.BlockSpec((1, 1, D_MODEL), lambda b, i: (mod_row(b), 0, MOD_SC1)),
                pl.BlockSpec((1, 1, D_MODEL), lambda b, i: (mod_row(b), 0, MOD_SH1)),
                _resident((1, D_MODEL))]
    in_specs += [_resident(w.shape) for w in weights]
    extra = []
    if conv is not None:
        extra = [conv[1], conv[2].reshape(1, -1)]
        in_specs += [_resident(a.shape) for a in extra]
    out_specs = [pl.BlockSpec((1, tl, w.shape[1]), lambda b, i: (b, i, 0)) for w in weights]
    out_shape = [jax.ShapeDtypeStruct((bsz, seq, w.shape[1]), dt) for w, dt in zip(weights, out_dtypes)]
    return pl.pallas_call(
        functools.partial(_inproj_kernel, n_out, None if conv is None else conv[0]),
        grid=(bsz, seq // tl),
        in_specs=in_specs, out_specs=out_specs, out_shape=out_shape,
        compiler_params=_params("parallel", "parallel"),
        name="inproj",
    )(x, mods, mods, norm_w.reshape(1, -1), *weights, *extra)


GLA_TILE = 256
GLA_NCH = GLA_TILE // GLA_CHUNK
GLA_SCALE = GLA_HEAD_K ** -0.5
GLA_HPS = 4
GLA_KW = GLA_HPS * GLA_HEAD_K
GLA_VW = GLA_HPS * GLA_HEAD_V


@functools.lru_cache(maxsize=None)
def _chunk_triangles():
    i = np.arange(GLA_TILE)
    same = (i[:, None] // GLA_CHUNK) == (i[None, :] // GLA_CHUNK)
    return np.stack([same & (i[None, :] <= i[:, None]), same & (i[None, :] >= i[:, None])]).astype(np.float32)


def _gla_kernel(tri_ref, qf_ref, kf_ref, vf_ref, rf_ref, qb_ref, kb_ref, vb_ref, rb_ref,
                kc_ref, vc_ref, rcf_ref, rcb_ref, wgk_ref, bgk_ref,
                of_ref, ob_ref, sf_ref, sb_ref):
    t = pl.program_id(2)
    crow = lax.broadcasted_iota(jnp.int32, (GLA_CHUNK, GLA_CHUNK), 0)
    ccol = lax.broadcasted_iota(jnp.int32, (GLA_CHUNK, GLA_CHUNK), 1)
    heads = range(GLA_HPS)

    def rows(c):
        return slice(c * GLA_CHUNK, (c + 1) * GLA_CHUNK)

    def hk(h):
        return slice(h * GLA_HEAD_K, (h + 1) * GLA_HEAD_K)

    def hv(h):
        return slice(h * GLA_HEAD_V, (h + 1) * GLA_HEAD_V)

    def order(fwd):
        return range(GLA_NCH) if fwd else range(GLA_NCH - 1, -1, -1)

    def cum_decay(r, d):
        rh, rl = _split(r)
        wh, wl = _split(wgk_ref[d])
        z = _dot(jnp.concatenate([rh, rl, rh], axis=1), jnp.concatenate([wh, wh, wl], axis=0)) + bgk_ref[d]
        g = _log_sigmoid(z) * (1.0 / GLA_GATE_NORM)
        gh, gl = _split(g)
        return _dot(tri_ref[d], gh) + _dot(tri_ref[d], gl)

    def updates(k_ref, v_ref, b, fwd):
        out = {}
        for c in range(GLA_NCH):
            i = c * GLA_CHUNK + (GLA_CHUNK - 1 if fwd else 0)
            total = b[i:i + 1, :]
            k_upd = (k_ref[0, rows(c), :].astype(F32) * jnp.exp(total - b[rows(c)])).astype(BF16)
            dec = jnp.exp(total)
            for h in heads:
                col = jnp.broadcast_to(dec[:, hk(h)], (GLA_HEAD_K, GLA_HEAD_K)).T
                col = jnp.concatenate([col] * (GLA_HEAD_V // GLA_HEAD_K), axis=1)
                out[c, h] = (col, _dot_tn(k_upd[:, hk(h)], v_ref[0, rows(c), hv(h)]))
        return out

    def context_state(s_ref, r_ref, d, fwd):
        upd = updates(kc_ref, vc_ref, cum_decay(r_ref[0], d), fwd)
        for h in heads:
            st = jnp.zeros((GLA_HEAD_K, GLA_HEAD_V), F32)
            for c in order(fwd):
                dec, inc = upd[c, h]
                st = st * dec + inc
            s_ref[h] = st

    def scan(s_ref, q_ref, k_ref, v_ref, r_ref, o_ref, d, fwd):
        mask = (ccol <= crow) if fwd else (ccol >= crow)
        b = cum_decay(r_ref[0], d)
        q_dec = (q_ref[0].astype(F32) * (jnp.exp(b) * GLA_SCALE)).astype(BF16)
        k_inv = (k_ref[0].astype(F32) * jnp.exp(-b)).astype(BF16)
        upd = updates(k_ref, v_ref, b, fwd)
        att = {}
        for c in range(GLA_NCH):
            for h in heads:
                scores = _dot_nt(q_dec[rows(c), hk(h)], k_inv[rows(c), hk(h)])
                att[c, h] = jnp.where(mask, scores, 0.0).astype(BF16)
        enter = {}
        for h in heads:
            st = s_ref[h]
            for c in order(fwd):
                enter[c, h] = st.astype(BF16)
                dec, inc = upd[c, h]
                st = st * dec + inc
            s_ref[h] = st
        for c in range(GLA_NCH):
            for h in heads:
                lhs = jnp.concatenate([q_dec[rows(c), hk(h)], att[c, h]], axis=1)
                rhs = jnp.concatenate([enter[c, h], v_ref[0, rows(c), hv(h)]], axis=0)
                o_ref[0, rows(c), hv(h)] = _dot(lhs, rhs).astype(o_ref.dtype)

    @pl.when(t == 0)
    def _():
        context_state(sf_ref, rcf_ref, 0, True)
        context_state(sb_ref, rcb_ref, 1, False)

    scan(sf_ref, qf_ref, kf_ref, vf_ref, rf_ref, of_ref, 0, True)
    scan(sb_ref, qb_ref, kb_ref, vb_ref, rb_ref, ob_ref, 1, False)


def _gla(q, k, v, r_f, r_b, k_c, v_c, rc_f, rc_b, w_gk, b_gk):
    bsz, seq, _ = q.shape
    assert k_c.shape[1] == GLA_TILE
    nt = seq // GLA_TILE
    tri = jnp.asarray(_chunk_triangles()).astype(BF16)
    fwd = lambda b, h, t: (b, t, h)
    bwd = lambda b, h, t: (b, nt - 1 - t, h)
    fwd0 = lambda b, h, t: (b, t, 0)
    bwd0 = lambda b, h, t: (b, nt - 1 - t, 0)
    ctx = lambda b, h, t: (b, 0, h)
    ctx0 = lambda b, h, t: (b, 0, 0)
    kblk = (1, GLA_TILE, GLA_KW)
    vblk = (1, GLA_TILE, GLA_VW)
    rblk = (1, GLA_TILE, GLA_GATE_RANK)
    in_specs = [_resident(tri.shape),
                pl.BlockSpec(kblk, fwd), pl.BlockSpec(kblk, fwd), pl.BlockSpec(vblk, fwd), pl.BlockSpec(rblk, fwd0),
                pl.BlockSpec(kblk, bwd), pl.BlockSpec(kblk, bwd), pl.BlockSpec(vblk, bwd), pl.BlockSpec(rblk, bwd0),
                pl.BlockSpec(kblk, ctx), pl.BlockSpec(vblk, ctx), pl.BlockSpec(rblk, ctx0), pl.BlockSpec(rblk, ctx0),
                pl.BlockSpec((2, GLA_GATE_RANK, GLA_KW), lambda b, h, t: (0, 0, h)),
                pl.BlockSpec((2, 1, GLA_KW), lambda b, h, t: (0, 0, h))]
    out_specs = [pl.BlockSpec(vblk, fwd), pl.BlockSpec(vblk, bwd)]
    out_shape = [jax.ShapeDtypeStruct((bsz, seq, GLA_DV), BF16)] * 2
    return pl.pallas_call(
        _gla_kernel,
        grid=(bsz, GLA_HEADS // GLA_HPS, nt),
        in_specs=in_specs, out_specs=out_specs, out_shape=out_shape,
        scratch_shapes=[pltpu.VMEM((GLA_HPS, GLA_HEAD_K, GLA_HEAD_V), F32)] * 2,
        compiler_params=_params("parallel", "parallel", "arbitrary"),
        name="gla",
    )(tri, q, k, v, r_f, q, k, v, r_b, k_c, v_c, rc_f, rc_b, w_gk, b_gk)


HY_CW = 512
HY_ROW_CHUNK = 512
MLP_PAD = 128


@functools.lru_cache(maxsize=None)
def _dft_tables(seq):
    n, half = 2 * seq, seq // 2
    m = np.arange(half)[:, None]
    s = np.arange(half)[None, :]
    ang = 2.0 * np.pi * np.arange(n) / n
    cos = lambda k: np.cos(ang)[(k * s) % n]
    sin = lambda k: np.sin(ang)[(k * s) % n]
    sym = np.concatenate([cos(2 * m), sin(2 * m + 1)], axis=0)
    anti = np.concatenate([sin(2 * m), cos(2 * m + 1)], axis=0)
    tables = (sym, anti, sym.T, anti.T)
    return tuple(np.ascontiguousarray(t, dtype=np.float32) for t in tables)


@functools.lru_cache(maxsize=None)
def _filter_features(seq):
    bands = (HY_EMB - 1) // 2
    pos = np.arange(seq, dtype=np.float64)[:, None]
    t = pos / max(seq - 1, 1)
    f = np.linspace(1e-4, bands - 1, bands)[None]
    ang = (2.0 * math.pi / seq) * pos * f
    z = np.concatenate([t, np.cos(ang), -np.sin(ang)], axis=-1)
    out = np.zeros((seq, MLP_PAD), np.float32)
    out[:, :HY_EMB] = z
    deltas = np.abs(np.linspace(math.log(HY_TARGET) / HY_SLOW_DECAY, math.log(HY_TARGET) / HY_FAST_DECAY, HY_WIDTH))
    return out, deltas.astype(np.float32)[None]


def _filter_kernel(z_ref, w1_ref, b1_ref, w2_ref, b2_ref, w3_ref, b3_ref, fr_ref, w4f_ref, w4b_ref, dl_ref,
                   hs_ref, hd_ref, pn_ref, hid_ref):
    z = z_ref[...]

    @pl.when(pl.program_id(0) == 0)
    def _():
        fr = fr_ref[...]
        h = jnp.sin(fr * (_dot3(z, w1_ref[...]) + b1_ref[...]))
        h = jnp.sin(fr * (_dot3(h, w2_ref[...]) + b2_ref[...]))
        hid_ref[...] = jnp.sin(fr * (_dot3(h, w3_ref[...]) + b3_ref[...]))

    h = hid_ref[...]
    window = jnp.exp(-z[:, 0:1] * dl_ref[...])
    h_f = _dot3(h, w4f_ref[...]) * window
    h_b = _dot3(h, w4b_ref[...]) * window
    pos = lax.broadcasted_iota(jnp.int32, (z.shape[0], 1), 0)
    h_b = jnp.where(pos == 0, 0.0, h_b)
    hs = h_f + h_b
    hs_ref[...] = hs.astype(BF16)
    hd_ref[...] = (h_b - h_f).astype(BF16)
    sign = jnp.where(pos % 2 == 0, 1.0, -1.0)
    pn_ref[...] = jnp.sum(hs * sign, axis=0, keepdims=True)


def _pad2(a, rows, cols):
    return jnp.zeros((rows, cols), F32).at[:a.shape[0], :a.shape[1]].set(a.astype(F32))


def _hyena_filter(seq, w1, b1, w2, b2, w3, b3, w4, freq):
    z_np, deltas_np = _filter_features(seq)
    p = MLP_PAD
    args = [jnp.asarray(z_np), _pad2(w1, p, p), _pad2(b1[None], 1, p), _pad2(w2, p, p), _pad2(b2[None], 1, p),
            _pad2(w3, p, p), _pad2(b3[None], 1, p), _pad2(freq[None], 1, p),
            _pad2(w4[:, :HY_WIDTH], p, HY_WIDTH), _pad2(w4[:, HY_WIDTH:], p, HY_WIDTH), jnp.asarray(deltas_np)]
    full = lambda shape: pl.BlockSpec(shape, lambda j: (0, 0))
    colblk = lambda rows: pl.BlockSpec((rows, HY_CW), lambda j: (0, j))
    in_specs = [full((seq, p)), full((p, p)), full((1, p)), full((p, p)), full((1, p)), full((p, p)), full((1, p)),
                full((1, p)), colblk(p), colblk(p), colblk(1)]
    return pl.pallas_call(
        _filter_kernel,
        grid=(HY_WIDTH // HY_CW,),
        in_specs=in_specs,
        out_specs=[colblk(seq), colblk(seq), colblk(1)],
        out_shape=[jax.ShapeDtypeStruct((seq, HY_WIDTH), BF16), jax.ShapeDtypeStruct((seq, HY_WIDTH), BF16),
                   jax.ShapeDtypeStruct((1, HY_WIDTH), F32)],
        scratch_shapes=[pltpu.VMEM((seq, p), F32)],
        compiler_params=_params("arbitrary"),
        name="hy_filter",
    )(*args)


REV_BLOCK = 256


def _negate_index(x):
    n = x.shape[0]
    r = lax.broadcasted_iota(jnp.int32, (REV_BLOCK, REV_BLOCK), 0)
    c = lax.broadcasted_iota(jnp.int32, (REV_BLOCK, REV_BLOCK), 1)
    flip = jnp.where(r + c == REV_BLOCK - 1, 1.0, 0.0).astype(BF16)
    if x.dtype == BF16:
        pieces = [x]
    else:
        x0 = x.astype(BF16)
        r1 = x - x0.astype(F32)
        x1 = r1.astype(BF16)
        pieces = [x0, x1, (r1 - x1.astype(F32)).astype(BF16)]
    blocks = []
    for i in range(n // REV_BLOCK - 1, -1, -1):
        rows = slice(i * REV_BLOCK, (i + 1) * REV_BLOCK)
        acc = _dot(flip, pieces[0][rows])
        for p in pieces[1:]:
            acc = acc + _dot(flip, p[rows])
        blocks.append(acc)
    return pltpu.roll(jnp.concatenate(blocks, axis=0), 1, 0)


def _fold(x):
    half = x.shape[0] // 2
    lo = x[:half].astype(F32)
    hi = _negate_index(x[half:])
    pos = lax.broadcasted_iota(jnp.int32, (half, 1), 0)
    mid = hi[0:1]
    hi = jnp.where(pos == 0, 0.0, hi)
    return (lo + hi).astype(BF16), (lo - hi).astype(BF16), mid


def _alternating(n):
    pos = lax.broadcasted_iota(jnp.int32, (n, 1), 0)
    return jnp.where(pos % 2 == 0, 1.0, -1.0)


def _spectrum_kernel(ts_ref, ta_ref, hs_ref, hd_ref, pn_ref, bias_ref, p_ref, q_ref, kn_ref):
    half = hs_ref.shape[0] // 2
    bias = bias_ref[...]
    sgn = _alternating(half)
    s_sym, s_anti, s_mid = _fold(hs_ref[...])
    d_sym, d_anti, d_mid = _fold(hd_ref[...])
    p_ref[:half, :] = _dot(ts_ref[:half, :], s_sym) + sgn * s_mid + bias
    p_ref[half:, :] = _dot(ta_ref[half:, :], s_anti) + bias
    q_ref[:half, :] = _dot(ta_ref[:half, :], d_anti)
    q_ref[half:, :] = _dot(ts_ref[half:, :], d_sym) + sgn * d_mid
    kn_ref[...] = pn_ref[...] + bias


def _hyena_spectrum(t_sym, t_anti, hs, hd, pn, bias):
    seq = hs.shape[0]
    colblk = lambda rows: pl.BlockSpec((rows, HY_CW), lambda j: (0, j))
    return pl.pallas_call(
        _spectrum_kernel,
        grid=(HY_WIDTH // HY_CW,),
        in_specs=[_resident(t_sym.shape), _resident(t_anti.shape), colblk(seq), colblk(seq), colblk(1), colblk(1)],
        out_specs=[colblk(seq), colblk(seq), colblk(1)],
        out_shape=[jax.ShapeDtypeStruct((seq, HY_WIDTH), F32)] * 2 + [jax.ShapeDtypeStruct((1, HY_WIDTH), F32)],
        compiler_params=_params("parallel"),
        name="hy_spectrum",
    )(t_sym, t_anti, hs, hd, pn, bias.reshape(1, -1))


def _short_conv(u, w, b):
    n = u.shape[0]
    pos = lax.broadcasted_iota(jnp.int32, (n, 1), 0) % GRID_W
    up = jnp.where(pos == 0, 0.0, pltpu.roll(u, 1, 0))
    dn = jnp.where(pos == GRID_W - 1, 0.0, pltpu.roll(u, n - 1, 0))
    return up * w[0:1] + u * w[1:2] + dn * w[2:3] + b


def _hy_fwd_kernel(ts_ref, ta_ref, x1_ref, v_ref, p_ref, q_ref, kn_ref, s_ref):
    seq = x1_ref.shape[1]
    n, half = 2 * seq, seq // 2
    u_sym, u_anti, u_mid = _fold((v_ref[0].astype(F32) * x1_ref[0].astype(F32)).astype(BF16))
    sgn = _alternating(half)
    nyq = (jnp.sum(sgn * u_sym.astype(F32), axis=0, keepdims=True) + u_mid) * kn_ref[...] * (1.0 / n)
    for r0 in range(0, half, HY_ROW_CHUNK):
        r1 = r0 + HY_ROW_CHUNK
        mid = sgn[r0:r1] * u_mid
        re_e = _dot(ts_ref[r0:r1, :], u_sym) + mid
        im_o = _dot(ts_ref[half + r0:half + r1, :], u_sym) + mid
        im_e = _dot(ta_ref[r0:r1, :], u_anti)
        re_o = _dot(ta_ref[half + r0:half + r1, :], u_anti)
        pe, po = p_ref[r0:r1, :], p_ref[half + r0:half + r1, :]
        qe, qo = q_ref[r0:r1, :], q_ref[half + r0:half + r1, :]
        s_re_e = (re_e * pe + im_e * qe) * (2.0 / n)
        s_im_e = (im_e * pe - re_e * qe) * (2.0 / n)
        if r0 == 0:
            pos = lax.broadcasted_iota(jnp.int32, (HY_ROW_CHUNK, 1), 0)
            s_re_e = jnp.where(pos == 0, re_e * pe * (1.0 / n), s_re_e)
            s_im_e = jnp.where(pos == 0, nyq, s_im_e)
        s_ref[0, r0:r1, :] = s_re_e.astype(BF16)
        s_ref[0, half + r0:half + r1, :] = ((im_o * po - re_o * qo) * (2.0 / n)).astype(BF16)
        s_ref[0, seq + r0:seq + r1, :] = s_im_e.astype(BF16)
        s_ref[0, seq + half + r0:seq + half + r1, :] = ((re_o * po + im_o * qo) * (2.0 / n)).astype(BF16)


def _hy_inv_kernel(gs_ref, ga_ref, s_ref, x0_ref, o_ref):
    seq = x0_ref.shape[1]
    half = seq // 2
    x0 = x0_ref[0].astype(F32)
    sgn = _alternating(half)
    nyq = sgn * s_ref[0, seq:seq + 1, :].astype(F32)
    mirrored = []
    for r0 in range(0, half, HY_ROW_CHUNK):
        r1 = r0 + HY_ROW_CHUNK
        sym = _dot(gs_ref[r0:r1, :], s_ref[0, :seq, :]) + nyq[r0:r1]
        anti = _dot(ga_ref[r0:r1, :], s_ref[0, seq:, :])
        o_ref[0, r0:r1, :] = (x0[r0:r1] * (sym + anti)).astype(o_ref.dtype)
        mirrored.append(sym - anti)
    centre = (jnp.sum(sgn * (s_ref[0, :half, :].astype(F32) + s_ref[0, half:seq, :].astype(F32)), axis=0, keepdims=True)
              + nyq[0:1])
    pos = lax.broadcasted_iota(jnp.int32, (half, 1), 0)
    upper = _negate_index(jnp.where(pos == 0, centre, jnp.concatenate(mirrored, axis=0)).astype(BF16))
    o_ref[0, half:, :] = (x0[half:] * upper).astype(o_ref.dtype)


def _hyena_conv(hy, tabs, p_spec, q_spec, k_nyq):
    bsz, seq, _ = hy.shape
    nc = HY_WIDTH // HY_CW
    t_sym, t_anti, g_sym, g_anti = tabs
    sig = lambda part: pl.BlockSpec((1, seq, HY_CW), lambda j, b: (b, 0, part * nc + j))
    spec = lambda rows: pl.BlockSpec((rows, HY_CW), lambda j, b: (0, j))
    s = pl.pallas_call(
        _hy_fwd_kernel,
        grid=(nc, bsz),
        in_specs=[_resident(t_sym.shape), _resident(t_anti.shape), sig(1), sig(2),
                  spec(seq), spec(seq), spec(1)],
        out_specs=pl.BlockSpec((1, 2 * seq, HY_CW), lambda j, b: (b, 0, j)),
        out_shape=jax.ShapeDtypeStruct((bsz, 2 * seq, HY_WIDTH), BF16),
        compiler_params=_params("parallel", "parallel"),
        name="hy_fwd",
    )(t_sym, t_anti, hy, hy, p_spec, q_spec, k_nyq)
    return pl.pallas_call(
        _hy_inv_kernel,
        grid=(nc, bsz),
        in_specs=[_resident(g_sym.shape), _resident(g_anti.shape),
                  pl.BlockSpec((1, 2 * seq, HY_CW), lambda j, b: (b, 0, j)), sig(0)],
        out_specs=pl.BlockSpec((1, seq, HY_CW), lambda j, b: (b, 0, j)),
        out_shape=jax.ShapeDtypeStruct((bsz, seq, HY_WIDTH), BF16),
        compiler_params=_params("parallel", "parallel"),
        name="hy_inv",
    )(g_sym, g_anti, s, hy)


ROW_PARTS = D_MODEL // LANES


def _token_tile(t):
    return pl.ds(pl.multiple_of(t * ROW_PARTS, ROW_PARTS), ROW_PARTS)


def _load_token_rows(ref, n_tok):
    return jnp.concatenate([ref[pl.ds(j, n_tok, stride=ROW_PARTS), :] for j in range(ROW_PARTS)], axis=-1)


def _store_token_rows(ref, val):
    for j in range(ROW_PARTS):
        ref[pl.ds(j, val.shape[0], stride=ROW_PARTS), :] = val[:, j * LANES:(j + 1) * LANES]


POST_TILE = 512


def _post_kernel(of_ref, ob_ref, og_ref, yh_ref, ma_ref, mb_ref, x_ref, g1_ref, sc2_ref, sh2_ref,
                 gn_ref, nf_ref, wa_ref, wb_ref, wo_ref, wr_ref, br_ref,
                 x1_ref, h2_ref, idx_ref, wts_ref, cnt_ref):
    o = of_ref[0].astype(F32) + ob_ref[0].astype(F32)
    gn = gn_ref[...]
    heads = [_rms(o[:, h * GLA_HEAD_V:(h + 1) * GLA_HEAD_V], gn) for h in range(GLA_HEADS)]
    og = og_ref[0].astype(F32)
    a_in = jnp.concatenate(heads, axis=-1) * (og * _sigmoid(og))
    branch_a = _dot(a_in.astype(BF16), wa_ref[...])
    branch_b = _dot(yh_ref[0], wb_ref[...])
    y = _sigmoid(ma_ref[0].astype(F32)) * branch_a + _sigmoid(mb_ref[0].astype(F32)) * branch_b
    x1 = x_ref[0] + g1_ref[0] * _dot(y.astype(BF16), wo_ref[...])
    x1_ref[0] = x1
    h2 = _rms(x1, nf_ref[...]) * (1.0 + sc2_ref[0]) + sh2_ref[0]
    _store_token_rows(h2_ref.at[0], h2)

    h_hi, h_lo = _split(h2)
    w_hi, w_lo = _split(wr_ref[...])
    both = _dot(h_hi, jnp.concatenate([w_hi, w_lo], axis=-1))
    logits = both[:, :N_EXPERTS] + (both[:, N_EXPERTS:] + _dot(h_lo, w_hi)) + br_ref[...]
    lane = lax.broadcasted_iota(jnp.int32, logits.shape, 1)
    slot = lax.broadcasted_iota(jnp.int32, (logits.shape[0], TOP_K), 1)
    work = logits
    tops = []
    idx = jnp.zeros((logits.shape[0], TOP_K), jnp.int32)
    chosen = jnp.zeros(logits.shape, F32)
    for k in range(TOP_K):
        m = jnp.max(work, axis=-1, keepdims=True)
        first = jnp.min(jnp.where(work == m, lane, N_EXPERTS), axis=-1, keepdims=True)
        tops.append(m)
        idx = jnp.where(slot == k, first, idx)
        chosen = jnp.where(lane == first, 1.0, chosen)
        work = jnp.where(lane == first, -jnp.inf, work)
    es = [jnp.exp(m - tops[0]) for m in tops]
    inv = 1.0 / functools.reduce(lambda a, b: a + b, es)
    wts = jnp.zeros((logits.shape[0], TOP_K), F32)
    for k, e in enumerate(es):
        wts = jnp.where(slot == k, e * inv, wts)
    idx_ref[0] = idx
    wts_ref[0] = wts

    @pl.when((pl.program_id(0) == 0) & (pl.program_id(1) == 0))
    def _():
        cnt_ref[...] = jnp.zeros_like(cnt_ref)

    cnt_ref[...] += jnp.sum(chosen, axis=0, keepdims=True)


def _post(o_f, o_b, og, y_h, m_a, m_b, x, mods, gla_norm, norm_ffn, w_a, w_b, w_o, w_router, b_router):
    bsz, seq, _ = x.shape
    tok = lambda n: pl.BlockSpec((1, POST_TILE, n), lambda b, i: (b, i, 0))
    mod = lambda which: pl.BlockSpec((1, 1, D_MODEL), lambda b, i: (b, 0, which))
    in_specs = [tok(D_MODEL)] * 7 + [mod(MOD_G1), mod(MOD_SC2), mod(MOD_SH2),
                                     _resident((1, GLA_HEAD_V)), _resident((1, D_MODEL)),
                                     _resident(w_a.shape), _resident(w_b.shape), _resident(w_o.shape),
                                     _resident(w_router.shape), _resident((1, N_EXPERTS))]
    return pl.pallas_call(
        _post_kernel,
        grid=(bsz, seq // POST_TILE),
        in_specs=in_specs,
        out_specs=[tok(D_MODEL), pl.BlockSpec((1, POST_TILE * ROW_PARTS, LANES), lambda b, i: (b, i, 0)),
                   tok(TOP_K), tok(TOP_K), pl.BlockSpec((1, N_EXPERTS), lambda b, i: (0, 0))],
        out_shape=[jax.ShapeDtypeStruct((bsz, seq, D_MODEL), F32),
                   jax.ShapeDtypeStruct((bsz, seq * ROW_PARTS, LANES), F32),
                   jax.ShapeDtypeStruct((bsz, seq, TOP_K), jnp.int32), jax.ShapeDtypeStruct((bsz, seq, TOP_K), F32),
                   jax.ShapeDtypeStruct((1, N_EXPERTS), F32)],
        compiler_params=_params("arbitrary", "arbitrary"),
        name="post",
    )(o_f, o_b, og, y_h, m_a, m_b, x, mods, mods, mods, gla_norm.reshape(1, -1), norm_ffn.reshape(1, -1),
      w_a, w_b, w_o, w_router, b_router.reshape(1, -1))


UP_GROUP = 2 * LANES
N_UP_GROUPS = 2 * D_MODEL // UP_GROUP


def _stage_expert_weights(wu_ref, wd_ref, wu_s, wd_s):
    r = lax.broadcasted_iota(jnp.int32, (UP_GROUP, UP_GROUP), 0)
    c = lax.broadcasted_iota(jnp.int32, (UP_GROUP, UP_GROUP), 1)
    perm = jnp.where(c == (r % 2) * LANES + r // 2, 1.0, 0.0).astype(BF16)
    for g in range(N_UP_GROUPS):
        cols = slice(g * UP_GROUP, (g + 1) * UP_GROUP)
        wu_s[:, cols] = _dot(wu_ref[:, cols].astype(BF16), perm).astype(BF16)
    wd_s[...] = wd_ref[...].astype(BF16)


RANK_TILE = 1024
ROUTE_TILE = 512
EXPERT_TILE = 512
EXPERT_BLOCK = 512
DOWN_COLS_PER_DOT = 256
COMBINE_TILE = 256


def _exact_count_dot(a, m):
    a0 = a.astype(BF16)
    r1 = a - a0.astype(F32)
    a1 = r1.astype(BF16)
    a2 = (r1 - a1.astype(F32)).astype(BF16)
    return _dot(a0, m) + (_dot(a1, m) + _dot(a2, m))


def _route_kernel(idx_ref, cnt_ref, pos_ref, base, offs):
    i = pl.program_id(0)
    n_tok = idx_ref.shape[0]
    idx = idx_ref[...]
    lane = lax.broadcasted_iota(jnp.int32, (n_tok, N_EXPERTS), 1)
    hot = jnp.zeros((n_tok, N_EXPERTS), F32)
    for k in range(TOP_K):
        hot = hot + jnp.where(lane == idx[:, k:k + 1], 1.0, 0.0)

    @pl.when(i == 0)
    def _():
        r = lax.broadcasted_iota(jnp.int32, (N_EXPERTS, N_EXPERTS), 0)
        c = lax.broadcasted_iota(jnp.int32, (N_EXPERTS, N_EXPERTS), 1)
        before = jnp.where(r < c, 1.0, 0.0).astype(BF16)
        offs[...] = _exact_count_dot(jnp.broadcast_to(cnt_ref[...], (8, N_EXPERTS)), before)[0:1]
        base[...] = jnp.zeros_like(base)

    r = lax.broadcasted_iota(jnp.int32, (n_tok, n_tok), 0)
    c = lax.broadcasted_iota(jnp.int32, (n_tok, n_tok), 1)
    earlier = jnp.where(c < r, 1.0, 0.0).astype(BF16)
    dense = offs[...] + base[...] + _dot(earlier, hot.astype(BF16))
    slot = lax.broadcasted_iota(jnp.int32, (n_tok, TOP_K), 1)
    pos = jnp.zeros((n_tok, TOP_K), F32)
    for k in range(TOP_K):
        mine = jnp.sum(jnp.where(lane == idx[:, k:k + 1], dense, 0.0), axis=-1, keepdims=True)
        pos = jnp.where(slot == k, mine, pos)
    pos_ref[...] = pos.astype(jnp.int32)
    base[...] += jnp.sum(hot, axis=0, keepdims=True)


def _route(idx, counts):
    n_tok = idx.shape[0]
    return pl.pallas_call(
        _route_kernel,
        grid=(n_tok // RANK_TILE,),
        in_specs=[pl.BlockSpec((RANK_TILE, TOP_K), lambda i: (i, 0)),
                  pl.BlockSpec((1, N_EXPERTS), lambda i: (0, 0))],
        out_specs=pl.BlockSpec((RANK_TILE, TOP_K), lambda i: (i, 0)),
        out_shape=jax.ShapeDtypeStruct((n_tok, TOP_K), jnp.int32),
        scratch_shapes=[pltpu.VMEM((1, N_EXPERTS), F32)] * 2,
        compiler_params=_params("arbitrary"),
        name="route",
    )(idx, counts)


def _start_rows(n_tok, copy):
    def issue(t, carry):
        for k in range(TOP_K):
            copy(t, k).start(priority=k % 2)
        return carry

    lax.fori_loop(0, n_tok, issue, 0, unroll=4)


def _wait_rows(n_tok, copy):
    def drain(t, carry):
        for k in range(TOP_K):
            copy(t, k).wait()
        return carry

    lax.fori_loop(0, n_tok, drain, 0, unroll=4)


def _dispatch_kernel(pos_ref, h_ref, xs_ref, sem):
    def copy(t, k):
        return pltpu.make_async_copy(h_ref.at[_token_tile(t), :], xs_ref.at[_token_tile(pos_ref[t * TOP_K + k]), :], sem)

    _start_rows(ROUTE_TILE, copy)
    _wait_rows(ROUTE_TILE, copy)


def _index_blocks(tile, n_steps):
    blk = (tile * TOP_K,)
    return [pl.BlockSpec(blk, lambda i: (i,), memory_space=pltpu.SMEM),
            pl.BlockSpec(blk, lambda i: (jnp.minimum(i + 1, n_steps - 1),), memory_space=pltpu.SMEM)]


def _dispatch(pos_flat, h2):
    n_tok = h2.shape[0] // ROW_PARTS
    n_steps = n_tok // ROUTE_TILE
    return pl.pallas_call(
        _dispatch_kernel,
        grid=(n_steps,),
        in_specs=[_index_blocks(ROUTE_TILE, n_steps)[0],
                  pl.BlockSpec((ROUTE_TILE * ROW_PARTS, LANES), lambda i: (i, 0))],
        out_specs=pl.BlockSpec(memory_space=pl.ANY),
        out_shape=jax.ShapeDtypeStruct((n_tok * TOP_K * ROW_PARTS, LANES), F32),
        scratch_shapes=[pltpu.SemaphoreType.DMA],
        compiler_params=_params("arbitrary"),
        name="dispatch",
    )(pos_flat, h2)


def _experts_kernel(tile_ref, exp_ref, nxt_ref, lo_ref, hi_ref, x_ref, wu_hbm, bu_ref, wd_hbm, bd_ref, o_ref,
                    acc_ref, wu_s, wd_s, wu_f, wd_f, sem):
    w = pl.program_id(0)
    lo = lo_ref[w]
    hi = hi_ref[w]

    def fetch(e):
        return (pltpu.make_async_copy(wu_hbm.at[e], wu_f, sem.at[0]),
                pltpu.make_async_copy(wd_hbm.at[e], wd_f, sem.at[1]))

    @pl.when(w == 0)
    def _():
        acc_ref[...] = jnp.zeros_like(acc_ref)
        for copy in fetch(exp_ref[0]):
            copy.start()

    @pl.when((w == 0) | (exp_ref[w] != exp_ref[jnp.maximum(w - 1, 0)]))
    def _():
        for copy in fetch(exp_ref[w]):
            copy.wait()
        _stage_expert_weights(wu_f, wd_f, wu_s, wd_s)

        @pl.when(nxt_ref[w] >= 0)
        def _():
            for copy in fetch(nxt_ref[w]):
                copy.start()

    def block(r0):
        first = tile_ref[w] * EXPERT_TILE + r0
        rows = slice(r0, r0 + EXPERT_BLOCK)

        @pl.when((hi > lo) & (lo < first + EXPERT_BLOCK) & (hi > first))
        def _():
            x_rows = x_ref.at[pl.ds(r0 * ROW_PARTS, EXPERT_BLOCK * ROW_PARTS), :]
            h = _load_token_rows(x_rows, EXPERT_BLOCK).astype(BF16)
            acts = []
            for g in range(N_UP_GROUPS):
                cols = slice(g * UP_GROUP, (g + 1) * UP_GROUP)
                u = _dot(h, wu_s[:, cols]) + bu_ref[0, :, cols]
                glu = jnp.minimum(u[:, :LANES], SWIGLU_LIMIT)
                lin = jnp.clip(u[:, LANES:], -SWIGLU_LIMIT, SWIGLU_LIMIT)
                acts.append((glu * _sigmoid(SWIGLU_ALPHA * glu) * (lin + 1.0)).astype(BF16))
            a = jnp.concatenate(acts, axis=-1)
            row = first + lax.broadcasted_iota(jnp.int32, (EXPERT_BLOCK, 1), 0)
            mine = (row >= lo) & (row < hi)
            for c0 in range(0, D_MODEL, DOWN_COLS_PER_DOT):
                cols = slice(c0, c0 + DOWN_COLS_PER_DOT)
                kept = jnp.where(mine, _dot(a, wd_s[:, cols]) + bd_ref[0, :, cols], acc_ref[rows, cols])
                acc_ref[rows, cols] = kept
                for j in range(DOWN_COLS_PER_DOT // LANES):
                    part = pl.ds(r0 * ROW_PARTS + c0 // LANES + j, EXPERT_BLOCK, stride=ROW_PARTS)
                    o_ref[part, :] = kept[:, j * LANES:(j + 1) * LANES]

    for r0 in range(0, EXPERT_TILE, EXPERT_BLOCK):
        block(r0)


def _work_items(counts, n_rows):
    n_tiles = n_rows // EXPERT_TILE
    ends = jnp.cumsum(counts)
    tile_ends = jnp.arange(1, n_tiles + 1, dtype=jnp.int32) * EXPERT_TILE
    n_items = n_tiles + N_EXPERTS
    count = lambda cond: jnp.sum(cond.astype(jnp.int32), axis=1)
    slot_t = jnp.arange(n_tiles, dtype=jnp.int32) + count(ends[None, :] < tile_ends[:, None])
    slot_e = jnp.arange(N_EXPERTS, dtype=jnp.int32) + count(tile_ends[None, :] <= ends[:, None])
    slots = jnp.concatenate([slot_t, slot_e])
    vals = jnp.concatenate([tile_ends, ends])
    item = jnp.arange(n_items, dtype=jnp.int32)
    hi = jnp.sum(jnp.where(slots[None, :] == item[:, None], vals[None, :], 0), axis=1)
    lo = jnp.concatenate([jnp.zeros((1,), jnp.int32), hi[:-1]])
    tile_id = jnp.minimum(lo // EXPERT_TILE, n_tiles - 1)
    exp_id = jnp.minimum(count(ends[None, :] <= lo[:, None]), N_EXPERTS - 1)
    later = jnp.where(exp_id[None, :] > exp_id[:, None], exp_id[None, :], N_EXPERTS)
    nxt_id = jnp.min(later, axis=1)
    nxt_id = jnp.where(nxt_id == N_EXPERTS, -1, nxt_id)
    return tile_id, exp_id, nxt_id, lo, hi


def _experts(items, xs, w_up, b_up_g, w_down, b_down):
    n_items = items[0].shape[0]
    rows = lambda w, t, e, nxt, lo, hi: (t[w], 0)
    bias = lambda c: pl.BlockSpec((1, 1, c), lambda w, t, e, nxt, lo, hi: (e[w], 0, 0))
    hbm = pl.BlockSpec(memory_space=pl.ANY)
    return pl.pallas_call(
        _experts_kernel,
        grid_spec=pltpu.PrefetchScalarGridSpec(
            num_scalar_prefetch=5,
            grid=(n_items,),
            in_specs=[pl.BlockSpec((EXPERT_TILE * ROW_PARTS, LANES), rows),
                      hbm, bias(2 * D_MODEL), hbm, bias(D_MODEL)],
            out_specs=pl.BlockSpec((EXPERT_TILE * ROW_PARTS, LANES), rows),
            scratch_shapes=[pltpu.VMEM((EXPERT_TILE, D_MODEL), F32),
                            pltpu.VMEM((D_MODEL, 2 * D_MODEL), BF16), pltpu.VMEM((D_MODEL, D_MODEL), BF16),
                            pltpu.VMEM((D_MODEL, 2 * D_MODEL), F32), pltpu.VMEM((D_MODEL, D_MODEL), F32),
                            pltpu.SemaphoreType.DMA((2,))]),
        out_shape=jax.ShapeDtypeStruct(xs.shape, F32),
        compiler_params=_params("arbitrary"),
        name="experts",
    )(*items, xs, w_up, b_up_g, w_down, b_down)


def _combine_kernel(pos_ref, nxt_ref, ys_ref, wts_ref, x1_ref, g2_ref, nf_ref, o_ref, buf_a, buf_b, sem):
    i = pl.program_id(0)
    sub = COMBINE_TILE

    def gather(p_ref, first_tok, buf, s):
        def copy(t, k):
            src = ys_ref.at[_token_tile(p_ref[(first_tok + t) * TOP_K + k]), :]
            return pltpu.make_async_copy(src, buf.at[k, _token_tile(t), :], sem.at[s])
        return copy

    this_a = gather(pos_ref, 0, buf_a, 0)
    this_b = gather(pos_ref, sub, buf_b, 1)
    next_a = gather(nxt_ref, 0, buf_a, 0)

    def reduce_rows(buf, half, start_other):
        rows = slice(half * sub, (half + 1) * sub)
        wts = wts_ref[rows, :]
        parts = []
        for j in range(ROW_PARTS):
            for t in range(j * sub // ROW_PARTS, (j + 1) * sub // ROW_PARTS):
                for k in range(TOP_K):
                    start_other(t, k).start(priority=k % 2)
            part = wts[:, 0:1] * buf[0, pl.ds(j, sub, stride=ROW_PARTS), :]
            for k in range(1, TOP_K):
                part = part + wts[:, k:k + 1] * buf[k, pl.ds(j, sub, stride=ROW_PARTS), :]
            parts.append(part)
        acc = jnp.concatenate(parts, axis=-1)
        o_ref[rows, :] = _rms(x1_ref[rows, :] + g2_ref[0] * acc, nf_ref[...])

    @pl.when(i == 0)
    def _():
        _start_rows(sub, this_a)

    _wait_rows(sub, this_a)
    reduce_rows(buf_a, 0, this_b)
    _wait_rows(sub, this_b)
    reduce_rows(buf_b, 1, next_a)

    @pl.when(i == pl.num_programs(0) - 1)
    def _():
        _wait_rows(sub, next_a)


def _combine(pos_flat, ys, wts, x1, mods, norm_final, seq):
    n_tok = x1.shape[0]
    step_tok = 2 * COMBINE_TILE
    steps_per_sample = seq // step_tok
    tok = lambda n: pl.BlockSpec((step_tok, n), lambda i: (i, 0))
    n_steps = n_tok // step_tok
    gather_buf = pltpu.VMEM((TOP_K, COMBINE_TILE * ROW_PARTS, LANES), F32)
    return pl.pallas_call(
        _combine_kernel,
        grid=(n_steps,),
        in_specs=_index_blocks(step_tok, n_steps) + [
            pl.BlockSpec(memory_space=pl.ANY), tok(TOP_K), tok(D_MODEL),
            pl.BlockSpec((1, 1, D_MODEL), lambda i: (i // steps_per_sample, 0, MOD_G2)),
            pl.BlockSpec((1, D_MODEL), lambda i: (0, 0))],
        out_specs=tok(D_MODEL),
        out_shape=jax.ShapeDtypeStruct((n_tok, D_MODEL), F32),
        scratch_shapes=[gather_buf, gather_buf, pltpu.SemaphoreType.DMA((2,))],
        compiler_params=_params("arbitrary"),
        name="combine",
    )(pos_flat, pos_flat, ys, wts, x1, mods, norm_final.reshape(1, -1))


def _moe(h2, idx, wts, counts, w_up, b_up_g, w_down, b_down, x1, mods, norm_final):
    bsz, seq, _ = x1.shape
    n_tok = bsz * seq
    pos_flat = _route(idx.reshape(n_tok, TOP_K), counts).reshape(-1)
    xs = _dispatch(pos_flat, h2.reshape(n_tok * ROW_PARTS, LANES))
    items = _work_items(counts.reshape(-1).astype(jnp.int32), n_tok * TOP_K)
    ys = _experts(items, xs, w_up, b_up_g, w_down, b_down)
    out = _combine(pos_flat, ys, wts.reshape(n_tok, TOP_K), x1.reshape(n_tok, D_MODEL), mods, norm_final, seq)
    return out.reshape(bsz, seq, D_MODEL)


def kernel(x, c, ctx, c_ctx, w_ada, b_ada, norm_mix, norm_ffn, w_in, w_gk_f, b_gk_f, w_gk_b, b_gk_b, gla_norm, w_gla_out, hy_conv_w, hy_conv_b, hy_f_w1, hy_f_b1, hy_f_w2, hy_f_b2, hy_f_w3, hy_f_b3, hy_f_w4, hy_sin_freq, hy_bias, w_hy_out, w_out, w_router, b_router, w_up, b_up, w_down, b_down, norm_final):
    depth = w_ada.shape[0]
    assert depth == 1, "single-layer block: the context stream only feeds later layers"
    bsz, seq, _ = x.shape

    n_rows = -(-(bsz + 1) // 8) * 8
    cc = jnp.zeros((n_rows, D_MODEL), F32).at[:bsz].set(c).at[bsz].set(c_ctx)
    mods = _ada(cc, w_ada[0], b_ada[0]).reshape(n_rows, 1, N_MOD * D_MODEL)

    sizes = (GLA_DK, GLA_DK, GLA_DV, GLA_DV, GLA_GATE_RANK, GLA_GATE_RANK, 3 * HY_WIDTH, D_MODEL, D_MODEL)
    offs = np.concatenate([[0], np.cumsum(sizes)])
    w_in_b = w_in[0].astype(BF16)
    wq, wk, wv, wog, wrf, wrb, why, wma, wmb = [w_in_b[:, offs[i]:offs[i + 1]] for i in range(len(sizes))]
    q, k, v, og, r_f, r_b, hy, m_a, m_b = _inproj(
        x, mods, lambda b: b, norm_mix[0], [wq, wk, wv, wog, wrf, wrb, why, wma, wmb],
        [BF16, BF16, BF16, BF16, F32, F32, BF16, BF16, BF16], tl=512, conv=(6, hy_conv_w[0], hy_conv_b[0]))
    k_c, v_c, rc_f, rc_b = _inproj(
        ctx, mods, lambda b: bsz, norm_mix[0], [wk, wv, wrf, wrb], [BF16, BF16, F32, F32], tl=ctx.shape[1])

    w_gk = jnp.stack([w_gk_f[0], w_gk_b[0]])
    b_gk = jnp.stack([b_gk_f[0], b_gk_b[0]])[:, None, :]
    o_f, o_b = _gla(q, k, v, r_f, r_b, k_c, v_c, rc_f, rc_b, w_gk, b_gk)

    tabs = [jnp.asarray(t).astype(BF16) for t in _dft_tables(seq)]
    t_sym, t_anti = tabs[:2]
    hs, hd, pn = _hyena_filter(seq, hy_f_w1[0], hy_f_b1[0], hy_f_w2[0], hy_f_b2[0], hy_f_w3[0], hy_f_b3[0],
                               hy_f_w4[0], hy_sin_freq[0])
    p_spec, q_spec, k_nyq = _hyena_spectrum(t_sym, t_anti, hs, hd, pn, hy_bias[0])
    y_h = _hyena_conv(hy, tabs, p_spec, q_spec, k_nyq)

    x1, h2, idx, wts, counts = _post(o_f, o_b, og, y_h, m_a, m_b, x, mods, gla_norm[0], norm_ffn[0],
                                     w_gla_out[0].astype(BF16), w_hy_out[0].astype(BF16), w_out[0].astype(BF16),
                                     w_router[0], b_router[0])

    b_up_g = b_up[0].reshape(N_EXPERTS, N_UP_GROUPS, LANES, 2).transpose(0, 1, 3, 2).reshape(N_EXPERTS, 1, -1)
    return _moe(h2, idx, wts, counts, w_up[0], b_up_g, w_down[0], b_down[0][:, None, :],
                x1, mods, norm_final)
```

```python
import functools
import math

import jax
import jax.numpy as jnp
import numpy as np
from jax import lax
from jax.experimental import pallas as pl
from jax.experimental.pallas import tpu as pltpu

F32 = jnp.float32
BF16 = jnp.bfloat16

D_MODEL = 1024
GRID_W = 64
EPS = 1e-6
N_MOD = 6

GLA_HEADS = 4
GLA_HEAD_K = 128
GLA_HEAD_V = 256
GLA_DK = GLA_HEADS * GLA_HEAD_K
GLA_DV = GLA_HEADS * GLA_HEAD_V
GLA_GATE_RANK = 16
GLA_GATE_NORM = 16.0
GLA_CHUNK = 64

HY_WIDTH = D_MODEL
HY_EMB = 33
HY_FAST_DECAY = 0.3
HY_SLOW_DECAY = 1.5
HY_TARGET = 1e-2

N_EXPERTS = 32
TOP_K = 4
SWIGLU_LIMIT = 7.0
SWIGLU_ALPHA = 1.702

V7X_VMEM_BYTES = 64 * 1024 * 1024
VMEM_LIMIT = V7X_VMEM_BYTES - 8 * 1024 * 1024
LANES = 128

MOD_SH1, MOD_SC1, MOD_G1, MOD_SH2, MOD_SC2, MOD_G2 = range(N_MOD)


def _params(*sem):
    return pltpu.CompilerParams(dimension_semantics=sem, vmem_limit_bytes=VMEM_LIMIT)


def _resident(shape):
    return pl.BlockSpec(shape, lambda *_: (0,) * len(shape), pipeline_mode=pl.Buffered(1))


def _dot(a, b):
    return jnp.dot(a, b, preferred_element_type=F32)


def _dot_nt(a, b):
    return lax.dot_general(a, b, (((1,), (1,)), ((), ())), preferred_element_type=F32)


def _dot_tn(a, b):
    return lax.dot_general(a, b, (((0,), (0,)), ((), ())), preferred_element_type=F32)


def _split(a):
    hi = a.astype(BF16)
    lo = (a - hi.astype(F32)).astype(BF16)
    return hi, lo


def _dot3(a, b):
    ah, al = _split(a)
    bh, bl = _split(b)
    return _dot(ah, bh) + (_dot(ah, bl) + _dot(al, bh))


def _sigmoid(x):
    return 1.0 / (1.0 + jnp.exp(-x))


def _log_sigmoid(x):
    return jnp.minimum(x, 0.0) - jnp.log(1.0 + jnp.exp(-jnp.abs(x)))


def _rms(x, w):
    return x * lax.rsqrt(jnp.mean(x * x, axis=-1, keepdims=True) + EPS) * w


def _ada_kernel(c_ref, w_ref, b_ref, o_ref):
    c = c_ref[...]
    o_ref[...] = _dot3(c * _sigmoid(c), w_ref[...]) + b_ref[...]


def _ada(cc, w_ada, b_ada):
    rows = cc.shape[0]
    return pl.pallas_call(
        _ada_kernel,
        grid=(N_MOD,),
        in_specs=[pl.BlockSpec((rows, D_MODEL), lambda j: (0, 0)),
                  pl.BlockSpec((D_MODEL, D_MODEL), lambda j: (0, j)),
                  pl.BlockSpec((1, D_MODEL), lambda j: (0, j))],
        out_specs=pl.BlockSpec((rows, D_MODEL), lambda j: (0, j)),
        out_shape=jax.ShapeDtypeStruct((rows, N_MOD * D_MODEL), F32),
        compiler_params=_params("parallel"),
        name="ada",
    )(cc, w_ada, b_ada.reshape(1, -1))


INPROJ_COL_CHUNK = 512


def _inproj_kernel(n_out, conv_out, x_ref, sc_ref, sh_ref, nw_ref, *refs):
    n_in = n_out + (0 if conv_out is None else 2)
    w_refs, o_refs = refs[:n_out], refs[n_in:]
    h = _rms(x_ref[0], nw_ref[...]) * (1.0 + sc_ref[0]) + sh_ref[0]
    hb = h.astype(BF16)
    for i, (w_ref, o_ref) in enumerate(zip(w_refs, o_refs)):
        n = w_ref.shape[1]
        for c0 in range(0, n, INPROJ_COL_CHUNK):
            c1 = min(c0 + INPROJ_COL_CHUNK, n)
            y = _dot(hb, w_ref[:, c0:c1])
            if i == conv_out:
                cw_ref, cb_ref = refs[n_out:n_in]
                y = _short_conv(y, cw_ref[:, c0:c1], cb_ref[:, c0:c1])
            o_ref[0, :, c0:c1] = y.astype(o_ref.dtype)


def _inproj(x, mods, mod_row, norm_w, weights, out_dtypes, tl, conv=None):
    bsz, seq, _ = x.shape
    n_out = len(weights)
    assert tl % GRID_W == 0
    in_specs = [pl.BlockSpec((1, tl, D_MODEL), lambda b, i: (b, i, 0)),
                pl.BlockSpec((1, 1, D_MODEL), lambda b, i: (mod_row(b), 0, MOD_SC1)),
                pl.BlockSpec((1, 1, D_MODEL), lambda b, i: (mod_row(b), 0, MOD_SH1)),
                _resident((1, D_MODEL))]
    in_specs += [_resident(w.shape) for w in weights]
    extra = []
    if conv is not None:
        extra = [conv[1], conv[2].reshape(1, -1)]
        in_specs += [_resident(a.shape) for a in extra]
    out_specs = [pl.BlockSpec((1, tl, w.shape[1]), lambda b, i: (b, i, 0)) for w in weights]
    out_shape = [jax.ShapeDtypeStruct((bsz, seq, w.shape[1]), dt) for w, dt in zip(weights, out_dtypes)]
    return pl.pallas_call(
        functools.partial(_inproj_kernel, n_out, None if conv is None else conv[0]),
        grid=(bsz, seq // tl),
        in_specs=in_specs, out_specs=out_specs, out_shape=out_shape,
        compiler_params=_params("parallel", "parallel"),
        name="inproj",
    )(x, mods, mods, norm_w.reshape(1, -1), *weights, *extra)


GLA_TILE = 256
GLA_NCH = GLA_TILE // GLA_CHUNK
GLA_SCALE = GLA_HEAD_K ** -0.5
GLA_HPS = 4
GLA_KW = GLA_HPS * GLA_HEAD_K
GLA_VW = GLA_HPS * GLA_HEAD_V


@functools.lru_cache(maxsize=None)
def _chunk_triangles():
    i = np.arange(GLA_TILE)
    same = (i[:, None] // GLA_CHUNK) == (i[None, :] // GLA_CHUNK)
    return np.stack([same & (i[None, :] <= i[:, None]), same & (i[None, :] >= i[:, None])]).astype(np.float32)


def _gla_kernel(tri_ref, qf_ref, kf_ref, vf_ref, rf_ref, qb_ref, kb_ref, vb_ref, rb_ref,
                kc_ref, vc_ref, rcf_ref, rcb_ref, wgk_ref, bgk_ref,
                of_ref, ob_ref, sf_ref, sb_ref):
    t = pl.program_id(2)
    crow = lax.broadcasted_iota(jnp.int32, (GLA_CHUNK, GLA_CHUNK), 0)
    ccol = lax.broadcasted_iota(jnp.int32, (GLA_CHUNK, GLA_CHUNK), 1)
    heads = range(GLA_HPS)

    def rows(c):
        return slice(c * GLA_CHUNK, (c + 1) * GLA_CHUNK)

    def hk(h):
        return slice(h * GLA_HEAD_K, (h + 1) * GLA_HEAD_K)

    def hv(h):
        return slice(h * GLA_HEAD_V, (h + 1) * GLA_HEAD_V)

    def order(fwd):
        return range(GLA_NCH) if fwd else range(GLA_NCH - 1, -1, -1)

    def cum_decay(r, d):
        rh, rl = _split(r)
        wh, wl = _split(wgk_ref[d])
        z = _dot(jnp.concatenate([rh, rl, rh], axis=1), jnp.concatenate([wh, wh, wl], axis=0)) + bgk_ref[d]
        g = _log_sigmoid(z) * (1.0 / GLA_GATE_NORM)
        gh, gl = _split(g)
        return _dot(tri_ref[d], gh) + _dot(tri_ref[d], gl)

    def updates(k_ref, v_ref, b, fwd):
        out = {}
        for c in range(GLA_NCH):
            i = c * GLA_CHUNK + (GLA_CHUNK - 1 if fwd else 0)
            total = b[i:i + 1, :]
            k_upd = (k_ref[0, rows(c), :].astype(F32) * jnp.exp(total - b[rows(c)])).astype(BF16)
            dec = jnp.exp(total)
            for h in heads:
                col = jnp.broadcast_to(dec[:, hk(h)], (GLA_HEAD_K, GLA_HEAD_K)).T
                col = jnp.concatenate([col] * (GLA_HEAD_V // GLA_HEAD_K), axis=1)
                out[c, h] = (col, _dot_tn(k_upd[:, hk(h)], v_ref[0, rows(c), hv(h)]))
        return out

    def context_state(s_ref, r_ref, d, fwd):
        upd = updates(kc_ref, vc_ref, cum_decay(r_ref[0], d), fwd)
        for h in heads:
            st = jnp.zeros((GLA_HEAD_K, GLA_HEAD_V), F32)
            for c in order(fwd):
                dec, inc = upd[c, h]
                st = st * dec + inc
            s_ref[h] = st

    def scan(s_ref, q_ref, k_ref, v_ref, r_ref, o_ref, d, fwd):
        mask = (ccol <= crow) if fwd else (ccol >= crow)
        b = cum_decay(r_ref[0], d)
        q_dec = (q_ref[0].astype(F32) * (jnp.exp(b) * GLA_SCALE)).astype(BF16)
        k_inv = (k_ref[0].astype(F32) * jnp.exp(-b)).astype(BF16)
        upd = updates(k_ref, v_ref, b, fwd)
        att = {}
        for c in range(GLA_NCH):
            for h in heads:
                scores = _dot_nt(q_dec[rows(c), hk(h)], k_inv[rows(c), hk(h)])
                att[c, h] = jnp.where(mask, scores, 0.0).astype(BF16)
        enter = {}
        for h in heads:
            st = s_ref[h]
            for c in order(fwd):
                enter[c, h] = st.astype(BF16)
                dec, inc = upd[c, h]
                st = st * dec + inc
            s_ref[h] = st
        for c in range(GLA_NCH):
            for h in heads:
                lhs = jnp.concatenate([q_dec[rows(c), hk(h)], att[c, h]], axis=1)
                rhs = jnp.concatenate([enter[c, h], v_ref[0, rows(c), hv(h)]], axis=0)
                o_ref[0, rows(c), hv(h)] = _dot(lhs, rhs).astype(o_ref.dtype)

    @pl.when(t == 0)
    def _():
        context_state(sf_ref, rcf_ref, 0, True)
        context_state(sb_ref, rcb_ref, 1, False)

    scan(sf_ref, qf_ref, kf_ref, vf_ref, rf_ref, of_ref, 0, True)
    scan(sb_ref, qb_ref, kb_ref, vb_ref, rb_ref, ob_ref, 1, False)


def _gla(q, k, v, r_f, r_b, k_c, v_c, rc_f, rc_b, w_gk, b_gk):
    bsz, seq, _ = q.shape
    assert k_c.shape[1] == GLA_TILE
    nt = seq // GLA_TILE
    tri = jnp.asarray(_chunk_triangles()).astype(BF16)
    fwd = lambda b, h, t: (b, t, h)
    bwd = lambda b, h, t: (b, nt - 1 - t, h)
    fwd0 = lambda b, h, t: (b, t, 0)
    bwd0 = lambda b, h, t: (b, nt - 1 - t, 0)
    ctx = lambda b, h, t: (b, 0, h)
    ctx0 = lambda b, h, t: (b, 0, 0)
    kblk = (1, GLA_TILE, GLA_KW)
    vblk = (1, GLA_TILE, GLA_VW)
    rblk = (1, GLA_TILE, GLA_GATE_RANK)
    in_specs = [_resident(tri.shape),
                pl.BlockSpec(kblk, fwd), pl.BlockSpec(kblk, fwd), pl.BlockSpec(vblk, fwd), pl.BlockSpec(rblk, fwd0),
                pl.BlockSpec(kblk, bwd), pl.BlockSpec(kblk, bwd), pl.BlockSpec(vblk, bwd), pl.BlockSpec(rblk, bwd0),
                pl.BlockSpec(kblk, ctx), pl.BlockSpec(vblk, ctx), pl.BlockSpec(rblk, ctx0), pl.BlockSpec(rblk, ctx0),
                pl.BlockSpec((2, GLA_GATE_RANK, GLA_KW), lambda b, h, t: (0, 0, h)),
                pl.BlockSpec((2, 1, GLA_KW), lambda b, h, t: (0, 0, h))]
    out_specs = [pl.BlockSpec(vblk, fwd), pl.BlockSpec(vblk, bwd)]
    out_shape = [jax.ShapeDtypeStruct((bsz, seq, GLA_DV), BF16)] * 2
    return pl.pallas_call(
        _gla_kernel,
        grid=(bsz, GLA_HEADS // GLA_HPS, nt),
        in_specs=in_specs, out_specs=out_specs, out_shape=out_shape,
        scratch_shapes=[pltpu.VMEM((GLA_HPS, GLA_HEAD_K, GLA_HEAD_V), F32)] * 2,
        compiler_params=_params("parallel", "parallel", "arbitrary"),
        name="gla",
    )(tri, q, k, v, r_f, q, k, v, r_b, k_c, v_c, rc_f, rc_b, w_gk, b_gk)


HY_CW = 512
HY_ROW_CHUNK = 512
MLP_PAD = 128


@functools.lru_cache(maxsize=None)
def _dft_tables(seq):
    n, half = 2 * seq, seq // 2
    m = np.arange(half)[:, None]
    s = np.arange(half)[None, :]
    ang = 2.0 * np.pi * np.arange(n) / n
    cos = lambda k: np.cos(ang)[(k * s) % n]
    sin = lambda k: np.sin(ang)[(k * s) % n]
    sym = np.concatenate([cos(2 * m), sin(2 * m + 1)], axis=0)
    anti = np.concatenate([sin(2 * m), cos(2 * m + 1)], axis=0)
    tables = (sym, anti, sym.T, anti.T)
    return tuple(np.ascontiguousarray(t, dtype=np.float32) for t in tables)


@functools.lru_cache(maxsize=None)
def _filter_features(seq):
    bands = (HY_EMB - 1) // 2
    pos = np.arange(seq, dtype=np.float64)[:, None]
    t = pos / max(seq - 1, 1)
    f = np.linspace(1e-4, bands - 1, bands)[None]
    ang = (2.0 * math.pi / seq) * pos * f
    z = np.concatenate([t, np.cos(ang), -np.sin(ang)], axis=-1)
    out = np.zeros((seq, MLP_PAD), np.float32)
    out[:, :HY_EMB] = z
    deltas = np.abs(np.linspace(math.log(HY_TARGET) / HY_SLOW_DECAY, math.log(HY_TARGET) / HY_FAST_DECAY, HY_WIDTH))
    return out, deltas.astype(np.float32)[None]


def _filter_kernel(z_ref, w1_ref, b1_ref, w2_ref, b2_ref, w3_ref, b3_ref, fr_ref, w4f_ref, w4b_ref, dl_ref,
                   hs_ref, hd_ref, pn_ref, hid_ref):
    z = z_ref[...]

    @pl.when(pl.program_id(0) == 0)
    def _():
        fr = fr_ref[...]
        h = jnp.sin(fr * (_dot3(z, w1_ref[...]) + b1_ref[...]))
        h = jnp.sin(fr * (_dot3(h, w2_ref[...]) + b2_ref[...]))
        hid_ref[...] = jnp.sin(fr * (_dot3(h, w3_ref[...]) + b3_ref[...]))

    h = hid_ref[...]
    window = jnp.exp(-z[:, 0:1] * dl_ref[...])
    h_f = _dot3(h, w4f_ref[...]) * window
    h_b = _dot3(h, w4b_ref[...]) * window
    pos = lax.broadcasted_iota(jnp.int32, (z.shape[0], 1), 0)
    h_b = jnp.where(pos == 0, 0.0, h_b)
    hs = h_f + h_b
    hs_ref[...] = hs.astype(BF16)
    hd_ref[...] = (h_b - h_f).astype(BF16)
    sign = jnp.where(pos % 2 == 0, 1.0, -1.0)
    pn_ref[...] = jnp.sum(hs * sign, axis=0, keepdims=True)


def _pad2(a, rows, cols):
    return jnp.zeros((rows, cols), F32).at[:a.shape[0], :a.shape[1]].set(a.astype(F32))


def _hyena_filter(seq, w1, b1, w2, b2, w3, b3, w4, freq):
    z_np, deltas_np = _filter_features(seq)
    p = MLP_PAD
    args = [jnp.asarray(z_np), _pad2(w1, p, p), _pad2(b1[None], 1, p), _pad2(w2, p, p), _pad2(b2[None], 1, p),
            _pad2(w3, p, p), _pad2(b3[None], 1, p), _pad2(freq[None], 1, p),
            _pad2(w4[:, :HY_WIDTH], p, HY_WIDTH), _pad2(w4[:, HY_WIDTH:], p, HY_WIDTH), jnp.asarray(deltas_np)]
    full = lambda shape: pl.BlockSpec(shape, lambda j: (0, 0))
    colblk = lambda rows: pl.BlockSpec((rows, HY_CW), lambda j: (0, j))
    in_specs = [full((seq, p)), full((p, p)), full((1, p)), full((p, p)), full((1, p)), full((p, p)), full((1, p)),
                full((1, p)), colblk(p), colblk(p), colblk(1)]
    return pl.pallas_call(
        _filter_kernel,
        grid=(HY_WIDTH // HY_CW,),
        in_specs=in_specs,
        out_specs=[colblk(seq), colblk(seq), colblk(1)],
        out_shape=[jax.ShapeDtypeStruct((seq, HY_WIDTH), BF16), jax.ShapeDtypeStruct((seq, HY_WIDTH), BF16),
                   jax.ShapeDtypeStruct((1, HY_WIDTH), F32)],
        scratch_shapes=[pltpu.VMEM((seq, p), F32)],
        compiler_params=_params("arbitrary"),
        name="hy_filter",
    )(*args)


REV_BLOCK = 256


def _negate_index(x):
    n = x.shape[0]
    r = lax.broadcasted_iota(jnp.int32, (REV_BLOCK, REV_BLOCK), 0)
    c = lax.broadcasted_iota(jnp.int32, (REV_BLOCK, REV_BLOCK), 1)
    flip = jnp.where(r + c == REV_BLOCK - 1, 1.0, 0.0).astype(BF16)
    if x.dtype == BF16:
        pieces = [x]
    else:
        x0 = x.astype(BF16)
        r1 = x - x0.astype(F32)
        x1 = r1.astype(BF16)
        pieces = [x0, x1, (r1 - x1.astype(F32)).astype(BF16)]
    blocks = []
    for i in range(n // REV_BLOCK - 1, -1, -1):
        rows = slice(i * REV_BLOCK, (i + 1) * REV_BLOCK)
        acc = _dot(flip, pieces[0][rows])
        for p in pieces[1:]:
            acc = acc + _dot(flip, p[rows])
        blocks.append(acc)
    return pltpu.roll(jnp.concatenate(blocks, axis=0), 1, 0)


def _fold(x):
    half = x.shape[0] // 2
    lo = x[:half].astype(F32)
    hi = _negate_index(x[half:])
    pos = lax.broadcasted_iota(jnp.int32, (half, 1), 0)
    mid = hi[0:1]
    hi = jnp.where(pos == 0, 0.0, hi)
    return (lo + hi).astype(BF16), (lo - hi).astype(BF16), mid


def _alternating(n):
    pos = lax.broadcasted_iota(jnp.int32, (n, 1), 0)
    return jnp.where(pos % 2 == 0, 1.0, -1.0)


def _spectrum_kernel(ts_ref, ta_ref, hs_ref, hd_ref, pn_ref, bias_ref, p_ref, q_ref, kn_ref):
    half = hs_ref.shape[0] // 2
    bias = bias_ref[...]
    sgn = _alternating(half)
    s_sym, s_anti, s_mid = _fold(hs_ref[...])
    d_sym, d_anti, d_mid = _fold(hd_ref[...])
    p_ref[:half, :] = _dot(ts_ref[:half, :], s_sym) + sgn * s_mid + bias
    p_ref[half:, :] = _dot(ta_ref[half:, :], s_anti) + bias
    q_ref[:half, :] = _dot(ta_ref[:half, :], d_anti)
    q_ref[half:, :] = _dot(ts_ref[half:, :], d_sym) + sgn * d_mid
    kn_ref[...] = pn_ref[...] + bias


def _hyena_spectrum(t_sym, t_anti, hs, hd, pn, bias):
    seq = hs.shape[0]
    colblk = lambda rows: pl.BlockSpec((rows, HY_CW), lambda j: (0, j))
    return pl.pallas_call(
        _spectrum_kernel,
        grid=(HY_WIDTH // HY_CW,),
        in_specs=[_resident(t_sym.shape), _resident(t_anti.shape), colblk(seq), colblk(seq), colblk(1), colblk(1)],
        out_specs=[colblk(seq), colblk(seq), colblk(1)],
        out_shape=[jax.ShapeDtypeStruct((seq, HY_WIDTH), F32)] * 2 + [jax.ShapeDtypeStruct((1, HY_WIDTH), F32)],
        compiler_params=_params("parallel"),
        name="hy_spectrum",
    )(t_sym, t_anti, hs, hd, pn, bias.reshape(1, -1))


def _short_conv(u, w, b):
    n = u.shape[0]
    pos = lax.broadcasted_iota(jnp.int32, (n, 1), 0) % GRID_W
    up = jnp.where(pos == 0, 0.0, pltpu.roll(u, 1, 0))
    dn = jnp.where(pos == GRID_W - 1, 0.0, pltpu.roll(u, n - 1, 0))
    return up * w[0:1] + u * w[1:2] + dn * w[2:3] + b


def _hy_fwd_kernel(ts_ref, ta_ref, x1_ref, v_ref, p_ref, q_ref, kn_ref, s_ref):
    seq = x1_ref.shape[1]
    n, half = 2 * seq, seq // 2
    u_sym, u_anti, u_mid = _fold((v_ref[0].astype(F32) * x1_ref[0].astype(F32)).astype(BF16))
    sgn = _alternating(half)
    nyq = (jnp.sum(sgn * u_sym.astype(F32), axis=0, keepdims=True) + u_mid) * kn_ref[...] * (1.0 / n)
    for r0 in range(0, half, HY_ROW_CHUNK):
        r1 = r0 + HY_ROW_CHUNK
        mid = sgn[r0:r1] * u_mid
        re_e = _dot(ts_ref[r0:r1, :], u_sym) + mid
        im_o = _dot(ts_ref[half + r0:half + r1, :], u_sym) + mid
        im_e = _dot(ta_ref[r0:r1, :], u_anti)
        re_o = _dot(ta_ref[half + r0:half + r1, :], u_anti)
        pe, po = p_ref[r0:r1, :], p_ref[half + r0:half + r1, :]
        qe, qo = q_ref[r0:r1, :], q_ref[half + r0:half + r1, :]
        s_re_e = (re_e * pe + im_e * qe) * (2.0 / n)
        s_im_e = (im_e * pe - re_e * qe) * (2.0 / n)
        if r0 == 0:
            pos = lax.broadcasted_iota(jnp.int32, (HY_ROW_CHUNK, 1), 0)
            s_re_e = jnp.where(pos == 0, re_e * pe * (1.0 / n), s_re_e)
            s_im_e = jnp.where(pos == 0, nyq, s_im_e)
        s_ref[0, r0:r1, :] = s_re_e.astype(BF16)
        s_ref[0, half + r0:half + r1, :] = ((im_o * po - re_o * qo) * (2.0 / n)).astype(BF16)
        s_ref[0, seq + r0:seq + r1, :] = s_im_e.astype(BF16)
        s_ref[0, seq + half + r0:seq + half + r1, :] = ((re_o * po + im_o * qo) * (2.0 / n)).astype(BF16)


def _hy_inv_kernel(gs_ref, ga_ref, s_ref, x0_ref, o_ref):
    seq = x0_ref.shape[1]
    half = seq // 2
    x0 = x0_ref[0].astype(F32)
    sgn = _alternating(half)
    nyq = sgn * s_ref[0, seq:seq + 1, :].astype(F32)
    mirrored = []
    for r0 in range(0, half, HY_ROW_CHUNK):
        r1 = r0 + HY_ROW_CHUNK
        sym = _dot(gs_ref[r0:r1, :], s_ref[0, :seq, :]) + nyq[r0:r1]
        anti = _dot(ga_ref[r0:r1, :], s_ref[0, seq:, :])
        o_ref[0, r0:r1, :] = (x0[r0:r1] * (sym + anti)).astype(o_ref.dtype)
        mirrored.append(sym - anti)
    centre = (jnp.sum(sgn * (s_ref[0, :half, :].astype(F32) + s_ref[0, half:seq, :].astype(F32)), axis=0, keepdims=True)
              + nyq[0:1])
    pos = lax.broadcasted_iota(jnp.int32, (half, 1), 0)
    upper = _negate_index(jnp.where(pos == 0, centre, jnp.concatenate(mirrored, axis=0)).astype(BF16))
    o_ref[0, half:, :] = (x0[half:] * upper).astype(o_ref.dtype)


def _hyena_conv(hy, tabs, p_spec, q_spec, k_nyq):
    bsz, seq, _ = hy.shape
    nc = HY_WIDTH // HY_CW
    t_sym, t_anti, g_sym, g_anti = tabs
    sig = lambda part: pl.BlockSpec((1, seq, HY_CW), lambda j, b: (b, 0, part * nc + j))
    spec = lambda rows: pl.BlockSpec((rows, HY_CW), lambda j, b: (0, j))
    s = pl.pallas_call(
        _hy_fwd_kernel,
        grid=(nc, bsz),
        in_specs=[_resident(t_sym.shape), _resident(t_anti.shape), sig(1), sig(2),
                  spec(seq), spec(seq), spec(1)],
        out_specs=pl.BlockSpec((1, 2 * seq, HY_CW), lambda j, b: (b, 0, j)),
        out_shape=jax.ShapeDtypeStruct((bsz, 2 * seq, HY_WIDTH), BF16),
        compiler_params=_params("parallel", "parallel"),
        name="hy_fwd",
    )(t_sym, t_anti, hy, hy, p_spec, q_spec, k_nyq)
    return pl.pallas_call(
        _hy_inv_kernel,
        grid=(nc, bsz),
        in_specs=[_resident(g_sym.shape), _resident(g_anti.shape),
                  pl.BlockSpec((1, 2 * seq, HY_CW), lambda j, b: (b, 0, j)), sig(0)],
        out_specs=pl.BlockSpec((1, seq, HY_CW), lambda j, b: (b, 0, j)),
        out_shape=jax.ShapeDtypeStruct((bsz, seq, HY_WIDTH), BF16),
        compiler_params=_params("parallel", "parallel"),
        name="hy_inv",
    )(g_sym, g_anti, s, hy)


ROW_PARTS = D_MODEL // LANES


def _token_tile(t):
    return pl.ds(pl.multiple_of(t * ROW_PARTS, ROW_PARTS), ROW_PARTS)


def _load_token_rows(ref, n_tok):
    return jnp.concatenate([ref[pl.ds(j, n_tok, stride=ROW_PARTS), :] for j in range(ROW_PARTS)], axis=-1)


def _store_token_rows(ref, val):
    for j in range(ROW_PARTS):
        ref[pl.ds(j, val.shape[0], stride=ROW_PARTS), :] = val[:, j * LANES:(j + 1) * LANES]


POST_TILE = 512


def _post_kernel(of_ref, ob_ref, og_ref, yh_ref, ma_ref, mb_ref, x_ref, g1_ref, sc2_ref, sh2_ref,
                 gn_ref, nf_ref, wa_ref, wb_ref, wo_ref, wr_ref, br_ref,
                 x1_ref, h2_ref, idx_ref, wts_ref, cnt_ref):
    o = of_ref[0].astype(F32) + ob_ref[0].astype(F32)
    gn = gn_ref[...]
    heads = [_rms(o[:, h * GLA_HEAD_V:(h + 1) * GLA_HEAD_V], gn) for h in range(GLA_HEADS)]
    og = og_ref[0].astype(F32)
    a_in = jnp.concatenate(heads, axis=-1) * (og * _sigmoid(og))
    branch_a = _dot(a_in.astype(BF16), wa_ref[...])
    branch_b = _dot(yh_ref[0], wb_ref[...])
    y = _sigmoid(ma_ref[0].astype(F32)) * branch_a + _sigmoid(mb_ref[0].astype(F32)) * branch_b
    x1 = x_ref[0] + g1_ref[0] * _dot(y.astype(BF16), wo_ref[...])
    x1_ref[0] = x1
    h2 = _rms(x1, nf_ref[...]) * (1.0 + sc2_ref[0]) + sh2_ref[0]
    _store_token_rows(h2_ref.at[0], h2)

    h_hi, h_lo = _split(h2)
    w_hi, w_lo = _split(wr_ref[...])
    both = _dot(h_hi, jnp.concatenate([w_hi, w_lo], axis=-1))
    logits = both[:, :N_EXPERTS] + (both[:, N_EXPERTS:] + _dot(h_lo, w_hi)) + br_ref[...]
    lane = lax.broadcasted_iota(jnp.int32, logits.shape, 1)
    slot = lax.broadcasted_iota(jnp.int32, (logits.shape[0], TOP_K), 1)
    work = logits
    tops = []
    idx = jnp.zeros((logits.shape[0], TOP_K), jnp.int32)
    chosen = jnp.zeros(logits.shape, F32)
    for k in range(TOP_K):
        m = jnp.max(work, axis=-1, keepdims=True)
        first = jnp.min(jnp.where(work == m, lane, N_EXPERTS), axis=-1, keepdims=True)
        tops.append(m)
        idx = jnp.where(slot == k, first, idx)
        chosen = jnp.where(lane == first, 1.0, chosen)
        work = jnp.where(lane == first, -jnp.inf, work)
    es = [jnp.exp(m - tops[0]) for m in tops]
    inv = 1.0 / functools.reduce(lambda a, b: a + b, es)
    wts = jnp.zeros((logits.shape[0], TOP_K), F32)
    for k, e in enumerate(es):
        wts = jnp.where(slot == k, e * inv, wts)
    idx_ref[0] = idx
    wts_ref[0] = wts

    @pl.when((pl.program_id(0) == 0) & (pl.program_id(1) == 0))
    def _():
        cnt_ref[...] = jnp.zeros_like(cnt_ref)

    cnt_ref[...] += jnp.sum(chosen, axis=0, keepdims=True)


def _post(o_f, o_b, og, y_h, m_a, m_b, x, mods, gla_norm, norm_ffn, w_a, w_b, w_o, w_router, b_router):
    bsz, seq, _ = x.shape
    tok = lambda n: pl.BlockSpec((1, POST_TILE, n), lambda b, i: (b, i, 0))
    mod = lambda which: pl.BlockSpec((1, 1, D_MODEL), lambda b, i: (b, 0, which))
    in_specs = [tok(D_MODEL)] * 7 + [mod(MOD_G1), mod(MOD_SC2), mod(MOD_SH2),
                                     _resident((1, GLA_HEAD_V)), _resident((1, D_MODEL)),
                                     _resident(w_a.shape), _resident(w_b.shape), _resident(w_o.shape),
                                     _resident(w_router.shape), _resident((1, N_EXPERTS))]
    return pl.pallas_call(
        _post_kernel,
        grid=(bsz, seq // POST_TILE),
        in_specs=in_specs,
        out_specs=[tok(D_MODEL), pl.BlockSpec((1, POST_TILE * ROW_PARTS, LANES), lambda b, i: (b, i, 0)),
                   tok(TOP_K), tok(TOP_K), pl.BlockSpec((1, N_EXPERTS), lambda b, i: (0, 0))],
        out_shape=[jax.ShapeDtypeStruct((bsz, seq, D_MODEL), F32),
                   jax.ShapeDtypeStruct((bsz, seq * ROW_PARTS, LANES), F32),
                   jax.ShapeDtypeStruct((bsz, seq, TOP_K), jnp.int32), jax.ShapeDtypeStruct((bsz, seq, TOP_K), F32),
                   jax.ShapeDtypeStruct((1, N_EXPERTS), F32)],
        compiler_params=_params("arbitrary", "arbitrary"),
        name="post",
    )(o_f, o_b, og, y_h, m_a, m_b, x, mods, mods, mods, gla_norm.reshape(1, -1), norm_ffn.reshape(1, -1),
      w_a, w_b, w_o, w_router, b_router.reshape(1, -1))


UP_GROUP = 2 * LANES
N_UP_GROUPS = 2 * D_MODEL // UP_GROUP


def _stage_expert_weights(wu_ref, wd_ref, wu_s, wd_s):
    r = lax.broadcasted_iota(jnp.int32, (UP_GROUP, UP_GROUP), 0)
    c = lax.broadcasted_iota(jnp.int32, (UP_GROUP, UP_GROUP), 1)
    perm = jnp.where(c == (r % 2) * LANES + r // 2, 1.0, 0.0).astype(BF16)
    for g in range(N_UP_GROUPS):
        cols = slice(g * UP_GROUP, (g + 1) * UP_GROUP)
        wu_s[:, cols] = _dot(wu_ref[:, cols].astype(BF16), perm).astype(BF16)
    wd_s[...] = wd_ref[...].astype(BF16)


RANK_TILE = 1024
ROUTE_TILE = 512
EXPERT_TILE = 512
EXPERT_BLOCK = 512
DOWN_COLS_PER_DOT = 256
COMBINE_TILE = 256


def _exact_count_dot(a, m):
    a0 = a.astype(BF16)
    r1 = a - a0.astype(F32)
    a1 = r1.astype(BF16)
    a2 = (r1 - a1.astype(F32)).astype(BF16)
    return _dot(a0, m) + (_dot(a1, m) + _dot(a2, m))


def _route_kernel(idx_ref, cnt_ref, pos_ref, base, offs):
    i = pl.program_id(0)
    n_tok = idx_ref.shape[0]
    idx = idx_ref[...]
    lane = lax.broadcasted_iota(jnp.int32, (n_tok, N_EXPERTS), 1)
    hot = jnp.zeros((n_tok, N_EXPERTS), F32)
    for k in range(TOP_K):
        hot = hot + jnp.where(lane == idx[:, k:k + 1], 1.0, 0.0)

    @pl.when(i == 0)
    def _():
        r = lax.broadcasted_iota(jnp.int32, (N_EXPERTS, N_EXPERTS), 0)
        c = lax.broadcasted_iota(jnp.int32, (N_EXPERTS, N_EXPERTS), 1)
        before = jnp.where(r < c, 1.0, 0.0).astype(BF16)
        offs[...] = _exact_count_dot(jnp.broadcast_to(cnt_ref[...], (8, N_EXPERTS)), before)[0:1]
        base[...] = jnp.zeros_like(base)

    r = lax.broadcasted_iota(jnp.int32, (n_tok, n_tok), 0)
    c = lax.broadcasted_iota(jnp.int32, (n_tok, n_tok), 1)
    earlier = jnp.where(c < r, 1.0, 0.0).astype(BF16)
    dense = offs[...] + base[...] + _dot(earlier, hot.astype(BF16))
    slot = lax.broadcasted_iota(jnp.int32, (n_tok, TOP_K), 1)
    pos = jnp.zeros((n_tok, TOP_K), F32)
    for k in range(TOP_K):
        mine = jnp.sum(jnp.where(lane == idx[:, k:k + 1], dense, 0.0), axis=-1, keepdims=True)
        pos = jnp.where(slot == k, mine, pos)
    pos_ref[...] = pos.astype(jnp.int32)
    base[...] += jnp.sum(hot, axis=0, keepdims=True)


def _route(idx, counts):
    n_tok = idx.shape[0]
    return pl.pallas_call(
        _route_kernel,
        grid=(n_tok // RANK_TILE,),
        in_specs=[pl.BlockSpec((RANK_TILE, TOP_K), lambda i: (i, 0)),
                  pl.BlockSpec((1, N_EXPERTS), lambda i: (0, 0))],
        out_specs=pl.BlockSpec((RANK_TILE, TOP_K), lambda i: (i, 0)),
        out_shape=jax.ShapeDtypeStruct((n_tok, TOP_K), jnp.int32),
        scratch_shapes=[pltpu.VMEM((1, N_EXPERTS), F32)] * 2,
        compiler_params=_params("arbitrary"),
        name="route",
    )(idx, counts)


def _start_rows(n_tok, copy):
    def issue(t, carry):
        for k in range(TOP_K):
            copy(t, k).start(priority=k % 2)
        return carry

    lax.fori_loop(0, n_tok, issue, 0, unroll=4)


def _wait_rows(n_tok, copy):
    def drain(t, carry):
        for k in range(TOP_K):
            copy(t, k).wait()
        return carry

    lax.fori_loop(0, n_tok, drain, 0, unroll=4)


def _dispatch_kernel(pos_ref, h_ref, xs_ref, sem):
    def copy(t, k):
        return pltpu.make_async_copy(h_ref.at[_token_tile(t), :], xs_ref.at[_token_tile(pos_ref[t * TOP_K + k]), :], sem)

    _start_rows(ROUTE_TILE, copy)
    _wait_rows(ROUTE_TILE, copy)


def _index_blocks(tile, n_steps):
    blk = (tile * TOP_K,)
    return [pl.BlockSpec(blk, lambda i: (i,), memory_space=pltpu.SMEM),
            pl.BlockSpec(blk, lambda i: (jnp.minimum(i + 1, n_steps - 1),), memory_space=pltpu.SMEM)]


def _dispatch(pos_flat, h2):
    n_tok = h2.shape[0] // ROW_PARTS
    n_steps = n_tok // ROUTE_TILE
    return pl.pallas_call(
        _dispatch_kernel,
        grid=(n_steps,),
        in_specs=[_index_blocks(ROUTE_TILE, n_steps)[0],
                  pl.BlockSpec((ROUTE_TILE * ROW_PARTS, LANES), lambda i: (i, 0))],
        out_specs=pl.BlockSpec(memory_space=pl.ANY),
        out_shape=jax.ShapeDtypeStruct((n_tok * TOP_K * ROW_PARTS, LANES), F32),
        scratch_shapes=[pltpu.SemaphoreType.DMA],
        compiler_params=_params("arbitrary"),
        name="dispatch",
    )(pos_flat, h2)


def _experts_kernel(tile_ref, exp_ref, nxt_ref, lo_ref, hi_ref, x_ref, wu_hbm, bu_ref, wd_hbm, bd_ref, o_ref,
                    acc_ref, wu_s, wd_s, wu_f, wd_f, sem):
    w = pl.program_id(0)
    lo = lo_ref[w]
    hi = hi_ref[w]

    def fetch(e):
        return (pltpu.make_async_copy(wu_hbm.at[e], wu_f, sem.at[0]),
                pltpu.make_async_copy(wd_hbm.at[e], wd_f, sem.at[1]))

    @pl.when(w == 0)
    def _():
        acc_ref[...] = jnp.zeros_like(acc_ref)
        for copy in fetch(exp_ref[0]):
            copy.start()

    @pl.when((w == 0) | (exp_ref[w] != exp_ref[jnp.maximum(w - 1, 0)]))
    def _():
        for copy in fetch(exp_ref[w]):
            copy.wait()
        _stage_expert_weights(wu_f, wd_f, wu_s, wd_s)

        @pl.when(nxt_ref[w] >= 0)
        def _():
            for copy in fetch(nxt_ref[w]):
                copy.start()

    def block(r0):
        first = tile_ref[w] * EXPERT_TILE + r0
        rows = slice(r0, r0 + EXPERT_BLOCK)
        touched = (hi > lo) & (lo < first + EXPERT_BLOCK) & (hi > first)
        whole = (lo <= first) & (hi >= first + EXPERT_BLOCK)

        def compute(partial):
            x_rows = x_ref.at[pl.ds(r0 * ROW_PARTS, EXPERT_BLOCK * ROW_PARTS), :]
            h = _load_token_rows(x_rows, EXPERT_BLOCK).astype(BF16)
            acts = []
            for g in range(N_UP_GROUPS):
                cols = slice(g * UP_GROUP, (g + 1) * UP_GROUP)
                u = _dot(h, wu_s[:, cols]) + bu_ref[0, :, cols]
                glu = jnp.minimum(u[:, :LANES], SWIGLU_LIMIT)
                lin = jnp.clip(u[:, LANES:], -SWIGLU_LIMIT, SWIGLU_LIMIT)
                acts.append((glu * _sigmoid(SWIGLU_ALPHA * glu) * (lin + 1.0)).astype(BF16))
            a = jnp.concatenate(acts, axis=-1)
            row = first + lax.broadcasted_iota(jnp.int32, (EXPERT_BLOCK, 1), 0)
            mine = (row >= lo) & (row < hi)
            for c0 in range(0, D_MODEL, DOWN_COLS_PER_DOT):
                cols = slice(c0, c0 + DOWN_COLS_PER_DOT)
                kept = _dot(a, wd_s[:, cols]) + bd_ref[0, :, cols]
                if partial:
                    kept = jnp.where(mine, kept, acc_ref[rows, cols])
                    acc_ref[rows, cols] = kept
                for j in range(DOWN_COLS_PER_DOT // LANES):
                    part = pl.ds(r0 * ROW_PARTS + c0 // LANES + j, EXPERT_BLOCK, stride=ROW_PARTS)
                    o_ref[part, :] = kept[:, j * LANES:(j + 1) * LANES]

        pl.when(touched & whole)(functools.partial(compute, False))
        pl.when(touched & jnp.logical_not(whole))(functools.partial(compute, True))

    for r0 in range(0, EXPERT_TILE, EXPERT_BLOCK):
        block(r0)


def _work_items(counts, n_rows):
    n_tiles = n_rows // EXPERT_TILE
    ends = jnp.cumsum(counts)
    tile_ends = jnp.arange(1, n_tiles + 1, dtype=jnp.int32) * EXPERT_TILE
    n_items = n_tiles + N_EXPERTS
    count = lambda cond: jnp.sum(cond.astype(jnp.int32), axis=1)
    slot_t = jnp.arange(n_tiles, dtype=jnp.int32) + count(ends[None, :] < tile_ends[:, None])
    slot_e = jnp.arange(N_EXPERTS, dtype=jnp.int32) + count(tile_ends[None, :] <= ends[:, None])
    slots = jnp.concatenate([slot_t, slot_e])
    vals = jnp.concatenate([tile_ends, ends])
    item = jnp.arange(n_items, dtype=jnp.int32)
    hi = jnp.sum(jnp.where(slots[None, :] == item[:, None], vals[None, :], 0), axis=1)
    lo = jnp.concatenate([jnp.zeros((1,), jnp.int32), hi[:-1]])
    tile_id = jnp.minimum(lo // EXPERT_TILE, n_tiles - 1)
    exp_id = jnp.minimum(count(ends[None, :] <= lo[:, None]), N_EXPERTS - 1)
    later = jnp.where(exp_id[None, :] > exp_id[:, None], exp_id[None, :], N_EXPERTS)
    nxt_id = jnp.min(later, axis=1)
    nxt_id = jnp.where(nxt_id == N_EXPERTS, -1, nxt_id)
    return tile_id, exp_id, nxt_id, lo, hi


def _experts(items, xs, w_up, b_up_g, w_down, b_down):
    n_items = items[0].shape[0]
    rows = lambda w, t, e, nxt, lo, hi: (t[w], 0)
    bias = lambda c: pl.BlockSpec((1, 1, c), lambda w, t, e, nxt, lo, hi: (e[w], 0, 0))
    hbm = pl.BlockSpec(memory_space=pl.ANY)
    return pl.pallas_call(
        _experts_kernel,
        grid_spec=pltpu.PrefetchScalarGridSpec(
            num_scalar_prefetch=5,
            grid=(n_items,),
            in_specs=[pl.BlockSpec((EXPERT_TILE * ROW_PARTS, LANES), rows),
                      hbm, bias(2 * D_MODEL), hbm, bias(D_MODEL)],
            out_specs=pl.BlockSpec((EXPERT_TILE * ROW_PARTS, LANES), rows),
            scratch_shapes=[pltpu.VMEM((EXPERT_TILE, D_MODEL), F32),
                            pltpu.VMEM((D_MODEL, 2 * D_MODEL), BF16), pltpu.VMEM((D_MODEL, D_MODEL), BF16),
                            pltpu.VMEM((D_MODEL, 2 * D_MODEL), F32), pltpu.VMEM((D_MODEL, D_MODEL), F32),
                            pltpu.SemaphoreType.DMA((2,))]),
        out_shape=jax.ShapeDtypeStruct(xs.shape, F32),
        compiler_params=_params("arbitrary"),
        name="experts",
    )(*items, xs, w_up, b_up_g, w_down, b_down)


def _combine_kernel(pos_ref, nxt_ref, ys_ref, wts_ref, x1_ref, g2_ref, nf_ref, o_ref, buf_a, buf_b, sem):
    i = pl.program_id(0)
    sub = COMBINE_TILE

    def gather(p_ref, first_tok, buf, s):
        def copy(t, k):
            src = ys_ref.at[_token_tile(p_ref[(first_tok + t) * TOP_K + k]), :]
            return pltpu.make_async_copy(src, buf.at[k, _token_tile(t), :], sem.at[s])
        return copy

    this_a = gather(pos_ref, 0, buf_a, 0)
    this_b = gather(pos_ref, sub, buf_b, 1)
    next_a = gather(nxt_ref, 0, buf_a, 0)

    def reduce_rows(buf, half, start_other):
        rows = slice(half * sub, (half + 1) * sub)
        wts = wts_ref[rows, :]
        parts = []
        for j in range(ROW_PARTS):
            for t in range(j * sub // ROW_PARTS, (j + 1) * sub // ROW_PARTS):
                for k in range(TOP_K):
                    start_other(t, k).start(priority=k % 2)
            part = wts[:, 0:1] * buf[0, pl.ds(j, sub, stride=ROW_PARTS), :]
            for k in range(1, TOP_K):
                part = part + wts[:, k:k + 1] * buf[k, pl.ds(j, sub, stride=ROW_PARTS), :]
            parts.append(part)
        acc = jnp.concatenate(parts, axis=-1)
        o_ref[rows, :] = _rms(x1_ref[rows, :] + g2_ref[0] * acc, nf_ref[...])

    @pl.when(i == 0)
    def _():
        _start_rows(sub, this_a)

    _wait_rows(sub, this_a)
    reduce_rows(buf_a, 0, this_b)
    _wait_rows(sub, this_b)
    reduce_rows(buf_b, 1, next_a)

    @pl.when(i == pl.num_programs(0) - 1)
    def _():
        _wait_rows(sub, next_a)


def _combine(pos_flat, ys, wts, x1, mods, norm_final, seq):
    n_tok = x1.shape[0]
    step_tok = 2 * COMBINE_TILE
    steps_per_sample = seq // step_tok
    tok = lambda n: pl.BlockSpec((step_tok, n), lambda i: (i, 0))
    n_steps = n_tok // step_tok
    gather_buf = pltpu.VMEM((TOP_K, COMBINE_TILE * ROW_PARTS, LANES), F32)
    return pl.pallas_call(
        _combine_kernel,
        grid=(n_steps,),
        in_specs=_index_blocks(step_tok, n_steps) + [
            pl.BlockSpec(memory_space=pl.ANY), tok(TOP_K), tok(D_MODEL),
            pl.BlockSpec((1, 1, D_MODEL), lambda i: (i // steps_per_sample, 0, MOD_G2)),
            pl.BlockSpec((1, D_MODEL), lambda i: (0, 0))],
        out_specs=tok(D_MODEL),
        out_shape=jax.ShapeDtypeStruct((n_tok, D_MODEL), F32),
        scratch_shapes=[gather_buf, gather_buf, pltpu.SemaphoreType.DMA((2,))],
        compiler_params=_params("arbitrary"),
        name="combine",
    )(pos_flat, pos_flat, ys, wts, x1, mods, norm_final.reshape(1, -1))


def _moe(h2, idx, wts, counts, w_up, b_up_g, w_down, b_down, x1, mods, norm_final):
    bsz, seq, _ = x1.shape
    n_tok = bsz * seq
    pos_flat = _route(idx.reshape(n_tok, TOP_K), counts).reshape(-1)
    xs = _dispatch(pos_flat, h2.reshape(n_tok * ROW_PARTS, LANES))
    items = _work_items(counts.reshape(-1).astype(jnp.int32), n_tok * TOP_K)
    ys = _experts(items, xs, w_up, b_up_g, w_down, b_down)
    out = _combine(pos_flat, ys, wts.reshape(n_tok, TOP_K), x1.reshape(n_tok, D_MODEL), mods, norm_final, seq)
    return out.reshape(bsz, seq, D_MODEL)


def kernel(x, c, ctx, c_ctx, w_ada, b_ada, norm_mix, norm_ffn, w_in, w_gk_f, b_gk_f, w_gk_b, b_gk_b, gla_norm, w_gla_out, hy_conv_w, hy_conv_b, hy_f_w1, hy_f_b1, hy_f_w2, hy_f_b2, hy_f_w3, hy_f_b3, hy_f_w4, hy_sin_freq, hy_bias, w_hy_out, w_out, w_router, b_router, w_up, b_up, w_down, b_down, norm_final):
    depth = w_ada.shape[0]
    assert depth == 1, "single-layer block: the context stream only feeds later layers"
    bsz, seq, _ = x.shape

    n_rows = -(-(bsz + 1) // 8) * 8
    cc = jnp.zeros((n_rows, D_MODEL), F32).at[:bsz].set(c).at[bsz].set(c_ctx)
    mods = _ada(cc, w_ada[0], b_ada[0]).reshape(n_rows, 1, N_MOD * D_MODEL)

    sizes = (GLA_DK, GLA_DK, GLA_DV, GLA_DV, GLA_GATE_RANK, GLA_GATE_RANK, 3 * HY_WIDTH, D_MODEL, D_MODEL)
    offs = np.concatenate([[0], np.cumsum(sizes)])
    w_in_b = w_in[0].astype(BF16)
    wq, wk, wv, wog, wrf, wrb, why, wma, wmb = [w_in_b[:, offs[i]:offs[i + 1]] for i in range(len(sizes))]
    q, k, v, og, r_f, r_b, hy, m_a, m_b = _inproj(
        x, mods, lambda b: b, norm_mix[0], [wq, wk, wv, wog, wrf, wrb, why, wma, wmb],
        [BF16, BF16, BF16, BF16, F32, F32, BF16, BF16, BF16], tl=512, conv=(6, hy_conv_w[0], hy_conv_b[0]))
    k_c, v_c, rc_f, rc_b = _inproj(
        ctx, mods, lambda b: bsz, norm_mix[0], [wk, wv, wrf, wrb], [BF16, BF16, F32, F32], tl=ctx.shape[1])

    w_gk = jnp.stack([w_gk_f[0], w_gk_b[0]])
    b_gk = jnp.stack([b_gk_f[0], b_gk_b[0]])[:, None, :]
    o_f, o_b = _gla(q, k, v, r_f, r_b, k_c, v_c, rc_f, rc_b, w_gk, b_gk)

    tabs = [jnp.asarray(t).astype(BF16) for t in _dft_tables(seq)]
    t_sym, t_anti = tabs[:2]
    hs, hd, pn = _hyena_filter(seq, hy_f_w1[0], hy_f_b1[0], hy_f_w2[0], hy_f_b2[0], hy_f_w3[0], hy_f_b3[0],
                               hy_f_w4[0], hy_sin_freq[0])
    p_spec, q_spec, k_nyq = _hyena_spectrum(t_sym, t_anti, hs, hd, pn, hy_bias[0])
    y_h = _hyena_conv(hy, tabs, p_spec, q_spec, k_nyq)

    x1, h2, idx, wts, counts = _post(o_f, o_b, og, y_h, m_a, m_b, x, mods, gla_norm[0], norm_ffn[0],
                                     w_gla_out[0].astype(BF16), w_hy_out[0].astype(BF16), w_out[0].astype(BF16),
                                     w_router[0], b_router[0])

    b_up_g = b_up[0].reshape(N_EXPERTS, N_UP_GROUPS, LANES, 2).transpose(0, 1, 3, 2).reshape(N_EXPERTS, 1, -1)
    return _moe(h2, idx, wts, counts, w_up[0], b_up_g, w_down[0], b_down[0][:, None, :],
                x1, mods, norm_final)
```

```python
import functools
import math

import jax
import jax.numpy as jnp
import numpy as np
from jax import lax
from jax.experimental import pallas as pl
from jax.experimental.pallas import tpu as pltpu

F32 = jnp.float32
BF16 = jnp.bfloat16

D_MODEL = 1024
GRID_W = 64
EPS = 1e-6
N_MOD = 6

GLA_HEADS = 4
GLA_HEAD_K = 128
GLA_HEAD_V = 256
GLA_DK = GLA_HEADS * GLA_HEAD_K
GLA_DV = GLA_HEADS * GLA_HEAD_V
GLA_GATE_RANK = 16
GLA_GATE_NORM = 16.0
GLA_CHUNK = 64

HY_WIDTH = D_MODEL
HY_EMB = 33
HY_FAST_DECAY = 0.3
HY_SLOW_DECAY = 1.5
HY_TARGET = 1e-2

N_EXPERTS = 32
TOP_K = 4
SWIGLU_LIMIT = 7.0
SWIGLU_ALPHA = 1.702

V7X_VMEM_BYTES = 64 * 1024 * 1024
VMEM_LIMIT = V7X_VMEM_BYTES - 8 * 1024 * 1024
LANES = 128

MOD_SH1, MOD_SC1, MOD_G1, MOD_SH2, MOD_SC2, MOD_G2 = range(N_MOD)


def _params(*sem):
    return pltpu.CompilerParams(dimension_semantics=sem, vmem_limit_bytes=VMEM_LIMIT)


def _resident(shape):
    return pl.BlockSpec(shape, lambda *_: (0,) * len(shape), pipeline_mode=pl.Buffered(1))


def _dot(a, b):
    return jnp.dot(a, b, preferred_element_type=F32)


def _dot_nt(a, b):
    return lax.dot_general(a, b, (((1,), (1,)), ((), ())), preferred_element_type=F32)


def _dot_tn(a, b):
    return lax.dot_general(a, b, (((0,), (0,)), ((), ())), preferred_element_type=F32)


def _split(a):
    hi = a.astype(BF16)
    lo = (a - hi.astype(F32)).astype(BF16)
    return hi, lo


def _dot3(a, b):
    ah, al = _split(a)
    bh, bl = _split(b)
    return _dot(ah, bh) + (_dot(ah, bl) + _dot(al, bh))


def _sigmoid(x):
    return 1.0 / (1.0 + jnp.exp(-x))


def _log_sigmoid(x):
    return jnp.minimum(x, 0.0) - jnp.log(1.0 + jnp.exp(-jnp.abs(x)))


def _rms(x, w):
    return x * lax.rsqrt(jnp.mean(x * x, axis=-1, keepdims=True) + EPS) * w


def _ada_kernel(c_ref, w_ref, b_ref, o_ref):
    c = c_ref[...]
    o_ref[...] = _dot3(c * _sigmoid(c), w_ref[...]) + b_ref[...]


def _ada(cc, w_ada, b_ada):
    rows = cc.shape[0]
    return pl.pallas_call(
        _ada_kernel,
        grid=(N_MOD,),
        in_specs=[pl.BlockSpec((rows, D_MODEL), lambda j: (0, 0)),
                  pl.BlockSpec((D_MODEL, D_MODEL), lambda j: (0, j)),
                  pl.BlockSpec((1, D_MODEL), lambda j: (0, j))],
        out_specs=pl.BlockSpec((rows, D_MODEL), lambda j: (0, j)),
        out_shape=jax.ShapeDtypeStruct((rows, N_MOD * D_MODEL), F32),
        compiler_params=_params("parallel"),
        name="ada",
    )(cc, w_ada, b_ada.reshape(1, -1))


INPROJ_COL_CHUNK = 512
CTX_PROJ_TILE = 1024


def _inproj_kernel(n_out, conv_out, x_ref, sc_ref, sh_ref, nw_ref, *refs):
    n_in = n_out + (0 if conv_out is None else 2)
    w_refs, o_refs = refs[:n_out], refs[n_in:]
    h = _rms(x_ref[0], nw_ref[...]) * (1.0 + sc_ref[0]) + sh_ref[0]
    hb = h.astype(BF16)
    for i, (w_ref, o_ref) in enumerate(zip(w_refs, o_refs)):
        n = w_ref.shape[1]
        for c0 in range(0, n, INPROJ_COL_CHUNK):
            c1 = min(c0 + INPROJ_COL_CHUNK, n)
            y = _dot(hb, w_ref[:, c0:c1])
            if i == conv_out:
                cw_ref, cb_ref = refs[n_out:n_in]
                y = _short_conv(y, cw_ref[:, c0:c1], cb_ref[:, c0:c1])
            o_ref[0, :, c0:c1] = y.astype(o_ref.dtype)


def _inproj(x, mods, mod_row, norm_w, weights, out_dtypes, tl, conv=None):
    bsz, seq, _ = x.shape
    n_out = len(weights)
    assert tl % GRID_W == 0
    in_specs = [pl.BlockSpec((1, tl, D_MODEL), lambda b, i: (b, i, 0)),
                pl.BlockSpec((1, 1, D_MODEL), lambda b, i: (mod_row(b), 0, MOD_SC1)),
                pl.BlockSpec((1, 1, D_MODEL), lambda b, i: (mod_row(b), 0, MOD_SH1)),
                _resident((1, D_MODEL))]
    in_specs += [_resident(w.shape) for w in weights]
    extra = []
    if conv is not None:
        extra = [conv[1], conv[2].reshape(1, -1)]
        in_specs += [_resident(a.shape) for a in extra]
    out_specs = [pl.BlockSpec((1, tl, w.shape[1]), lambda b, i: (b, i, 0)) for w in weights]
    out_shape = [jax.ShapeDtypeStruct((bsz, seq, w.shape[1]), dt) for w, dt in zip(weights, out_dtypes)]
    return pl.pallas_call(
        functools.partial(_inproj_kernel, n_out, None if conv is None else conv[0]),
        grid=(bsz, seq // tl),
        in_specs=in_specs, out_specs=out_specs, out_shape=out_shape,
        compiler_params=_params("parallel", "parallel"),
        name="inproj",
    )(x, mods, mods, norm_w.reshape(1, -1), *weights, *extra)


GLA_TILE = 256
GLA_NCH = GLA_TILE // GLA_CHUNK
GLA_SCALE = GLA_HEAD_K ** -0.5
GLA_HPS = 4
GLA_KW = GLA_HPS * GLA_HEAD_K
GLA_VW = GLA_HPS * GLA_HEAD_V


@functools.lru_cache(maxsize=None)
def _chunk_triangles():
    i = np.arange(GLA_TILE)
    same = (i[:, None] // GLA_CHUNK) == (i[None, :] // GLA_CHUNK)
    return np.stack([same & (i[None, :] <= i[:, None]), same & (i[None, :] >= i[:, None])]).astype(np.float32)


def _gla_kernel(tri_ref, qf_ref, kf_ref, vf_ref, rf_ref, qb_ref, kb_ref, vb_ref, rb_ref,
                kc_ref, vc_ref, rcf_ref, rcb_ref, wgk_ref, bgk_ref,
                of_ref, ob_ref, sf_ref, sb_ref):
    t = pl.program_id(2)
    crow = lax.broadcasted_iota(jnp.int32, (GLA_CHUNK, GLA_CHUNK), 0)
    ccol = lax.broadcasted_iota(jnp.int32, (GLA_CHUNK, GLA_CHUNK), 1)
    heads = range(GLA_HPS)

    def rows(c):
        return slice(c * GLA_CHUNK, (c + 1) * GLA_CHUNK)

    def hk(h):
        return slice(h * GLA_HEAD_K, (h + 1) * GLA_HEAD_K)

    def hv(h):
        return slice(h * GLA_HEAD_V, (h + 1) * GLA_HEAD_V)

    def order(fwd):
        return range(GLA_NCH) if fwd else range(GLA_NCH - 1, -1, -1)

    def cum_decay(r, d):
        rh, rl = _split(r)
        wh, wl = _split(wgk_ref[d])
        z = _dot(jnp.concatenate([rh, rl, rh], axis=1), jnp.concatenate([wh, wh, wl], axis=0)) + bgk_ref[d]
        g = _log_sigmoid(z) * (1.0 / GLA_GATE_NORM)
        gh, gl = _split(g)
        return _dot(tri_ref[d], gh) + _dot(tri_ref[d], gl)

    def updates(k_ref, v_ref, b, fwd):
        out = {}
        for c in range(GLA_NCH):
            i = c * GLA_CHUNK + (GLA_CHUNK - 1 if fwd else 0)
            total = b[i:i + 1, :]
            k_upd = (k_ref[0, rows(c), :].astype(F32) * jnp.exp(total - b[rows(c)])).astype(BF16)
            dec = jnp.exp(total)
            for h in heads:
                col = jnp.broadcast_to(dec[:, hk(h)], (GLA_HEAD_K, GLA_HEAD_K)).T
                col = jnp.concatenate([col] * (GLA_HEAD_V // GLA_HEAD_K), axis=1)
                out[c, h] = (col, _dot_tn(k_upd[:, hk(h)], v_ref[0, rows(c), hv(h)]))
        return out

    def context_state(s_ref, r_ref, d, fwd):
        upd = updates(kc_ref, vc_ref, cum_decay(r_ref[0], d), fwd)
        for h in heads:
            st = jnp.zeros((GLA_HEAD_K, GLA_HEAD_V), F32)
            for c in order(fwd):
                dec, inc = upd[c, h]
                st = st * dec + inc
            s_ref[h] = st

    def scan(s_ref, q_ref, k_ref, v_ref, r_ref, o_ref, d, fwd):
        mask = (ccol <= crow) if fwd else (ccol >= crow)
        b = cum_decay(r_ref[0], d)
        q_dec = (q_ref[0].astype(F32) * (jnp.exp(b) * GLA_SCALE)).astype(BF16)
        k_inv = (k_ref[0].astype(F32) * jnp.exp(-b)).astype(BF16)
        upd = updates(k_ref, v_ref, b, fwd)
        att = {}
        for c in range(GLA_NCH):
            for h in heads:
                scores = _dot_nt(q_dec[rows(c), hk(h)], k_inv[rows(c), hk(h)])
                att[c, h] = jnp.where(mask, scores, 0.0).astype(BF16)
        enter = {}
        for h in heads:
            st = s_ref[h]
            for c in order(fwd):
                enter[c, h] = st.astype(BF16)
                dec, inc = upd[c, h]
                st = st * dec + inc
            s_ref[h] = st
        for c in range(GLA_NCH):
            for h in heads:
                lhs = jnp.concatenate([q_dec[rows(c), hk(h)], att[c, h]], axis=1)
                rhs = jnp.concatenate([enter[c, h], v_ref[0, rows(c), hv(h)]], axis=0)
                o_ref[0, rows(c), hv(h)] = _dot(lhs, rhs).astype(o_ref.dtype)

    @pl.when(t == 0)
    def _():
        context_state(sf_ref, rcf_ref, 0, True)
        context_state(sb_ref, rcb_ref, 1, False)

    scan(sf_ref, qf_ref, kf_ref, vf_ref, rf_ref, of_ref, 0, True)
    scan(sb_ref, qb_ref, kb_ref, vb_ref, rb_ref, ob_ref, 1, False)


def _gla(q, k, v, r_f, r_b, k_c, v_c, rc_f, rc_b, w_gk, b_gk):
    bsz, seq, _ = q.shape
    assert k_c.shape[1] == GLA_TILE
    nt = seq // GLA_TILE
    tri = jnp.asarray(_chunk_triangles()).astype(BF16)
    fwd = lambda b, h, t: (b, t, h)
    bwd = lambda b, h, t: (b, nt - 1 - t, h)
    fwd0 = lambda b, h, t: (b, t, 0)
    bwd0 = lambda b, h, t: (b, nt - 1 - t, 0)
    ctx = lambda b, h, t: (b, 0, h)
    ctx0 = lambda b, h, t: (b, 0, 0)
    kblk = (1, GLA_TILE, GLA_KW)
    vblk = (1, GLA_TILE, GLA_VW)
    rblk = (1, GLA_TILE, GLA_GATE_RANK)
    in_specs = [_resident(tri.shape),
                pl.BlockSpec(kblk, fwd), pl.BlockSpec(kblk, fwd), pl.BlockSpec(vblk, fwd), pl.BlockSpec(rblk, fwd0),
                pl.BlockSpec(kblk, bwd), pl.BlockSpec(kblk, bwd), pl.BlockSpec(vblk, bwd), pl.BlockSpec(rblk, bwd0),
                pl.BlockSpec(kblk, ctx), pl.BlockSpec(vblk, ctx), pl.BlockSpec(rblk, ctx0), pl.BlockSpec(rblk, ctx0),
                pl.BlockSpec((2, GLA_GATE_RANK, GLA_KW), lambda b, h, t: (0, 0, h)),
                pl.BlockSpec((2, 1, GLA_KW), lambda b, h, t: (0, 0, h))]
    out_specs = [pl.BlockSpec(vblk, fwd), pl.BlockSpec(vblk, bwd)]
    out_shape = [jax.ShapeDtypeStruct((bsz, seq, GLA_DV), BF16)] * 2
    return pl.pallas_call(
        _gla_kernel,
        grid=(bsz, GLA_HEADS // GLA_HPS, nt),
        in_specs=in_specs, out_specs=out_specs, out_shape=out_shape,
        scratch_shapes=[pltpu.VMEM((GLA_HPS, GLA_HEAD_K, GLA_HEAD_V), F32)] * 2,
        compiler_params=_params("parallel", "parallel", "arbitrary"),
        name="gla",
    )(tri, q, k, v, r_f, q, k, v, r_b, k_c, v_c, rc_f, rc_b, w_gk, b_gk)


HY_CW = 512
HY_ROW_CHUNK = 512
MLP_PAD = 128


@functools.lru_cache(maxsize=None)
def _dft_tables(seq):
    n, half = 2 * seq, seq // 2
    m = np.arange(half)[:, None]
    s = np.arange(half)[None, :]
    ang = 2.0 * np.pi * np.arange(n) / n
    cos = lambda k: np.cos(ang)[(k * s) % n]
    sin = lambda k: np.sin(ang)[(k * s) % n]
    sym = np.concatenate([cos(2 * m), sin(2 * m + 1)], axis=0)
    anti = np.concatenate([sin(2 * m), cos(2 * m + 1)], axis=0)
    tables = (sym, anti, sym.T, anti.T)
    return tuple(np.ascontiguousarray(t, dtype=np.float32) for t in tables)


@functools.lru_cache(maxsize=None)
def _filter_features(seq):
    bands = (HY_EMB - 1) // 2
    pos = np.arange(seq, dtype=np.float64)[:, None]
    t = pos / max(seq - 1, 1)
    f = np.linspace(1e-4, bands - 1, bands)[None]
    ang = (2.0 * math.pi / seq) * pos * f
    z = np.concatenate([t, np.cos(ang), -np.sin(ang)], axis=-1)
    out = np.zeros((seq, MLP_PAD), np.float32)
    out[:, :HY_EMB] = z
    deltas = np.abs(np.linspace(math.log(HY_TARGET) / HY_SLOW_DECAY, math.log(HY_TARGET) / HY_FAST_DECAY, HY_WIDTH))
    return out, deltas.astype(np.float32)[None]


def _filter_kernel(z_ref, w1_ref, b1_ref, w2_ref, b2_ref, w3_ref, b3_ref, fr_ref, w4f_ref, w4b_ref, dl_ref,
                   hs_ref, hd_ref, pn_ref, hid_ref):
    z = z_ref[...]

    @pl.when(pl.program_id(0) == 0)
    def _():
        fr = fr_ref[...]
        h = jnp.sin(fr * (_dot3(z, w1_ref[...]) + b1_ref[...]))
        h = jnp.sin(fr * (_dot3(h, w2_ref[...]) + b2_ref[...]))
        hid_ref[...] = jnp.sin(fr * (_dot3(h, w3_ref[...]) + b3_ref[...]))

    h = hid_ref[...]
    window = jnp.exp(-z[:, 0:1] * dl_ref[...])
    h_f = _dot3(h, w4f_ref[...]) * window
    h_b = _dot3(h, w4b_ref[...]) * window
    pos = lax.broadcasted_iota(jnp.int32, (z.shape[0], 1), 0)
    h_b = jnp.where(pos == 0, 0.0, h_b)
    hs = h_f + h_b
    hs_ref[...] = hs.astype(BF16)
    hd_ref[...] = (h_b - h_f).astype(BF16)
    sign = jnp.where(pos % 2 == 0, 1.0, -1.0)
    pn_ref[...] = jnp.sum(hs * sign, axis=0, keepdims=True)


def _pad2(a, rows, cols):
    return jnp.zeros((rows, cols), F32).at[:a.shape[0], :a.shape[1]].set(a.astype(F32))


def _hyena_filter(seq, w1, b1, w2, b2, w3, b3, w4, freq):
    z_np, deltas_np = _filter_features(seq)
    p = MLP_PAD
    args = [jnp.asarray(z_np), _pad2(w1, p, p), _pad2(b1[None], 1, p), _pad2(w2, p, p), _pad2(b2[None], 1, p),
            _pad2(w3, p, p), _pad2(b3[None], 1, p), _pad2(freq[None], 1, p),
            _pad2(w4[:, :HY_WIDTH], p, HY_WIDTH), _pad2(w4[:, HY_WIDTH:], p, HY_WIDTH), jnp.asarray(deltas_np)]
    full = lambda shape: pl.BlockSpec(shape, lambda j: (0, 0))
    colblk = lambda rows: pl.BlockSpec((rows, HY_CW), lambda j: (0, j))
    in_specs = [full((seq, p)), full((p, p)), full((1, p)), full((p, p)), full((1, p)), full((p, p)), full((1, p)),
                full((1, p)), colblk(p), colblk(p), colblk(1)]
    return pl.pallas_call(
        _filter_kernel,
        grid=(HY_WIDTH // HY_CW,),
        in_specs=in_specs,
        out_specs=[colblk(seq), colblk(seq), colblk(1)],
        out_shape=[jax.ShapeDtypeStruct((seq, HY_WIDTH), BF16), jax.ShapeDtypeStruct((seq, HY_WIDTH), BF16),
                   jax.ShapeDtypeStruct((1, HY_WIDTH), F32)],
        scratch_shapes=[pltpu.VMEM((seq, p), F32)],
        compiler_params=_params("arbitrary"),
        name="hy_filter",
    )(*args)


REV_BLOCK = 256


def _negate_index(x):
    n = x.shape[0]
    r = lax.broadcasted_iota(jnp.int32, (REV_BLOCK, REV_BLOCK), 0)
    c = lax.broadcasted_iota(jnp.int32, (REV_BLOCK, REV_BLOCK), 1)
    flip = jnp.where(r + c == REV_BLOCK - 1, 1.0, 0.0).astype(BF16)
    if x.dtype == BF16:
        pieces = [x]
    else:
        x0 = x.astype(BF16)
        r1 = x - x0.astype(F32)
        x1 = r1.astype(BF16)
        pieces = [x0, x1, (r1 - x1.astype(F32)).astype(BF16)]
    blocks = []
    for i in range(n // REV_BLOCK - 1, -1, -1):
        rows = slice(i * REV_BLOCK, (i + 1) * REV_BLOCK)
        acc = _dot(flip, pieces[0][rows])
        for p in pieces[1:]:
            acc = acc + _dot(flip, p[rows])
        blocks.append(acc)
    return pltpu.roll(jnp.concatenate(blocks, axis=0), 1, 0)


def _fold(x):
    half = x.shape[0] // 2
    lo = x[:half].astype(F32)
    hi = _negate_index(x[half:])
    pos = lax.broadcasted_iota(jnp.int32, (half, 1), 0)
    mid = hi[0:1]
    hi = jnp.where(pos == 0, 0.0, hi)
    return (lo + hi).astype(BF16), (lo - hi).astype(BF16), mid


def _alternating(n):
    pos = lax.broadcasted_iota(jnp.int32, (n, 1), 0)
    return jnp.where(pos % 2 == 0, 1.0, -1.0)


def _spectrum_kernel(ts_ref, ta_ref, hs_ref, hd_ref, pn_ref, bias_ref, p_ref, q_ref, kn_ref):
    half = hs_ref.shape[0] // 2
    bias = bias_ref[...]
    sgn = _alternating(half)
    s_sym, s_anti, s_mid = _fold(hs_ref[...])
    d_sym, d_anti, d_mid = _fold(hd_ref[...])
    p_ref[:half, :] = _dot(ts_ref[:half, :], s_sym) + sgn * s_mid + bias
    p_ref[half:, :] = _dot(ta_ref[half:, :], s_anti) + bias
    q_ref[:half, :] = _dot(ta_ref[:half, :], d_anti)
    q_ref[half:, :] = _dot(ts_ref[half:, :], d_sym) + sgn * d_mid
    kn_ref[...] = pn_ref[...] + bias


def _hyena_spectrum(t_sym, t_anti, hs, hd, pn, bias):
    seq = hs.shape[0]
    colblk = lambda rows: pl.BlockSpec((rows, HY_CW), lambda j: (0, j))
    return pl.pallas_call(
        _spectrum_kernel,
        grid=(HY_WIDTH // HY_CW,),
        in_specs=[_resident(t_sym.shape), _resident(t_anti.shape), colblk(seq), colblk(seq), colblk(1), colblk(1)],
        out_specs=[colblk(seq), colblk(seq), colblk(1)],
        out_shape=[jax.ShapeDtypeStruct((seq, HY_WIDTH), F32)] * 2 + [jax.ShapeDtypeStruct((1, HY_WIDTH), F32)],
        compiler_params=_params("parallel"),
        name="hy_spectrum",
    )(t_sym, t_anti, hs, hd, pn, bias.reshape(1, -1))


def _short_conv(u, w, b):
    n = u.shape[0]
    pos = lax.broadcasted_iota(jnp.int32, (n, 1), 0) % GRID_W
    up = jnp.where(pos == 0, 0.0, pltpu.roll(u, 1, 0))
    dn = jnp.where(pos == GRID_W - 1, 0.0, pltpu.roll(u, n - 1, 0))
    return up * w[0:1] + u * w[1:2] + dn * w[2:3] + b


def _hy_fwd_kernel(ts_ref, ta_ref, x1_ref, v_ref, p_ref, q_ref, kn_ref, s_ref):
    seq = x1_ref.shape[1]
    n, half = 2 * seq, seq // 2
    u_sym, u_anti, u_mid = _fold((v_ref[0].astype(F32) * x1_ref[0].astype(F32)).astype(BF16))
    sgn = _alternating(half)
    nyq = (jnp.sum(sgn * u_sym.astype(F32), axis=0, keepdims=True) + u_mid) * kn_ref[...] * (1.0 / n)
    for r0 in range(0, half, HY_ROW_CHUNK):
        r1 = r0 + HY_ROW_CHUNK
        mid = sgn[r0:r1] * u_mid
        re_e = _dot(ts_ref[r0:r1, :], u_sym) + mid
        im_o = _dot(ts_ref[half + r0:half + r1, :], u_sym) + mid
        im_e = _dot(ta_ref[r0:r1, :], u_anti)
        re_o = _dot(ta_ref[half + r0:half + r1, :], u_anti)
        pe, po = p_ref[r0:r1, :], p_ref[half + r0:half + r1, :]
        qe, qo = q_ref[r0:r1, :], q_ref[half + r0:half + r1, :]
        s_re_e = (re_e * pe + im_e * qe) * (2.0 / n)
        s_im_e = (im_e * pe - re_e * qe) * (2.0 / n)
        if r0 == 0:
            pos = lax.broadcasted_iota(jnp.int32, (HY_ROW_CHUNK, 1), 0)
            s_re_e = jnp.where(pos == 0, re_e * pe * (1.0 / n), s_re_e)
            s_im_e = jnp.where(pos == 0, nyq, s_im_e)
        s_ref[0, r0:r1, :] = s_re_e.astype(BF16)
        s_ref[0, half + r0:half + r1, :] = ((im_o * po - re_o * qo) * (2.0 / n)).astype(BF16)
        s_ref[0, seq + r0:seq + r1, :] = s_im_e.astype(BF16)
        s_ref[0, seq + half + r0:seq + half + r1, :] = ((re_o * po + im_o * qo) * (2.0 / n)).astype(BF16)


def _hy_inv_kernel(gs_ref, ga_ref, s_ref, x0_ref, o_ref):
    seq = x0_ref.shape[1]
    half = seq // 2
    x0 = x0_ref[0].astype(F32)
    sgn = _alternating(half)
    nyq = sgn * s_ref[0, seq:seq + 1, :].astype(F32)
    mirrored = []
    for r0 in range(0, half, HY_ROW_CHUNK):
        r1 = r0 + HY_ROW_CHUNK
        sym = _dot(gs_ref[r0:r1, :], s_ref[0, :seq, :]) + nyq[r0:r1]
        anti = _dot(ga_ref[r0:r1, :], s_ref[0, seq:, :])
        o_ref[0, r0:r1, :] = (x0[r0:r1] * (sym + anti)).astype(o_ref.dtype)
        mirrored.append(sym - anti)
    centre = (jnp.sum(sgn * (s_ref[0, :half, :].astype(F32) + s_ref[0, half:seq, :].astype(F32)), axis=0, keepdims=True)
              + nyq[0:1])
    pos = lax.broadcasted_iota(jnp.int32, (half, 1), 0)
    upper = _negate_index(jnp.where(pos == 0, centre, jnp.concatenate(mirrored, axis=0)).astype(BF16))
    o_ref[0, half:, :] = (x0[half:] * upper).astype(o_ref.dtype)


def _hyena_conv(hy, tabs, p_spec, q_spec, k_nyq):
    bsz, seq, _ = hy.shape
    nc = HY_WIDTH // HY_CW
    t_sym, t_anti, g_sym, g_anti = tabs
    sig = lambda part: pl.BlockSpec((1, seq, HY_CW), lambda j, b: (b, 0, part * nc + j))
    spec = lambda rows: pl.BlockSpec((rows, HY_CW), lambda j, b: (0, j))
    s = pl.pallas_call(
        _hy_fwd_kernel,
        grid=(nc, bsz),
        in_specs=[_resident(t_sym.shape), _resident(t_anti.shape), sig(1), sig(2),
                  spec(seq), spec(seq), spec(1)],
        out_specs=pl.BlockSpec((1, 2 * seq, HY_CW), lambda j, b: (b, 0, j)),
        out_shape=jax.ShapeDtypeStruct((bsz, 2 * seq, HY_WIDTH), BF16),
        compiler_params=_params("parallel", "parallel"),
        name="hy_fwd",
    )(t_sym, t_anti, hy, hy, p_spec, q_spec, k_nyq)
    return pl.pallas_call(
        _hy_inv_kernel,
        grid=(nc, bsz),
        in_specs=[_resident(g_sym.shape), _resident(g_anti.shape),
                  pl.BlockSpec((1, 2 * seq, HY_CW), lambda j, b: (b, 0, j)), sig(0)],
        out_specs=pl.BlockSpec((1, seq, HY_CW), lambda j, b: (b, 0, j)),
        out_shape=jax.ShapeDtypeStruct((bsz, seq, HY_WIDTH), BF16),
        compiler_params=_params("parallel", "parallel"),
        name="hy_inv",
    )(g_sym, g_anti, s, hy)


ROW_PARTS = D_MODEL // LANES


def _token_tile(t):
    return pl.ds(pl.multiple_of(t * ROW_PARTS, ROW_PARTS), ROW_PARTS)


def _load_token_rows(ref, n_tok):
    return jnp.concatenate([ref[pl.ds(j, n_tok, stride=ROW_PARTS), :] for j in range(ROW_PARTS)], axis=-1)


def _store_token_rows(ref, val):
    for j in range(ROW_PARTS):
        ref[pl.ds(j, val.shape[0], stride=ROW_PARTS), :] = val[:, j * LANES:(j + 1) * LANES]


POST_TILE = 512


def _post_kernel(of_ref, ob_ref, og_ref, yh_ref, ma_ref, mb_ref, x_ref, g1_ref, sc2_ref, sh2_ref,
                 gn_ref, nf_ref, wa_ref, wb_ref, wo_ref, wr_ref, br_ref,
                 x1_ref, h2_ref, idx_ref, wts_ref, cnt_ref):
    o = of_ref[0].astype(F32) + ob_ref[0].astype(F32)
    gn = gn_ref[...]
    heads = [_rms(o[:, h * GLA_HEAD_V:(h + 1) * GLA_HEAD_V], gn) for h in range(GLA_HEADS)]
    og = og_ref[0].astype(F32)
    a_in = jnp.concatenate(heads, axis=-1) * (og * _sigmoid(og))
    branch_a = _dot(a_in.astype(BF16), wa_ref[...])
    branch_b = _dot(yh_ref[0], wb_ref[...])
    y = _sigmoid(ma_ref[0].astype(F32)) * branch_a + _sigmoid(mb_ref[0].astype(F32)) * branch_b
    x1 = x_ref[0] + g1_ref[0] * _dot(y.astype(BF16), wo_ref[...])
    x1_ref[0] = x1
    h2 = _rms(x1, nf_ref[...]) * (1.0 + sc2_ref[0]) + sh2_ref[0]
    _store_token_rows(h2_ref.at[0], h2)

    h_hi, h_lo = _split(h2)
    w_hi, w_lo = _split(wr_ref[...])
    both = _dot(h_hi, jnp.concatenate([w_hi, w_lo], axis=-1))
    logits = both[:, :N_EXPERTS] + (both[:, N_EXPERTS:] + _dot(h_lo, w_hi)) + br_ref[...]
    lane = lax.broadcasted_iota(jnp.int32, logits.shape, 1)
    slot = lax.broadcasted_iota(jnp.int32, (logits.shape[0], TOP_K), 1)
    work = logits
    tops = []
    idx = jnp.zeros((logits.shape[0], TOP_K), jnp.int32)
    chosen = jnp.zeros(logits.shape, F32)
    for k in range(TOP_K):
        m = jnp.max(work, axis=-1, keepdims=True)
        first = jnp.min(jnp.where(work == m, lane, N_EXPERTS), axis=-1, keepdims=True)
        tops.append(m)
        idx = jnp.where(slot == k, first, idx)
        chosen = jnp.where(lane == first, 1.0, chosen)
        work = jnp.where(lane == first, -jnp.inf, work)
    es = [jnp.exp(m - tops[0]) for m in tops]
    inv = 1.0 / functools.reduce(lambda a, b: a + b, es)
    wts = jnp.zeros((logits.shape[0], TOP_K), F32)
    for k, e in enumerate(es):
        wts = jnp.where(slot == k, e * inv, wts)
    idx_ref[0] = idx
    wts_ref[0] = wts

    @pl.when((pl.program_id(0) == 0) & (pl.program_id(1) == 0))
    def _():
        cnt_ref[...] = jnp.zeros_like(cnt_ref)

    cnt_ref[...] += jnp.sum(chosen, axis=0, keepdims=True)


def _post(o_f, o_b, og, y_h, m_a, m_b, x, mods, gla_norm, norm_ffn, w_a, w_b, w_o, w_router, b_router):
    bsz, seq, _ = x.shape
    tok = lambda n: pl.BlockSpec((1, POST_TILE, n), lambda b, i: (b, i, 0))
    mod = lambda which: pl.BlockSpec((1, 1, D_MODEL), lambda b, i: (b, 0, which))
    in_specs = [tok(D_MODEL)] * 7 + [mod(MOD_G1), mod(MOD_SC2), mod(MOD_SH2),
                                     _resident((1, GLA_HEAD_V)), _resident((1, D_MODEL)),
                                     _resident(w_a.shape), _resident(w_b.shape), _resident(w_o.shape),
                                     _resident(w_router.shape), _resident((1, N_EXPERTS))]
    return pl.pallas_call(
        _post_kernel,
        grid=(bsz, seq // POST_TILE),
        in_specs=in_specs,
        out_specs=[tok(D_MODEL), pl.BlockSpec((1, POST_TILE * ROW_PARTS, LANES), lambda b, i: (b, i, 0)),
                   tok(TOP_K), tok(TOP_K), pl.BlockSpec((1, N_EXPERTS), lambda b, i: (0, 0))],
        out_shape=[jax.ShapeDtypeStruct((bsz, seq, D_MODEL), F32),
                   jax.ShapeDtypeStruct((bsz, seq * ROW_PARTS, LANES), F32),
                   jax.ShapeDtypeStruct((bsz, seq, TOP_K), jnp.int32), jax.ShapeDtypeStruct((bsz, seq, TOP_K), F32),
                   jax.ShapeDtypeStruct((1, N_EXPERTS), F32)],
        compiler_params=_params("arbitrary", "arbitrary"),
        name="post",
    )(o_f, o_b, og, y_h, m_a, m_b, x, mods, mods, mods, gla_norm.reshape(1, -1), norm_ffn.reshape(1, -1),
      w_a, w_b, w_o, w_router, b_router.reshape(1, -1))


UP_GROUP = 2 * LANES
N_UP_GROUPS = 2 * D_MODEL // UP_GROUP


def _stage_expert_weights(wu_ref, wd_ref, wu_s, wd_s):
    r = lax.broadcasted_iota(jnp.int32, (UP_GROUP, UP_GROUP), 0)
    c = lax.broadcasted_iota(jnp.int32, (UP_GROUP, UP_GROUP), 1)
    perm = jnp.where(c == (r % 2) * LANES + r // 2, 1.0, 0.0).astype(BF16)
    for g in range(N_UP_GROUPS):
        cols = slice(g * UP_GROUP, (g + 1) * UP_GROUP)
        wu_s[:, cols] = _dot(wu_ref[:, cols].astype(BF16), perm).astype(BF16)
    wd_s[...] = wd_ref[...].astype(BF16)


RANK_TILE = 1024
ROUTE_TILE = 512
EXPERT_TILE = 512
EXPERT_BLOCK = 512
DOWN_COLS_PER_DOT = 256
COMBINE_TILE = 256


def _exact_count_dot(a, m):
    a0 = a.astype(BF16)
    r1 = a - a0.astype(F32)
    a1 = r1.astype(BF16)
    a2 = (r1 - a1.astype(F32)).astype(BF16)
    return _dot(a0, m) + (_dot(a1, m) + _dot(a2, m))


def _route_kernel(idx_ref, cnt_ref, pos_ref, base, offs):
    i = pl.program_id(0)
    n_tok = idx_ref.shape[0]
    idx = idx_ref[...]
    lane = lax.broadcasted_iota(jnp.int32, (n_tok, N_EXPERTS), 1)
    hot = jnp.zeros((n_tok, N_EXPERTS), F32)
    for k in range(TOP_K):
        hot = hot + jnp.where(lane == idx[:, k:k + 1], 1.0, 0.0)

    @pl.when(i == 0)
    def _():
        r = lax.broadcasted_iota(jnp.int32, (N_EXPERTS, N_EXPERTS), 0)
        c = lax.broadcasted_iota(jnp.int32, (N_EXPERTS, N_EXPERTS), 1)
        before = jnp.where(r < c, 1.0, 0.0).astype(BF16)
        offs[...] = _exact_count_dot(jnp.broadcast_to(cnt_ref[...], (8, N_EXPERTS)), before)[0:1]
        base[...] = jnp.zeros_like(base)

    r = lax.broadcasted_iota(jnp.int32, (n_tok, n_tok), 0)
    c = lax.broadcasted_iota(jnp.int32, (n_tok, n_tok), 1)
    earlier = jnp.where(c < r, 1.0, 0.0).astype(BF16)
    dense = offs[...] + base[...] + _dot(earlier, hot.astype(BF16))
    slot = lax.broadcasted_iota(jnp.int32, (n_tok, TOP_K), 1)
    pos = jnp.zeros((n_tok, TOP_K), F32)
    for k in range(TOP_K):
        mine = jnp.sum(jnp.where(lane == idx[:, k:k + 1], dense, 0.0), axis=-1, keepdims=True)
        pos = jnp.where(slot == k, mine, pos)
    pos_ref[...] = pos.astype(jnp.int32)
    base[...] += jnp.sum(hot, axis=0, keepdims=True)


def _route(idx, counts):
    n_tok = idx.shape[0]
    return pl.pallas_call(
        _route_kernel,
        grid=(n_tok // RANK_TILE,),
        in_specs=[pl.BlockSpec((RANK_TILE, TOP_K), lambda i: (i, 0)),
                  pl.BlockSpec((1, N_EXPERTS), lambda i: (0, 0))],
        out_specs=pl.BlockSpec((RANK_TILE, TOP_K), lambda i: (i, 0)),
        out_shape=jax.ShapeDtypeStruct((n_tok, TOP_K), jnp.int32),
        scratch_shapes=[pltpu.VMEM((1, N_EXPERTS), F32)] * 2,
        compiler_params=_params("arbitrary"),
        name="route",
    )(idx, counts)


def _start_rows(n_tok, copy):
    def issue(t, carry):
        for k in range(TOP_K):
            copy(t, k).start(priority=k % 2)
        return carry

    lax.fori_loop(0, n_tok, issue, 0, unroll=4)


def _wait_rows(n_tok, copy):
    def drain(t, carry):
        for k in range(TOP_K):
            copy(t, k).wait()
        return carry

    lax.fori_loop(0, n_tok, drain, 0, unroll=4)


def _dispatch_kernel(pos_ref, h_ref, xs_ref, sem):
    def copy(t, k):
        return pltpu.make_async_copy(h_ref.at[_token_tile(t), :], xs_ref.at[_token_tile(pos_ref[t * TOP_K + k]), :], sem)

    _start_rows(ROUTE_TILE, copy)
    _wait_rows(ROUTE_TILE, copy)


def _index_blocks(tile, n_steps):
    blk = (tile * TOP_K,)
    return [pl.BlockSpec(blk, lambda i: (i,), memory_space=pltpu.SMEM),
            pl.BlockSpec(blk, lambda i: (jnp.minimum(i + 1, n_steps - 1),), memory_space=pltpu.SMEM)]


def _dispatch(pos_flat, h2):
    n_tok = h2.shape[0] // ROW_PARTS
    n_steps = n_tok // ROUTE_TILE
    return pl.pallas_call(
        _dispatch_kernel,
        grid=(n_steps,),
        in_specs=[_index_blocks(ROUTE_TILE, n_steps)[0],
                  pl.BlockSpec((ROUTE_TILE * ROW_PARTS, LANES), lambda i: (i, 0))],
        out_specs=pl.BlockSpec(memory_space=pl.ANY),
        out_shape=jax.ShapeDtypeStruct((n_tok * TOP_K * ROW_PARTS, LANES), F32),
        scratch_shapes=[pltpu.SemaphoreType.DMA],
        compiler_params=_params("arbitrary"),
        name="dispatch",
    )(pos_flat, h2)


def _experts_kernel(tile_ref, exp_ref, nxt_ref, lo_ref, hi_ref, x_ref, wu_hbm, bu_ref, wd_hbm, bd_ref, o_ref,
                    acc_ref, wu_s, wd_s, wu_f, wd_f, sem):
    w = pl.program_id(0)
    lo = lo_ref[w]
    hi = hi_ref[w]

    def fetch(e):
        return (pltpu.make_async_copy(wu_hbm.at[e], wu_f, sem.at[0]),
                pltpu.make_async_copy(wd_hbm.at[e], wd_f, sem.at[1]))

    @pl.when(w == 0)
    def _():
        acc_ref[...] = jnp.zeros_like(acc_ref)
        for copy in fetch(exp_ref[0]):
            copy.start()

    @pl.when((w == 0) | (exp_ref[w] != exp_ref[jnp.maximum(w - 1, 0)]))
    def _():
        for copy in fetch(exp_ref[w]):
            copy.wait()
        _stage_expert_weights(wu_f, wd_f, wu_s, wd_s)

        @pl.when(nxt_ref[w] >= 0)
        def _():
            for copy in fetch(nxt_ref[w]):
                copy.start()

    def block(r0):
        first = tile_ref[w] * EXPERT_TILE + r0
        rows = slice(r0, r0 + EXPERT_BLOCK)
        touched = (hi > lo) & (lo < first + EXPERT_BLOCK) & (hi > first)
        whole = (lo <= first) & (hi >= first + EXPERT_BLOCK)

        def compute(partial):
            x_rows = x_ref.at[pl.ds(r0 * ROW_PARTS, EXPERT_BLOCK * ROW_PARTS), :]
            h = _load_token_rows(x_rows, EXPERT_BLOCK).astype(BF16)
            acts = []
            for g in range(N_UP_GROUPS):
                cols = slice(g * UP_GROUP, (g + 1) * UP_GROUP)
                u = _dot(h, wu_s[:, cols]) + bu_ref[0, :, cols]
                glu = jnp.minimum(u[:, :LANES], SWIGLU_LIMIT)
                lin = jnp.clip(u[:, LANES:], -SWIGLU_LIMIT, SWIGLU_LIMIT)
                acts.append((glu * _sigmoid(SWIGLU_ALPHA * glu) * (lin + 1.0)).astype(BF16))
            a = jnp.concatenate(acts, axis=-1)
            row = first + lax.broadcasted_iota(jnp.int32, (EXPERT_BLOCK, 1), 0)
            mine = (row >= lo) & (row < hi)
            for c0 in range(0, D_MODEL, DOWN_COLS_PER_DOT):
                cols = slice(c0, c0 + DOWN_COLS_PER_DOT)
                kept = _dot(a, wd_s[:, cols]) + bd_ref[0, :, cols]
                if partial:
                    kept = jnp.where(mine, kept, acc_ref[rows, cols])
                    acc_ref[rows, cols] = kept
                for j in range(DOWN_COLS_PER_DOT // LANES):
                    part = pl.ds(r0 * ROW_PARTS + c0 // LANES + j, EXPERT_BLOCK, stride=ROW_PARTS)
                    o_ref[part, :] = kept[:, j * LANES:(j + 1) * LANES]

        pl.when(touched & whole)(functools.partial(compute, False))
        pl.when(touched & jnp.logical_not(whole))(functools.partial(compute, True))

    for r0 in range(0, EXPERT_TILE, EXPERT_BLOCK):
        block(r0)


def _work_items(counts, n_rows):
    n_tiles = n_rows // EXPERT_TILE
    ends = jnp.cumsum(counts)
    tile_ends = jnp.arange(1, n_tiles + 1, dtype=jnp.int32) * EXPERT_TILE
    n_items = n_tiles + N_EXPERTS
    count = lambda cond: jnp.sum(cond.astype(jnp.int32), axis=1)
    slot_t = jnp.arange(n_tiles, dtype=jnp.int32) + count(ends[None, :] < tile_ends[:, None])
    slot_e = jnp.arange(N_EXPERTS, dtype=jnp.int32) + count(tile_ends[None, :] <= ends[:, None])
    slots = jnp.concatenate([slot_t, slot_e])
    vals = jnp.concatenate([tile_ends, ends])
    item = jnp.arange(n_items, dtype=jnp.int32)
    hi = jnp.sum(jnp.where(slots[None, :] == item[:, None], vals[None, :], 0), axis=1)
    lo = jnp.concatenate([jnp.zeros((1,), jnp.int32), hi[:-1]])
    tile_id = jnp.minimum(lo // EXPERT_TILE, n_tiles - 1)
    exp_id = jnp.minimum(count(ends[None, :] <= lo[:, None]), N_EXPERTS - 1)
    later = jnp.where(exp_id[None, :] > exp_id[:, None], exp_id[None, :], N_EXPERTS)
    nxt_id = jnp.min(later, axis=1)
    nxt_id = jnp.where(nxt_id == N_EXPERTS, -1, nxt_id)
    return tile_id, exp_id, nxt_id, lo, hi


def _experts(items, xs, w_up, b_up_g, w_down, b_down):
    n_items = items[0].shape[0]
    rows = lambda w, t, e, nxt, lo, hi: (t[w], 0)
    bias = lambda c: pl.BlockSpec((1, 1, c), lambda w, t, e, nxt, lo, hi: (e[w], 0, 0))
    hbm = pl.BlockSpec(memory_space=pl.ANY)
    return pl.pallas_call(
        _experts_kernel,
        grid_spec=pltpu.PrefetchScalarGridSpec(
            num_scalar_prefetch=5,
            grid=(n_items,),
            in_specs=[pl.BlockSpec((EXPERT_TILE * ROW_PARTS, LANES), rows),
                      hbm, bias(2 * D_MODEL), hbm, bias(D_MODEL)],
            out_specs=pl.BlockSpec((EXPERT_TILE * ROW_PARTS, LANES), rows),
            scratch_shapes=[pltpu.VMEM((EXPERT_TILE, D_MODEL), F32),
                            pltpu.VMEM((D_MODEL, 2 * D_MODEL), BF16), pltpu.VMEM((D_MODEL, D_MODEL), BF16),
                            pltpu.VMEM((D_MODEL, 2 * D_MODEL), F32), pltpu.VMEM((D_MODEL, D_MODEL), F32),
                            pltpu.SemaphoreType.DMA((2,))]),
        out_shape=jax.ShapeDtypeStruct(xs.shape, F32),
        compiler_params=_params("arbitrary"),
        name="experts",
    )(*items, xs, w_up, b_up_g, w_down, b_down)


def _combine_kernel(pos_ref, nxt_ref, ys_ref, wts_ref, x1_ref, g2_ref, nf_ref, o_ref, buf_a, buf_b, sem):
    i = pl.program_id(0)
    sub = COMBINE_TILE

    def gather(p_ref, first_tok, buf, s):
        def copy(t, k):
            src = ys_ref.at[_token_tile(p_ref[(first_tok + t) * TOP_K + k]), :]
            return pltpu.make_async_copy(src, buf.at[k, _token_tile(t), :], sem.at[s])
        return copy

    this_a = gather(pos_ref, 0, buf_a, 0)
    this_b = gather(pos_ref, sub, buf_b, 1)
    next_a = gather(nxt_ref, 0, buf_a, 0)

    def reduce_rows(buf, half, start_other):
        rows = slice(half * sub, (half + 1) * sub)
        wts = wts_ref[rows, :]
        parts = []
        for j in range(ROW_PARTS):
            for t in range(j * sub // ROW_PARTS, (j + 1) * sub // ROW_PARTS):
                for k in range(TOP_K):
                    start_other(t, k).start(priority=k % 2)
            part = wts[:, 0:1] * buf[0, pl.ds(j, sub, stride=ROW_PARTS), :]
            for k in range(1, TOP_K):
                part = part + wts[:, k:k + 1] * buf[k, pl.ds(j, sub, stride=ROW_PARTS), :]
            parts.append(part)
        acc = jnp.concatenate(parts, axis=-1)
        o_ref[rows, :] = _rms(x1_ref[rows, :] + g2_ref[0] * acc, nf_ref[...])

    @pl.when(i == 0)
    def _():
        _start_rows(sub, this_a)

    _wait_rows(sub, this_a)
    reduce_rows(buf_a, 0, this_b)
    _wait_rows(sub, this_b)
    reduce_rows(buf_b, 1, next_a)

    @pl.when(i == pl.num_programs(0) - 1)
    def _():
        _wait_rows(sub, next_a)


def _combine(pos_flat, ys, wts, x1, mods, norm_final, seq):
    n_tok = x1.shape[0]
    step_tok = 2 * COMBINE_TILE
    steps_per_sample = seq // step_tok
    tok = lambda n: pl.BlockSpec((step_tok, n), lambda i: (i, 0))
    n_steps = n_tok // step_tok
    gather_buf = pltpu.VMEM((TOP_K, COMBINE_TILE * ROW_PARTS, LANES), F32)
    return pl.pallas_call(
        _combine_kernel,
        grid=(n_steps,),
        in_specs=_index_blocks(step_tok, n_steps) + [
            pl.BlockSpec(memory_space=pl.ANY), tok(TOP_K), tok(D_MODEL),
            pl.BlockSpec((1, 1, D_MODEL), lambda i: (i // steps_per_sample, 0, MOD_G2)),
            pl.BlockSpec((1, D_MODEL), lambda i: (0, 0))],
        out_specs=tok(D_MODEL),
        out_shape=jax.ShapeDtypeStruct((n_tok, D_MODEL), F32),
        scratch_shapes=[gather_buf, gather_buf, pltpu.SemaphoreType.DMA((2,))],
        compiler_params=_params("arbitrary"),
        name="combine",
    )(pos_flat, pos_flat, ys, wts, x1, mods, norm_final.reshape(1, -1))


def _moe(h2, idx, wts, counts, w_up, b_up_g, w_down, b_down, x1, mods, norm_final):
    bsz, seq, _ = x1.shape
    n_tok = bsz * seq
    pos_flat = _route(idx.reshape(n_tok, TOP_K), counts).reshape(-1)
    xs = _dispatch(pos_flat, h2.reshape(n_tok * ROW_PARTS, LANES))
    items = _work_items(counts.reshape(-1).astype(jnp.int32), n_tok * TOP_K)
    ys = _experts(items, xs, w_up, b_up_g, w_down, b_down)
    out = _combine(pos_flat, ys, wts.reshape(n_tok, TOP_K), x1.reshape(n_tok, D_MODEL), mods, norm_final, seq)
    return out.reshape(bsz, seq, D_MODEL)


def kernel(x, c, ctx, c_ctx, w_ada, b_ada, norm_mix, norm_ffn, w_in, w_gk_f, b_gk_f, w_gk_b, b_gk_b, gla_norm, w_gla_out, hy_conv_w, hy_conv_b, hy_f_w1, hy_f_b1, hy_f_w2, hy_f_b2, hy_f_w3, hy_f_b3, hy_f_w4, hy_sin_freq, hy_bias, w_hy_out, w_out, w_router, b_router, w_up, b_up, w_down, b_down, norm_final):
    depth = w_ada.shape[0]
    assert depth == 1, "single-layer block: the context stream only feeds later layers"
    bsz, seq, _ = x.shape

    n_rows = -(-(bsz + 1) // 8) * 8
    cc = jnp.zeros((n_rows, D_MODEL), F32).at[:bsz].set(c).at[bsz].set(c_ctx)
    mods = _ada(cc, w_ada[0], b_ada[0]).reshape(n_rows, 1, N_MOD * D_MODEL)

    sizes = (GLA_DK, GLA_DK, GLA_DV, GLA_DV, GLA_GATE_RANK, GLA_GATE_RANK, 3 * HY_WIDTH, D_MODEL, D_MODEL)
    offs = np.concatenate([[0], np.cumsum(sizes)])
    w_in_b = w_in[0].astype(BF16)
    wq, wk, wv, wog, wrf, wrb, why, wma, wmb = [w_in_b[:, offs[i]:offs[i + 1]] for i in range(len(sizes))]
    q, k, v, og, r_f, r_b, hy, m_a, m_b = _inproj(
        x, mods, lambda b: b, norm_mix[0], [wq, wk, wv, wog, wrf, wrb, why, wma, wmb],
        [BF16, BF16, BF16, BF16, F32, F32, BF16, BF16, BF16], tl=512, conv=(6, hy_conv_w[0], hy_conv_b[0]))
    ctx_len = ctx.shape[1]
    ctx_out = _inproj(ctx.reshape(1, bsz * ctx_len, D_MODEL), mods, lambda b: bsz, norm_mix[0],
                      [wk, wv, wrf, wrb], [BF16, BF16, F32, F32], tl=CTX_PROJ_TILE)
    k_c, v_c, rc_f, rc_b = [a.reshape(bsz, ctx_len, a.shape[-1]) for a in ctx_out]

    w_gk = jnp.stack([w_gk_f[0], w_gk_b[0]])
    b_gk = jnp.stack([b_gk_f[0], b_gk_b[0]])[:, None, :]
    o_f, o_b = _gla(q, k, v, r_f, r_b, k_c, v_c, rc_f, rc_b, w_gk, b_gk)

    tabs = [jnp.asarray(t).astype(BF16) for t in _dft_tables(seq)]
    t_sym, t_anti = tabs[:2]
    hs, hd, pn = _hyena_filter(seq, hy_f_w1[0], hy_f_b1[0], hy_f_w2[0], hy_f_b2[0], hy_f_w3[0], hy_f_b3[0],
                               hy_f_w4[0], hy_sin_freq[0])
    p_spec, q_spec, k_nyq = _hyena_spectrum(t_sym, t_anti, hs, hd, pn, hy_bias[0])
    y_h = _hyena_conv(hy, tabs, p_spec, q_spec, k_nyq)

    x1, h2, idx, wts, counts = _post(o_f, o_b, og, y_h, m_a, m_b, x, mods, gla_norm[0], norm_ffn[0],
                                     w_gla_out[0].astype(BF16), w_hy_out[0].astype(BF16), w_out[0].astype(BF16),
                                     w_router[0], b_router[0])

    b_up_g = b_up[0].reshape(N_EXPERTS, N_UP_GROUPS, LANES, 2).transpose(0, 1, 3, 2).reshape(N_EXPERTS, 1, -1)
    return _moe(h2, idx, wts, counts, w_up[0], b_up_g, w_down[0], b_down[0][:, None, :],
                x1, mods, norm_final)
```
